```python
import math
import jax
import jax.numpy as jnp
from jax import lax
import numpy as np

D_MODEL = 1024
BATCH = 8
SEQ = 4096
DEPTH = 1

EPS = 1e-6
MLA_HEADS = 8
MLA_NOPE = 64
MLA_ROPE = 32
MLA_V = 64
MLA_QK = MLA_NOPE + MLA_ROPE
MLA_Q_LORA = 384
MLA_KV_LORA = 256
ROPE_THETA = 10000.0
Q_BLOCK = 128
GDN_HEADS = 8
GDN_DK = 64
GDN_DV = 64
CONV_WIDTH = 4
CHUNK = 64
MLA_OUT = MLA_HEADS * MLA_V
GDN_OUT = GDN_HEADS * GDN_DV
D_MIX = MLA_OUT + GDN_OUT
IN_SIZES = (MLA_Q_LORA, MLA_KV_LORA, MLA_ROPE,
            GDN_HEADS * GDN_DK, GDN_HEADS * GDN_DK, GDN_HEADS * GDN_DV,
            GDN_HEADS * GDN_DV, GDN_HEADS, GDN_HEADS)
IN_DIM = sum(IN_SIZES)
CONV_CH = 2 * GDN_HEADS * GDN_DK + GDN_HEADS * GDN_DV
N_EXPERTS = 32
TOP_K = 4
D_EXPERT = D_MODEL
SWIGLU_ALPHA = 1.702
SWIGLU_LIMIT = 7.0
EXPERT_BLOCK = 256

kernel_name = 'hybrid_mla_gdn_moe_adaln_block'


def rms_norm(x, gain):
    xf = x.astype(jnp.float32)
    y = xf * lax.rsqrt(jnp.mean(xf * xf, axis=-1, keepdims=True) + EPS)
    return (y * gain.astype(jnp.float32)).astype(x.dtype)


def l2_normalize(x):
    xf = x.astype(jnp.float32)
    return (xf * lax.rsqrt(jnp.sum(xf * xf, axis=-1, keepdims=True) + EPS)).astype(x.dtype)


def rope_tables(positions):
    half = MLA_ROPE // 2
    inv_freq = ROPE_THETA ** (-jnp.arange(half, dtype=jnp.float32) / half)
    ang = positions.astype(jnp.float32)[..., None] * inv_freq
    return jnp.cos(ang), jnp.sin(ang)


def apply_rope(x, cos, sin):
    half = MLA_ROPE // 2
    xf = x.astype(jnp.float32)
    x1, x2 = xf[..., :half], xf[..., half:]
    return jnp.concatenate([x1 * cos - x2 * sin, x2 * cos + x1 * sin], axis=-1).astype(x.dtype)


def mla_group(q_lat, kv_lat, k_pe, positions, q_norm_g, w_q_b, kv_norm_g, w_kv_b, out_g):
    b, s, _ = q_lat.shape
    n_blocks = s // Q_BLOCK
    q = (rms_norm(q_lat, q_norm_g) @ w_q_b).reshape(b, s, MLA_HEADS, MLA_QK)
    kv = (rms_norm(kv_lat, kv_norm_g) @ w_kv_b).reshape(b, s, MLA_HEADS, MLA_NOPE + MLA_V)
    k_nope, v = kv[..., :MLA_NOPE], kv[..., MLA_NOPE:]
    cos, sin = rope_tables(positions)
    q_pe = apply_rope(q[..., MLA_NOPE:], cos[:, :, None, :], sin[:, :, None, :])
    k_pe = apply_rope(k_pe, cos, sin)
    q = jnp.concatenate([q[..., :MLA_NOPE], q_pe], axis=-1) * (MLA_QK ** -0.5)
    k = jnp.concatenate(
        [k_nope, jnp.broadcast_to(k_pe[:, :, None, :], (b, s, MLA_HEADS, MLA_ROPE))], axis=-1)
    q_blocks = jnp.moveaxis(q.reshape(b, n_blocks, Q_BLOCK, MLA_HEADS, MLA_QK), 1, 0)
    key_pos = jnp.arange(s, dtype=jnp.int32)

    def attend(args):
        q_blk, blk = args
        scores = jnp.einsum('bqhd,bkhd->bhqk', q_blk, k, preferred_element_type=jnp.float32)
        q_pos = blk * Q_BLOCK + jnp.arange(Q_BLOCK, dtype=jnp.int32)
        causal = key_pos[None, :] <= q_pos[:, None]
        probs = jax.nn.softmax(jnp.where(causal, scores, -jnp.inf), axis=-1).astype(v.dtype)
        return jnp.einsum('bhqk,bkhd->bqhd', probs, v)

    o = lax.map(attend, (q_blocks, jnp.arange(n_blocks, dtype=jnp.int32)))
    o = jnp.moveaxis(o, 0, 1).reshape(b, s, MLA_OUT)
    return rms_norm(o, out_g)


def causal_conv(x, w):
    ch = x.shape[-1]
    y = lax.conv_general_dilated(
        x, w[:, None, :].astype(x.dtype), window_strides=(1,),
        padding=((CONV_WIDTH - 1, 0),), dimension_numbers=('NWC', 'WIO', 'NWC'),
        feature_group_count=ch)
    return jax.nn.silu(y)


def gated_delta_rule(q, k, v, g, beta):
    b, s, h, dk = q.shape
    dv = v.shape[-1]
    n = s // CHUNK
    f32 = jnp.float32

    def to_chunks(t):
        return t.astype(f32).reshape(b, n, CHUNK, h, -1).transpose(0, 3, 1, 2, 4)

    qc = to_chunks(q) * (dk ** -0.5)
    kc = to_chunks(k)
    vc = to_chunks(v)
    gc = g.astype(f32).reshape(b, n, CHUNK, h).transpose(0, 3, 1, 2)
    bc = beta.astype(f32).reshape(b, n, CHUNK, h).transpose(0, 3, 1, 2)
    g_cum = jnp.cumsum(gc, axis=-1)
    tri_incl = jnp.tril(jnp.ones((CHUNK, CHUNK), dtype=bool))
    tri_strict = jnp.tril(jnp.ones((CHUNK, CHUNK), dtype=bool), -1)
    diff = g_cum[..., :, None] - g_cum[..., None, :]
    decay = jnp.where(tri_incl, jnp.exp(jnp.where(tri_incl, diff, 0.0)), 0.0)
    k_beta = kc * bc[..., None]
    a_low = jnp.where(tri_strict, jnp.einsum('bhnid,bhnjd->bhnij', k_beta, kc) * decay, 0.0)
    eye = jnp.eye(CHUNK, dtype=f32)
    rhs = jnp.concatenate([vc * bc[..., None], k_beta * jnp.exp(g_cum)[..., None]], axis=-1)
    sol = lax.linalg.triangular_solve(eye + a_low, rhs, left_side=True, lower=True,
                                      unit_diagonal=True)
    u, w = sol[..., :dv], sol[..., dv:]
    qk = jnp.where(tri_incl, jnp.einsum('bhnid,bhnjd->bhnij', qc, kc) * decay, 0.0)
    q_decay = qc * jnp.exp(g_cum)[..., None]
    k_tail = kc * jnp.exp(g_cum[..., -1:] - g_cum)[..., None]
    chunk_decay = jnp.exp(g_cum[..., -1])

    def step(state, xs):
        u_c, w_c, qk_c, qd_c, kt_c, cd_c = xs
        v_new = u_c - jnp.einsum('bhcd,bhde->bhce', w_c, state)
        o_c = (jnp.einsum('bhcd,bhde->bhce', qd_c, state)
               + jnp.einsum('bhij,bhje->bhie', qk_c, v_new))
        state = state * cd_c[..., None, None] + jnp.einsum('bhcd,bhce->bhde', kt_c, v_new)
        return state, o_c

    xs = tuple(jnp.moveaxis(t, 2, 0) for t in (u, w, qk, q_decay, k_tail, chunk_decay))
    _, o = lax.scan(step, jnp.zeros((b, h, dk, dv), f32), xs)
    return o.transpose(1, 0, 3, 2, 4).reshape(b, s, h, dv)


def gdn_group(q_in, k_in, v_in, z, a_in, b_in, conv_w, A_log, dt_bias, norm_g):
    b, s, _ = q_in.shape
    nk = GDN_HEADS * GDN_DK
    qkv = causal_conv(jnp.concatenate([q_in, k_in, v_in], axis=-1), conv_w)
    q = l2_normalize(qkv[..., :nk].reshape(b, s, GDN_HEADS, GDN_DK))
    k = l2_normalize(qkv[..., nk:2 * nk].reshape(b, s, GDN_HEADS, GDN_DK))
    v = qkv[..., 2 * nk:].reshape(b, s, GDN_HEADS, GDN_DV)
    beta = jax.nn.sigmoid(b_in.astype(jnp.float32))
    g = -jnp.exp(A_log.astype(jnp.float32)) * jax.nn.softplus(
        a_in.astype(jnp.float32) + dt_bias.astype(jnp.float32))
    o = gated_delta_rule(q, k, v, g, beta).astype(q_in.dtype)
    o = rms_norm(o, norm_g) * jax.nn.silu(z.reshape(b, s, GDN_HEADS, GDN_DV))
    return o.reshape(b, s, GDN_OUT)


def hybrid_mixer(h, positions, w_in, q_norm_g, w_q_b, kv_norm_g, w_kv_b, mla_out_g,
                 conv_w, A_log, dt_bias, gdn_norm_g, w_out):
    proj = h @ w_in
    cuts = [int(i) for i in np.cumsum(IN_SIZES)[:-1]]
    q_lat, kv_lat, k_pe, g_q, g_k, g_v, g_z, g_a, g_b = jnp.split(proj, cuts, axis=-1)
    mla_o = mla_group(q_lat, kv_lat, k_pe, positions, q_norm_g, w_q_b, kv_norm_g, w_kv_b,
                      mla_out_g)
    gdn_o = gdn_group(g_q, g_k, g_v, g_z, g_a, g_b, conv_w, A_log, dt_bias, gdn_norm_g)
    return jnp.concatenate([mla_o, gdn_o], axis=-1) @ w_out


def moe_ffn(h, router_w, router_b, w_gate_up, b_gate_up, w_down, b_down):
    b, s, d = h.shape
    t = b * s
    m = t * TOP_K
    xf = h.reshape(t, d)
    logits = (xf @ router_w + router_b).astype(jnp.float32)
    top_val, top_idx = lax.top_k(logits, TOP_K)
    weights = jax.nn.softmax(top_val, axis=-1)
    e_flat = top_idx.reshape(m).astype(jnp.int32)
    tok_flat = jnp.arange(m, dtype=jnp.int32) // TOP_K
    w_flat = weights.reshape(m)
    order = jnp.argsort(e_flat)
    e_sorted = e_flat[order]
    counts = jnp.zeros((N_EXPERTS,), jnp.int32).at[e_flat].add(1)
    padded = ((counts + EXPERT_BLOCK - 1) // EXPERT_BLOCK) * EXPERT_BLOCK
    start = jnp.cumsum(counts) - counts
    pstart = jnp.cumsum(padded) - padded
    dest = pstart[e_sorted] + (jnp.arange(m, dtype=jnp.int32) - start[e_sorted])
    n_blocks = (m + N_EXPERTS * (EXPERT_BLOCK - 1) + EXPERT_BLOCK - 1) // EXPERT_BLOCK
    rows = n_blocks * EXPERT_BLOCK
    row_tok = jnp.full((rows,), t, jnp.int32).at[dest].set(tok_flat[order])
    row_w = jnp.zeros((rows,), jnp.float32).at[dest].set(w_flat[order])
    row_e = jnp.full((rows,), N_EXPERTS - 1, jnp.int32).at[dest].set(e_sorted)
    blk_tok = row_tok.reshape(n_blocks, EXPERT_BLOCK)
    blk_e = row_e[::EXPERT_BLOCK]
    x_pad = jnp.concatenate([xf, jnp.zeros((1, d), xf.dtype)], axis=0)

    def expert_block(args):
        tok, e = args
        xb = x_pad[tok]
        gu = xb @ w_gate_up[e] + b_gate_up[e]
        gate = jnp.minimum(gu[:, :D_EXPERT], SWIGLU_LIMIT)
        up = jnp.clip(gu[:, D_EXPERT:], -SWIGLU_LIMIT, SWIGLU_LIMIT)
        act = (up + 1.0) * (gate * jax.nn.sigmoid(SWIGLU_ALPHA * gate))
        return act @ w_down[e] + b_down[e]

    y_rows = lax.map(expert_block, (blk_tok, blk_e)).reshape(rows, d)
    y_rows = y_rows * row_w[:, None].astype(y_rows.dtype)
    y = jnp.zeros((t + 1, d), y_rows.dtype).at[row_tok].add(y_rows)
    return y[:t].reshape(b, s, d)


def setup_inputs(seed: int = 0) -> dict:
    key = jax.random.key(seed)
    ks = jax.random.split(key, 24)

    def nrm(k, shape, scale):
        return jax.random.normal(k, shape, jnp.float32) * scale

    x = nrm(ks[0], (BATCH, SEQ, D_MODEL), 1.0)
    c = nrm(ks[1], (BATCH, D_MODEL), 1.0)
    positions = jnp.broadcast_to(jnp.arange(SEQ, dtype=jnp.int32), (BATCH, SEQ))
    ada_w = nrm(ks[2], (DEPTH, D_MODEL, 6 * D_MODEL), 0.5 * D_MODEL ** -0.5)
    ada_b = nrm(ks[3], (DEPTH, 6 * D_MODEL), 0.02)
    norm1_g = 1.0 + nrm(ks[4], (DEPTH, D_MODEL), 0.1)
    w_in = nrm(ks[5], (DEPTH, D_MODEL, IN_DIM), D_MODEL ** -0.5)
    q_norm_g = 1.0 + nrm(ks[6], (DEPTH, MLA_Q_LORA), 0.1)
    w_q_b = nrm(ks[7], (DEPTH, MLA_Q_LORA, MLA_HEADS * MLA_QK), MLA_Q_LORA ** -0.5)
    kv_norm_g = 1.0 + nrm(ks[8], (DEPTH, MLA_KV_LORA), 0.1)
    w_kv_b = nrm(ks[9], (DEPTH, MLA_KV_LORA, MLA_HEADS * (MLA_NOPE + MLA_V)), MLA_KV_LORA ** -0.5)
    mla_out_g = 1.0 + nrm(ks[10], (DEPTH, MLA_OUT), 0.1)
    conv_w = nrm(ks[11], (DEPTH, CONV_WIDTH, CONV_CH), CONV_WIDTH ** -0.5)
    A_log = jnp.log(jax.random.uniform(ks[12], (DEPTH, GDN_HEADS), jnp.float32, 1.0, 16.0))
    dt = jnp.exp(jax.random.uniform(ks[13], (DEPTH, GDN_HEADS), jnp.float32,
                                    math.log(1e-3), math.log(1e-1)))
    dt_bias = dt + jnp.log(-jnp.expm1(-dt))
    gdn_norm_g = 1.0 + nrm(ks[14], (DEPTH, GDN_DV), 0.1)
    w_out = nrm(ks[15], (DEPTH, D_MIX, D_MODEL), D_MIX ** -0.5)
    norm2_g = 1.0 + nrm(ks[16], (DEPTH, D_MODEL), 0.1)
    router_w = nrm(ks[17], (DEPTH, D_MODEL, N_EXPERTS), D_MODEL ** -0.5)
    router_b = nrm(ks[18], (DEPTH, N_EXPERTS), 0.01)
    w_gate_up = nrm(ks[19], (DEPTH, N_EXPERTS, D_MODEL, 2 * D_EXPERT), D_MODEL ** -0.5)
    b_gate_up = nrm(ks[20], (DEPTH, N_EXPERTS, 2 * D_EXPERT), 0.01)
    w_down = nrm(ks[21], (DEPTH, N_EXPERTS, D_EXPERT, D_MODEL), D_EXPERT ** -0.5)
    b_down = nrm(ks[22], (DEPTH, N_EXPERTS, D_MODEL), 0.01)
    final_g = 1.0 + nrm(ks[23], (D_MODEL,), 0.1)
    return {'x': x, 'c': c, 'positions': positions, 'ada_w': ada_w, 'ada_b': ada_b,
            'norm1_g': norm1_g, 'w_in': w_in, 'q_norm_g': q_norm_g, 'w_q_b': w_q_b,
            'kv_norm_g': kv_norm_g, 'w_kv_b': w_kv_b, 'mla_out_g': mla_out_g,
            'conv_w': conv_w, 'A_log': A_log, 'dt_bias': dt_bias, 'gdn_norm_g': gdn_norm_g,
            'w_out': w_out, 'norm2_g': norm2_g, 'router_w': router_w, 'router_b': router_b,
            'w_gate_up': w_gate_up, 'b_gate_up': b_gate_up, 'w_down': w_down,
            'b_down': b_down, 'final_g': final_g}


def reference(x, c, positions, ada_w, ada_b, norm1_g, w_in, q_norm_g, w_q_b, kv_norm_g,
              w_kv_b, mla_out_g, conv_w, A_log, dt_bias, gdn_norm_g, w_out, norm2_g,
              router_w, router_b, w_gate_up, b_gate_up, w_down, b_down, final_g):
    c_act = jax.nn.silu(c)
    for l in range(DEPTH):
        mod = (c_act @ ada_w[l] + ada_b[l])[:, None, :]
        sh1, sc1, gt1, sh2, sc2, gt2 = jnp.split(mod, 6, axis=-1)
        h = rms_norm(x, norm1_g[l]) * (1.0 + sc1) + sh1
        mix = hybrid_mixer(h, positions, w_in[l], q_norm_g[l], w_q_b[l], kv_norm_g[l],
                           w_kv_b[l], mla_out_g[l], conv_w[l], A_log[l], dt_bias[l],
                           gdn_norm_g[l], w_out[l])
        x = x + gt1 * mix
        h = rms_norm(x, norm2_g[l]) * (1.0 + sc2) + sh2
        ffn = moe_ffn(h, router_w[l], router_b[l], w_gate_up[l], b_gate_up[l], w_down[l],
                      b_down[l])
        x = x + gt2 * ffn
    return rms_norm(x, final_g)
```

```python
import functools
import math

import jax
import jax.numpy as jnp
from jax import lax
from jax.experimental import pallas as pl
from jax.experimental.pallas import tpu as pltpu

F32 = jnp.float32
BF16 = jnp.bfloat16

D_MODEL = 1024
EPS = 1e-6
MLA_HEADS = 8
MLA_NOPE = 64
MLA_ROPE = 32
MLA_V = 64
MLA_QK = MLA_NOPE + MLA_ROPE
MLA_Q_LORA = 384
MLA_KV_LORA = 256
ROPE_THETA = 10000.0
GDN_HEADS = 8
GDN_DK = 64
GDN_DV = 64
CONV_WIDTH = 4
CHUNK = 64
N_EXPERTS = 32
TOP_K = 4
D_EXPERT = D_MODEL
SWIGLU_ALPHA = 1.702
SWIGLU_LIMIT = 7.0

LANES = 128
HEAD_SLOT = 128
GDN_SLAB = 256
MISC_W = 128
PROJ_W = MLA_Q_LORA + MLA_KV_LORA + MISC_W + GDN_HEADS * GDN_SLAB
VMEM_LIMIT = 56 * 1024 * 1024


def _cparams(sem):
    return pltpu.CompilerParams(dimension_semantics=sem, vmem_limit_bytes=VMEM_LIMIT)


def _rms(x, g):
    return x * lax.rsqrt(jnp.mean(x * x, axis=-1, keepdims=True) + EPS) * g


def _sigmoid(x):
    return 1.0 / (1.0 + jnp.exp(-x))


def _silu(x):
    return x * _sigmoid(x)


def _dot_split(a, b):
    ah = a.astype(BF16)
    al = (a - ah.astype(F32)).astype(BF16)
    bh = b.astype(BF16)
    bl = (b - bh.astype(F32)).astype(BF16)
    return (jnp.dot(ah, bh, preferred_element_type=F32)
            + jnp.dot(ah, bl, preferred_element_type=F32)
            + jnp.dot(al, bh, preferred_element_type=F32))


def _mod_kernel(c_ref, w_ref, b_ref, o_ref):
    c = c_ref[...]
    o_ref[...] = jnp.dot(_silu(c), w_ref[...], preferred_element_type=F32,
                         precision=lax.Precision.HIGHEST) + b_ref[...]


def _modulation(c, ada_w, ada_b):
    b, d = c.shape
    n = ada_w.shape[1]
    return pl.pallas_call(
        _mod_kernel,
        out_shape=jax.ShapeDtypeStruct((b, n), F32),
        grid=(n // d,),
        in_specs=[pl.BlockSpec((b, d), lambda j: (0, 0)),
                  pl.BlockSpec((d, d), lambda j: (0, j)),
                  pl.BlockSpec((1, d), lambda j: (0, j))],
        out_specs=pl.BlockSpec((b, d), lambda j: (0, j)),
        compiler_params=_cparams(("arbitrary",)),
        name="adaln_mod",
    )(c, ada_w, ada_b.reshape(1, n))


def _inproj_kernel(x_ref, pos_ref, sc_ref, sh_ref, g1_ref, w1_ref, qg_ref, wq_ref, kvg_ref,
                   wkv_ref, freq_ref, alog_ref, dtb_ref,
                   q_ref, k_ref, v_ref, gdn_ref, gb_ref):
    x = x_ref[0]
    h = _rms(x, g1_ref[...]) * (1.0 + sc_ref[0]) + sh_ref[0]
    proj = jnp.dot(h.astype(BF16), w1_ref[...], preferred_element_type=F32)

    tb = x.shape[0]
    lane = lax.broadcasted_iota(jnp.int32, (tb, LANES), 1)
    ang = pos_ref[0].astype(F32) * freq_ref[...]
    cosv = jnp.cos(ang)
    sinv = jnp.sin(ang)
    in_rope = (lane >= MLA_NOPE) & (lane < MLA_QK)

    scale = MLA_QK ** -0.5
    qn = _rms(proj[:, :MLA_Q_LORA], qg_ref[...])
    qa = jnp.dot(qn.astype(BF16), wq_ref[...], preferred_element_type=F32)
    cq = jnp.where(lane < MLA_NOPE, scale, jnp.where(in_rope, cosv * scale, 0.0))
    sq = sinv * scale
    cq_t = jnp.concatenate([cq] * MLA_HEADS, axis=1)
    sq_t = jnp.concatenate([sq] * MLA_HEADS, axis=1)
    width = MLA_HEADS * HEAD_SLOT
    q = qa * cq_t + pltpu.roll(qa, width - MLA_ROPE, axis=1) * sq_t
    q_ref[0] = q.astype(BF16)

    kvn = _rms(proj[:, MLA_Q_LORA:MLA_Q_LORA + MLA_KV_LORA], kvg_ref[...])
    kva = jnp.dot(kvn.astype(BF16), wkv_ref[...], preferred_element_type=F32)
    misc = proj[:, MLA_Q_LORA + MLA_KV_LORA:MLA_Q_LORA + MLA_KV_LORA + MISC_W]
    kp = misc * jnp.where(in_rope, cosv, 0.0) + pltpu.roll(misc, MISC_W - MLA_ROPE, axis=1) * sinv
    k = kva[:, :width] + jnp.concatenate([kp] * MLA_HEADS, axis=1)
    k_ref[0] = k.astype(BF16)
    v_ref[0] = kva[:, width:].astype(BF16)

    z = misc + dtb_ref[...]
    softplus = jnp.maximum(z, 0.0) + jnp.log(1.0 + jnp.exp(-jnp.abs(z)))
    g = -jnp.exp(alog_ref[...]) * softplus
    gb_ref[0] = jnp.where(lane < GDN_HEADS, g, _sigmoid(misc))

    gdn_ref[0] = proj[:, MLA_Q_LORA + MLA_KV_LORA + MISC_W:].astype(BF16)


def _in_projection(x, positions, sc1, sh1, norm1_g, w1, q_norm_g, wq, kv_norm_g, wkv,
                   freq, alog, dtb, tb):
    b, s, d = x.shape
    hw = MLA_HEADS * HEAD_SLOT
    const = lambda shape: pl.BlockSpec(shape, lambda i, j: (0,) * len(shape))
    tok = lambda w: pl.BlockSpec((1, tb, w), lambda i, j: (i, j, 0))
    per_b = pl.BlockSpec((1, 1, d), lambda i, j: (i, 0, 0))
    return pl.pallas_call(
        _inproj_kernel,
        out_shape=(jax.ShapeDtypeStruct((b, s, hw), BF16),
                   jax.ShapeDtypeStruct((b, s, hw), BF16),
                   jax.ShapeDtypeStruct((b, s, MLA_HEADS * MLA_V), BF16),
                   jax.ShapeDtypeStruct((b, s, GDN_HEADS * GDN_SLAB), BF16),
                   jax.ShapeDtypeStruct((b, s, MISC_W), F32)),
        grid=(b, s // tb),
        in_specs=[tok(d), tok(1), per_b, per_b, const((1, d)), const(w1.shape),
                  const((1, MLA_Q_LORA)), const(wq.shape), const((1, MLA_KV_LORA)),
                  const(wkv.shape), const((1, LANES)), const((1, LANES)), const((1, LANES))],
        out_specs=(tok(hw), tok(hw), tok(MLA_HEADS * MLA_V), tok(GDN_HEADS * GDN_SLAB),
                   tok(MISC_W)),
        compiler_params=_cparams(("parallel", "parallel")),
        name="in_projection",
    )(x, positions.reshape(b, s, 1), sc1, sh1, norm1_g.reshape(1, d), w1,
      q_norm_g.reshape(1, -1), wq, kv_norm_g.reshape(1, -1), wkv, freq, alog, dtb)


def _attn_kernel(q_ref, k_ref, v_ref, o_ref, *, tq):
    qi = pl.program_id(2)
    qs = [q_ref[0, :, h * HEAD_SLOT:(h + 1) * HEAD_SLOT] for h in range(2)]

    def step(off, carry, masked):
        vj = v_ref[0, pl.ds(off, tq), :]
        new = []
        for h in range(2):
            m, l, acc = carry[h]
            kj = k_ref[0, pl.ds(off, tq), h * HEAD_SLOT:(h + 1) * HEAD_SLOT]
            s = lax.dot_general(qs[h], kj, (((1,), (1,)), ((), ())),
                                preferred_element_type=F32)
            if masked:
                row = lax.broadcasted_iota(jnp.int32, (tq, tq), 0)
                col = lax.broadcasted_iota(jnp.int32, (tq, tq), 1)
                s = jnp.where(col <= row, s, -jnp.inf)
            m_new = jnp.maximum(m, jnp.max(s, axis=-1, keepdims=True))
            p = jnp.exp(s - m_new)
            alpha = jnp.exp(m - m_new)
            l = alpha * l + jnp.sum(p, axis=-1, keepdims=True)
            acc = alpha * acc + jnp.dot(p.astype(BF16), vj, preferred_element_type=F32)
            new.append((m_new, l, acc))
        return tuple(new)

    def body(j, carry):
        return step(pl.multiple_of(j * tq, tq), carry, False)

    init = tuple((jnp.full((tq, 1), -jnp.inf, F32), jnp.zeros((tq, 1), F32),
                  jnp.zeros((tq, 2 * MLA_V), F32)) for _ in range(2))
    carry = lax.fori_loop(0, qi, body, init)
    carry = step(pl.multiple_of(qi * tq, tq), carry, True)
    lane = lax.broadcasted_iota(jnp.int32, (tq, 2 * MLA_V), 1)
    o0 = carry[0][2] / carry[0][1]
    o1 = carry[1][2] / carry[1][1]
    o_ref[0] = jnp.where(lane < MLA_V, o0, o1).astype(o_ref.dtype)


def _attention(q, k, v, tq):
    b, s, _ = q.shape
    pairs = MLA_HEADS // 2
    return pl.pallas_call(
        functools.partial(_attn_kernel, tq=tq),
        out_shape=jax.ShapeDtypeStruct((b, s, MLA_HEADS * MLA_V), BF16),
        grid=(b, pairs, s // tq),
        in_specs=[pl.BlockSpec((1, tq, 2 * HEAD_SLOT), lambda i, p, j: (i, j, p)),
                  pl.BlockSpec((1, s, 2 * HEAD_SLOT), lambda i, p, j: (i, 0, p)),
                  pl.BlockSpec((1, s, 2 * MLA_V), lambda i, p, j: (i, 0, p))],
        out_specs=pl.BlockSpec((1, tq, 2 * MLA_V), lambda i, p, j: (i, j, p)),
        compiler_params=_cparams(("parallel", "parallel", "arbitrary")),
        name="mla_attention",
    )(q, k, v)


def _gdn_kernel(slab_ref, gb_ref, cw_ref, ng_ref, o_ref, xbuf_ref, state_ref, *, ts):
    si = pl.program_id(2)
    hp = pl.program_id(1)
    pad = 8

    @pl.when(si == 0)
    def _():
        xbuf_ref[0:pad, :] = jnp.zeros((pad, 2 * GDN_SLAB), F32)
        state_ref[...] = jnp.zeros(state_ref.shape, F32)

    xs = slab_ref[0].astype(F32)
    xbuf_ref[pad:pad + ts, :] = xs
    conv = jnp.zeros((ts, 2 * GDN_SLAB), F32)
    for j in range(CONV_WIDTH):
        start = pad - (CONV_WIDTH - 1) + j
        conv = conv + cw_ref[j:j + 1, :] * xbuf_ref[start:start + ts, :]
    xbuf_ref[0:pad, :] = xs[ts - pad:ts, :]
    act = _silu(conv)

    gb = gb_ref[0]
    lane = lax.broadcasted_iota(jnp.int32, (ts, LANES), 1)
    n_chunks = ts // CHUNK
    row = lax.broadcasted_iota(jnp.int32, (CHUNK, CHUNK), 0)
    col = lax.broadcasted_iota(jnp.int32, (CHUNK, CHUNK), 1)
    tri_incl = col <= row
    tri_strict = col < row
    eye = col == row
    crow = lax.broadcasted_iota(jnp.int32, (ts, 1), 0) % CHUNK

    outs = []
    for hh in range(2):
        head = 2 * hp + hh
        base = hh * GDN_SLAB
        q_all = act[:, base:base + GDN_DK]
        k_all = act[:, base + GDN_DK:base + 2 * GDN_DK]
        v_all = act[:, base + 2 * GDN_DK:base + 2 * GDN_DK + GDN_DV]
        z_all = xs[:, base + 3 * GDN_DK:base + 3 * GDN_DK + GDN_DV]
        q_all = q_all * lax.rsqrt(jnp.sum(q_all * q_all, axis=-1, keepdims=True) + EPS)
        k_all = k_all * lax.rsqrt(jnp.sum(k_all * k_all, axis=-1, keepdims=True) + EPS)
        q_all = q_all * (GDN_DK ** -0.5)
        g_all = jnp.sum(jnp.where(lane == head, gb, 0.0), axis=-1, keepdims=True)
        b_all = jnp.sum(jnp.where(lane == head + GDN_HEADS, gb, 0.0), axis=-1, keepdims=True)

        gc_all = g_all
        shift = 1
        while shift < CHUNK:
            rolled = pltpu.roll(gc_all, shift, axis=0)
            gc_all = gc_all + jnp.where(crow >= shift, rolled, 0.0)
            shift *= 2

        state = state_ref[hh]
        o_chunks = []
        for c in range(n_chunks):
            sl = slice(c * CHUNK, (c + 1) * CHUNK)
            qc, kc, vc = q_all[sl], k_all[sl], v_all[sl]
            bc, gcum = b_all[sl], gc_all[sl]
            gcol = jnp.broadcast_to(gcum, (CHUNK, CHUNK))
            grow = jnp.sum(jnp.where(eye, gcol, 0.0), axis=0, keepdims=True)
            diff = gcol - grow
            decay = jnp.where(tri_incl, jnp.exp(jnp.where(tri_incl, diff, 0.0)), 0.0)
            k_beta = kc * bc
            kq = jnp.concatenate([k_beta, qc], axis=0).astype(BF16)
            kk = lax.dot_general(kq, kc.astype(BF16), (((1,), (1,)), ((), ())),
                                 preferred_element_type=F32)
            a_low = jnp.where(tri_strict, kk[:CHUNK] * decay, 0.0)
            qk = jnp.where(tri_incl, kk[CHUNK:] * decay, 0.0)
            eg = jnp.exp(gcum)
            y = jnp.concatenate([vc * bc, k_beta * eg], axis=1)
            m = -a_low
            n_steps = int(math.log2(CHUNK))
            for it in range(n_steps):
                if it < n_steps - 1:
                    prod = _dot_split(m, jnp.concatenate([y, m], axis=1))
                    y = y + prod[:, :2 * GDN_DV]
                    m = prod[:, 2 * GDN_DV:]
                else:
                    y = y + _dot_split(m, y)
            u = y[:, :GDN_DV]
            w = y[:, GDN_DV:]
            sb = state.astype(BF16)
            v_new = u - jnp.dot(w.astype(BF16), sb, preferred_element_type=F32)
            q_decay = qc * eg
            glast = gcum[CHUNK - 1:CHUNK, :]
            k_tail = kc * jnp.exp(glast - gcum)
            vb = v_new.astype(BF16)
            o_c = (jnp.dot(q_decay.astype(BF16), sb, preferred_element_type=F32)
                   + jnp.dot(qk.astype(BF16), vb, preferred_element_type=F32))
            state = state * jnp.exp(glast) + lax.dot_general(
                k_tail.astype(BF16), vb, (((0,), (0,)), ((), ())), preferred_element_type=F32)
            o_chunks.append(o_c)
        state_ref[hh] = state
        o_all = jnp.concatenate(o_chunks, axis=0)
        o_all = _rms(o_all, ng_ref[...]) * _silu(z_all)
        outs.append(o_all)
    o_ref[0] = jnp.concatenate(outs, axis=1).astype(o_ref.dtype)


def _gdn(slab, gb, conv_slab, gdn_norm_g, ts):
    b, s, _ = slab.shape
    pairs = GDN_HEADS // 2
    return pl.pallas_call(
        functools.partial(_gdn_kernel, ts=ts),
        out_shape=jax.ShapeDtypeStruct((b, s, GDN_HEADS * GDN_DV), BF16),
        grid=(b, pairs, s // ts),
        in_specs=[pl.BlockSpec((1, ts, 2 * GDN_SLAB), lambda i, p, j: (i, j, p)),
                  pl.BlockSpec((1, ts, MISC_W), lambda i, p, j: (i, j, 0)),
                  pl.BlockSpec((CONV_WIDTH, 2 * GDN_SLAB), lambda i, p, j: (0, p)),
                  pl.BlockSpec((1, GDN_DV), lambda i, p, j: (0, 0))],
        out_specs=pl.BlockSpec((1, ts, 2 * GDN_DV), lambda i, p, j: (i, j, p)),
        scratch_shapes=[pltpu.VMEM((ts + 8, 2 * GDN_SLAB), F32),
                        pltpu.VMEM((2, GDN_DK, GDN_DV), F32)],
        compiler_params=_cparams(("parallel", "parallel", "arbitrary")),
        name="gated_deltanet",
    )(slab, gb, conv_slab, gdn_norm_g.reshape(1, GDN_DV))


def _outproj_kernel(x_ref, ao_ref, go_ref, gt_ref, sc_ref, sh_ref, mg_ref, wo_ref, g2_ref,
                    rw_ref, rb_ref, x1_ref, h2_ref, route_ref, cnt_ref, carry_ref, *, tb):
    first = (pl.program_id(0) == 0) & (pl.program_id(1) == 0)

    @pl.when(first)
    def _():
        carry_ref[...] = jnp.zeros(carry_ref.shape, F32)

    mla = _rms(ao_ref[0].astype(F32), mg_ref[...])
    cat = jnp.concatenate([mla.astype(BF16), go_ref[0]], axis=1)
    mix = jnp.dot(cat, wo_ref[...], preferred_element_type=F32)
    x1 = x_ref[0] + gt_ref[0] * mix
    x1_ref[0] = x1
    h2 = _rms(x1, g2_ref[...]) * (1.0 + sc_ref[0]) + sh_ref[0]
    h2_ref[0] = h2.astype(BF16)

    logits = jnp.dot(h2, rw_ref[...], preferred_element_type=F32,
                     precision=lax.Precision.HIGHEST) + rb_ref[...]
    lane = lax.broadcasted_iota(jnp.int32, (tb, LANES), 1)
    work = jnp.where(lane < N_EXPERTS, logits, -jnp.inf)
    vals, idxs = [], []
    onehot = jnp.zeros((tb, LANES), F32)
    for _ in range(TOP_K):
        mx = jnp.max(work, axis=-1, keepdims=True)
        ix = jnp.min(jnp.where(work == mx, lane, LANES), axis=-1, keepdims=True)
        sel = lane == ix
        onehot = jnp.where(sel, 1.0, onehot)
        work = jnp.where(sel, -jnp.inf, work)
        vals.append(mx)
        idxs.append(ix)
    exps = [jnp.exp(v - vals[0]) for v in vals]
    den = exps[0] + exps[1] + exps[2] + exps[3]

    r = lax.broadcasted_iota(jnp.int32, (tb, tb), 0)
    c = lax.broadcasted_iota(jnp.int32, (tb, tb), 1)
    tri = jnp.where(c < r, 1.0, 0.0).astype(BF16)
    before = jnp.dot(tri, onehot.astype(BF16), preferred_element_type=F32) + carry_ref[...]
    route = jnp.zeros((tb, LANES), F32)
    for kk in range(TOP_K):
        rank = jnp.sum(jnp.where(lane == idxs[kk], before, 0.0), axis=-1, keepdims=True)
        route = jnp.where(lane == kk, idxs[kk].astype(F32), route)
        route = jnp.where(lane == TOP_K + kk, rank, route)
        route = jnp.where(lane == 2 * TOP_K + kk, exps[kk] / den, route)
    route_ref[0] = route
    total = carry_ref[...] + jnp.sum(onehot, axis=0, keepdims=True)
    carry_ref[...] = total
    cnt_ref[...] = total


def _out_projection(x, attn_o, gdn_o, gt1, sc2, sh2, mla_out_g, w_out, norm2_g, rw, rb, tb):
    b, s, d = x.shape
    const = lambda shape: pl.BlockSpec(shape, lambda i, j: (0,) * len(shape))
    tok = lambda w: pl.BlockSpec((1, tb, w), lambda i, j: (i, j, 0))
    per_b = pl.BlockSpec((1, 1, d), lambda i, j: (i, 0, 0))
    half = attn_o.shape[-1]
    return pl.pallas_call(
        functools.partial(_outproj_kernel, tb=tb),
        out_shape=(jax.ShapeDtypeStruct((b, s, d), F32),
                   jax.ShapeDtypeStruct((b, s, d), BF16),
                   jax.ShapeDtypeStruct((b, s, LANES), F32),
                   jax.ShapeDtypeStruct((1, LANES), F32)),
        grid=(b, s // tb),
        in_specs=[tok(d), tok(half), tok(half), per_b, per_b, per_b, const((1, half)),
                  const(w_out.shape), const((1, d)), const(rw.shape), const((1, LANES))],
        out_specs=(tok(d), tok(d), tok(LANES), const((1, LANES))),
        scratch_shapes=[pltpu.VMEM((1, LANES), F32)],
        compiler_params=_cparams(("arbitrary", "arbitrary")),
        name="out_projection_router",
    )(x, attn_o, gdn_o, gt1, sc2, sh2, mla_out_g.reshape(1, half), w_out,
      norm2_g.reshape(1, d), rw, rb)


def _experts_kernel(be_ref, nu_ref, xs_ref, rw_ref, wgu_ref, bgu_ref, wd_ref, bd_ref, y_ref):
    @pl.when(pl.program_id(0) < nu_ref[0])
    def _():
        gu = jnp.dot(xs_ref[...], wgu_ref[0], preferred_element_type=F32) + bgu_ref[0]
        gate = jnp.minimum(gu[:, :D_EXPERT], SWIGLU_LIMIT)
        up = jnp.clip(gu[:, D_EXPERT:], -SWIGLU_LIMIT, SWIGLU_LIMIT)
        act = (up + 1.0) * (gate * _sigmoid(SWIGLU_ALPHA * gate))
        y = jnp.dot(act.astype(BF16), wd_ref[0], preferred_element_type=F32) + bd_ref[0]
        y_ref[...] = (y * rw_ref[...]).astype(y_ref.dtype)


def _experts(blk_e, n_used, xs, row_w, wgu, bgu, wd, bd, bm):
    rows, d = xs.shape
    n_blocks = rows // bm
    row_map = lambda i, be, nu: (jnp.minimum(i, nu[0] - 1), 0)
    exp_map = lambda i, be, nu: (be[i], 0, 0)
    grid_spec = pltpu.PrefetchScalarGridSpec(
        num_scalar_prefetch=2,
        grid=(n_blocks,),
        in_specs=[pl.BlockSpec((bm, d), row_map),
                  pl.BlockSpec((bm, 1), row_map),
                  pl.BlockSpec((1, d, 2 * D_EXPERT), exp_map),
                  pl.BlockSpec((1, 1, 2 * D_EXPERT), exp_map),
                  pl.BlockSpec((1, D_EXPERT, d), exp_map),
                  pl.BlockSpec((1, 1, d), exp_map)],
        out_specs=pl.BlockSpec((bm, d), row_map))
    return pl.pallas_call(
        _experts_kernel,
        out_shape=jax.ShapeDtypeStruct((rows, d), F32),
        grid_spec=grid_spec,
        compiler_params=_cparams(("arbitrary",)),
        name="expert_mlp",
    )(blk_e, n_used, xs, row_w, wgu, bgu, wd, bd)


def _final_kernel(x1_ref, yk_ref, gt_ref, fg_ref, o_ref):
    d = x1_ref.shape[-1]
    ffn = yk_ref[0, :, 0:d]
    for kk in range(1, TOP_K):
        ffn = ffn + yk_ref[0, :, kk * d:(kk + 1) * d]
    x2 = x1_ref[0] + gt_ref[0] * ffn
    o_ref[0] = _rms(x2, fg_ref[...])


def _final(x1, yk, gt2, final_g, tb):
    b, s, d = x1.shape
    return pl.pallas_call(
        _final_kernel,
        out_shape=jax.ShapeDtypeStruct((b, s, d), F32),
        grid=(b, s // tb),
        in_specs=[pl.BlockSpec((1, tb, d), lambda i, j: (i, j, 0)),
                  pl.BlockSpec((1, tb, TOP_K * d), lambda i, j: (i, j, 0)),
                  pl.BlockSpec((1, 1, d), lambda i, j: (i, 0, 0)),
                  pl.BlockSpec((1, d), lambda i, j: (0, 0))],
        out_specs=pl.BlockSpec((1, tb, d), lambda i, j: (i, j, 0)),
        compiler_params=_cparams(("parallel", "parallel")),
        name="combine_final_norm",
    )(x1, yk, gt2, final_g.reshape(1, d))


def _rot_cols(w):
    half = MLA_ROPE // 2
    return jnp.concatenate([-w[..., half:], w[..., :half]], axis=-1)


def _prep_weights(w_in, w_q_b, w_kv_b, conv_w, A_log, dt_bias):
    d = w_in.shape[0]
    cuts = [MLA_Q_LORA, MLA_KV_LORA, MLA_ROPE, GDN_HEADS * GDN_DK, GDN_HEADS * GDN_DK,
            GDN_HEADS * GDN_DV, GDN_HEADS * GDN_DV, GDN_HEADS, GDN_HEADS]
    offs = [0]
    for cw in cuts:
        offs.append(offs[-1] + cw)
    part = lambda i: w_in[:, offs[i]:offs[i + 1]]
    k_pe = part(2)
    misc = jnp.concatenate(
        [part(7), part(8), jnp.zeros((d, MLA_NOPE - 2 * GDN_HEADS), w_in.dtype), k_pe,
         _rot_cols(k_pe)], axis=1)
    heads = lambda w, n: w.reshape(w.shape[0], GDN_HEADS, n)
    slab = jnp.concatenate([heads(part(3), GDN_DK), heads(part(4), GDN_DK),
                            heads(part(5), GDN_DV), heads(part(6), GDN_DV)], axis=-1)
    w1 = jnp.concatenate([part(0), part(1), misc, slab.reshape(d, GDN_HEADS * GDN_SLAB)],
                         axis=1).astype(BF16)

    wq3 = w_q_b.reshape(MLA_Q_LORA, MLA_HEADS, MLA_QK)
    pe = wq3[..., MLA_NOPE:]
    wq = jnp.concatenate([wq3[..., :MLA_NOPE], pe, _rot_cols(pe)], axis=-1)
    wq = wq.reshape(MLA_Q_LORA, MLA_HEADS * HEAD_SLOT).astype(BF16)

    wkv3 = w_kv_b.reshape(MLA_KV_LORA, MLA_HEADS, MLA_NOPE + MLA_V)
    wk = jnp.concatenate([wkv3[..., :MLA_NOPE],
                          jnp.zeros((MLA_KV_LORA, MLA_HEADS, HEAD_SLOT - MLA_NOPE), w_kv_b.dtype)],
                         axis=-1).reshape(MLA_KV_LORA, MLA_HEADS * HEAD_SLOT)
    wv = wkv3[..., MLA_NOPE:].reshape(MLA_KV_LORA, MLA_HEADS * MLA_V)
    wkv = jnp.concatenate([wk, wv], axis=1).astype(BF16)

    nk = GDN_HEADS * GDN_DK
    cheads = lambda w, n: w.reshape(CONV_WIDTH, GDN_HEADS, n)
    conv_slab = jnp.concatenate(
        [cheads(conv_w[:, :nk], GDN_DK), cheads(conv_w[:, nk:2 * nk], GDN_DK),
         cheads(conv_w[:, 2 * nk:], GDN_DV), jnp.zeros((CONV_WIDTH, GDN_HEADS, GDN_DV), conv_w.dtype)],
        axis=-1).reshape(CONV_WIDTH, GDN_HEADS * GDN_SLAB)

    half = MLA_ROPE // 2
    inv_freq = ROPE_THETA ** (-jnp.arange(half, dtype=F32) / half)
    freq = jnp.concatenate([jnp.zeros((MLA_NOPE,), F32), inv_freq, inv_freq,
                            jnp.zeros((LANES - MLA_QK,), F32)]).reshape(1, LANES)
    padl = lambda a: jnp.concatenate([a.astype(F32), jnp.zeros((LANES - a.shape[0],), F32)]).reshape(1, LANES)
    return w1, wq, wkv, conv_slab, freq, padl(A_log), padl(dt_bias)


def _layer(x, mod, positions, w_in, q_norm_g, w_q_b, kv_norm_g, w_kv_b, mla_out_g, conv_w,
           A_log, dt_bias, gdn_norm_g, w_out, norm1_g, norm2_g, router_w, router_b, w_gate_up,
           b_gate_up, w_down, b_down):
    b, s, d = x.shape
    t = b * s
    sh1, sc1, gt1, sh2, sc2, gt2 = [m.reshape(b, 1, d) for m in jnp.split(mod, 6, axis=-1)]
    w1, wq, wkv, conv_slab, freq, alog, dtb = _prep_weights(w_in, w_q_b, w_kv_b, conv_w, A_log,
                                                            dt_bias)
    tb = min(256, s)
    q, k, v, slab, gb = _in_projection(x, positions, sc1, sh1, norm1_g, w1, q_norm_g, wq,
                                       kv_norm_g, wkv, freq, alog, dtb, tb)
    attn_o = _attention(q, k, v, min(512, s))
    gdn_o = _gdn(slab, gb, conv_slab, gdn_norm_g, min(256, s))

    rw = jnp.concatenate([router_w, jnp.zeros((d, LANES - N_EXPERTS), router_w.dtype)], axis=1)
    rb = jnp.concatenate([router_b, jnp.zeros((LANES - N_EXPERTS,), router_b.dtype)]).reshape(1, LANES)
    x1, h2, route, counts = _out_projection(x, attn_o, gdn_o, gt1, sc2, sh2, mla_out_g,
                                            w_out.astype(BF16), norm2_g, rw, rb, tb)

    bm = 512
    route = route.reshape(t, LANES)
    idx = route[:, :TOP_K].astype(jnp.int32)
    rank = route[:, TOP_K:2 * TOP_K].astype(jnp.int32)
    wts = route[:, 2 * TOP_K:3 * TOP_K]
    cnt = counts[0, :N_EXPERTS].astype(jnp.int32)
    padded = ((cnt + bm - 1) // bm) * bm
    pend = jnp.cumsum(padded)
    pstart = pend - padded
    dest = pstart[idx] + rank
    n_blocks = (t * TOP_K + N_EXPERTS * (bm - 1) + bm - 1) // bm
    rows = n_blocks * bm
    n_used = (pend[-1] // bm).astype(jnp.int32).reshape(1)
    blk_start = jnp.arange(n_blocks, dtype=jnp.int32) * bm
    blk_e = jnp.minimum(jnp.searchsorted(pend, blk_start, side="right"), N_EXPERTS - 1)
    last_e = blk_e[jnp.maximum(n_used[0] - 1, 0)]
    blk_e = jnp.where(jnp.arange(n_blocks) < n_used[0], blk_e, last_e).astype(jnp.int32)

    dflat = dest.reshape(-1)
    tok_flat = jnp.arange(t * TOP_K, dtype=jnp.int32) // TOP_K
    row_tok = jnp.zeros((rows,), jnp.int32).at[dflat].set(tok_flat)
    row_w = jnp.zeros((rows,), F32).at[dflat].set(wts.reshape(-1))
    xs = h2.reshape(t, d)[row_tok]
    y_rows = _experts(blk_e, n_used, xs, row_w.reshape(rows, 1), w_gate_up.astype(BF16),
                      b_gate_up.reshape(N_EXPERTS, 1, -1), w_down.astype(BF16),
                      b_down.reshape(N_EXPERTS, 1, -1), bm)
    yk = y_rows[dflat].reshape(b, s, TOP_K * d)
    return x1, yk, gt2


def kernel(x, c, positions, ada_w, ada_b, norm1_g, w_in, q_norm_g, w_q_b, kv_norm_g, w_kv_b, mla_out_g, conv_w, A_log, dt_bias, gdn_norm_g, w_out, norm2_g, router_w, router_b, w_gate_up, b_gate_up, w_down, b_down, final_g):
    depth = ada_w.shape[0]
    assert depth == 1
    l = 0
    mod = _modulation(c, ada_w[l], ada_b[l])
    x1, yk, gt2 = _layer(x, mod, positions, w_in[l], q_norm_g[l], w_q_b[l], kv_norm_g[l],
                         w_kv_b[l], mla_out_g[l], conv_w[l], A_log[l], dt_bias[l], gdn_norm_g[l],
                         w_out[l], norm1_g[l], norm2_g[l], router_w[l], router_b[l],
                         w_gate_up[l], b_gate_up[l], w_down[l], b_down[l])
    return _final(x1, yk, gt2, final_g, min(256, x.shape[1]))
```

```python
import functools

import jax
import jax.numpy as jnp
from jax import lax
from jax.experimental import pallas as pl
from jax.experimental.pallas import tpu as pltpu

F32 = jnp.float32
BF16 = jnp.bfloat16

D_MODEL = 1024
EPS = 1e-6
MLA_HEADS = 8
MLA_NOPE = 64
MLA_ROPE = 32
MLA_V = 64
MLA_QK = MLA_NOPE + MLA_ROPE
MLA_Q_LORA = 384
MLA_KV_LORA = 256
ROPE_THETA = 10000.0
GDN_HEADS = 8
GDN_DK = 64
GDN_DV = 64
CONV_WIDTH = 4
CHUNK = 64
N_EXPERTS = 32
TOP_K = 4
D_EXPERT = D_MODEL
SWIGLU_ALPHA = 1.702
SWIGLU_LIMIT = 7.0

LANES = 128
HEAD_SLOT = 128
GDN_SLAB = 256
MISC_W = 128
PROJ_W = MLA_Q_LORA + MLA_KV_LORA + MISC_W + GDN_HEADS * GDN_SLAB
VMEM_LIMIT = 56 * 1024 * 1024


def _cparams(sem):
    return pltpu.CompilerParams(dimension_semantics=sem, vmem_limit_bytes=VMEM_LIMIT)


def _rms(x, g):
    return x * lax.rsqrt(jnp.mean(x * x, axis=-1, keepdims=True) + EPS) * g


def _sigmoid(x):
    return 1.0 / (1.0 + jnp.exp(-x))


def _silu(x):
    return x * _sigmoid(x)


def _mod_kernel(c_ref, w_ref, b_ref, o_ref):
    c = c_ref[...]
    o_ref[...] = jnp.dot(_silu(c), w_ref[...], preferred_element_type=F32,
                         precision=lax.Precision.HIGHEST) + b_ref[...]


def _modulation(c, ada_w, ada_b):
    b, d = c.shape
    n = ada_w.shape[1]
    return pl.pallas_call(
        _mod_kernel,
        out_shape=jax.ShapeDtypeStruct((b, n), F32),
        grid=(n // d,),
        in_specs=[pl.BlockSpec((b, d), lambda j: (0, 0)),
                  pl.BlockSpec((d, d), lambda j: (0, j)),
                  pl.BlockSpec((1, d), lambda j: (0, j))],
        out_specs=pl.BlockSpec((b, d), lambda j: (0, j)),
        compiler_params=_cparams(("arbitrary",)),
        name="adaln_mod",
    )(c, ada_w, ada_b.reshape(1, n))


def _inproj_kernel(x_ref, pos_ref, sc_ref, sh_ref, g1_ref, w1_ref, qg_ref, wq_ref, kvg_ref,
                   wkv_ref, freq_ref, alog_ref, dtb_ref,
                   q_ref, k_ref, v_ref, gdn_ref, gb_ref):
    x = x_ref[0]
    h = _rms(x, g1_ref[...]) * (1.0 + sc_ref[0]) + sh_ref[0]
    proj = jnp.dot(h.astype(BF16), w1_ref[...], preferred_element_type=F32)

    tb = x.shape[0]
    lane = lax.broadcasted_iota(jnp.int32, (tb, LANES), 1)
    ang = pos_ref[0].astype(F32) * freq_ref[...]
    cosv = jnp.cos(ang)
    sinv = jnp.sin(ang)
    in_rope = (lane >= MLA_NOPE) & (lane < MLA_QK)

    scale = MLA_QK ** -0.5
    qn = _rms(proj[:, :MLA_Q_LORA], qg_ref[...])
    qa = jnp.dot(qn.astype(BF16), wq_ref[...], preferred_element_type=F32)
    cq = jnp.where(lane < MLA_NOPE, scale, jnp.where(in_rope, cosv * scale, 0.0))
    sq = sinv * scale
    cq_t = jnp.concatenate([cq] * MLA_HEADS, axis=1)
    sq_t = jnp.concatenate([sq] * MLA_HEADS, axis=1)
    width = MLA_HEADS * HEAD_SLOT
    q = qa * cq_t + pltpu.roll(qa, width - MLA_ROPE, axis=1) * sq_t
    q_ref[0] = q.astype(BF16)

    kvn = _rms(proj[:, MLA_Q_LORA:MLA_Q_LORA + MLA_KV_LORA], kvg_ref[...])
    kva = jnp.dot(kvn.astype(BF16), wkv_ref[...], preferred_element_type=F32)
    misc = proj[:, MLA_Q_LORA + MLA_KV_LORA:MLA_Q_LORA + MLA_KV_LORA + MISC_W]
    kp = misc * jnp.where(in_rope, cosv, 0.0) + pltpu.roll(misc, MISC_W - MLA_ROPE, axis=1) * sinv
    k = kva[:, :width] + jnp.concatenate([kp] * MLA_HEADS, axis=1)
    k_ref[0] = k.astype(BF16)
    v_ref[0] = kva[:, width:].astype(BF16)

    z = misc + dtb_ref[...]
    softplus = jnp.maximum(z, 0.0) + jnp.log(1.0 + jnp.exp(-jnp.abs(z)))
    g = -jnp.exp(alog_ref[...]) * softplus
    gb_ref[0] = jnp.where(lane < GDN_HEADS, g, _sigmoid(misc))

    gdn_ref[0] = proj[:, MLA_Q_LORA + MLA_KV_LORA + MISC_W:].astype(BF16)


def _in_projection(x, positions, sc1, sh1, norm1_g, w1, q_norm_g, wq, kv_norm_g, wkv,
                   freq, alog, dtb, tb):
    b, s, d = x.shape
    hw = MLA_HEADS * HEAD_SLOT
    const = lambda shape: pl.BlockSpec(shape, lambda i, j: (0,) * len(shape))
    tok = lambda w: pl.BlockSpec((1, tb, w), lambda i, j: (i, j, 0))
    per_b = pl.BlockSpec((1, 1, d), lambda i, j: (i, 0, 0))
    return pl.pallas_call(
        _inproj_kernel,
        out_shape=(jax.ShapeDtypeStruct((b, s, hw), BF16),
                   jax.ShapeDtypeStruct((b, s, hw), BF16),
                   jax.ShapeDtypeStruct((b, s, MLA_HEADS * MLA_V), BF16),
                   jax.ShapeDtypeStruct((b, s, GDN_HEADS * GDN_SLAB), BF16),
                   jax.ShapeDtypeStruct((b, s, MISC_W), F32)),
        grid=(b, s // tb),
        in_specs=[tok(d), tok(1), per_b, per_b, const((1, d)), const(w1.shape),
                  const((1, MLA_Q_LORA)), const(wq.shape), const((1, MLA_KV_LORA)),
                  const(wkv.shape), const((1, LANES)), const((1, LANES)), const((1, LANES))],
        out_specs=(tok(hw), tok(hw), tok(MLA_HEADS * MLA_V), tok(GDN_HEADS * GDN_SLAB),
                   tok(MISC_W)),
        compiler_params=_cparams(("parallel", "parallel")),
        name="in_projection",
    )(x, positions.reshape(b, s, 1), sc1, sh1, norm1_g.reshape(1, d), w1,
      q_norm_g.reshape(1, -1), wq, kv_norm_g.reshape(1, -1), wkv, freq, alog, dtb)


def _attn_kernel(q_ref, k_ref, v_ref, o_ref, *, tq):
    qi = pl.program_id(2)
    qs = [q_ref[0, :, h * HEAD_SLOT:(h + 1) * HEAD_SLOT] for h in range(2)]

    def step(off, carry, masked):
        vj = v_ref[0, pl.ds(off, tq), :]
        new = []
        for h in range(2):
            m, l, acc = carry[h]
            kj = k_ref[0, pl.ds(off, tq), h * HEAD_SLOT:(h + 1) * HEAD_SLOT]
            s = lax.dot_general(qs[h], kj, (((1,), (1,)), ((), ())),
                                preferred_element_type=F32)
            if masked:
                row = lax.broadcasted_iota(jnp.int32, (tq, tq), 0)
                col = lax.broadcasted_iota(jnp.int32, (tq, tq), 1)
                s = jnp.where(col <= row, s, -jnp.inf)
            m_new = jnp.maximum(m, jnp.max(s, axis=-1, keepdims=True))
            p = jnp.exp(s - m_new)
            alpha = jnp.exp(m - m_new)
            l = alpha * l + jnp.sum(p, axis=-1, keepdims=True)
            acc = alpha * acc + jnp.dot(p.astype(BF16), vj, preferred_element_type=F32)
            new.append((m_new, l, acc))
        return tuple(new)

    def body(j, carry):
        return step(pl.multiple_of(j * tq, tq), carry, False)

    init = tuple((jnp.full((tq, 1), -jnp.inf, F32), jnp.zeros((tq, 1), F32),
                  jnp.zeros((tq, 2 * MLA_V), F32)) for _ in range(2))
    carry = lax.fori_loop(0, qi, body, init)
    carry = step(pl.multiple_of(qi * tq, tq), carry, True)
    lane = lax.broadcasted_iota(jnp.int32, (tq, 2 * MLA_V), 1)
    o0 = carry[0][2] / carry[0][1]
    o1 = carry[1][2] / carry[1][1]
    o_ref[0] = jnp.where(lane < MLA_V, o0, o1).astype(o_ref.dtype)


def _attention(q, k, v, tq):
    b, s, _ = q.shape
    pairs = MLA_HEADS // 2
    return pl.pallas_call(
        functools.partial(_attn_kernel, tq=tq),
        out_shape=jax.ShapeDtypeStruct((b, s, MLA_HEADS * MLA_V), BF16),
        grid=(b, pairs, s // tq),
        in_specs=[pl.BlockSpec((1, tq, 2 * HEAD_SLOT), lambda i, p, j: (i, j, p)),
                  pl.BlockSpec((1, s, 2 * HEAD_SLOT), lambda i, p, j: (i, 0, p)),
                  pl.BlockSpec((1, s, 2 * MLA_V), lambda i, p, j: (i, 0, p))],
        out_specs=pl.BlockSpec((1, tq, 2 * MLA_V), lambda i, p, j: (i, j, p)),
        compiler_params=_cparams(("parallel", "parallel", "arbitrary")),
        name="mla_attention",
    )(q, k, v)


def _bdot(a, b):
    return jnp.dot(a.astype(BF16), b.astype(BF16), preferred_element_type=F32)


def _gdn_kernel(slab_ref, gb_ref, cw_ref, ng_ref, o_ref, xbuf_ref, state_ref, *, ts, hg):
    si = pl.program_id(2)
    group = pl.program_id(1)
    width = hg * GDN_SLAB
    pad = 8

    @pl.when(si == 0)
    def _():
        xbuf_ref[0:pad, :] = jnp.zeros((pad, width), F32)
        state_ref[...] = jnp.zeros(state_ref.shape, F32)

    xs = slab_ref[0].astype(F32)
    xbuf_ref[pad:pad + ts, :] = xs
    conv = jnp.zeros((ts, width), F32)
    for j in range(CONV_WIDTH):
        start = pad - (CONV_WIDTH - 1) + j
        conv = conv + cw_ref[j:j + 1, :] * xbuf_ref[start:start + ts, :]
    xbuf_ref[0:pad, :] = xs[ts - pad:ts, :]
    act = _silu(conv)

    gb = gb_ref[0]
    lane = lax.broadcasted_iota(jnp.int32, (ts, LANES), 1)
    n_chunks = ts // CHUNK
    row = lax.broadcasted_iota(jnp.int32, (CHUNK, CHUNK), 0)
    col = lax.broadcasted_iota(jnp.int32, (CHUNK, CHUNK), 1)
    tri_incl = col <= row
    tri_strict = col < row
    eye = col == row
    crow = lax.broadcasted_iota(jnp.int32, (ts, 1), 0) % CHUNK

    heads = []
    for hh in range(hg):
        head = hg * group + hh
        base = hh * GDN_SLAB
        q_all = act[:, base:base + GDN_DK]
        k_all = act[:, base + GDN_DK:base + 2 * GDN_DK]
        v_all = act[:, base + 2 * GDN_DK:base + 2 * GDN_DK + GDN_DV]
        z_all = xs[:, base + 3 * GDN_DK:base + 3 * GDN_DK + GDN_DV]
        q_all = q_all * lax.rsqrt(jnp.sum(q_all * q_all, axis=-1, keepdims=True) + EPS)
        k_all = k_all * lax.rsqrt(jnp.sum(k_all * k_all, axis=-1, keepdims=True) + EPS)
        q_all = q_all * (GDN_DK ** -0.5)
        g_all = jnp.sum(jnp.where(lane == head, gb, 0.0), axis=-1, keepdims=True)
        b_all = jnp.sum(jnp.where(lane == head + GDN_HEADS, gb, 0.0), axis=-1, keepdims=True)

        gc_all = g_all
        shift = 1
        while shift < CHUNK:
            rolled = pltpu.roll(gc_all, shift, axis=0)
            gc_all = gc_all + jnp.where(crow >= shift, rolled, 0.0)
            shift *= 2
        heads.append((q_all, k_all, v_all, z_all, b_all, gc_all))

    units = [(hh, c) for c in range(n_chunks) for hh in range(hg)]
    a_low, qk, y0, qd, kt, cd = {}, {}, {}, {}, {}, {}
    for u in units:
        hh, c = u
        q_all, k_all, v_all, _, b_all, gc_all = heads[hh]
        sl = slice(c * CHUNK, (c + 1) * CHUNK)
        qc, kc, vc, bc, gcum = q_all[sl], k_all[sl], v_all[sl], b_all[sl], gc_all[sl]
        gcol = jnp.broadcast_to(gcum, (CHUNK, CHUNK))
        grow = jnp.sum(jnp.where(eye, gcol, 0.0), axis=0, keepdims=True)
        diff = gcol - grow
        decay = jnp.where(tri_incl, jnp.exp(jnp.where(tri_incl, diff, 0.0)), 0.0)
        k_beta = kc * bc
        kq = jnp.concatenate([k_beta, qc], axis=0).astype(BF16)
        kk = lax.dot_general(kq, kc.astype(BF16), (((1,), (1,)), ((), ())),
                             preferred_element_type=F32)
        a_low[u] = jnp.where(tri_strict, kk[:CHUNK] * decay, 0.0)
        qk[u] = jnp.where(tri_incl, kk[CHUNK:] * decay, 0.0)
        eg = jnp.exp(gcum)
        y0[u] = jnp.concatenate([vc * bc, k_beta * eg], axis=1)
        qd[u] = qc * eg
        glast = gcum[CHUNK - 1:CHUNK, :]
        kt[u] = kc * jnp.exp(glast - gcum)
        cd[u] = jnp.exp(glast)

    blk = 2
    tinv = {u: jnp.where(eye, 1.0, 0.0) - jnp.where((row // blk) == (col // blk), a_low[u], 0.0)
            for u in units}
    while blk < CHUNK:
        in_big = (row // (2 * blk)) == (col // (2 * blk))
        off_mask = in_big & ((row // blk) != (col // blk))
        left = {u: _bdot(tinv[u], jnp.where(off_mask, a_low[u], 0.0)) for u in units}
        tinv = {u: tinv[u] - _bdot(left[u], tinv[u]) for u in units}
        blk *= 2

    y = {u: _bdot(tinv[u], y0[u]) for u in units}
    ky = {u: lax.dot_general(kt[u].astype(BF16), y[u].astype(BF16), (((0,), (0,)), ((), ())),
                             preferred_element_type=F32) for u in units}
    qy = {u: _bdot(qk[u], y[u]) for u in units}

    outs = [[] for _ in range(hg)]
    states = [state_ref[hh] for hh in range(hg)]
    for c in range(n_chunks):
        for hh in range(hg):
            u = (hh, c)
            sb = states[hh].astype(BF16)
            r_mat = qd[u] - qy[u][:, GDN_DV:]
            outs[hh].append(jnp.dot(r_mat.astype(BF16), sb, preferred_element_type=F32)
                            + qy[u][:, :GDN_DV])
            states[hh] = (states[hh] * cd[u]
                          - jnp.dot(ky[u][:, GDN_DV:].astype(BF16), sb,
                                    preferred_element_type=F32)
                          + ky[u][:, :GDN_DV])
    finals = []
    for hh in range(hg):
        state_ref[hh] = states[hh]
        o_all = jnp.concatenate(outs[hh], axis=0)
        finals.append(_rms(o_all, ng_ref[...]) * _silu(heads[hh][3]))
    o_ref[0] = jnp.concatenate(finals, axis=1).astype(o_ref.dtype)


def _gdn(slab, gb, conv_slab, gdn_norm_g, ts, hg):
    b, s, _ = slab.shape
    groups = GDN_HEADS // hg
    return pl.pallas_call(
        functools.partial(_gdn_kernel, ts=ts, hg=hg),
        out_shape=jax.ShapeDtypeStruct((b, s, GDN_HEADS * GDN_DV), BF16),
        grid=(b, groups, s // ts),
        in_specs=[pl.BlockSpec((1, ts, hg * GDN_SLAB), lambda i, p, j: (i, j, p)),
                  pl.BlockSpec((1, ts, MISC_W), lambda i, p, j: (i, j, 0)),
                  pl.BlockSpec((CONV_WIDTH, hg * GDN_SLAB), lambda i, p, j: (0, p)),
                  pl.BlockSpec((1, GDN_DV), lambda i, p, j: (0, 0))],
        out_specs=pl.BlockSpec((1, ts, hg * GDN_DV), lambda i, p, j: (i, j, p)),
        scratch_shapes=[pltpu.VMEM((ts + 8, hg * GDN_SLAB), F32),
                        pltpu.VMEM((hg, GDN_DK, GDN_DV), F32)],
        compiler_params=_cparams(("parallel", "parallel", "arbitrary")),
        name="gated_deltanet",
    )(slab, gb, conv_slab, gdn_norm_g.reshape(1, GDN_DV))


def _outproj_kernel(x_ref, ao_ref, go_ref, gt_ref, sc_ref, sh_ref, mg_ref, wo_ref, g2_ref,
                    rw_ref, rb_ref, x1_ref, h2_ref, route_ref, cnt_ref, carry_ref, *, tb):
    first = (pl.program_id(0) == 0) & (pl.program_id(1) == 0)

    @pl.when(first)
    def _():
        carry_ref[...] = jnp.zeros(carry_ref.shape, F32)

    mla = _rms(ao_ref[0].astype(F32), mg_ref[...])
    cat = jnp.concatenate([mla.astype(BF16), go_ref[0]], axis=1)
    mix = jnp.dot(cat, wo_ref[...], preferred_element_type=F32)
    x1 = x_ref[0] + gt_ref[0] * mix
    x1_ref[0] = x1
    h2 = _rms(x1, g2_ref[...]) * (1.0 + sc_ref[0]) + sh_ref[0]
    h2_ref[0] = h2.astype(BF16)

    logits = jnp.dot(h2, rw_ref[...], preferred_element_type=F32,
                     precision=lax.Precision.HIGHEST) + rb_ref[...]
    lane = lax.broadcasted_iota(jnp.int32, (tb, LANES), 1)
    work = jnp.where(lane < N_EXPERTS, logits, -jnp.inf)
    vals, idxs = [], []
    onehot = jnp.zeros((tb, LANES), F32)
    for _ in range(TOP_K):
        mx = jnp.max(work, axis=-1, keepdims=True)
        ix = jnp.min(jnp.where(work == mx, lane, LANES), axis=-1, keepdims=True)
        sel = lane == ix
        onehot = jnp.where(sel, 1.0, onehot)
        work = jnp.where(sel, -jnp.inf, work)
        vals.append(mx)
        idxs.append(ix)
    exps = [jnp.exp(v - vals[0]) for v in vals]
    den = exps[0] + exps[1] + exps[2] + exps[3]

    r = lax.broadcasted_iota(jnp.int32, (tb, tb), 0)
    c = lax.broadcasted_iota(jnp.int32, (tb, tb), 1)
    tri = jnp.where(c < r, 1.0, 0.0).astype(BF16)
    before = jnp.dot(tri, onehot.astype(BF16), preferred_element_type=F32) + carry_ref[...]
    route = jnp.zeros((tb, LANES), F32)
    for kk in range(TOP_K):
        rank = jnp.sum(jnp.where(lane == idxs[kk], before, 0.0), axis=-1, keepdims=True)
        route = jnp.where(lane == kk, idxs[kk].astype(F32), route)
        route = jnp.where(lane == TOP_K + kk, rank, route)
        route = jnp.where(lane == 2 * TOP_K + kk, exps[kk] / den, route)
    route_ref[0] = route
    total = carry_ref[...] + jnp.sum(onehot, axis=0, keepdims=True)
    carry_ref[...] = total
    cnt_ref[...] = total


def _out_projection(x, attn_o, gdn_o, gt1, sc2, sh2, mla_out_g, w_out, norm2_g, rw, rb, tb):
    b, s, d = x.shape
    const = lambda shape: pl.BlockSpec(shape, lambda i, j: (0,) * len(shape))
    tok = lambda w: pl.BlockSpec((1, tb, w), lambda i, j: (i, j, 0))
    per_b = pl.BlockSpec((1, 1, d), lambda i, j: (i, 0, 0))
    half = attn_o.shape[-1]
    return pl.pallas_call(
        functools.partial(_outproj_kernel, tb=tb),
        out_shape=(jax.ShapeDtypeStruct((b, s, d), F32),
                   jax.ShapeDtypeStruct((b, s, d), BF16),
                   jax.ShapeDtypeStruct((b, s, LANES), F32),
                   jax.ShapeDtypeStruct((1, LANES), F32)),
        grid=(b, s // tb),
        in_specs=[tok(d), tok(half), tok(half), per_b, per_b, per_b, const((1, half)),
                  const(w_out.shape), const((1, d)), const(rw.shape), const((1, LANES))],
        out_specs=(tok(d), tok(d), tok(LANES), const((1, LANES))),
        scratch_shapes=[pltpu.VMEM((1, LANES), F32)],
        compiler_params=_cparams(("arbitrary", "arbitrary")),
        name="out_projection_router",
    )(x, attn_o, gdn_o, gt1, sc2, sh2, mla_out_g.reshape(1, half), w_out,
      norm2_g.reshape(1, d), rw, rb)


def _experts_kernel(be_ref, nu_ref, xs_ref, wgu_ref, bgu_ref, wd_ref, bd_ref, y_ref,
                    wgu_bf, wd_bf):
    i = pl.program_id(0)
    new_expert = (i == 0) | (be_ref[i] != be_ref[jnp.maximum(i - 1, 0)])

    @pl.when(new_expert)
    def _():
        wgu_bf[...] = wgu_ref[0].astype(BF16)
        wd_bf[...] = wd_ref[0].astype(BF16)

    @pl.when(i < nu_ref[0])
    def _():
        gu = jnp.dot(xs_ref[...], wgu_bf[...], preferred_element_type=F32) + bgu_ref[0]
        gate = jnp.minimum(gu[:, :D_EXPERT], SWIGLU_LIMIT)
        up = jnp.clip(gu[:, D_EXPERT:], -SWIGLU_LIMIT, SWIGLU_LIMIT)
        act = (up + 1.0) * (gate * _sigmoid(SWIGLU_ALPHA * gate))
        y = jnp.dot(act.astype(BF16), wd_bf[...], preferred_element_type=F32) + bd_ref[0]
        y_ref[...] = y.astype(y_ref.dtype)


def _experts(blk_e, n_used, xs, wgu, bgu, wd, bd, bm):
    rows, d = xs.shape
    n_blocks = rows // bm
    row_map = lambda i, be, nu: (jnp.minimum(i, nu[0] - 1), 0)
    exp_map = lambda i, be, nu: (be[i], 0, 0)
    grid_spec = pltpu.PrefetchScalarGridSpec(
        num_scalar_prefetch=2,
        grid=(n_blocks,),
        in_specs=[pl.BlockSpec((bm, d), row_map),
                  pl.BlockSpec((1, d, 2 * D_EXPERT), exp_map),
                  pl.BlockSpec((1, 1, 2 * D_EXPERT), exp_map),
                  pl.BlockSpec((1, D_EXPERT, d), exp_map),
                  pl.BlockSpec((1, 1, d), exp_map)],
        out_specs=pl.BlockSpec((bm, d), row_map),
        scratch_shapes=[pltpu.VMEM((d, 2 * D_EXPERT), BF16),
                        pltpu.VMEM((D_EXPERT, d), BF16)])
    return pl.pallas_call(
        _experts_kernel,
        out_shape=jax.ShapeDtypeStruct((rows, d), BF16),
        grid_spec=grid_spec,
        compiler_params=_cparams(("arbitrary",)),
        name="expert_mlp",
    )(blk_e, n_used, xs, wgu, bgu, wd, bd)


def _final_kernel(x1_ref, yk_ref, route_ref, gt_ref, fg_ref, o_ref):
    route = route_ref[0]
    ffn = jnp.zeros(x1_ref.shape[1:], F32)
    for kk in range(TOP_K):
        wk = route[:, 2 * TOP_K + kk:2 * TOP_K + kk + 1]
        ffn = ffn + wk * yk_ref[kk, 0].astype(F32)
    x2 = x1_ref[0] + gt_ref[0] * ffn
    o_ref[0] = _rms(x2, fg_ref[...])


def _final(x1, yk, route, gt2, final_g, tb):
    b, s, d = x1.shape
    return pl.pallas_call(
        _final_kernel,
        out_shape=jax.ShapeDtypeStruct((b, s, d), F32),
        grid=(b, s // tb),
        in_specs=[pl.BlockSpec((1, tb, d), lambda i, j: (i, j, 0)),
                  pl.BlockSpec((TOP_K, 1, tb, d), lambda i, j: (0, i, j, 0)),
                  pl.BlockSpec((1, tb, LANES), lambda i, j: (i, j, 0)),
                  pl.BlockSpec((1, 1, d), lambda i, j: (i, 0, 0)),
                  pl.BlockSpec((1, d), lambda i, j: (0, 0))],
        out_specs=pl.BlockSpec((1, tb, d), lambda i, j: (i, j, 0)),
        compiler_params=_cparams(("parallel", "parallel")),
        name="combine_final_norm",
    )(x1, yk, route, gt2, final_g.reshape(1, d))


def _rot_cols(w):
    half = MLA_ROPE // 2
    return jnp.concatenate([-w[..., half:], w[..., :half]], axis=-1)


def _prep_weights(w_in, w_q_b, w_kv_b, conv_w, A_log, dt_bias):
    d = w_in.shape[0]
    cuts = [MLA_Q_LORA, MLA_KV_LORA, MLA_ROPE, GDN_HEADS * GDN_DK, GDN_HEADS * GDN_DK,
            GDN_HEADS * GDN_DV, GDN_HEADS * GDN_DV, GDN_HEADS, GDN_HEADS]
    offs = [0]
    for cw in cuts:
        offs.append(offs[-1] + cw)
    part = lambda i: w_in[:, offs[i]:offs[i + 1]]
    k_pe = part(2)
    misc = jnp.concatenate(
        [part(7), part(8), jnp.zeros((d, MLA_NOPE - 2 * GDN_HEADS), w_in.dtype), k_pe,
         _rot_cols(k_pe)], axis=1)
    heads = lambda w, n: w.reshape(w.shape[0], GDN_HEADS, n)
    slab = jnp.concatenate([heads(part(3), GDN_DK), heads(part(4), GDN_DK),
                            heads(part(5), GDN_DV), heads(part(6), GDN_DV)], axis=-1)
    w1 = jnp.concatenate([part(0), part(1), misc, slab.reshape(d, GDN_HEADS * GDN_SLAB)],
                         axis=1).astype(BF16)

    wq3 = w_q_b.reshape(MLA_Q_LORA, MLA_HEADS, MLA_QK)
    pe = wq3[..., MLA_NOPE:]
    wq = jnp.concatenate([wq3[..., :MLA_NOPE], pe, _rot_cols(pe)], axis=-1)
    wq = wq.reshape(MLA_Q_LORA, MLA_HEADS * HEAD_SLOT).astype(BF16)

    wkv3 = w_kv_b.reshape(MLA_KV_LORA, MLA_HEADS, MLA_NOPE + MLA_V)
    wk = jnp.concatenate([wkv3[..., :MLA_NOPE],
                          jnp.zeros((MLA_KV_LORA, MLA_HEADS, HEAD_SLOT - MLA_NOPE), w_kv_b.dtype)],
                         axis=-1).reshape(MLA_KV_LORA, MLA_HEADS * HEAD_SLOT)
    wv = wkv3[..., MLA_NOPE:].reshape(MLA_KV_LORA, MLA_HEADS * MLA_V)
    wkv = jnp.concatenate([wk, wv], axis=1).astype(BF16)

    nk = GDN_HEADS * GDN_DK
    cheads = lambda w, n: w.reshape(CONV_WIDTH, GDN_HEADS, n)
    conv_slab = jnp.concatenate(
        [cheads(conv_w[:, :nk], GDN_DK), cheads(conv_w[:, nk:2 * nk], GDN_DK),
         cheads(conv_w[:, 2 * nk:], GDN_DV), jnp.zeros((CONV_WIDTH, GDN_HEADS, GDN_DV), conv_w.dtype)],
        axis=-1).reshape(CONV_WIDTH, GDN_HEADS * GDN_SLAB)

    half = MLA_ROPE // 2
    inv_freq = ROPE_THETA ** (-jnp.arange(half, dtype=F32) / half)
    freq = jnp.concatenate([jnp.zeros((MLA_NOPE,), F32), inv_freq, inv_freq,
                            jnp.zeros((LANES - MLA_QK,), F32)]).reshape(1, LANES)
    padl = lambda a: jnp.concatenate([a.astype(F32), jnp.zeros((LANES - a.shape[0],), F32)]).reshape(1, LANES)
    return w1, wq, wkv, conv_slab, freq, padl(A_log), padl(dt_bias)


def _layer(x, mod, positions, w_in, q_norm_g, w_q_b, kv_norm_g, w_kv_b, mla_out_g, conv_w,
           A_log, dt_bias, gdn_norm_g, w_out, norm1_g, norm2_g, router_w, router_b, w_gate_up,
           b_gate_up, w_down, b_down):
    b, s, d = x.shape
    t = b * s
    sh1, sc1, gt1, sh2, sc2, gt2 = [m.reshape(b, 1, d) for m in jnp.split(mod, 6, axis=-1)]
    w1, wq, wkv, conv_slab, freq, alog, dtb = _prep_weights(w_in, w_q_b, w_kv_b, conv_w, A_log,
                                                            dt_bias)
    tb = min(256, s)
    q, k, v, slab, gb = _in_projection(x, positions, sc1, sh1, norm1_g, w1, q_norm_g, wq,
                                       kv_norm_g, wkv, freq, alog, dtb, tb)
    attn_o = _attention(q, k, v, min(512, s))
    gdn_o = _gdn(slab, gb, conv_slab, gdn_norm_g, min(256, s), 4)

    rw = jnp.concatenate([router_w, jnp.zeros((d, LANES - N_EXPERTS), router_w.dtype)], axis=1)
    rb = jnp.concatenate([router_b, jnp.zeros((LANES - N_EXPERTS,), router_b.dtype)]).reshape(1, LANES)
    x1, h2, route, counts = _out_projection(x, attn_o, gdn_o, gt1, sc2, sh2, mla_out_g,
                                            w_out.astype(BF16), norm2_g, rw, rb, tb)

    bm = 512
    route2 = route.reshape(t, LANES)
    idx = route2[:, :TOP_K].astype(jnp.int32)
    rank = route2[:, TOP_K:2 * TOP_K].astype(jnp.int32)
    cnt = counts[0, :N_EXPERTS].astype(jnp.int32)
    padded = ((cnt + bm - 1) // bm) * bm
    pend = jnp.cumsum(padded)
    pstart = pend - padded
    dest = pstart[idx] + rank
    n_blocks = (t * TOP_K + N_EXPERTS * (bm - 1) + bm - 1) // bm
    rows = n_blocks * bm
    n_used = (pend[-1] // bm).astype(jnp.int32).reshape(1)
    blk_start = jnp.arange(n_blocks, dtype=jnp.int32) * bm
    blk_e = jnp.minimum(jnp.searchsorted(pend, blk_start, side="right"), N_EXPERTS - 1)
    last_e = blk_e[jnp.maximum(n_used[0] - 1, 0)]
    blk_e = jnp.where(jnp.arange(n_blocks) < n_used[0], blk_e, last_e).astype(jnp.int32)

    tok_flat = jnp.arange(t * TOP_K, dtype=jnp.int32) // TOP_K
    row_tok = jnp.zeros((rows,), jnp.int32).at[dest.reshape(-1)].set(tok_flat)
    xs = h2.reshape(t, d)[row_tok]
    y_rows = _experts(blk_e, n_used, xs, w_gate_up, b_gate_up.reshape(N_EXPERTS, 1, -1),
                      w_down, b_down.reshape(N_EXPERTS, 1, -1), bm)
    yk = y_rows[dest.T.reshape(-1)].reshape(TOP_K, b, s, d)
    return x1, yk, route, gt2


def kernel(x, c, positions, ada_w, ada_b, norm1_g, w_in, q_norm_g, w_q_b, kv_norm_g, w_kv_b, mla_out_g, conv_w, A_log, dt_bias, gdn_norm_g, w_out, norm2_g, router_w, router_b, w_gate_up, b_gate_up, w_down, b_down, final_g):
    depth = ada_w.shape[0]
    assert depth == 1
    l = 0
    mod = _modulation(c, ada_w[l], ada_b[l])
    x1, yk, route, gt2 = _layer(x, mod, positions, w_in[l], q_norm_g[l], w_q_b[l], kv_norm_g[l],
                         w_kv_b[l], mla_out_g[l], conv_w[l], A_log[l], dt_bias[l], gdn_norm_g[l],
                         w_out[l], norm1_g[l], norm2_g[l], router_w[l], router_b[l],
                         w_gate_up[l], b_gate_up[l], w_down[l], b_down[l])
    return _final(x1, yk, route, gt2, final_g, min(256, x.shape[1]))
```

```python
import functools

import jax
import jax.numpy as jnp
from jax import lax
from jax.experimental import pallas as pl
from jax.experimental.pallas import tpu as pltpu

F32 = jnp.float32
BF16 = jnp.bfloat16

D_MODEL = 1024
EPS = 1e-6
MLA_HEADS = 8
MLA_NOPE = 64
MLA_ROPE = 32
MLA_V = 64
MLA_QK = MLA_NOPE + MLA_ROPE
MLA_Q_LORA = 384
MLA_KV_LORA = 256
ROPE_THETA = 10000.0
GDN_HEADS = 8
GDN_DK = 64
GDN_DV = 64
CONV_WIDTH = 4
CHUNK = 64
N_EXPERTS = 32
TOP_K = 4
D_EXPERT = D_MODEL
SWIGLU_ALPHA = 1.702
SWIGLU_LIMIT = 7.0

LANES = 128
HEAD_SLOT = 128
GDN_SLAB = 256
MISC_W = 128
PROJ_W = MLA_Q_LORA + MLA_KV_LORA + MISC_W + GDN_HEADS * GDN_SLAB
VMEM_LIMIT = 56 * 1024 * 1024
ATT_STRIP = 32
LOG2E = 1.4426950408889634


def _cparams(sem):
    return pltpu.CompilerParams(dimension_semantics=sem, vmem_limit_bytes=VMEM_LIMIT)


def _rms(x, g):
    return x * lax.rsqrt(jnp.mean(x * x, axis=-1, keepdims=True) + EPS) * g


def _sigmoid(x):
    return 1.0 / (1.0 + jnp.exp(-x))


def _silu(x):
    return x * _sigmoid(x)


def _mod_kernel(c_ref, w_ref, b_ref, o_ref):
    c = c_ref[...]
    o_ref[...] = jnp.dot(_silu(c), w_ref[...], preferred_element_type=F32,
                         precision=lax.Precision.HIGHEST) + b_ref[...]


def _modulation(c, ada_w, ada_b):
    b, d = c.shape
    n = ada_w.shape[1]
    return pl.pallas_call(
        _mod_kernel,
        out_shape=jax.ShapeDtypeStruct((b, n), F32),
        grid=(n // d,),
        in_specs=[pl.BlockSpec((b, d), lambda j: (0, 0)),
                  pl.BlockSpec((d, d), lambda j: (0, j)),
                  pl.BlockSpec((1, d), lambda j: (0, j))],
        out_specs=pl.BlockSpec((b, d), lambda j: (0, j)),
        compiler_params=_cparams(("arbitrary",)),
        name="adaln_mod",
    )(c, ada_w, ada_b.reshape(1, n))


def _inproj_kernel(x_ref, pos_ref, sc_ref, sh_ref, g1_ref, w1_ref, qg_ref, wq_ref, kvg_ref,
                   wkv_ref, freq_ref, alog_ref, dtb_ref,
                   q_ref, k_ref, v_ref, gdn_ref, gb_ref):
    x = x_ref[0]
    h = _rms(x, g1_ref[...]) * (1.0 + sc_ref[0]) + sh_ref[0]
    proj = jnp.dot(h.astype(BF16), w1_ref[...], preferred_element_type=F32)

    tb = x.shape[0]
    lane = lax.broadcasted_iota(jnp.int32, (tb, LANES), 1)
    ang = pos_ref[0].astype(F32) * freq_ref[...]
    cosv = jnp.cos(ang)
    sinv = jnp.sin(ang)
    in_rope = (lane >= MLA_NOPE) & (lane < MLA_QK)

    scale = (MLA_QK ** -0.5) * LOG2E
    qn = _rms(proj[:, :MLA_Q_LORA], qg_ref[...])
    qa = jnp.dot(qn.astype(BF16), wq_ref[...], preferred_element_type=F32)
    cq = jnp.where(lane < MLA_NOPE, scale, jnp.where(in_rope, cosv * scale, 0.0))
    sq = sinv * scale
    cq_t = jnp.concatenate([cq] * MLA_HEADS, axis=1)
    sq_t = jnp.concatenate([sq] * MLA_HEADS, axis=1)
    width = MLA_HEADS * HEAD_SLOT
    q = qa * cq_t + pltpu.roll(qa, width - MLA_ROPE, axis=1) * sq_t
    q_ref[0] = q.astype(BF16)

    kvn = _rms(proj[:, MLA_Q_LORA:MLA_Q_LORA + MLA_KV_LORA], kvg_ref[...])
    kva = jnp.dot(kvn.astype(BF16), wkv_ref[...], preferred_element_type=F32)
    misc = proj[:, MLA_Q_LORA + MLA_KV_LORA:MLA_Q_LORA + MLA_KV_LORA + MISC_W]
    kp = misc * jnp.where(in_rope, cosv, 0.0) + pltpu.roll(misc, MISC_W - MLA_ROPE, axis=1) * sinv
    k = kva[:, :width] + jnp.concatenate([kp] * MLA_HEADS, axis=1)
    k_ref[0] = k.astype(BF16)
    v_ref[0] = kva[:, width:].astype(BF16)

    z = misc + dtb_ref[...]
    softplus = jnp.maximum(z, 0.0) + jnp.log(1.0 + jnp.exp(-jnp.abs(z)))
    g = -jnp.exp(alog_ref[...]) * softplus
    gb_ref[0] = jnp.where(lane < GDN_HEADS, g, _sigmoid(misc))

    gdn_ref[0] = proj[:, MLA_Q_LORA + MLA_KV_LORA + MISC_W:].astype(BF16)


def _in_projection(x, positions, sc1, sh1, norm1_g, w1, q_norm_g, wq, kv_norm_g, wkv,
                   freq, alog, dtb, tb):
    b, s, d = x.shape
    hw = MLA_HEADS * HEAD_SLOT
    const = lambda shape: pl.BlockSpec(shape, lambda i, j: (0,) * len(shape))
    tok = lambda w: pl.BlockSpec((1, tb, w), lambda i, j: (i, j, 0))
    per_b = pl.BlockSpec((1, 1, d), lambda i, j: (i, 0, 0))
    return pl.pallas_call(
        _inproj_kernel,
        out_shape=(jax.ShapeDtypeStruct((b, s, hw), BF16),
                   jax.ShapeDtypeStruct((b, s, hw), BF16),
                   jax.ShapeDtypeStruct((b, s, MLA_HEADS * MLA_V), BF16),
                   jax.ShapeDtypeStruct((b, s, GDN_HEADS * GDN_SLAB), BF16),
                   jax.ShapeDtypeStruct((b, s, MISC_W), F32)),
        grid=(b, s // tb),
        in_specs=[tok(d), tok(1), per_b, per_b, const((1, d)), const(w1.shape),
                  const((1, MLA_Q_LORA)), const(wq.shape), const((1, MLA_KV_LORA)),
                  const(wkv.shape), const((1, LANES)), const((1, LANES)), const((1, LANES))],
        out_specs=(tok(hw), tok(hw), tok(MLA_HEADS * MLA_V), tok(GDN_HEADS * GDN_SLAB),
                   tok(MISC_W)),
        compiler_params=_cparams(("parallel", "parallel")),
        name="in_projection",
    )(x, positions.reshape(b, s, 1), sc1, sh1, norm1_g.reshape(1, d), w1,
      q_norm_g.reshape(1, -1), wq, kv_norm_g.reshape(1, -1), wkv, freq, alog, dtb)


def _attn_kernel(q_ref, k_ref, v_ref, o_ref, s_ref, p_ref, m_ref, l_ref, acc_ref, *, tq):
    qi = pl.program_id(2)
    m_ref[...] = jnp.full(m_ref.shape, -jnp.inf, F32)
    l_ref[...] = jnp.zeros(l_ref.shape, F32)
    acc_ref[...] = jnp.zeros(acc_ref.shape, F32)

    def step(off, masked):
        vj = v_ref[0, pl.ds(off, tq), :]
        for h in range(2):
            kj = k_ref[0, pl.ds(off, tq), h * HEAD_SLOT:(h + 1) * HEAD_SLOT]
            s_ref[h] = lax.dot_general(q_ref[0, :, h * HEAD_SLOT:(h + 1) * HEAD_SLOT], kj,
                                       (((1,), (1,)), ((), ())), preferred_element_type=F32)
        n_strips = tq // ATT_STRIP

        def strip(h, r):
            rows = slice(r * ATT_STRIP, (r + 1) * ATT_STRIP)
            sc = s_ref[h, rows, :]
            if masked:
                rid = lax.broadcasted_iota(jnp.int32, (ATT_STRIP, tq), 0) + r * ATT_STRIP
                cid = lax.broadcasted_iota(jnp.int32, (ATT_STRIP, tq), 1)
                sc = jnp.where(cid <= rid, sc, -jnp.inf)
            return rows, sc

        rep = lambda col: jnp.broadcast_to(col, (col.shape[0], LANES))
        m_new = []
        for h in range(2):
            mx = jnp.concatenate([rep(jnp.max(strip(h, r)[1], axis=-1, keepdims=True))
                                  for r in range(n_strips)], axis=0)
            m_new.append(jnp.maximum(m_ref[h], mx))
        for h in range(2):
            sums = []
            for r in range(n_strips):
                rows, sc = strip(h, r)
                p = jnp.exp2(sc - jnp.concatenate([m_new[h][rows]] * (tq // LANES), axis=1))
                p_ref[h, rows, :] = p.astype(BF16)
                sums.append(rep(jnp.sum(p, axis=-1, keepdims=True)))
            alpha = jnp.exp2(m_ref[h] - m_new[h])
            l_ref[h] = alpha * l_ref[h] + jnp.concatenate(sums, axis=0)
            m_ref[h] = m_new[h]
            acc_ref[h] = alpha * acc_ref[h] + jnp.dot(p_ref[h], vj, preferred_element_type=F32)

    def body(j, carry):
        step(pl.multiple_of(j * tq, tq), False)
        return carry

    lax.fori_loop(0, qi, body, 0)
    step(pl.multiple_of(qi * tq, tq), True)
    lane = lax.broadcasted_iota(jnp.int32, (tq, 2 * MLA_V), 1)
    o0 = acc_ref[0] / l_ref[0]
    o1 = acc_ref[1] / l_ref[1]
    o_ref[0] = jnp.where(lane < MLA_V, o0, o1).astype(o_ref.dtype)


def _attention(q, k, v, tq):
    b, s, _ = q.shape
    pairs = MLA_HEADS // 2
    return pl.pallas_call(
        functools.partial(_attn_kernel, tq=tq),
        out_shape=jax.ShapeDtypeStruct((b, s, MLA_HEADS * MLA_V), BF16),
        grid=(b, pairs, s // tq),
        in_specs=[pl.BlockSpec((1, tq, 2 * HEAD_SLOT), lambda i, p, j: (i, j, p)),
                  pl.BlockSpec((1, s, 2 * HEAD_SLOT), lambda i, p, j: (i, 0, p)),
                  pl.BlockSpec((1, s, 2 * MLA_V), lambda i, p, j: (i, 0, p))],
        out_specs=pl.BlockSpec((1, tq, 2 * MLA_V), lambda i, p, j: (i, j, p)),
        scratch_shapes=[pltpu.VMEM((2, tq, tq), F32), pltpu.VMEM((2, tq, tq), BF16),
                        pltpu.VMEM((2, tq, LANES), F32), pltpu.VMEM((2, tq, LANES), F32),
                        pltpu.VMEM((2, tq, 2 * MLA_V), F32)],
        compiler_params=_cparams(("parallel", "parallel", "arbitrary")),
        name="mla_attention",
    )(q, k, v)


def _bdot(a, b):
    return jnp.dot(a.astype(BF16), b.astype(BF16), preferred_element_type=F32)


def _gdn_kernel(slab_ref, gb_ref, cw_ref, ng_ref, o_ref, xbuf_ref, state_ref, *, ts, hg):
    si = pl.program_id(2)
    group = pl.program_id(1)
    width = hg * GDN_SLAB
    pad = 8

    @pl.when(si == 0)
    def _():
        xbuf_ref[0:pad, :] = jnp.zeros((pad, width), F32)
        state_ref[...] = jnp.zeros(state_ref.shape, F32)

    xs = slab_ref[0].astype(F32)
    xbuf_ref[pad:pad + ts, :] = xs
    conv = jnp.zeros((ts, width), F32)
    for j in range(CONV_WIDTH):
        start = pad - (CONV_WIDTH - 1) + j
        conv = conv + cw_ref[j:j + 1, :] * xbuf_ref[start:start + ts, :]
    xbuf_ref[0:pad, :] = xs[ts - pad:ts, :]
    act = _silu(conv)

    gb = gb_ref[0]
    lane = lax.broadcasted_iota(jnp.int32, (ts, LANES), 1)
    n_chunks = ts // CHUNK
    row = lax.broadcasted_iota(jnp.int32, (CHUNK, CHUNK), 0)
    col = lax.broadcasted_iota(jnp.int32, (CHUNK, CHUNK), 1)
    tri_incl = col <= row
    tri_strict = col < row
    eye = col == row
    crow = lax.broadcasted_iota(jnp.int32, (ts, 1), 0) % CHUNK

    heads = []
    for hh in range(hg):
        head = hg * group + hh
        base = hh * GDN_SLAB
        q_all = act[:, base:base + GDN_DK]
        k_all = act[:, base + GDN_DK:base + 2 * GDN_DK]
        v_all = act[:, base + 2 * GDN_DK:base + 2 * GDN_DK + GDN_DV]
        z_all = xs[:, base + 3 * GDN_DK:base + 3 * GDN_DK + GDN_DV]
        q_all = q_all * lax.rsqrt(jnp.sum(q_all * q_all, axis=-1, keepdims=True) + EPS)
        k_all = k_all * lax.rsqrt(jnp.sum(k_all * k_all, axis=-1, keepdims=True) + EPS)
        q_all = q_all * (GDN_DK ** -0.5)
        g_all = jnp.sum(jnp.where(lane == head, gb, 0.0), axis=-1, keepdims=True)
        b_all = jnp.sum(jnp.where(lane == head + GDN_HEADS, gb, 0.0), axis=-1, keepdims=True)

        gc_all = g_all
        shift = 1
        while shift < CHUNK:
            rolled = pltpu.roll(gc_all, shift, axis=0)
            gc_all = gc_all + jnp.where(crow >= shift, rolled, 0.0)
            shift *= 2
        heads.append((q_all, k_all, v_all, z_all, b_all, gc_all))

    units = [(hh, c) for c in range(n_chunks) for hh in range(hg)]
    a_low, qk, y0, qd, kt, cd = {}, {}, {}, {}, {}, {}
    for u in units:
        hh, c = u
        q_all, k_all, v_all, _, b_all, gc_all = heads[hh]
        sl = slice(c * CHUNK, (c + 1) * CHUNK)
        qc, kc, vc, bc, gcum = q_all[sl], k_all[sl], v_all[sl], b_all[sl], gc_all[sl]
        gcol = jnp.broadcast_to(gcum, (CHUNK, CHUNK))
        grow = jnp.sum(jnp.where(eye, gcol, 0.0), axis=0, keepdims=True)
        diff = gcol - grow
        decay = jnp.where(tri_incl, jnp.exp(jnp.where(tri_incl, diff, 0.0)), 0.0)
        k_beta = kc * bc
        kq = jnp.concatenate([k_beta, qc], axis=0).astype(BF16)
        kk = lax.dot_general(kq, kc.astype(BF16), (((1,), (1,)), ((), ())),
                             preferred_element_type=F32)
        a_low[u] = jnp.where(tri_strict, kk[:CHUNK] * decay, 0.0)
        qk[u] = jnp.where(tri_incl, kk[CHUNK:] * decay, 0.0)
        eg = jnp.exp(gcum)
        y0[u] = jnp.concatenate([vc * bc, k_beta * eg], axis=1)
        qd[u] = qc * eg
        glast = gcum[CHUNK - 1:CHUNK, :]
        kt[u] = kc * jnp.exp(glast - gcum)
        cd[u] = jnp.exp(glast)

    blk = 2
    tinv = {u: jnp.where(eye, 1.0, 0.0) - jnp.where((row // blk) == (col // blk), a_low[u], 0.0)
            for u in units}
    while blk < CHUNK:
        in_big = (row // (2 * blk)) == (col // (2 * blk))
        off_mask = in_big & ((row // blk) != (col // blk))
        left = {u: _bdot(tinv[u], jnp.where(off_mask, a_low[u], 0.0)) for u in units}
        tinv = {u: tinv[u] - _bdot(left[u], tinv[u]) for u in units}
        blk *= 2

    y = {u: _bdot(tinv[u], y0[u]) for u in units}
    ky = {u: lax.dot_general(kt[u].astype(BF16), y[u].astype(BF16), (((0,), (0,)), ((), ())),
                             preferred_element_type=F32) for u in units}
    qy = {u: _bdot(qk[u], y[u]) for u in units}

    outs = [[] for _ in range(hg)]
    states = [state_ref[hh] for hh in range(hg)]
    for c in range(n_chunks):
        for hh in range(hg):
            u = (hh, c)
            sb = states[hh].astype(BF16)
            r_mat = qd[u] - qy[u][:, GDN_DV:]
            outs[hh].append(jnp.dot(r_mat.astype(BF16), sb, preferred_element_type=F32)
                            + qy[u][:, :GDN_DV])
            states[hh] = (states[hh] * cd[u]
                          - jnp.dot(ky[u][:, GDN_DV:].astype(BF16), sb,
                                    preferred_element_type=F32)
                          + ky[u][:, :GDN_DV])
    finals = []
    for hh in range(hg):
        state_ref[hh] = states[hh]
        o_all = jnp.concatenate(outs[hh], axis=0)
        finals.append(_rms(o_all, ng_ref[...]) * _silu(heads[hh][3]))
    o_ref[0] = jnp.concatenate(finals, axis=1).astype(o_ref.dtype)


def _gdn(slab, gb, conv_slab, gdn_norm_g, ts, hg):
    b, s, _ = slab.shape
    groups = GDN_HEADS // hg
    return pl.pallas_call(
        functools.partial(_gdn_kernel, ts=ts, hg=hg),
        out_shape=jax.ShapeDtypeStruct((b, s, GDN_HEADS * GDN_DV), BF16),
        grid=(b, groups, s // ts),
        in_specs=[pl.BlockSpec((1, ts, hg * GDN_SLAB), lambda i, p, j: (i, j, p)),
                  pl.BlockSpec((1, ts, MISC_W), lambda i, p, j: (i, j, 0)),
                  pl.BlockSpec((CONV_WIDTH, hg * GDN_SLAB), lambda i, p, j: (0, p)),
                  pl.BlockSpec((1, GDN_DV), lambda i, p, j: (0, 0))],
        out_specs=pl.BlockSpec((1, ts, hg * GDN_DV), lambda i, p, j: (i, j, p)),
        scratch_shapes=[pltpu.VMEM((ts + 8, hg * GDN_SLAB), F32),
                        pltpu.VMEM((hg, GDN_DK, GDN_DV), F32)],
        compiler_params=_cparams(("parallel", "parallel", "arbitrary")),
        name="gated_deltanet",
    )(slab, gb, conv_slab, gdn_norm_g.reshape(1, GDN_DV))


def _outproj_kernel(x_ref, ao_ref, go_ref, gt_ref, sc_ref, sh_ref, mg_ref, wo_ref, g2_ref,
                    rw_ref, rb_ref, x1_ref, h2_ref, route_ref, cnt_ref, carry_ref, *, tb):
    first = (pl.program_id(0) == 0) & (pl.program_id(1) == 0)

    @pl.when(first)
    def _():
        carry_ref[...] = jnp.zeros(carry_ref.shape, F32)

    mla = _rms(ao_ref[0].astype(F32), mg_ref[...])
    cat = jnp.concatenate([mla.astype(BF16), go_ref[0]], axis=1)
    mix = jnp.dot(cat, wo_ref[...], preferred_element_type=F32)
    x1 = x_ref[0] + gt_ref[0] * mix
    x1_ref[0] = x1
    h2 = _rms(x1, g2_ref[...]) * (1.0 + sc_ref[0]) + sh_ref[0]
    h2_ref[0] = h2

    logits = jnp.dot(h2, rw_ref[...], preferred_element_type=F32,
                     precision=lax.Precision.HIGHEST) + rb_ref[...]
    lane = lax.broadcasted_iota(jnp.int32, (tb, LANES), 1)
    work = jnp.where(lane < N_EXPERTS, logits, -jnp.inf)
    vals, idxs = [], []
    onehot = jnp.zeros((tb, LANES), F32)
    for _ in range(TOP_K):
        mx = jnp.max(work, axis=-1, keepdims=True)
        ix = jnp.min(jnp.where(work == mx, lane, LANES), axis=-1, keepdims=True)
        sel = lane == ix
        onehot = jnp.where(sel, 1.0, onehot)
        work = jnp.where(sel, -jnp.inf, work)
        vals.append(mx)
        idxs.append(ix)
    exps = [jnp.exp(v - vals[0]) for v in vals]
    den = exps[0] + exps[1] + exps[2] + exps[3]

    r = lax.broadcasted_iota(jnp.int32, (tb, tb), 0)
    c = lax.broadcasted_iota(jnp.int32, (tb, tb), 1)
    tri = jnp.where(c < r, 1.0, 0.0).astype(BF16)
    before = jnp.dot(tri, onehot.astype(BF16), preferred_element_type=F32) + carry_ref[...]
    route = jnp.zeros((tb, LANES), F32)
    for kk in range(TOP_K):
        rank = jnp.sum(jnp.where(lane == idxs[kk], before, 0.0), axis=-1, keepdims=True)
        route = jnp.where(lane == kk, idxs[kk].astype(F32), route)
        route = jnp.where(lane == TOP_K + kk, rank, route)
        route = jnp.where(lane == 2 * TOP_K + kk, exps[kk] / den, route)
    route_ref[0] = route
    total = carry_ref[...] + jnp.sum(onehot, axis=0, keepdims=True)
    carry_ref[...] = total
    cnt_ref[...] = total


def _out_projection(x, attn_o, gdn_o, gt1, sc2, sh2, mla_out_g, w_out, norm2_g, rw, rb, tb):
    b, s, d = x.shape
    const = lambda shape: pl.BlockSpec(shape, lambda i, j: (0,) * len(shape))
    tok = lambda w: pl.BlockSpec((1, tb, w), lambda i, j: (i, j, 0))
    per_b = pl.BlockSpec((1, 1, d), lambda i, j: (i, 0, 0))
    half = attn_o.shape[-1]
    return pl.pallas_call(
        functools.partial(_outproj_kernel, tb=tb),
        out_shape=(jax.ShapeDtypeStruct((b, s, d), F32),
                   jax.ShapeDtypeStruct((b, s, d), F32),
                   jax.ShapeDtypeStruct((b, s, LANES), F32),
                   jax.ShapeDtypeStruct((1, LANES), F32)),
        grid=(b, s // tb),
        in_specs=[tok(d), tok(half), tok(half), per_b, per_b, per_b, const((1, half)),
                  const(w_out.shape), const((1, d)), const(rw.shape), const((1, LANES))],
        out_specs=(tok(d), tok(d), tok(LANES), const((1, LANES))),
        scratch_shapes=[pltpu.VMEM((1, LANES), F32)],
        compiler_params=_cparams(("arbitrary", "arbitrary")),
        name="out_projection_router",
    )(x, attn_o, gdn_o, gt1, sc2, sh2, mla_out_g.reshape(1, half), w_out,
      norm2_g.reshape(1, d), rw, rb)


def _experts_kernel(be_ref, nu_ref, xs_ref, wgu_ref, bgu_ref, wd_ref, bd_ref, y_ref,
                    wgu_bf, wd_bf):
    i = pl.program_id(0)
    new_expert = (i == 0) | (be_ref[i] != be_ref[jnp.maximum(i - 1, 0)])

    @pl.when(new_expert)
    def _():
        wgu_bf[...] = wgu_ref[0].astype(BF16)
        wd_bf[...] = wd_ref[0].astype(BF16)

    @pl.when(i < nu_ref[0])
    def _():
        gu = jnp.dot(xs_ref[...].astype(BF16), wgu_bf[...], preferred_element_type=F32) + bgu_ref[0]
        gate = jnp.minimum(gu[:, :D_EXPERT], SWIGLU_LIMIT)
        up = jnp.clip(gu[:, D_EXPERT:], -SWIGLU_LIMIT, SWIGLU_LIMIT)
        act = (up + 1.0) * (gate * _sigmoid(SWIGLU_ALPHA * gate))
        y = jnp.dot(act.astype(BF16), wd_bf[...], preferred_element_type=F32) + bd_ref[0]
        y_ref[...] = y.astype(y_ref.dtype)


def _experts(blk_e, n_used, xs, wgu, bgu, wd, bd, bm):
    rows, d = xs.shape
    n_blocks = rows // bm
    row_map = lambda i, be, nu: (jnp.maximum(jnp.minimum(i, nu[0] - 1), 0), 0)
    exp_map = lambda i, be, nu: (be[i], 0, 0)
    grid_spec = pltpu.PrefetchScalarGridSpec(
        num_scalar_prefetch=2,
        grid=(n_blocks,),
        in_specs=[pl.BlockSpec((bm, d), row_map),
                  pl.BlockSpec((1, d, 2 * D_EXPERT), exp_map),
                  pl.BlockSpec((1, 1, 2 * D_EXPERT), exp_map),
                  pl.BlockSpec((1, D_EXPERT, d), exp_map),
                  pl.BlockSpec((1, 1, d), exp_map)],
        out_specs=pl.BlockSpec((bm, d), row_map),
        scratch_shapes=[pltpu.VMEM((d, 2 * D_EXPERT), BF16),
                        pltpu.VMEM((D_EXPERT, d), BF16)])
    return pl.pallas_call(
        _experts_kernel,
        out_shape=jax.ShapeDtypeStruct((rows, d), F32),
        grid_spec=grid_spec,
        compiler_params=_cparams(("arbitrary",)),
        name="expert_mlp",
    )(blk_e, n_used, xs, wgu, bgu, wd, bd)


def _final_kernel(x1_ref, yk_ref, route_ref, gt_ref, fg_ref, o_ref):
    route = route_ref[0]
    ffn = jnp.zeros(x1_ref.shape[1:], F32)
    for kk in range(TOP_K):
        wk = route[:, 2 * TOP_K + kk:2 * TOP_K + kk + 1]
        ffn = ffn + wk * yk_ref[kk, 0].astype(F32)
    x2 = x1_ref[0] + gt_ref[0] * ffn
    o_ref[0] = _rms(x2, fg_ref[...])


def _final(x1, yk, route, gt2, final_g, tb):
    b, s, d = x1.shape
    return pl.pallas_call(
        _final_kernel,
        out_shape=jax.ShapeDtypeStruct((b, s, d), F32),
        grid=(b, s // tb),
        in_specs=[pl.BlockSpec((1, tb, d), lambda i, j: (i, j, 0)),
                  pl.BlockSpec((TOP_K, 1, tb, d), lambda i, j: (0, i, j, 0)),
                  pl.BlockSpec((1, tb, LANES), lambda i, j: (i, j, 0)),
                  pl.BlockSpec((1, 1, d), lambda i, j: (i, 0, 0)),
                  pl.BlockSpec((1, d), lambda i, j: (0, 0))],
        out_specs=pl.BlockSpec((1, tb, d), lambda i, j: (i, j, 0)),
        compiler_params=_cparams(("parallel", "parallel")),
        name="combine_final_norm",
    )(x1, yk, route, gt2, final_g.reshape(1, d))


def _rot_cols(w):
    half = MLA_ROPE // 2
    return jnp.concatenate([-w[..., half:], w[..., :half]], axis=-1)


def _prep_weights(w_in, w_q_b, w_kv_b, conv_w, A_log, dt_bias):
    d = w_in.shape[0]
    cuts = [MLA_Q_LORA, MLA_KV_LORA, MLA_ROPE, GDN_HEADS * GDN_DK, GDN_HEADS * GDN_DK,
            GDN_HEADS * GDN_DV, GDN_HEADS * GDN_DV, GDN_HEADS, GDN_HEADS]
    offs = [0]
    for cw in cuts:
        offs.append(offs[-1] + cw)
    part = lambda i: w_in[:, offs[i]:offs[i + 1]]
    k_pe = part(2)
    misc = jnp.concatenate(
        [part(7), part(8), jnp.zeros((d, MLA_NOPE - 2 * GDN_HEADS), w_in.dtype), k_pe,
         _rot_cols(k_pe)], axis=1)
    heads = lambda w, n: w.reshape(w.shape[0], GDN_HEADS, n)
    slab = jnp.concatenate([heads(part(3), GDN_DK), heads(part(4), GDN_DK),
                            heads(part(5), GDN_DV), heads(part(6), GDN_DV)], axis=-1)
    w1 = jnp.concatenate([part(0), part(1), misc, slab.reshape(d, GDN_HEADS * GDN_SLAB)],
                         axis=1).astype(BF16)

    wq3 = w_q_b.reshape(MLA_Q_LORA, MLA_HEADS, MLA_QK)
    pe = wq3[..., MLA_NOPE:]
    wq = jnp.concatenate([wq3[..., :MLA_NOPE], pe, _rot_cols(pe)], axis=-1)
    wq = wq.reshape(MLA_Q_LORA, MLA_HEADS * HEAD_SLOT).astype(BF16)

    wkv3 = w_kv_b.reshape(MLA_KV_LORA, MLA_HEADS, MLA_NOPE + MLA_V)
    wk = jnp.concatenate([wkv3[..., :MLA_NOPE],
                          jnp.zeros((MLA_KV_LORA, MLA_HEADS, HEAD_SLOT - MLA_NOPE), w_kv_b.dtype)],
                         axis=-1).reshape(MLA_KV_LORA, MLA_HEADS * HEAD_SLOT)
    wv = wkv3[..., MLA_NOPE:].reshape(MLA_KV_LORA, MLA_HEADS * MLA_V)
    wkv = jnp.concatenate([wk, wv], axis=1).astype(BF16)

    nk = GDN_HEADS * GDN_DK
    cheads = lambda w, n: w.reshape(CONV_WIDTH, GDN_HEADS, n)
    conv_slab = jnp.concatenate(
        [cheads(conv_w[:, :nk], GDN_DK), cheads(conv_w[:, nk:2 * nk], GDN_DK),
         cheads(conv_w[:, 2 * nk:], GDN_DV), jnp.zeros((CONV_WIDTH, GDN_HEADS, GDN_DV), conv_w.dtype)],
        axis=-1).reshape(CONV_WIDTH, GDN_HEADS * GDN_SLAB)

    half = MLA_ROPE // 2
    inv_freq = ROPE_THETA ** (-jnp.arange(half, dtype=F32) / half)
    freq = jnp.concatenate([jnp.zeros((MLA_NOPE,), F32), inv_freq, inv_freq,
                            jnp.zeros((LANES - MLA_QK,), F32)]).reshape(1, LANES)
    padl = lambda a: jnp.concatenate([a.astype(F32), jnp.zeros((LANES - a.shape[0],), F32)]).reshape(1, LANES)
    return w1, wq, wkv, conv_slab, freq, padl(A_log), padl(dt_bias)


def _layer(x, mod, positions, w_in, q_norm_g, w_q_b, kv_norm_g, w_kv_b, mla_out_g, conv_w,
           A_log, dt_bias, gdn_norm_g, w_out, norm1_g, norm2_g, router_w, router_b, w_gate_up,
           b_gate_up, w_down, b_down):
    b, s, d = x.shape
    t = b * s
    sh1, sc1, gt1, sh2, sc2, gt2 = [m.reshape(b, 1, d) for m in jnp.split(mod, 6, axis=-1)]
    w1, wq, wkv, conv_slab, freq, alog, dtb = _prep_weights(w_in, w_q_b, w_kv_b, conv_w, A_log,
                                                            dt_bias)
    tb = min(256, s)
    q, k, v, slab, gb = _in_projection(x, positions, sc1, sh1, norm1_g, w1, q_norm_g, wq,
                                       kv_norm_g, wkv, freq, alog, dtb, tb)
    attn_o = _attention(q, k, v, min(512, s))
    gdn_o = _gdn(slab, gb, conv_slab, gdn_norm_g, min(256, s), 4)

    rw = jnp.concatenate([router_w, jnp.zeros((d, LANES - N_EXPERTS), router_w.dtype)], axis=1)
    rb = jnp.concatenate([router_b, jnp.zeros((LANES - N_EXPERTS,), router_b.dtype)]).reshape(1, LANES)
    x1, h2, route, counts = _out_projection(x, attn_o, gdn_o, gt1, sc2, sh2, mla_out_g,
                                            w_out.astype(BF16), norm2_g, rw, rb, tb)

    bm = 512
    route2 = route.reshape(t, LANES)
    idx = route2[:, :TOP_K].astype(jnp.int32)
    rank = route2[:, TOP_K:2 * TOP_K].astype(jnp.int32)
    cnt = counts[0, :N_EXPERTS].astype(jnp.int32)
    padded = ((cnt + bm - 1) // bm) * bm
    pend = jnp.cumsum(padded)
    pstart = pend - padded
    dest = pstart[idx] + rank
    n_blocks = (t * TOP_K + N_EXPERTS * (bm - 1) + bm - 1) // bm
    rows = n_blocks * bm
    n_used = (pend[-1] // bm).astype(jnp.int32).reshape(1)
    blk_start = jnp.arange(n_blocks, dtype=jnp.int32) * bm
    blk_e = jnp.minimum(jnp.sum(blk_start[:, None] >= pend[None, :], axis=1), N_EXPERTS - 1)
    last_e = blk_e[jnp.maximum(n_used[0] - 1, 0)]
    blk_e = jnp.where(jnp.arange(n_blocks) < n_used[0], blk_e, last_e).astype(jnp.int32)

    tok_flat = jnp.arange(t * TOP_K, dtype=jnp.int32) // TOP_K
    row_tok = jnp.zeros((rows,), jnp.int32).at[dest.reshape(-1)].set(tok_flat)
    xs = h2.reshape(t, d)[row_tok]
    y_rows = _experts(blk_e, n_used, xs, w_gate_up, b_gate_up.reshape(N_EXPERTS, 1, -1),
                      w_down, b_down.reshape(N_EXPERTS, 1, -1), bm)
    yk = y_rows[dest.T.reshape(-1)].reshape(TOP_K, b, s, d)
    return x1, yk, route, gt2


def kernel(x, c, positions, ada_w, ada_b, norm1_g, w_in, q_norm_g, w_q_b, kv_norm_g, w_kv_b, mla_out_g, conv_w, A_log, dt_bias, gdn_norm_g, w_out, norm2_g, router_w, router_b, w_gate_up, b_gate_up, w_down, b_down, final_g):
    depth = ada_w.shape[0]
    assert depth == 1
    l = 0
    mod = _modulation(c, ada_w[l], ada_b[l])
    x1, yk, route, gt2 = _layer(x, mod, positions, w_in[l], q_norm_g[l], w_q_b[l], kv_norm_g[l],
                         w_kv_b[l], mla_out_g[l], conv_w[l], A_log[l], dt_bias[l], gdn_norm_g[l],
                         w_out[l], norm1_g[l], norm2_g[l], router_w[l], router_b[l],
                         w_gate_up[l], b_gate_up[l], w_down[l], b_down[l])
    return _final(x1, yk, route, gt2, final_g, min(256, x.shape[1]))
```

```python
import functools

import jax
import jax.numpy as jnp
from jax import lax
from jax.experimental import pallas as pl
from jax.experimental.pallas import tpu as pltpu
from jax.experimental.pallas import tpu_sc as plsc

F32 = jnp.float32
BF16 = jnp.bfloat16

D_MODEL = 1024
EPS = 1e-6
MLA_HEADS = 8
MLA_NOPE = 64
MLA_ROPE = 32
MLA_V = 64
MLA_QK = MLA_NOPE + MLA_ROPE
MLA_Q_LORA = 384
MLA_KV_LORA = 256
ROPE_THETA = 10000.0
GDN_HEADS = 8
GDN_DK = 64
GDN_DV = 64
CONV_WIDTH = 4
CHUNK = 64
N_EXPERTS = 32
TOP_K = 4
D_EXPERT = D_MODEL
SWIGLU_ALPHA = 1.702
SWIGLU_LIMIT = 7.0

LANES = 128
HEAD_SLOT = 128
GDN_SLAB = 256
MISC_W = 128
PROJ_W = MLA_Q_LORA + MLA_KV_LORA + MISC_W + GDN_HEADS * GDN_SLAB
VMEM_LIMIT = 56 * 1024 * 1024
SC_CORES = 2
SC_SUBCORES = 16
SC_WORKERS = SC_CORES * SC_SUBCORES
SC_CHUNK = 64
ATT_STRIP = 32
LOG2E = 1.4426950408889634


def _cparams(sem):
    return pltpu.CompilerParams(dimension_semantics=sem, vmem_limit_bytes=VMEM_LIMIT)


def _rms(x, g):
    return x * lax.rsqrt(jnp.mean(x * x, axis=-1, keepdims=True) + EPS) * g


def _sigmoid(x):
    return 1.0 / (1.0 + jnp.exp(-x))


def _silu(x):
    return x * _sigmoid(x)


def _mod_kernel(c_ref, w_ref, b_ref, o_ref):
    c = c_ref[...]
    o_ref[...] = jnp.dot(_silu(c), w_ref[...], preferred_element_type=F32,
                         precision=lax.Precision.HIGHEST) + b_ref[...]


def _modulation(c, ada_w, ada_b):
    b, d = c.shape
    n = ada_w.shape[1]
    return pl.pallas_call(
        _mod_kernel,
        out_shape=jax.ShapeDtypeStruct((b, n), F32),
        grid=(n // d,),
        in_specs=[pl.BlockSpec((b, d), lambda j: (0, 0)),
                  pl.BlockSpec((d, d), lambda j: (0, j)),
                  pl.BlockSpec((1, d), lambda j: (0, j))],
        out_specs=pl.BlockSpec((b, d), lambda j: (0, j)),
        compiler_params=_cparams(("arbitrary",)),
        name="adaln_mod",
    )(c, ada_w, ada_b.reshape(1, n))


def _inproj_kernel(x_ref, pos_ref, sc_ref, sh_ref, g1_ref, w1_ref, qg_ref, wq_ref, kvg_ref,
                   wkv_ref, freq_ref, alog_ref, dtb_ref,
                   q_ref, k_ref, v_ref, gdn_ref, gb_ref):
    x = x_ref[0]
    h = _rms(x, g1_ref[...]) * (1.0 + sc_ref[0]) + sh_ref[0]
    proj = jnp.dot(h.astype(BF16), w1_ref[...], preferred_element_type=F32)

    tb = x.shape[0]
    lane = lax.broadcasted_iota(jnp.int32, (tb, LANES), 1)
    ang = pos_ref[0].astype(F32) * freq_ref[...]
    cosv = jnp.cos(ang)
    sinv = jnp.sin(ang)
    in_rope = (lane >= MLA_NOPE) & (lane < MLA_QK)

    scale = (MLA_QK ** -0.5) * LOG2E
    qn = _rms(proj[:, :MLA_Q_LORA], qg_ref[...])
    qa = jnp.dot(qn.astype(BF16), wq_ref[...], preferred_element_type=F32)
    cq = jnp.where(lane < MLA_NOPE, scale, jnp.where(in_rope, cosv * scale, 0.0))
    sq = sinv * scale
    cq_t = jnp.concatenate([cq] * MLA_HEADS, axis=1)
    sq_t = jnp.concatenate([sq] * MLA_HEADS, axis=1)
    width = MLA_HEADS * HEAD_SLOT
    q = qa * cq_t + pltpu.roll(qa, width - MLA_ROPE, axis=1) * sq_t
    q_ref[0] = q.astype(BF16)

    kvn = _rms(proj[:, MLA_Q_LORA:MLA_Q_LORA + MLA_KV_LORA], kvg_ref[...])
    kva = jnp.dot(kvn.astype(BF16), wkv_ref[...], preferred_element_type=F32)
    misc = proj[:, MLA_Q_LORA + MLA_KV_LORA:MLA_Q_LORA + MLA_KV_LORA + MISC_W]
    kp = misc * jnp.where(in_rope, cosv, 0.0) + pltpu.roll(misc, MISC_W - MLA_ROPE, axis=1) * sinv
    k = kva[:, :width] + jnp.concatenate([kp] * MLA_HEADS, axis=1)
    k_ref[0] = k.astype(BF16)
    v_ref[0] = kva[:, width:].astype(BF16)

    z = misc + dtb_ref[...]
    softplus = jnp.maximum(z, 0.0) + jnp.log(1.0 + jnp.exp(-jnp.abs(z)))
    g = -jnp.exp(alog_ref[...]) * softplus
    gb_ref[0] = jnp.where(lane < GDN_HEADS, g, _sigmoid(misc))

    gdn_ref[0] = proj[:, MLA_Q_LORA + MLA_KV_LORA + MISC_W:].astype(BF16)


def _in_projection(x, positions, sc1, sh1, norm1_g, w1, q_norm_g, wq, kv_norm_g, wkv,
                   freq, alog, dtb, tb):
    b, s, d = x.shape
    hw = MLA_HEADS * HEAD_SLOT
    const = lambda shape: pl.BlockSpec(shape, lambda i, j: (0,) * len(shape))
    tok = lambda w: pl.BlockSpec((1, tb, w), lambda i, j: (i, j, 0))
    per_b = pl.BlockSpec((1, 1, d), lambda i, j: (i, 0, 0))
    return pl.pallas_call(
        _inproj_kernel,
        out_shape=(jax.ShapeDtypeStruct((b, s, hw), BF16),
                   jax.ShapeDtypeStruct((b, s, hw), BF16),
                   jax.ShapeDtypeStruct((b, s, MLA_HEADS * MLA_V), BF16),
                   jax.ShapeDtypeStruct((b, s, GDN_HEADS * GDN_SLAB), BF16),
                   jax.ShapeDtypeStruct((b, s, MISC_W), F32)),
        grid=(b, s // tb),
        in_specs=[tok(d), tok(1), per_b, per_b, const((1, d)), const(w1.shape),
                  const((1, MLA_Q_LORA)), const(wq.shape), const((1, MLA_KV_LORA)),
                  const(wkv.shape), const((1, LANES)), const((1, LANES)), const((1, LANES))],
        out_specs=(tok(hw), tok(hw), tok(MLA_HEADS * MLA_V), tok(GDN_HEADS * GDN_SLAB),
                   tok(MISC_W)),
        compiler_params=_cparams(("parallel", "parallel")),
        name="in_projection",
    )(x, positions.reshape(b, s, 1), sc1, sh1, norm1_g.reshape(1, d), w1,
      q_norm_g.reshape(1, -1), wq, kv_norm_g.reshape(1, -1), wkv, freq, alog, dtb)


def _attn_kernel(q_ref, k_ref, v_ref, o_ref, s_ref, p_ref, m_ref, l_ref, acc_ref, *, tq):
    qi = pl.program_id(2)
    m_ref[...] = jnp.full(m_ref.shape, -jnp.inf, F32)
    l_ref[...] = jnp.zeros(l_ref.shape, F32)
    acc_ref[...] = jnp.zeros(acc_ref.shape, F32)

    def step(off, masked):
        vj = v_ref[0, pl.ds(off, tq), :]
        for h in range(2):
            kj = k_ref[0, pl.ds(off, tq), h * HEAD_SLOT:(h + 1) * HEAD_SLOT]
            s_ref[h] = lax.dot_general(q_ref[0, :, h * HEAD_SLOT:(h + 1) * HEAD_SLOT], kj,
                                       (((1,), (1,)), ((), ())), preferred_element_type=F32)
        n_strips = tq // ATT_STRIP

        def strip(h, r):
            rows = slice(r * ATT_STRIP, (r + 1) * ATT_STRIP)
            sc = s_ref[h, rows, :]
            if masked:
                rid = lax.broadcasted_iota(jnp.int32, (ATT_STRIP, tq), 0) + r * ATT_STRIP
                cid = lax.broadcasted_iota(jnp.int32, (ATT_STRIP, tq), 1)
                sc = jnp.where(cid <= rid, sc, -jnp.inf)
            return rows, sc

        rep = lambda col: jnp.broadcast_to(col, (col.shape[0], LANES))
        m_new = []
        for h in range(2):
            mx = jnp.concatenate([rep(jnp.max(strip(h, r)[1], axis=-1, keepdims=True))
                                  for r in range(n_strips)], axis=0)
            m_new.append(jnp.maximum(m_ref[h], mx))
        for h in range(2):
            sums = []
            for r in range(n_strips):
                rows, sc = strip(h, r)
                p = jnp.exp2(sc - jnp.concatenate([m_new[h][rows]] * (tq // LANES), axis=1))
                p_ref[h, rows, :] = p.astype(BF16)
                sums.append(rep(jnp.sum(p, axis=-1, keepdims=True)))
            alpha = jnp.exp2(m_ref[h] - m_new[h])
            l_ref[h] = alpha * l_ref[h] + jnp.concatenate(sums, axis=0)
            m_ref[h] = m_new[h]
            acc_ref[h] = alpha * acc_ref[h] + jnp.dot(p_ref[h], vj, preferred_element_type=F32)

    def body(j, carry):
        step(pl.multiple_of(j * tq, tq), False)
        return carry

    lax.fori_loop(0, qi, body, 0)
    step(pl.multiple_of(qi * tq, tq), True)
    lane = lax.broadcasted_iota(jnp.int32, (tq, 2 * MLA_V), 1)
    o0 = acc_ref[0] / l_ref[0]
    o1 = acc_ref[1] / l_ref[1]
    o_ref[0] = jnp.where(lane < MLA_V, o0, o1).astype(o_ref.dtype)


def _attention(q, k, v, tq):
    b, s, _ = q.shape
    pairs = MLA_HEADS // 2
    return pl.pallas_call(
        functools.partial(_attn_kernel, tq=tq),
        out_shape=jax.ShapeDtypeStruct((b, s, MLA_HEADS * MLA_V), BF16),
        grid=(b, pairs, s // tq),
        in_specs=[pl.BlockSpec((1, tq, 2 * HEAD_SLOT), lambda i, p, j: (i, j, p)),
                  pl.BlockSpec((1, s, 2 * HEAD_SLOT), lambda i, p, j: (i, 0, p)),
                  pl.BlockSpec((1, s, 2 * MLA_V), lambda i, p, j: (i, 0, p))],
        out_specs=pl.BlockSpec((1, tq, 2 * MLA_V), lambda i, p, j: (i, j, p)),
        scratch_shapes=[pltpu.VMEM((2, tq, tq), F32), pltpu.VMEM((2, tq, tq), BF16),
                        pltpu.VMEM((2, tq, LANES), F32), pltpu.VMEM((2, tq, LANES), F32),
                        pltpu.VMEM((2, tq, 2 * MLA_V), F32)],
        compiler_params=_cparams(("parallel", "parallel", "arbitrary")),
        name="mla_attention",
    )(q, k, v)


def _bdot(a, b):
    return jnp.dot(a.astype(BF16), b.astype(BF16), preferred_element_type=F32)


def _gdn_kernel(slab_ref, gb_ref, cw_ref, ng_ref, o_ref, xbuf_ref, state_ref, *, ts, hg):
    si = pl.program_id(2)
    group = pl.program_id(1)
    width = hg * GDN_SLAB
    pad = 8

    @pl.when(si == 0)
    def _():
        xbuf_ref[0:pad, :] = jnp.zeros((pad, width), F32)
        state_ref[...] = jnp.zeros(state_ref.shape, F32)

    xs = slab_ref[0].astype(F32)
    xbuf_ref[pad:pad + ts, :] = xs
    conv = jnp.zeros((ts, width), F32)
    for j in range(CONV_WIDTH):
        start = pad - (CONV_WIDTH - 1) + j
        conv = conv + cw_ref[j:j + 1, :] * xbuf_ref[start:start + ts, :]
    xbuf_ref[0:pad, :] = xs[ts - pad:ts, :]
    act = _silu(conv)

    gb = gb_ref[0]
    lane = lax.broadcasted_iota(jnp.int32, (ts, LANES), 1)
    n_chunks = ts // CHUNK
    row = lax.broadcasted_iota(jnp.int32, (CHUNK, CHUNK), 0)
    col = lax.broadcasted_iota(jnp.int32, (CHUNK, CHUNK), 1)
    tri_incl = col <= row
    tri_strict = col < row
    eye = col == row
    crow = lax.broadcasted_iota(jnp.int32, (ts, 1), 0) % CHUNK

    heads = []
    for hh in range(hg):
        head = hg * group + hh
        base = hh * GDN_SLAB
        q_all = act[:, base:base + GDN_DK]
        k_all = act[:, base + GDN_DK:base + 2 * GDN_DK]
        v_all = act[:, base + 2 * GDN_DK:base + 2 * GDN_DK + GDN_DV]
        z_all = xs[:, base + 3 * GDN_DK:base + 3 * GDN_DK + GDN_DV]
        q_all = q_all * lax.rsqrt(jnp.sum(q_all * q_all, axis=-1, keepdims=True) + EPS)
        k_all = k_all * lax.rsqrt(jnp.sum(k_all * k_all, axis=-1, keepdims=True) + EPS)
        q_all = q_all * (GDN_DK ** -0.5)
        g_all = jnp.sum(jnp.where(lane == head, gb, 0.0), axis=-1, keepdims=True)
        b_all = jnp.sum(jnp.where(lane == head + GDN_HEADS, gb, 0.0), axis=-1, keepdims=True)

        gc_all = g_all
        shift = 1
        while shift < CHUNK:
            rolled = pltpu.roll(gc_all, shift, axis=0)
            gc_all = gc_all + jnp.where(crow >= shift, rolled, 0.0)
            shift *= 2
        heads.append((q_all, k_all, v_all, z_all, b_all, gc_all))

    units = [(hh, c) for c in range(n_chunks) for hh in range(hg)]
    a_low, qk, y0, qd, kt, cd = {}, {}, {}, {}, {}, {}
    for u in units:
        hh, c = u
        q_all, k_all, v_all, _, b_all, gc_all = heads[hh]
        sl = slice(c * CHUNK, (c + 1) * CHUNK)
        qc, kc, vc, bc, gcum = q_all[sl], k_all[sl], v_all[sl], b_all[sl], gc_all[sl]
        gcol = jnp.broadcast_to(gcum, (CHUNK, CHUNK))
        grow = jnp.sum(jnp.where(eye, gcol, 0.0), axis=0, keepdims=True)
        diff = gcol - grow
        decay = jnp.where(tri_incl, jnp.exp(jnp.where(tri_incl, diff, 0.0)), 0.0)
        k_beta = kc * bc
        kq = jnp.concatenate([k_beta, qc], axis=0).astype(BF16)
        kk = lax.dot_general(kq, kc.astype(BF16), (((1,), (1,)), ((), ())),
                             preferred_element_type=F32)
        a_low[u] = jnp.where(tri_strict, kk[:CHUNK] * decay, 0.0)
        qk[u] = jnp.where(tri_incl, kk[CHUNK:] * decay, 0.0)
        eg = jnp.exp(gcum)
        y0[u] = jnp.concatenate([vc * bc, k_beta * eg], axis=1)
        qd[u] = qc * eg
        glast = gcum[CHUNK - 1:CHUNK, :]
        kt[u] = kc * jnp.exp(glast - gcum)
        cd[u] = jnp.exp(glast)

    blk = 2
    tinv = {u: jnp.where(eye, 1.0, 0.0) - jnp.where((row // blk) == (col // blk), a_low[u], 0.0)
            for u in units}
    while blk < CHUNK:
        in_big = (row // (2 * blk)) == (col // (2 * blk))
        off_mask = in_big & ((row // blk) != (col // blk))
        left = {u: _bdot(tinv[u], jnp.where(off_mask, a_low[u], 0.0)) for u in units}
        tinv = {u: tinv[u] - _bdot(left[u], tinv[u]) for u in units}
        blk *= 2

    y = {u: _bdot(tinv[u], y0[u]) for u in units}
    ky = {u: lax.dot_general(kt[u].astype(BF16), y[u].astype(BF16), (((0,), (0,)), ((), ())),
                             preferred_element_type=F32) for u in units}
    qy = {u: _bdot(qk[u], y[u]) for u in units}

    outs = [[] for _ in range(hg)]
    states = [state_ref[hh] for hh in range(hg)]
    for c in range(n_chunks):
        for hh in range(hg):
            u = (hh, c)
            sb = states[hh].astype(BF16)
            r_mat = qd[u] - qy[u][:, GDN_DV:]
            outs[hh].append(jnp.dot(r_mat.astype(BF16), sb, preferred_element_type=F32)
                            + qy[u][:, :GDN_DV])
            states[hh] = (states[hh] * cd[u]
                          - jnp.dot(ky[u][:, GDN_DV:].astype(BF16), sb,
                                    preferred_element_type=F32)
                          + ky[u][:, :GDN_DV])
    finals = []
    for hh in range(hg):
        state_ref[hh] = states[hh]
        o_all = jnp.concatenate(outs[hh], axis=0)
        finals.append(_rms(o_all, ng_ref[...]) * _silu(heads[hh][3]))
    o_ref[0] = jnp.concatenate(finals, axis=1).astype(o_ref.dtype)


def _gdn(slab, gb, conv_slab, gdn_norm_g, ts, hg):
    b, s, _ = slab.shape
    groups = GDN_HEADS // hg
    return pl.pallas_call(
        functools.partial(_gdn_kernel, ts=ts, hg=hg),
        out_shape=jax.ShapeDtypeStruct((b, s, GDN_HEADS * GDN_DV), BF16),
        grid=(b, groups, s // ts),
        in_specs=[pl.BlockSpec((1, ts, hg * GDN_SLAB), lambda i, p, j: (i, j, p)),
                  pl.BlockSpec((1, ts, MISC_W), lambda i, p, j: (i, j, 0)),
                  pl.BlockSpec((CONV_WIDTH, hg * GDN_SLAB), lambda i, p, j: (0, p)),
                  pl.BlockSpec((1, GDN_DV), lambda i, p, j: (0, 0))],
        out_specs=pl.BlockSpec((1, ts, hg * GDN_DV), lambda i, p, j: (i, j, p)),
        scratch_shapes=[pltpu.VMEM((ts + 8, hg * GDN_SLAB), F32),
                        pltpu.VMEM((hg, GDN_DK, GDN_DV), F32)],
        compiler_params=_cparams(("parallel", "parallel", "arbitrary")),
        name="gated_deltanet",
    )(slab, gb, conv_slab, gdn_norm_g.reshape(1, GDN_DV))


def _outproj_kernel(x_ref, ao_ref, go_ref, gt_ref, sc_ref, sh_ref, mg_ref, wo_ref, g2_ref,
                    rw_ref, rb_ref, x1_ref, h2_ref, route_ref, cnt_ref, carry_ref, *, tb):
    first = (pl.program_id(0) == 0) & (pl.program_id(1) == 0)

    @pl.when(first)
    def _():
        carry_ref[...] = jnp.zeros(carry_ref.shape, F32)

    mla = _rms(ao_ref[0].astype(F32), mg_ref[...])
    cat = jnp.concatenate([mla.astype(BF16), go_ref[0]], axis=1)
    mix = jnp.dot(cat, wo_ref[...], preferred_element_type=F32)
    x1 = x_ref[0] + gt_ref[0] * mix
    x1_ref[0] = x1
    h2 = _rms(x1, g2_ref[...]) * (1.0 + sc_ref[0]) + sh_ref[0]
    h2_ref[0] = h2

    logits = jnp.dot(h2, rw_ref[...], preferred_element_type=F32,
                     precision=lax.Precision.HIGHEST) + rb_ref[...]
    lane = lax.broadcasted_iota(jnp.int32, (tb, LANES), 1)
    work = jnp.where(lane < N_EXPERTS, logits, -jnp.inf)
    vals, idxs = [], []
    onehot = jnp.zeros((tb, LANES), F32)
    for _ in range(TOP_K):
        mx = jnp.max(work, axis=-1, keepdims=True)
        ix = jnp.min(jnp.where(work == mx, lane, LANES), axis=-1, keepdims=True)
        sel = lane == ix
        onehot = jnp.where(sel, 1.0, onehot)
        work = jnp.where(sel, -jnp.inf, work)
        vals.append(mx)
        idxs.append(ix)
    exps = [jnp.exp(v - vals[0]) for v in vals]
    den = exps[0] + exps[1] + exps[2] + exps[3]

    r = lax.broadcasted_iota(jnp.int32, (tb, tb), 0)
    c = lax.broadcasted_iota(jnp.int32, (tb, tb), 1)
    tri = jnp.where(c < r, 1.0, 0.0).astype(BF16)
    before = jnp.dot(tri, onehot.astype(BF16), preferred_element_type=F32) + carry_ref[...]
    route = jnp.zeros((tb, LANES), F32)
    for kk in range(TOP_K):
        rank = jnp.sum(jnp.where(lane == idxs[kk], before, 0.0), axis=-1, keepdims=True)
        route = jnp.where(lane == kk, idxs[kk].astype(F32), route)
        route = jnp.where(lane == TOP_K + kk, rank, route)
        route = jnp.where(lane == 2 * TOP_K + kk, exps[kk] / den, route)
    route_ref[0] = route
    total = carry_ref[...] + jnp.sum(onehot, axis=0, keepdims=True)
    carry_ref[...] = total
    cnt_ref[...] = total


def _out_projection(x, attn_o, gdn_o, gt1, sc2, sh2, mla_out_g, w_out, norm2_g, rw, rb, tb):
    b, s, d = x.shape
    const = lambda shape: pl.BlockSpec(shape, lambda i, j: (0,) * len(shape))
    tok = lambda w: pl.BlockSpec((1, tb, w), lambda i, j: (i, j, 0))
    per_b = pl.BlockSpec((1, 1, d), lambda i, j: (i, 0, 0))
    half = attn_o.shape[-1]
    return pl.pallas_call(
        functools.partial(_outproj_kernel, tb=tb),
        out_shape=(jax.ShapeDtypeStruct((b, s, d), F32),
                   jax.ShapeDtypeStruct((b, s, d), F32),
                   jax.ShapeDtypeStruct((b, s, LANES), F32),
                   jax.ShapeDtypeStruct((1, LANES), F32)),
        grid=(b, s // tb),
        in_specs=[tok(d), tok(half), tok(half), per_b, per_b, per_b, const((1, half)),
                  const(w_out.shape), const((1, d)), const(rw.shape), const((1, LANES))],
        out_specs=(tok(d), tok(d), tok(LANES), const((1, LANES))),
        scratch_shapes=[pltpu.VMEM((1, LANES), F32)],
        compiler_params=_cparams(("arbitrary", "arbitrary")),
        name="out_projection_router",
    )(x, attn_o, gdn_o, gt1, sc2, sh2, mla_out_g.reshape(1, half), w_out,
      norm2_g.reshape(1, d), rw, rb)


def _sc_mesh():
    return plsc.VectorSubcoreMesh(core_axis_name="c", subcore_axis_name="s",
                                  num_cores=SC_CORES, num_subcores=SC_SUBCORES)


def _sc_worker():
    return lax.axis_index("s") * SC_CORES + lax.axis_index("c")


def _dispatch_rows(h2, dest_km, rows):
    t, d = h2.shape
    per_worker = t // SC_WORKERS
    n_chunks = per_worker // SC_CHUNK

    @functools.partial(
        pl.kernel, out_type=jax.ShapeDtypeStruct((rows, d), h2.dtype), mesh=_sc_mesh(),
        scratch_types=[pltpu.VMEM((SC_CHUNK,), jnp.int32), pltpu.VMEM((SC_CHUNK, d), h2.dtype)],
        name="moe_dispatch")
    def run(h2_hbm, dest_hbm, xs_hbm, idx_v, rows_v):
        base_w = _sc_worker() * per_worker

        @pl.loop(0, n_chunks)
        def _(ci):
            base = pl.multiple_of(base_w + ci * SC_CHUNK, SC_CHUNK)
            pltpu.sync_copy(h2_hbm.at[pl.ds(base, SC_CHUNK)], rows_v)
            for kk in range(TOP_K):
                pltpu.sync_copy(dest_hbm.at[kk, pl.ds(base, SC_CHUNK)], idx_v)
                pltpu.sync_copy(rows_v, xs_hbm.at[idx_v])

    return run(h2, dest_km)


def _collect_rows(y_rows, dest_km):
    _, d = y_rows.shape
    t = dest_km.shape[1]
    per_worker = t // SC_WORKERS
    n_chunks = per_worker // SC_CHUNK

    @functools.partial(
        pl.kernel, out_type=jax.ShapeDtypeStruct((TOP_K, t, d), y_rows.dtype), mesh=_sc_mesh(),
        scratch_types=[pltpu.VMEM((SC_CHUNK,), jnp.int32), pltpu.VMEM((SC_CHUNK, d), y_rows.dtype)],
        name="moe_collect")
    def run(y_hbm, dest_hbm, out_hbm, idx_v, rows_v):
        base_w = _sc_worker() * per_worker

        @pl.loop(0, n_chunks)
        def _(ci):
            base = pl.multiple_of(base_w + ci * SC_CHUNK, SC_CHUNK)
            for kk in range(TOP_K):
                pltpu.sync_copy(dest_hbm.at[kk, pl.ds(base, SC_CHUNK)], idx_v)
                pltpu.sync_copy(y_hbm.at[idx_v], rows_v)
                pltpu.sync_copy(rows_v, out_hbm.at[kk, pl.ds(base, SC_CHUNK)])

    return run(y_rows, dest_km)


def _experts_kernel(be_ref, nu_ref, xs_ref, wgu_ref, bgu_ref, wd_ref, bd_ref, y_ref,
                    wgu_bf, wd_bf):
    i = pl.program_id(0)
    new_expert = (i == 0) | (be_ref[i] != be_ref[jnp.maximum(i - 1, 0)])

    @pl.when(new_expert)
    def _():
        wgu_bf[...] = wgu_ref[0].astype(BF16)
        wd_bf[...] = wd_ref[0].astype(BF16)

    @pl.when(i < nu_ref[0])
    def _():
        gu = jnp.dot(xs_ref[...].astype(BF16), wgu_bf[...], preferred_element_type=F32) + bgu_ref[0]
        gate = jnp.minimum(gu[:, :D_EXPERT], SWIGLU_LIMIT)
        up = jnp.clip(gu[:, D_EXPERT:], -SWIGLU_LIMIT, SWIGLU_LIMIT)
        act = (up + 1.0) * (gate * _sigmoid(SWIGLU_ALPHA * gate))
        y = jnp.dot(act.astype(BF16), wd_bf[...], preferred_element_type=F32) + bd_ref[0]
        y_ref[...] = y.astype(y_ref.dtype)


def _experts(blk_e, n_used, xs, wgu, bgu, wd, bd, bm):
    rows, d = xs.shape
    n_blocks = rows // bm
    row_map = lambda i, be, nu: (jnp.maximum(jnp.minimum(i, nu[0] - 1), 0), 0)
    exp_map = lambda i, be, nu: (be[i], 0, 0)
    grid_spec = pltpu.PrefetchScalarGridSpec(
        num_scalar_prefetch=2,
        grid=(n_blocks,),
        in_specs=[pl.BlockSpec((bm, d), row_map),
                  pl.BlockSpec((1, d, 2 * D_EXPERT), exp_map),
                  pl.BlockSpec((1, 1, 2 * D_EXPERT), exp_map),
                  pl.BlockSpec((1, D_EXPERT, d), exp_map),
                  pl.BlockSpec((1, 1, d), exp_map)],
        out_specs=pl.BlockSpec((bm, d), row_map),
        scratch_shapes=[pltpu.VMEM((d, 2 * D_EXPERT), BF16),
                        pltpu.VMEM((D_EXPERT, d), BF16)])
    return pl.pallas_call(
        _experts_kernel,
        out_shape=jax.ShapeDtypeStruct((rows, d), F32),
        grid_spec=grid_spec,
        compiler_params=_cparams(("arbitrary",)),
        name="expert_mlp",
    )(blk_e, n_used, xs, wgu, bgu, wd, bd)


def _final_kernel(x1_ref, yk_ref, route_ref, gt_ref, fg_ref, o_ref):
    route = route_ref[0]
    ffn = jnp.zeros(x1_ref.shape[1:], F32)
    for kk in range(TOP_K):
        wk = route[:, 2 * TOP_K + kk:2 * TOP_K + kk + 1]
        ffn = ffn + wk * yk_ref[kk, 0].astype(F32)
    x2 = x1_ref[0] + gt_ref[0] * ffn
    o_ref[0] = _rms(x2, fg_ref[...])


def _final(x1, yk, route, gt2, final_g, tb):
    b, s, d = x1.shape
    return pl.pallas_call(
        _final_kernel,
        out_shape=jax.ShapeDtypeStruct((b, s, d), F32),
        grid=(b, s // tb),
        in_specs=[pl.BlockSpec((1, tb, d), lambda i, j: (i, j, 0)),
                  pl.BlockSpec((TOP_K, 1, tb, d), lambda i, j: (0, i, j, 0)),
                  pl.BlockSpec((1, tb, LANES), lambda i, j: (i, j, 0)),
                  pl.BlockSpec((1, 1, d), lambda i, j: (i, 0, 0)),
                  pl.BlockSpec((1, d), lambda i, j: (0, 0))],
        out_specs=pl.BlockSpec((1, tb, d), lambda i, j: (i, j, 0)),
        compiler_params=_cparams(("parallel", "parallel")),
        name="combine_final_norm",
    )(x1, yk, route, gt2, final_g.reshape(1, d))


def _rot_cols(w):
    half = MLA_ROPE // 2
    return jnp.concatenate([-w[..., half:], w[..., :half]], axis=-1)


def _prep_weights(w_in, w_q_b, w_kv_b, conv_w, A_log, dt_bias):
    d = w_in.shape[0]
    cuts = [MLA_Q_LORA, MLA_KV_LORA, MLA_ROPE, GDN_HEADS * GDN_DK, GDN_HEADS * GDN_DK,
            GDN_HEADS * GDN_DV, GDN_HEADS * GDN_DV, GDN_HEADS, GDN_HEADS]
    offs = [0]
    for cw in cuts:
        offs.append(offs[-1] + cw)
    part = lambda i: w_in[:, offs[i]:offs[i + 1]]
    k_pe = part(2)
    misc = jnp.concatenate(
        [part(7), part(8), jnp.zeros((d, MLA_NOPE - 2 * GDN_HEADS), w_in.dtype), k_pe,
         _rot_cols(k_pe)], axis=1)
    heads = lambda w, n: w.reshape(w.shape[0], GDN_HEADS, n)
    slab = jnp.concatenate([heads(part(3), GDN_DK), heads(part(4), GDN_DK),
                            heads(part(5), GDN_DV), heads(part(6), GDN_DV)], axis=-1)
    w1 = jnp.concatenate([part(0), part(1), misc, slab.reshape(d, GDN_HEADS * GDN_SLAB)],
                         axis=1).astype(BF16)

    wq3 = w_q_b.reshape(MLA_Q_LORA, MLA_HEADS, MLA_QK)
    pe = wq3[..., MLA_NOPE:]
    wq = jnp.concatenate([wq3[..., :MLA_NOPE], pe, _rot_cols(pe)], axis=-1)
    wq = wq.reshape(MLA_Q_LORA, MLA_HEADS * HEAD_SLOT).astype(BF16)

    wkv3 = w_kv_b.reshape(MLA_KV_LORA, MLA_HEADS, MLA_NOPE + MLA_V)
    wk = jnp.concatenate([wkv3[..., :MLA_NOPE],
                          jnp.zeros((MLA_KV_LORA, MLA_HEADS, HEAD_SLOT - MLA_NOPE), w_kv_b.dtype)],
                         axis=-1).reshape(MLA_KV_LORA, MLA_HEADS * HEAD_SLOT)
    wv = wkv3[..., MLA_NOPE:].reshape(MLA_KV_LORA, MLA_HEADS * MLA_V)
    wkv = jnp.concatenate([wk, wv], axis=1).astype(BF16)

    nk = GDN_HEADS * GDN_DK
    cheads = lambda w, n: w.reshape(CONV_WIDTH, GDN_HEADS, n)
    conv_slab = jnp.concatenate(
        [cheads(conv_w[:, :nk], GDN_DK), cheads(conv_w[:, nk:2 * nk], GDN_DK),
         cheads(conv_w[:, 2 * nk:], GDN_DV), jnp.zeros((CONV_WIDTH, GDN_HEADS, GDN_DV), conv_w.dtype)],
        axis=-1).reshape(CONV_WIDTH, GDN_HEADS * GDN_SLAB)

    half = MLA_ROPE // 2
    inv_freq = ROPE_THETA ** (-jnp.arange(half, dtype=F32) / half)
    freq = jnp.concatenate([jnp.zeros((MLA_NOPE,), F32), inv_freq, inv_freq,
                            jnp.zeros((LANES - MLA_QK,), F32)]).reshape(1, LANES)
    padl = lambda a: jnp.concatenate([a.astype(F32), jnp.zeros((LANES - a.shape[0],), F32)]).reshape(1, LANES)
    return w1, wq, wkv, conv_slab, freq, padl(A_log), padl(dt_bias)


def _layer(x, mod, positions, w_in, q_norm_g, w_q_b, kv_norm_g, w_kv_b, mla_out_g, conv_w,
           A_log, dt_bias, gdn_norm_g, w_out, norm1_g, norm2_g, router_w, router_b, w_gate_up,
           b_gate_up, w_down, b_down):
    b, s, d = x.shape
    t = b * s
    sh1, sc1, gt1, sh2, sc2, gt2 = [m.reshape(b, 1, d) for m in jnp.split(mod, 6, axis=-1)]
    w1, wq, wkv, conv_slab, freq, alog, dtb = _prep_weights(w_in, w_q_b, w_kv_b, conv_w, A_log,
                                                            dt_bias)
    tb = min(256, s)
    q, k, v, slab, gb = _in_projection(x, positions, sc1, sh1, norm1_g, w1, q_norm_g, wq,
                                       kv_norm_g, wkv, freq, alog, dtb, tb)
    attn_o = _attention(q, k, v, min(512, s))
    gdn_o = _gdn(slab, gb, conv_slab, gdn_norm_g, min(256, s), 4)

    rw = jnp.concatenate([router_w, jnp.zeros((d, LANES - N_EXPERTS), router_w.dtype)], axis=1)
    rb = jnp.concatenate([router_b, jnp.zeros((LANES - N_EXPERTS,), router_b.dtype)]).reshape(1, LANES)
    x1, h2, route, counts = _out_projection(x, attn_o, gdn_o, gt1, sc2, sh2, mla_out_g,
                                            w_out.astype(BF16), norm2_g, rw, rb, tb)

    bm = 512
    route2 = route.reshape(t, LANES)
    idx = route2[:, :TOP_K].astype(jnp.int32)
    rank = route2[:, TOP_K:2 * TOP_K].astype(jnp.int32)
    cnt = counts[0, :N_EXPERTS].astype(jnp.int32)
    padded = ((cnt + bm - 1) // bm) * bm
    pend = jnp.cumsum(padded)
    pstart = pend - padded
    dest = pstart[idx] + rank
    n_blocks = (t * TOP_K + N_EXPERTS * (bm - 1) + bm - 1) // bm
    rows = n_blocks * bm
    n_used = (pend[-1] // bm).astype(jnp.int32).reshape(1)
    blk_start = jnp.arange(n_blocks, dtype=jnp.int32) * bm
    blk_e = jnp.minimum(jnp.sum(blk_start[:, None] >= pend[None, :], axis=1), N_EXPERTS - 1)
    last_e = blk_e[jnp.maximum(n_used[0] - 1, 0)]
    blk_e = jnp.where(jnp.arange(n_blocks) < n_used[0], blk_e, last_e).astype(jnp.int32)

    dest_km = dest.T
    xs = _dispatch_rows(h2.reshape(t, d), dest_km, rows)
    y_rows = _experts(blk_e, n_used, xs, w_gate_up, b_gate_up.reshape(N_EXPERTS, 1, -1),
                      w_down, b_down.reshape(N_EXPERTS, 1, -1), bm)
    yk = _collect_rows(y_rows, dest_km).reshape(TOP_K, b, s, d)
    return x1, yk, route, gt2


def kernel(x, c, positions, ada_w, ada_b, norm1_g, w_in, q_norm_g, w_q_b, kv_norm_g, w_kv_b, mla_out_g, conv_w, A_log, dt_bias, gdn_norm_g, w_out, norm2_g, router_w, router_b, w_gate_up, b_gate_up, w_down, b_down, final_g):
    depth = ada_w.shape[0]
    assert depth == 1
    l = 0
    mod = _modulation(c, ada_w[l], ada_b[l])
    x1, yk, route, gt2 = _layer(x, mod, positions, w_in[l], q_norm_g[l], w_q_b[l], kv_norm_g[l],
                         w_kv_b[l], mla_out_g[l], conv_w[l], A_log[l], dt_bias[l], gdn_norm_g[l],
                         w_out[l], norm1_g[l], norm2_g[l], router_w[l], router_b[l],
                         w_gate_up[l], b_gate_up[l], w_down[l], b_down[l])
    return _final(x1, yk, route, gt2, final_g, min(256, x.shape[1]))
```

```python
import functools

import jax
import jax.numpy as jnp
from jax import lax
from jax.experimental import pallas as pl
from jax.experimental.pallas import tpu as pltpu
from jax.experimental.pallas import tpu_sc as plsc

F32 = jnp.float32
BF16 = jnp.bfloat16

D_MODEL = 1024
EPS = 1e-6
MLA_HEADS = 8
MLA_NOPE = 64
MLA_ROPE = 32
MLA_V = 64
MLA_QK = MLA_NOPE + MLA_ROPE
MLA_Q_LORA = 384
MLA_KV_LORA = 256
ROPE_THETA = 10000.0
GDN_HEADS = 8
GDN_DK = 64
GDN_DV = 64
CONV_WIDTH = 4
CHUNK = 64
N_EXPERTS = 32
TOP_K = 4
D_EXPERT = D_MODEL
SWIGLU_ALPHA = 1.702
SWIGLU_LIMIT = 7.0

LANES = 128
HEAD_SLOT = 128
GDN_SLAB = 256
MISC_W = 128
PROJ_W = MLA_Q_LORA + MLA_KV_LORA + MISC_W + GDN_HEADS * GDN_SLAB
VMEM_LIMIT = 56 * 1024 * 1024
SC_CORES = 2
SC_SUBCORES = 16
SC_WORKERS = SC_CORES * SC_SUBCORES
SC_CHUNK = 64
ATT_STRIP = 32
LOG2E = 1.4426950408889634


def _cparams(sem):
    return pltpu.CompilerParams(dimension_semantics=sem, vmem_limit_bytes=VMEM_LIMIT)


def _rms(x, g):
    return x * lax.rsqrt(jnp.mean(x * x, axis=-1, keepdims=True) + EPS) * g


def _sigmoid(x):
    return 1.0 / (1.0 + jnp.exp(-x))


def _silu(x):
    return x * _sigmoid(x)


def _mod_kernel(c_ref, w_ref, b_ref, o_ref):
    c = c_ref[...]
    o_ref[...] = jnp.dot(_silu(c), w_ref[...], preferred_element_type=F32,
                         precision=lax.Precision.HIGHEST) + b_ref[...]


def _modulation(c, ada_w, ada_b):
    b, d = c.shape
    n = ada_w.shape[1]
    return pl.pallas_call(
        _mod_kernel,
        out_shape=jax.ShapeDtypeStruct((b, n), F32),
        grid=(n // d,),
        in_specs=[pl.BlockSpec((b, d), lambda j: (0, 0)),
                  pl.BlockSpec((d, d), lambda j: (0, j)),
                  pl.BlockSpec((1, d), lambda j: (0, j))],
        out_specs=pl.BlockSpec((b, d), lambda j: (0, j)),
        compiler_params=_cparams(("arbitrary",)),
        name="adaln_mod",
    )(c, ada_w, ada_b.reshape(1, n))


def _inproj_kernel(x_ref, pos_ref, sc_ref, sh_ref, g1_ref, w1_ref, qg_ref, wq_ref, kvg_ref,
                   wkv_ref, freq_ref, alog_ref, dtb_ref,
                   q_ref, k_ref, v_ref, gdn_ref, gb_ref):
    x = x_ref[0]
    h = _rms(x, g1_ref[...]) * (1.0 + sc_ref[0]) + sh_ref[0]
    proj = jnp.dot(h.astype(BF16), w1_ref[...], preferred_element_type=F32)

    tb = x.shape[0]
    lane = lax.broadcasted_iota(jnp.int32, (tb, LANES), 1)
    ang = pos_ref[0].astype(F32) * freq_ref[...]
    cosv = jnp.cos(ang)
    sinv = jnp.sin(ang)
    in_rope = (lane >= MLA_NOPE) & (lane < MLA_QK)

    scale = (MLA_QK ** -0.5) * LOG2E
    qn = _rms(proj[:, :MLA_Q_LORA], qg_ref[...])
    qa = jnp.dot(qn.astype(BF16), wq_ref[...], preferred_element_type=F32)
    cq = jnp.where(lane < MLA_NOPE, scale, jnp.where(in_rope, cosv * scale, 0.0))
    sq = sinv * scale
    cq_t = jnp.concatenate([cq] * MLA_HEADS, axis=1)
    sq_t = jnp.concatenate([sq] * MLA_HEADS, axis=1)
    width = MLA_HEADS * HEAD_SLOT
    q = qa * cq_t + pltpu.roll(qa, width - MLA_ROPE, axis=1) * sq_t
    q_ref[0] = q.astype(BF16)

    kvn = _rms(proj[:, MLA_Q_LORA:MLA_Q_LORA + MLA_KV_LORA], kvg_ref[...])
    kva = jnp.dot(kvn.astype(BF16), wkv_ref[...], preferred_element_type=F32)
    misc = proj[:, MLA_Q_LORA + MLA_KV_LORA:MLA_Q_LORA + MLA_KV_LORA + MISC_W]
    kp = misc * jnp.where(in_rope, cosv, 0.0) + pltpu.roll(misc, MISC_W - MLA_ROPE, axis=1) * sinv
    k = kva[:, :width] + jnp.concatenate([kp] * MLA_HEADS, axis=1)
    k_ref[0] = k.astype(BF16)
    v_ref[0] = kva[:, width:].astype(BF16)

    z = misc + dtb_ref[...]
    softplus = jnp.maximum(z, 0.0) + jnp.log(1.0 + jnp.exp(-jnp.abs(z)))
    g = -jnp.exp(alog_ref[...]) * softplus
    gb_ref[0] = jnp.where(lane < GDN_HEADS, g, _sigmoid(misc))

    gdn_ref[0] = proj[:, MLA_Q_LORA + MLA_KV_LORA + MISC_W:].astype(BF16)


def _in_projection(x, positions, sc1, sh1, norm1_g, w1, q_norm_g, wq, kv_norm_g, wkv,
                   freq, alog, dtb, tb):
    b, s, d = x.shape
    hw = MLA_HEADS * HEAD_SLOT
    const = lambda shape: pl.BlockSpec(shape, lambda i, j: (0,) * len(shape))
    tok = lambda w: pl.BlockSpec((1, tb, w), lambda i, j: (i, j, 0))
    per_b = pl.BlockSpec((1, 1, d), lambda i, j: (i, 0, 0))
    return pl.pallas_call(
        _inproj_kernel,
        out_shape=(jax.ShapeDtypeStruct((b, s, hw), BF16),
                   jax.ShapeDtypeStruct((b, s, hw), BF16),
                   jax.ShapeDtypeStruct((b, s, MLA_HEADS * MLA_V), BF16),
                   jax.ShapeDtypeStruct((b, s, GDN_HEADS * GDN_SLAB), BF16),
                   jax.ShapeDtypeStruct((b, s, MISC_W), F32)),
        grid=(b, s // tb),
        in_specs=[tok(d), tok(1), per_b, per_b, const((1, d)), const(w1.shape),
                  const((1, MLA_Q_LORA)), const(wq.shape), const((1, MLA_KV_LORA)),
                  const(wkv.shape), const((1, LANES)), const((1, LANES)), const((1, LANES))],
        out_specs=(tok(hw), tok(hw), tok(MLA_HEADS * MLA_V), tok(GDN_HEADS * GDN_SLAB),
                   tok(MISC_W)),
        compiler_params=_cparams(("parallel", "parallel")),
        name="in_projection",
    )(x, positions.reshape(b, s, 1), sc1, sh1, norm1_g.reshape(1, d), w1,
      q_norm_g.reshape(1, -1), wq, kv_norm_g.reshape(1, -1), wkv, freq, alog, dtb)


def _attn_kernel(q_ref, k_ref, v_ref, o_ref, s_ref, p_ref, m_ref, l_ref, acc_ref, *, tq):
    qi = pl.program_id(2)
    m_ref[...] = jnp.full(m_ref.shape, -jnp.inf, F32)
    l_ref[...] = jnp.zeros(l_ref.shape, F32)
    acc_ref[...] = jnp.zeros(acc_ref.shape, F32)

    def step(off, masked):
        vj = v_ref[0, pl.ds(off, tq), :]
        for h in range(2):
            kj = k_ref[0, pl.ds(off, tq), h * HEAD_SLOT:(h + 1) * HEAD_SLOT]
            s_ref[h] = lax.dot_general(q_ref[0, :, h * HEAD_SLOT:(h + 1) * HEAD_SLOT], kj,
                                       (((1,), (1,)), ((), ())), preferred_element_type=F32)
        n_strips = tq // ATT_STRIP

        def strip(h, r):
            rows = slice(r * ATT_STRIP, (r + 1) * ATT_STRIP)
            sc = s_ref[h, rows, :]
            if masked:
                rid = lax.broadcasted_iota(jnp.int32, (ATT_STRIP, tq), 0) + r * ATT_STRIP
                cid = lax.broadcasted_iota(jnp.int32, (ATT_STRIP, tq), 1)
                sc = jnp.where(cid <= rid, sc, -jnp.inf)
            return rows, sc

        rep = lambda col: jnp.broadcast_to(col, (col.shape[0], LANES))
        m_new = []
        for h in range(2):
            mx = jnp.concatenate([rep(jnp.max(strip(h, r)[1], axis=-1, keepdims=True))
                                  for r in range(n_strips)], axis=0)
            m_new.append(jnp.maximum(m_ref[h], mx))
        for h in range(2):
            sums = []
            for r in range(n_strips):
                rows, sc = strip(h, r)
                p = jnp.exp2(sc - jnp.concatenate([m_new[h][rows]] * (tq // LANES), axis=1))
                p_ref[h, rows, :] = p.astype(BF16)
                sums.append(rep(jnp.sum(p, axis=-1, keepdims=True)))
            alpha = jnp.exp2(m_ref[h] - m_new[h])
            l_ref[h] = alpha * l_ref[h] + jnp.concatenate(sums, axis=0)
            m_ref[h] = m_new[h]
            acc_ref[h] = alpha * acc_ref[h] + jnp.dot(p_ref[h], vj, preferred_element_type=F32)

    def body(j, carry):
        step(pl.multiple_of(j * tq, tq), False)
        return carry

    lax.fori_loop(0, qi, body, 0)
    step(pl.multiple_of(qi * tq, tq), True)
    lane = lax.broadcasted_iota(jnp.int32, (tq, 2 * MLA_V), 1)
    o0 = acc_ref[0] / l_ref[0]
    o1 = acc_ref[1] / l_ref[1]
    o_ref[0] = jnp.where(lane < MLA_V, o0, o1).astype(o_ref.dtype)


def _attention(q, k, v, tq):
    b, s, _ = q.shape
    pairs = MLA_HEADS // 2
    return pl.pallas_call(
        functools.partial(_attn_kernel, tq=tq),
        out_shape=jax.ShapeDtypeStruct((b, s, MLA_HEADS * MLA_V), BF16),
        grid=(b, pairs, s // tq),
        in_specs=[pl.BlockSpec((1, tq, 2 * HEAD_SLOT), lambda i, p, j: (i, j, p)),
                  pl.BlockSpec((1, s, 2 * HEAD_SLOT), lambda i, p, j: (i, 0, p)),
                  pl.BlockSpec((1, s, 2 * MLA_V), lambda i, p, j: (i, 0, p))],
        out_specs=pl.BlockSpec((1, tq, 2 * MLA_V), lambda i, p, j: (i, j, p)),
        scratch_shapes=[pltpu.VMEM((2, tq, tq), F32), pltpu.VMEM((2, tq, tq), BF16),
                        pltpu.VMEM((2, tq, LANES), F32), pltpu.VMEM((2, tq, LANES), F32),
                        pltpu.VMEM((2, tq, 2 * MLA_V), F32)],
        compiler_params=_cparams(("parallel", "parallel", "arbitrary")),
        name="mla_attention",
    )(q, k, v)


def _bdot(a, b):
    return jnp.dot(a.astype(BF16), b.astype(BF16), preferred_element_type=F32)


def _gdn_kernel(slab_ref, gb_ref, cw_ref, ng_ref, shf_ref, ind_ref, sel_ref, o_ref, tail_ref,
                state_ref, *, ts, hg):
    si = pl.program_id(2)
    width = hg * GDN_SLAB
    hist_rows = 8

    @pl.when(si == 0)
    def _():
        tail_ref[...] = jnp.zeros(tail_ref.shape, F32)
        state_ref[...] = jnp.zeros(state_ref.shape, F32)

    xs_bf = slab_ref[0]
    xs = xs_bf.astype(F32)
    conv = cw_ref[CONV_WIDTH - 1:CONV_WIDTH, :] * xs
    hist = jnp.zeros((hist_rows, width), F32)
    for j in range(CONV_WIDTH - 1):
        wj = cw_ref[j:j + 1, :]
        conv = conv + wj * jnp.dot(shf_ref[j], xs_bf, preferred_element_type=F32)
        start = hist_rows - (CONV_WIDTH - 1) + j
        hist = hist + wj * tail_ref[start:start + hist_rows, :]
    conv = jnp.concatenate([conv[:hist_rows] + hist, conv[hist_rows:]], axis=0)
    tail_ref[0:hist_rows, :] = xs[ts - hist_rows:ts, :]
    act = _silu(conv)

    gb = gb_ref[0]
    g1 = gb.astype(BF16)
    r1 = gb - g1.astype(F32)
    g2 = r1.astype(BF16)
    g3 = (r1 - g2.astype(F32)).astype(BF16)
    sel = sel_ref[0]
    gate_w = (jnp.dot(g1, sel, preferred_element_type=F32)
              + jnp.dot(g2, sel, preferred_element_type=F32)
              + jnp.dot(g3, sel, preferred_element_type=F32))

    n_chunks = ts // CHUNK
    row = lax.broadcasted_iota(jnp.int32, (CHUNK, CHUNK), 0)
    col = lax.broadcasted_iota(jnp.int32, (CHUNK, CHUNK), 1)
    tri_incl = col <= row
    tri_strict = col < row
    eye = col == row
    crow = lax.broadcasted_iota(jnp.int32, (ts, LANES), 0) % CHUNK
    slab_lane = lax.broadcasted_iota(jnp.int32, (ts, GDN_SLAB), 1)

    heads = []
    for hh in range(hg):
        sl_act = act[:, hh * GDN_SLAB:(hh + 1) * GDN_SLAB]
        sq = sl_act * sl_act
        sq_hi = sq.astype(BF16)
        sq_lo = (sq - sq_hi.astype(F32)).astype(BF16)
        ss = (jnp.dot(sq_hi, ind_ref[...], preferred_element_type=F32)
              + jnp.dot(sq_lo, ind_ref[...], preferred_element_type=F32))
        inv = lax.rsqrt(ss + EPS)
        factor = jnp.where(slab_lane < GDN_DK, inv * (GDN_DK ** -0.5),
                           jnp.where(slab_lane < 2 * GDN_DK, inv, 1.0))
        normed = sl_act * factor
        z_all = xs[:, hh * GDN_SLAB + 3 * GDN_DK:(hh + 1) * GDN_SLAB]
        g_w = gate_w[:, hh * LANES:(hh + 1) * LANES]
        b_w = gate_w[:, (hg + hh) * LANES:(hg + hh + 1) * LANES]
        gc_w = g_w
        shift = 1
        while shift < CHUNK:
            rolled = pltpu.roll(gc_w, shift, axis=0)
            gc_w = gc_w + jnp.where(crow >= shift, rolled, 0.0)
            shift *= 2
        heads.append((normed, z_all, b_w, gc_w, jnp.exp(gc_w)))

    units = [(hh, c) for c in range(n_chunks) for hh in range(hg)]
    a_low, qk, y0, qd, kt, cd = {}, {}, {}, {}, {}, {}
    lo, hi = slice(0, GDN_DK), slice(GDN_DK, 2 * GDN_DK)
    for u in units:
        hh, c = u
        normed, _, b_w, gc_w, eg_w = heads[hh]
        sl = slice(c * CHUNK, (c + 1) * CHUNK)
        qc, kc, vc = normed[sl, lo], normed[sl, hi], normed[sl, 2 * GDN_DK:2 * GDN_DK + GDN_DV]
        gcol = gc_w[sl, lo]
        grow = jnp.sum(jnp.where(eye, gcol, 0.0), axis=0, keepdims=True)
        diff = gcol - grow
        decay = jnp.where(tri_incl, jnp.exp(jnp.where(tri_incl, diff, 0.0)), 0.0)
        k_beta = kc * b_w[sl, hi]
        kq = jnp.concatenate([k_beta, qc], axis=0).astype(BF16)
        kk = lax.dot_general(kq, kc.astype(BF16), (((1,), (1,)), ((), ())),
                             preferred_element_type=F32)
        a_low[u] = jnp.where(tri_strict, kk[:CHUNK] * decay, 0.0)
        qk[u] = jnp.where(tri_incl, kk[CHUNK:] * decay, 0.0)
        y0[u] = jnp.concatenate([vc * b_w[sl, lo], k_beta * eg_w[sl, hi]], axis=1)
        qd[u] = qc * eg_w[sl, lo]
        glast_hi = gc_w[(c + 1) * CHUNK - 1:(c + 1) * CHUNK, hi]
        kt[u] = kc * jnp.exp(glast_hi - gc_w[sl, hi])
        cd[u] = eg_w[(c + 1) * CHUNK - 1:(c + 1) * CHUNK, lo]

    blk = 2
    tinv = {u: jnp.where(eye, 1.0, 0.0) - jnp.where((row // blk) == (col // blk), a_low[u], 0.0)
            for u in units}
    while blk < CHUNK:
        in_big = (row // (2 * blk)) == (col // (2 * blk))
        off_mask = in_big & ((row // blk) != (col // blk))
        left = {u: _bdot(tinv[u], jnp.where(off_mask, a_low[u], 0.0)) for u in units}
        tinv = {u: tinv[u] - _bdot(left[u], tinv[u]) for u in units}
        blk *= 2

    y = {u: _bdot(tinv[u], y0[u]) for u in units}
    ky = {u: lax.dot_general(kt[u].astype(BF16), y[u].astype(BF16), (((0,), (0,)), ((), ())),
                             preferred_element_type=F32) for u in units}
    qy = {u: _bdot(qk[u], y[u]) for u in units}

    outs = [[] for _ in range(hg)]
    states = [state_ref[hh] for hh in range(hg)]
    for c in range(n_chunks):
        for hh in range(hg):
            u = (hh, c)
            sb = states[hh].astype(BF16)
            r_mat = qd[u] - qy[u][:, GDN_DV:]
            outs[hh].append(jnp.dot(r_mat.astype(BF16), sb, preferred_element_type=F32)
                            + qy[u][:, :GDN_DV])
            states[hh] = (states[hh] * cd[u]
                          - jnp.dot(ky[u][:, GDN_DV:].astype(BF16), sb,
                                    preferred_element_type=F32)
                          + ky[u][:, :GDN_DV])
    finals = []
    for hh in range(hg):
        state_ref[hh] = states[hh]
        o_all = jnp.concatenate(outs[hh], axis=0)
        finals.append(_rms(o_all, ng_ref[...]) * _silu(heads[hh][1]))
    o_ref[0] = jnp.concatenate(finals, axis=1).astype(o_ref.dtype)


def _gdn(slab, gb, conv_slab, gdn_norm_g, ts, hg):
    b, s, _ = slab.shape
    groups = GDN_HEADS // hg
    r = jnp.arange(ts)
    shifts = jnp.stack([(r[:, None] - r[None, :]) == (CONV_WIDTH - 1 - j)
                        for j in range(CONV_WIDTH - 1)]).astype(BF16)
    li = jnp.arange(GDN_SLAB)
    ind = ((li[:, None] // GDN_DK == li[None, :] // GDN_DK)
           & (li[:, None] < 2 * GDN_DK) & (li[None, :] < 2 * GDN_DK)).astype(BF16)
    src = jnp.arange(LANES)[None, :, None]
    dst = jnp.arange(2 * hg * LANES)[None, None, :]
    grp = jnp.arange(groups)[:, None, None]
    want = jnp.where(dst < hg * LANES, grp * hg + dst // LANES,
                     GDN_HEADS + grp * hg + (dst - hg * LANES) // LANES)
    sel = (src == want).astype(BF16)
    return pl.pallas_call(
        functools.partial(_gdn_kernel, ts=ts, hg=hg),
        out_shape=jax.ShapeDtypeStruct((b, s, GDN_HEADS * GDN_DV), BF16),
        grid=(b, groups, s // ts),
        in_specs=[pl.BlockSpec((1, ts, hg * GDN_SLAB), lambda i, p, j: (i, j, p)),
                  pl.BlockSpec((1, ts, MISC_W), lambda i, p, j: (i, j, 0)),
                  pl.BlockSpec((CONV_WIDTH, hg * GDN_SLAB), lambda i, p, j: (0, p)),
                  pl.BlockSpec((1, GDN_DV), lambda i, p, j: (0, 0)),
                  pl.BlockSpec((CONV_WIDTH - 1, ts, ts), lambda i, p, j: (0, 0, 0)),
                  pl.BlockSpec((GDN_SLAB, GDN_SLAB), lambda i, p, j: (0, 0)),
                  pl.BlockSpec((1, LANES, 2 * hg * LANES), lambda i, p, j: (p, 0, 0))],
        out_specs=pl.BlockSpec((1, ts, hg * GDN_DV), lambda i, p, j: (i, j, p)),
        scratch_shapes=[pltpu.VMEM((16, hg * GDN_SLAB), F32),
                        pltpu.VMEM((hg, GDN_DK, GDN_DV), F32)],
        compiler_params=_cparams(("parallel", "parallel", "arbitrary")),
        name="gated_deltanet",
    )(slab, gb, conv_slab, gdn_norm_g.reshape(1, GDN_DV), shifts, ind, sel)


def _outproj_kernel(x_ref, ao_ref, go_ref, gt_ref, sc_ref, sh_ref, mg_ref, wo_ref, g2_ref,
                    rw_ref, rb_ref, x1_ref, h2_ref, route_ref, cnt_ref, carry_ref, *, tb):
    first = (pl.program_id(0) == 0) & (pl.program_id(1) == 0)

    @pl.when(first)
    def _():
        carry_ref[...] = jnp.zeros(carry_ref.shape, F32)

    mla = _rms(ao_ref[0].astype(F32), mg_ref[...])
    cat = jnp.concatenate([mla.astype(BF16), go_ref[0]], axis=1)
    mix = jnp.dot(cat, wo_ref[...], preferred_element_type=F32)
    x1 = x_ref[0] + gt_ref[0] * mix
    x1_ref[0] = x1
    h2 = _rms(x1, g2_ref[...]) * (1.0 + sc_ref[0]) + sh_ref[0]
    h2_ref[0] = h2

    h_hi = h2.astype(BF16)
    h_lo = (h2 - h_hi.astype(F32)).astype(BF16)
    main = jnp.dot(h_hi, rw_ref[...], preferred_element_type=F32)
    logits = (main[:, :LANES] + main[:, LANES:]
              + jnp.dot(h_lo, rw_ref[:, :LANES], preferred_element_type=F32) + rb_ref[...])
    lane = lax.broadcasted_iota(jnp.int32, (tb, LANES), 1).astype(F32)
    work = jnp.where(lane < N_EXPERTS, logits, -jnp.inf)
    vals, idxs = [], []
    onehot = jnp.zeros((tb, LANES), F32)
    for _ in range(TOP_K):
        mx = jnp.max(work, axis=-1, keepdims=True)
        ix = jnp.min(jnp.where(work == mx, lane, float(LANES)), axis=-1, keepdims=True)
        sel = lane == ix
        onehot = jnp.where(sel, 1.0, onehot)
        work = jnp.where(sel, -jnp.inf, work)
        vals.append(mx)
        idxs.append(ix)
    exps = [jnp.exp(v - vals[0]) for v in vals]
    den = exps[0] + exps[1] + exps[2] + exps[3]

    r = lax.broadcasted_iota(jnp.int32, (tb, tb), 0)
    c = lax.broadcasted_iota(jnp.int32, (tb, tb), 1)
    tri = jnp.where(c < r, 1.0, 0.0).astype(BF16)
    before = jnp.dot(tri, onehot.astype(BF16), preferred_element_type=F32) + carry_ref[...]
    route = jnp.zeros((tb, LANES), F32)
    for kk in range(TOP_K):
        rank = jnp.sum(jnp.where(lane == idxs[kk], before, 0.0), axis=-1, keepdims=True)
        route = jnp.where(lane == kk, idxs[kk], route)
        route = jnp.where(lane == TOP_K + kk, rank, route)
        route = jnp.where(lane == 2 * TOP_K + kk, exps[kk] / den, route)
    route_ref[0] = route
    total = carry_ref[...] + jnp.sum(onehot, axis=0, keepdims=True)
    carry_ref[...] = total
    cnt_ref[...] = total


def _out_projection(x, attn_o, gdn_o, gt1, sc2, sh2, mla_out_g, w_out, norm2_g, rw, rb, tb):
    b, s, d = x.shape
    const = lambda shape: pl.BlockSpec(shape, lambda i, j: (0,) * len(shape))
    tok = lambda w: pl.BlockSpec((1, tb, w), lambda i, j: (i, j, 0))
    per_b = pl.BlockSpec((1, 1, d), lambda i, j: (i, 0, 0))
    half = attn_o.shape[-1]
    return pl.pallas_call(
        functools.partial(_outproj_kernel, tb=tb),
        out_shape=(jax.ShapeDtypeStruct((b, s, d), F32),
                   jax.ShapeDtypeStruct((b, s, d), F32),
                   jax.ShapeDtypeStruct((b, s, LANES), F32),
                   jax.ShapeDtypeStruct((1, LANES), F32)),
        grid=(b, s // tb),
        in_specs=[tok(d), tok(half), tok(half), per_b, per_b, per_b, const((1, half)),
                  const(w_out.shape), const((1, d)), const(rw.shape), const((1, LANES))],
        out_specs=(tok(d), tok(d), tok(LANES), const((1, LANES))),
        scratch_shapes=[pltpu.VMEM((1, LANES), F32)],
        compiler_params=_cparams(("arbitrary", "arbitrary")),
        name="out_projection_router",
    )(x, attn_o, gdn_o, gt1, sc2, sh2, mla_out_g.reshape(1, half), w_out,
      norm2_g.reshape(1, d), rw, rb)


def _sc_mesh():
    return plsc.VectorSubcoreMesh(core_axis_name="c", subcore_axis_name="s",
                                  num_cores=SC_CORES, num_subcores=SC_SUBCORES)


def _sc_worker():
    return lax.axis_index("s") * SC_CORES + lax.axis_index("c")


def _dispatch_rows(h2, dest_km, rows):
    t, d = h2.shape
    per_worker = t // SC_WORKERS
    n_chunks = per_worker // SC_CHUNK

    @functools.partial(
        pl.kernel, out_type=jax.ShapeDtypeStruct((rows, d), h2.dtype), mesh=_sc_mesh(),
        scratch_types=[pltpu.VMEM((SC_CHUNK,), jnp.int32), pltpu.VMEM((SC_CHUNK, d), h2.dtype)],
        name="moe_dispatch")
    def run(h2_hbm, dest_hbm, xs_hbm, idx_v, rows_v):
        base_w = _sc_worker() * per_worker

        @pl.loop(0, n_chunks)
        def _(ci):
            base = pl.multiple_of(base_w + ci * SC_CHUNK, SC_CHUNK)
            pltpu.sync_copy(h2_hbm.at[pl.ds(base, SC_CHUNK)], rows_v)
            for kk in range(TOP_K):
                pltpu.sync_copy(dest_hbm.at[kk, pl.ds(base, SC_CHUNK)], idx_v)
                pltpu.sync_copy(rows_v, xs_hbm.at[idx_v])

    return run(h2, dest_km)


def _collect_rows(y_rows, dest_km):
    _, d = y_rows.shape
    t = dest_km.shape[1]
    per_worker = t // SC_WORKERS
    n_chunks = per_worker // SC_CHUNK

    @functools.partial(
        pl.kernel, out_type=jax.ShapeDtypeStruct((TOP_K, t, d), y_rows.dtype), mesh=_sc_mesh(),
        scratch_types=[pltpu.VMEM((SC_CHUNK,), jnp.int32), pltpu.VMEM((SC_CHUNK, d), y_rows.dtype)],
        name="moe_collect")
    def run(y_hbm, dest_hbm, out_hbm, idx_v, rows_v):
        base_w = _sc_worker() * per_worker

        @pl.loop(0, n_chunks)
        def _(ci):
            base = pl.multiple_of(base_w + ci * SC_CHUNK, SC_CHUNK)
            for kk in range(TOP_K):
                pltpu.sync_copy(dest_hbm.at[kk, pl.ds(base, SC_CHUNK)], idx_v)
                pltpu.sync_copy(y_hbm.at[idx_v], rows_v)
                pltpu.sync_copy(rows_v, out_hbm.at[kk, pl.ds(base, SC_CHUNK)])

    return run(y_rows, dest_km)


def _experts_kernel(be_ref, nu_ref, xs_ref, wgu_ref, bgu_ref, wd_ref, bd_ref, y_ref,
                    wgu_bf, wd_bf):
    i = pl.program_id(0)
    new_expert = (i == 0) | (be_ref[i] != be_ref[jnp.maximum(i - 1, 0)])

    @pl.when(new_expert)
    def _():
        wgu_bf[...] = wgu_ref[0].astype(BF16)
        wd_bf[...] = wd_ref[0].astype(BF16)

    @pl.when(i < nu_ref[0])
    def _():
        gu = jnp.dot(xs_ref[...].astype(BF16), wgu_bf[...], preferred_element_type=F32) + bgu_ref[0]
        gate = jnp.minimum(gu[:, :D_EXPERT], SWIGLU_LIMIT)
        up = jnp.clip(gu[:, D_EXPERT:], -SWIGLU_LIMIT, SWIGLU_LIMIT)
        act = (up + 1.0) * (gate * _sigmoid(SWIGLU_ALPHA * gate))
        y = jnp.dot(act.astype(BF16), wd_bf[...], preferred_element_type=F32) + bd_ref[0]
        y_ref[...] = y.astype(y_ref.dtype)


def _experts(blk_e, n_used, xs, wgu, bgu, wd, bd, bm):
    rows, d = xs.shape
    n_blocks = rows // bm
    row_map = lambda i, be, nu: (jnp.maximum(jnp.minimum(i, nu[0] - 1), 0), 0)
    exp_map = lambda i, be, nu: (be[i], 0, 0)
    grid_spec = pltpu.PrefetchScalarGridSpec(
        num_scalar_prefetch=2,
        grid=(n_blocks,),
        in_specs=[pl.BlockSpec((bm, d), row_map),
                  pl.BlockSpec((1, d, 2 * D_EXPERT), exp_map),
                  pl.BlockSpec((1, 1, 2 * D_EXPERT), exp_map),
                  pl.BlockSpec((1, D_EXPERT, d), exp_map),
                  pl.BlockSpec((1, 1, d), exp_map)],
        out_specs=pl.BlockSpec((bm, d), row_map),
        scratch_shapes=[pltpu.VMEM((d, 2 * D_EXPERT), BF16),
                        pltpu.VMEM((D_EXPERT, d), BF16)])
    return pl.pallas_call(
        _experts_kernel,
        out_shape=jax.ShapeDtypeStruct((rows, d), F32),
        grid_spec=grid_spec,
        compiler_params=_cparams(("arbitrary",)),
        name="expert_mlp",
    )(blk_e, n_used, xs, wgu, bgu, wd, bd)


def _final_kernel(x1_ref, yk_ref, route_ref, gt_ref, fg_ref, o_ref):
    route = route_ref[0]
    ffn = jnp.zeros(x1_ref.shape[1:], F32)
    for kk in range(TOP_K):
        wk = route[:, 2 * TOP_K + kk:2 * TOP_K + kk + 1]
        ffn = ffn + wk * yk_ref[kk, 0].astype(F32)
    x2 = x1_ref[0] + gt_ref[0] * ffn
    o_ref[0] = _rms(x2, fg_ref[...])


def _final(x1, yk, route, gt2, final_g, tb):
    b, s, d = x1.shape
    return pl.pallas_call(
        _final_kernel,
        out_shape=jax.ShapeDtypeStruct((b, s, d), F32),
        grid=(b, s // tb),
        in_specs=[pl.BlockSpec((1, tb, d), lambda i, j: (i, j, 0)),
                  pl.BlockSpec((TOP_K, 1, tb, d), lambda i, j: (0, i, j, 0)),
                  pl.BlockSpec((1, tb, LANES), lambda i, j: (i, j, 0)),
                  pl.BlockSpec((1, 1, d), lambda i, j: (i, 0, 0)),
                  pl.BlockSpec((1, d), lambda i, j: (0, 0))],
        out_specs=pl.BlockSpec((1, tb, d), lambda i, j: (i, j, 0)),
        compiler_params=_cparams(("parallel", "parallel")),
        name="combine_final_norm",
    )(x1, yk, route, gt2, final_g.reshape(1, d))


def _rot_cols(w):
    half = MLA_ROPE // 2
    return jnp.concatenate([-w[..., half:], w[..., :half]], axis=-1)


def _prep_weights(w_in, w_q_b, w_kv_b, conv_w, A_log, dt_bias):
    d = w_in.shape[0]
    cuts = [MLA_Q_LORA, MLA_KV_LORA, MLA_ROPE, GDN_HEADS * GDN_DK, GDN_HEADS * GDN_DK,
            GDN_HEADS * GDN_DV, GDN_HEADS * GDN_DV, GDN_HEADS, GDN_HEADS]
    offs = [0]
    for cw in cuts:
        offs.append(offs[-1] + cw)
    part = lambda i: w_in[:, offs[i]:offs[i + 1]]
    k_pe = part(2)
    misc = jnp.concatenate(
        [part(7), part(8), jnp.zeros((d, MLA_NOPE - 2 * GDN_HEADS), w_in.dtype), k_pe,
         _rot_cols(k_pe)], axis=1)
    heads = lambda w, n: w.reshape(w.shape[0], GDN_HEADS, n)
    slab = jnp.concatenate([heads(part(3), GDN_DK), heads(part(4), GDN_DK),
                            heads(part(5), GDN_DV), heads(part(6), GDN_DV)], axis=-1)
    w1 = jnp.concatenate([part(0), part(1), misc, slab.reshape(d, GDN_HEADS * GDN_SLAB)],
                         axis=1).astype(BF16)

    wq3 = w_q_b.reshape(MLA_Q_LORA, MLA_HEADS, MLA_QK)
    pe = wq3[..., MLA_NOPE:]
    wq = jnp.concatenate([wq3[..., :MLA_NOPE], pe, _rot_cols(pe)], axis=-1)
    wq = wq.reshape(MLA_Q_LORA, MLA_HEADS * HEAD_SLOT).astype(BF16)

    wkv3 = w_kv_b.reshape(MLA_KV_LORA, MLA_HEADS, MLA_NOPE + MLA_V)
    wk = jnp.concatenate([wkv3[..., :MLA_NOPE],
                          jnp.zeros((MLA_KV_LORA, MLA_HEADS, HEAD_SLOT - MLA_NOPE), w_kv_b.dtype)],
                         axis=-1).reshape(MLA_KV_LORA, MLA_HEADS * HEAD_SLOT)
    wv = wkv3[..., MLA_NOPE:].reshape(MLA_KV_LORA, MLA_HEADS * MLA_V)
    wkv = jnp.concatenate([wk, wv], axis=1).astype(BF16)

    nk = GDN_HEADS * GDN_DK
    cheads = lambda w, n: w.reshape(CONV_WIDTH, GDN_HEADS, n)
    conv_slab = jnp.concatenate(
        [cheads(conv_w[:, :nk], GDN_DK), cheads(conv_w[:, nk:2 * nk], GDN_DK),
         cheads(conv_w[:, 2 * nk:], GDN_DV), jnp.zeros((CONV_WIDTH, GDN_HEADS, GDN_DV), conv_w.dtype)],
        axis=-1).reshape(CONV_WIDTH, GDN_HEADS * GDN_SLAB)

    half = MLA_ROPE // 2
    inv_freq = ROPE_THETA ** (-jnp.arange(half, dtype=F32) / half)
    freq = jnp.concatenate([jnp.zeros((MLA_NOPE,), F32), inv_freq, inv_freq,
                            jnp.zeros((LANES - MLA_QK,), F32)]).reshape(1, LANES)
    padl = lambda a: jnp.concatenate([a.astype(F32), jnp.zeros((LANES - a.shape[0],), F32)]).reshape(1, LANES)
    return w1, wq, wkv, conv_slab, freq, padl(A_log), padl(dt_bias)


def _layer(x, mod, positions, w_in, q_norm_g, w_q_b, kv_norm_g, w_kv_b, mla_out_g, conv_w,
           A_log, dt_bias, gdn_norm_g, w_out, norm1_g, norm2_g, router_w, router_b, w_gate_up,
           b_gate_up, w_down, b_down):
    b, s, d = x.shape
    t = b * s
    sh1, sc1, gt1, sh2, sc2, gt2 = [m.reshape(b, 1, d) for m in jnp.split(mod, 6, axis=-1)]
    w1, wq, wkv, conv_slab, freq, alog, dtb = _prep_weights(w_in, w_q_b, w_kv_b, conv_w, A_log,
                                                            dt_bias)
    tb = min(256, s)
    q, k, v, slab, gb = _in_projection(x, positions, sc1, sh1, norm1_g, w1, q_norm_g, wq,
                                       kv_norm_g, wkv, freq, alog, dtb, tb)
    attn_o = _attention(q, k, v, min(512, s))
    gdn_o = _gdn(slab, gb, conv_slab, gdn_norm_g, min(256, s), 4)

    rw = jnp.concatenate([router_w, jnp.zeros((d, LANES - N_EXPERTS), router_w.dtype)], axis=1)
    rw_hi = rw.astype(BF16)
    rw = jnp.concatenate([rw_hi, (rw - rw_hi.astype(F32)).astype(BF16)], axis=1)
    rb = jnp.concatenate([router_b, jnp.zeros((LANES - N_EXPERTS,), router_b.dtype)]).reshape(1, LANES)
    x1, h2, route, counts = _out_projection(x, attn_o, gdn_o, gt1, sc2, sh2, mla_out_g,
                                            w_out.astype(BF16), norm2_g, rw, rb, tb)

    bm = 512
    route2 = route.reshape(t, LANES)
    idx = route2[:, :TOP_K].astype(jnp.int32)
    rank = route2[:, TOP_K:2 * TOP_K].astype(jnp.int32)
    cnt = counts[0, :N_EXPERTS].astype(jnp.int32)
    padded = ((cnt + bm - 1) // bm) * bm
    pend = jnp.cumsum(padded)
    pstart = pend - padded
    dest = pstart[idx] + rank
    n_blocks = (t * TOP_K + N_EXPERTS * (bm - 1) + bm - 1) // bm
    rows = n_blocks * bm
    n_used = (pend[-1] // bm).astype(jnp.int32).reshape(1)
    blk_start = jnp.arange(n_blocks, dtype=jnp.int32) * bm
    blk_e = jnp.minimum(jnp.sum(blk_start[:, None] >= pend[None, :], axis=1), N_EXPERTS - 1)
    last_e = blk_e[jnp.maximum(n_used[0] - 1, 0)]
    blk_e = jnp.where(jnp.arange(n_blocks) < n_used[0], blk_e, last_e).astype(jnp.int32)

    dest_km = dest.T
    xs = _dispatch_rows(h2.reshape(t, d), dest_km, rows)
    y_rows = _experts(blk_e, n_used, xs, w_gate_up, b_gate_up.reshape(N_EXPERTS, 1, -1),
                      w_down, b_down.reshape(N_EXPERTS, 1, -1), bm)
    yk = _collect_rows(y_rows, dest_km).reshape(TOP_K, b, s, d)
    return x1, yk, route, gt2


def kernel(x, c, positions, ada_w, ada_b, norm1_g, w_in, q_norm_g, w_q_b, kv_norm_g, w_kv_b, mla_out_g, conv_w, A_log, dt_bias, gdn_norm_g, w_out, norm2_g, router_w, router_b, w_gate_up, b_gate_up, w_down, b_down, final_g):
    depth = ada_w.shape[0]
    assert depth == 1
    l = 0
    mod = _modulation(c, ada_w[l], ada_b[l])
    x1, yk, route, gt2 = _layer(x, mod, positions, w_in[l], q_norm_g[l], w_q_b[l], kv_norm_g[l],
                         w_kv_b[l], mla_out_g[l], conv_w[l], A_log[l], dt_bias[l], gdn_norm_g[l],
                         w_out[l], norm1_g[l], norm2_g[l], router_w[l], router_b[l],
                         w_gate_up[l], b_gate_up[l], w_down[l], b_down[l])
    return _final(x1, yk, route, gt2, final_g, min(256, x.shape[1]))
```

```python
import functools

import jax
import jax.numpy as jnp
from jax import lax
from jax.experimental import pallas as pl
from jax.experimental.pallas import tpu as pltpu
from jax.experimental.pallas import tpu_sc as plsc

F32 = jnp.float32
BF16 = jnp.bfloat16

D_MODEL = 1024
EPS = 1e-6
MLA_HEADS = 8
MLA_NOPE = 64
MLA_ROPE = 32
MLA_V = 64
MLA_QK = MLA_NOPE + MLA_ROPE
MLA_Q_LORA = 384
MLA_KV_LORA = 256
ROPE_THETA = 10000.0
GDN_HEADS = 8
GDN_DK = 64
GDN_DV = 64
CONV_WIDTH = 4
CHUNK = 64
N_EXPERTS = 32
TOP_K = 4
D_EXPERT = D_MODEL
SWIGLU_ALPHA = 1.702
SWIGLU_LIMIT = 7.0

LANES = 128
HEAD_SLOT = 128
GDN_SLAB = 256
MISC_W = 128
PROJ_W = MLA_Q_LORA + MLA_KV_LORA + MISC_W + GDN_HEADS * GDN_SLAB
VMEM_LIMIT = 56 * 1024 * 1024
SC_CORES = 2
SC_SUBCORES = 16
SC_WORKERS = SC_CORES * SC_SUBCORES
SC_CHUNK = 128
ATT_STRIP = 32
LOG2E = 1.4426950408889634


def _cparams(sem):
    return pltpu.CompilerParams(dimension_semantics=sem, vmem_limit_bytes=VMEM_LIMIT)


def _rms(x, g):
    return x * lax.rsqrt(jnp.mean(x * x, axis=-1, keepdims=True) + EPS) * g


def _sigmoid(x):
    return 1.0 / (1.0 + jnp.exp(-x))


def _silu(x):
    return x * _sigmoid(x)


def _pack_halves(x):
    w = x.shape[-1] // 2
    lo = lax.bitcast_convert_type(x[:, :w].astype(BF16).astype(F32), jnp.uint32) >> 16
    hi = lax.bitcast_convert_type(x[:, w:].astype(BF16).astype(F32), jnp.uint32) & jnp.uint32(0xFFFF0000)
    return lax.bitcast_convert_type(lo | hi, F32)


def _unpack_halves(p):
    u = lax.bitcast_convert_type(p, jnp.uint32)
    return (lax.bitcast_convert_type(u << 16, F32),
            lax.bitcast_convert_type(u & jnp.uint32(0xFFFF0000), F32))


def _mod_kernel(c_ref, w_ref, b_ref, o_ref):
    c = c_ref[...]
    o_ref[...] = jnp.dot(_silu(c), w_ref[...], preferred_element_type=F32,
                         precision=lax.Precision.HIGHEST) + b_ref[...]


def _modulation(c, ada_w, ada_b):
    b, d = c.shape
    n = ada_w.shape[1]
    return pl.pallas_call(
        _mod_kernel,
        out_shape=jax.ShapeDtypeStruct((b, n), F32),
        grid=(n // d,),
        in_specs=[pl.BlockSpec((b, d), lambda j: (0, 0)),
                  pl.BlockSpec((d, d), lambda j: (0, j)),
                  pl.BlockSpec((1, d), lambda j: (0, j))],
        out_specs=pl.BlockSpec((b, d), lambda j: (0, j)),
        compiler_params=_cparams(("arbitrary",)),
        name="adaln_mod",
    )(c, ada_w, ada_b.reshape(1, n))


def _inproj_kernel(x_ref, pos_ref, sc_ref, sh_ref, g1_ref, w1_ref, qg_ref, wq_ref, kvg_ref,
                   wkv_ref, freq_ref, alog_ref, dtb_ref,
                   q_ref, k_ref, v_ref, gdn_ref, gb_ref):
    x = x_ref[0]
    h = _rms(x, g1_ref[...]) * (1.0 + sc_ref[0]) + sh_ref[0]
    proj = jnp.dot(h.astype(BF16), w1_ref[...], preferred_element_type=F32)

    tb = x.shape[0]
    lane = lax.broadcasted_iota(jnp.int32, (tb, LANES), 1)
    ang = pos_ref[0].astype(F32) * freq_ref[...]
    cosv = jnp.cos(ang)
    sinv = jnp.sin(ang)
    in_rope = (lane >= MLA_NOPE) & (lane < MLA_QK)

    scale = (MLA_QK ** -0.5) * LOG2E
    qn = _rms(proj[:, :MLA_Q_LORA], qg_ref[...])
    qa = jnp.dot(qn.astype(BF16), wq_ref[...], preferred_element_type=F32)
    cq = jnp.where(lane < MLA_NOPE, scale, jnp.where(in_rope, cosv * scale, 0.0))
    sq = sinv * scale
    cq_t = jnp.concatenate([cq] * MLA_HEADS, axis=1)
    sq_t = jnp.concatenate([sq] * MLA_HEADS, axis=1)
    width = MLA_HEADS * HEAD_SLOT
    q = qa * cq_t + pltpu.roll(qa, width - MLA_ROPE, axis=1) * sq_t
    q_ref[0] = q.astype(BF16)

    kvn = _rms(proj[:, MLA_Q_LORA:MLA_Q_LORA + MLA_KV_LORA], kvg_ref[...])
    kva = jnp.dot(kvn.astype(BF16), wkv_ref[...], preferred_element_type=F32)
    misc = proj[:, MLA_Q_LORA + MLA_KV_LORA:MLA_Q_LORA + MLA_KV_LORA + MISC_W]
    kp = misc * jnp.where(in_rope, cosv, 0.0) + pltpu.roll(misc, MISC_W - MLA_ROPE, axis=1) * sinv
    k = kva[:, :width] + jnp.concatenate([kp] * MLA_HEADS, axis=1)
    k_ref[0] = k.astype(BF16)
    v_ref[0] = kva[:, width:].astype(BF16)

    z = misc + dtb_ref[...]
    softplus = jnp.maximum(z, 0.0) + jnp.log(1.0 + jnp.exp(-jnp.abs(z)))
    g = -jnp.exp(alog_ref[...]) * softplus
    gb_ref[0] = jnp.where(lane < GDN_HEADS, g, _sigmoid(misc))

    gdn_ref[0] = proj[:, MLA_Q_LORA + MLA_KV_LORA + MISC_W:].astype(BF16)


def _in_projection(x, positions, sc1, sh1, norm1_g, w1, q_norm_g, wq, kv_norm_g, wkv,
                   freq, alog, dtb, tb):
    b, s, d = x.shape
    hw = MLA_HEADS * HEAD_SLOT
    const = lambda shape: pl.BlockSpec(shape, lambda i, j: (0,) * len(shape))
    tok = lambda w: pl.BlockSpec((1, tb, w), lambda i, j: (i, j, 0))
    per_b = pl.BlockSpec((1, 1, d), lambda i, j: (i, 0, 0))
    return pl.pallas_call(
        _inproj_kernel,
        out_shape=(jax.ShapeDtypeStruct((b, s, hw), BF16),
                   jax.ShapeDtypeStruct((b, s, hw), BF16),
                   jax.ShapeDtypeStruct((b, s, MLA_HEADS * MLA_V), BF16),
                   jax.ShapeDtypeStruct((b, s, GDN_HEADS * GDN_SLAB), BF16),
                   jax.ShapeDtypeStruct((b, s, MISC_W), F32)),
        grid=(b, s // tb),
        in_specs=[tok(d), tok(1), per_b, per_b, const((1, d)), const(w1.shape),
                  const((1, MLA_Q_LORA)), const(wq.shape), const((1, MLA_KV_LORA)),
                  const(wkv.shape), const((1, LANES)), const((1, LANES)), const((1, LANES))],
        out_specs=(tok(hw), tok(hw), tok(MLA_HEADS * MLA_V), tok(GDN_HEADS * GDN_SLAB),
                   tok(MISC_W)),
        compiler_params=_cparams(("parallel", "parallel")),
        name="in_projection",
    )(x, positions.reshape(b, s, 1), sc1, sh1, norm1_g.reshape(1, d), w1,
      q_norm_g.reshape(1, -1), wq, kv_norm_g.reshape(1, -1), wkv, freq, alog, dtb)


def _attn_kernel(q_ref, k_ref, v_ref, o_ref, s_ref, p_ref, m_ref, l_ref, acc_ref, *, tq):
    qi = pl.program_id(2)
    m_ref[...] = jnp.full(m_ref.shape, -jnp.inf, F32)
    l_ref[...] = jnp.zeros(l_ref.shape, F32)
    acc_ref[...] = jnp.zeros(acc_ref.shape, F32)

    def step(off, masked):
        vj = v_ref[0, pl.ds(off, tq), :]
        for h in range(2):
            kj = k_ref[0, pl.ds(off, tq), h * HEAD_SLOT:(h + 1) * HEAD_SLOT]
            s_ref[h] = lax.dot_general(q_ref[0, :, h * HEAD_SLOT:(h + 1) * HEAD_SLOT], kj,
                                       (((1,), (1,)), ((), ())), preferred_element_type=F32)
        n_strips = tq // ATT_STRIP

        def strip(h, r):
            rows = slice(r * ATT_STRIP, (r + 1) * ATT_STRIP)
            sc = s_ref[h, rows, :]
            if masked:
                rid = lax.broadcasted_iota(jnp.int32, (ATT_STRIP, tq), 0) + r * ATT_STRIP
                cid = lax.broadcasted_iota(jnp.int32, (ATT_STRIP, tq), 1)
                sc = jnp.where(cid <= rid, sc, -jnp.inf)
            return rows, sc

        rep = lambda col: jnp.broadcast_to(col, (col.shape[0], LANES))
        m_new = []
        for h in range(2):
            mx = jnp.concatenate([rep(jnp.max(strip(h, r)[1], axis=-1, keepdims=True))
                                  for r in range(n_strips)], axis=0)
            m_new.append(jnp.maximum(m_ref[h], mx))
        for h in range(2):
            sums = []
            for r in range(n_strips):
                rows, sc = strip(h, r)
                p = jnp.exp2(sc - jnp.concatenate([m_new[h][rows]] * (tq // LANES), axis=1))
                p_ref[h, rows, :] = p.astype(BF16)
                sums.append(rep(jnp.sum(p, axis=-1, keepdims=True)))
            alpha = jnp.exp2(m_ref[h] - m_new[h])
            l_ref[h] = alpha * l_ref[h] + jnp.concatenate(sums, axis=0)
            m_ref[h] = m_new[h]
            acc_ref[h] = alpha * acc_ref[h] + jnp.dot(p_ref[h], vj, preferred_element_type=F32)

    def body(j, carry):
        step(pl.multiple_of(j * tq, tq), False)
        return carry

    lax.fori_loop(0, qi, body, 0)
    step(pl.multiple_of(qi * tq, tq), True)
    lane = lax.broadcasted_iota(jnp.int32, (tq, 2 * MLA_V), 1)
    o0 = acc_ref[0] / l_ref[0]
    o1 = acc_ref[1] / l_ref[1]
    o_ref[0] = jnp.where(lane < MLA_V, o0, o1).astype(o_ref.dtype)


def _attention(q, k, v, tq):
    b, s, _ = q.shape
    pairs = MLA_HEADS // 2
    return pl.pallas_call(
        functools.partial(_attn_kernel, tq=tq),
        out_shape=jax.ShapeDtypeStruct((b, s, MLA_HEADS * MLA_V), BF16),
        grid=(b, pairs, s // tq),
        in_specs=[pl.BlockSpec((1, tq, 2 * HEAD_SLOT), lambda i, p, j: (i, j, p)),
                  pl.BlockSpec((1, s, 2 * HEAD_SLOT), lambda i, p, j: (i, 0, p)),
                  pl.BlockSpec((1, s, 2 * MLA_V), lambda i, p, j: (i, 0, p))],
        out_specs=pl.BlockSpec((1, tq, 2 * MLA_V), lambda i, p, j: (i, j, p)),
        scratch_shapes=[pltpu.VMEM((2, tq, tq), F32), pltpu.VMEM((2, tq, tq), BF16),
                        pltpu.VMEM((2, tq, LANES), F32), pltpu.VMEM((2, tq, LANES), F32),
                        pltpu.VMEM((2, tq, 2 * MLA_V), F32)],
        compiler_params=_cparams(("parallel", "parallel", "arbitrary")),
        name="mla_attention",
    )(q, k, v)


def _bdot(a, b):
    return jnp.dot(a.astype(BF16), b.astype(BF16), preferred_element_type=F32)


def _gdn_kernel(slab_ref, gb_ref, cw_ref, ng_ref, shf_ref, ind_ref, sel_ref, o_ref, tail_ref,
                state_ref, *, ts, hg):
    si = pl.program_id(2)
    width = hg * GDN_SLAB
    hist_rows = 8

    @pl.when(si == 0)
    def _():
        tail_ref[...] = jnp.zeros(tail_ref.shape, F32)
        state_ref[...] = jnp.zeros(state_ref.shape, F32)

    xs_bf = slab_ref[0]
    xs = xs_bf.astype(F32)
    conv = cw_ref[CONV_WIDTH - 1:CONV_WIDTH, :] * xs
    hist = jnp.zeros((hist_rows, width), F32)
    for j in range(CONV_WIDTH - 1):
        wj = cw_ref[j:j + 1, :]
        conv = conv + wj * jnp.dot(shf_ref[j], xs_bf, preferred_element_type=F32)
        start = hist_rows - (CONV_WIDTH - 1) + j
        hist = hist + wj * tail_ref[start:start + hist_rows, :]
    conv = jnp.concatenate([conv[:hist_rows] + hist, conv[hist_rows:]], axis=0)
    tail_ref[0:hist_rows, :] = xs[ts - hist_rows:ts, :]
    act = _silu(conv)

    gb = gb_ref[0]
    g1 = gb.astype(BF16)
    r1 = gb - g1.astype(F32)
    g2 = r1.astype(BF16)
    g3 = (r1 - g2.astype(F32)).astype(BF16)
    sel = sel_ref[0]
    gate_w = (jnp.dot(g1, sel, preferred_element_type=F32)
              + jnp.dot(g2, sel, preferred_element_type=F32)
              + jnp.dot(g3, sel, preferred_element_type=F32))

    n_chunks = ts // CHUNK
    row = lax.broadcasted_iota(jnp.int32, (CHUNK, CHUNK), 0)
    col = lax.broadcasted_iota(jnp.int32, (CHUNK, CHUNK), 1)
    tri_incl = col <= row
    tri_strict = col < row
    eye = col == row
    crow = lax.broadcasted_iota(jnp.int32, (ts, LANES), 0) % CHUNK
    slab_lane = lax.broadcasted_iota(jnp.int32, (ts, GDN_SLAB), 1)

    heads = []
    for hh in range(hg):
        sl_act = act[:, hh * GDN_SLAB:(hh + 1) * GDN_SLAB]
        sq = sl_act * sl_act
        sq_hi = sq.astype(BF16)
        sq_lo = (sq - sq_hi.astype(F32)).astype(BF16)
        ss = (jnp.dot(sq_hi, ind_ref[...], preferred_element_type=F32)
              + jnp.dot(sq_lo, ind_ref[...], preferred_element_type=F32))
        inv = lax.rsqrt(ss + EPS)
        factor = jnp.where(slab_lane < GDN_DK, inv * (GDN_DK ** -0.5),
                           jnp.where(slab_lane < 2 * GDN_DK, inv, 1.0))
        normed = sl_act * factor
        z_all = xs[:, hh * GDN_SLAB + 3 * GDN_DK:(hh + 1) * GDN_SLAB]
        g_w = gate_w[:, hh * LANES:(hh + 1) * LANES]
        b_w = gate_w[:, (hg + hh) * LANES:(hg + hh + 1) * LANES]
        gc_w = g_w
        shift = 1
        while shift < CHUNK:
            rolled = pltpu.roll(gc_w, shift, axis=0)
            gc_w = gc_w + jnp.where(crow >= shift, rolled, 0.0)
            shift *= 2
        heads.append((normed, z_all, b_w, gc_w, jnp.exp(gc_w)))

    units = [(hh, c) for c in range(n_chunks) for hh in range(hg)]
    a_low, qk, y0, qd, kt, cd = {}, {}, {}, {}, {}, {}
    lo, hi = slice(0, GDN_DK), slice(GDN_DK, 2 * GDN_DK)
    for u in units:
        hh, c = u
        normed, _, b_w, gc_w, eg_w = heads[hh]
        sl = slice(c * CHUNK, (c + 1) * CHUNK)
        qc, kc, vc = normed[sl, lo], normed[sl, hi], normed[sl, 2 * GDN_DK:2 * GDN_DK + GDN_DV]
        gcol = gc_w[sl, lo]
        grow = jnp.sum(jnp.where(eye, gcol, 0.0), axis=0, keepdims=True)
        diff = gcol - grow
        decay = jnp.where(tri_incl, jnp.exp(jnp.where(tri_incl, diff, 0.0)), 0.0)
        k_beta = kc * b_w[sl, hi]
        kq = jnp.concatenate([k_beta, qc], axis=0).astype(BF16)
        kk = lax.dot_general(kq, kc.astype(BF16), (((1,), (1,)), ((), ())),
                             preferred_element_type=F32)
        a_low[u] = jnp.where(tri_strict, kk[:CHUNK] * decay, 0.0)
        qk[u] = jnp.where(tri_incl, kk[CHUNK:] * decay, 0.0)
        y0[u] = jnp.concatenate([vc * b_w[sl, lo], k_beta * eg_w[sl, hi]], axis=1)
        qd[u] = qc * eg_w[sl, lo]
        glast_hi = gc_w[(c + 1) * CHUNK - 1:(c + 1) * CHUNK, hi]
        kt[u] = kc * jnp.exp(glast_hi - gc_w[sl, hi])
        cd[u] = eg_w[(c + 1) * CHUNK - 1:(c + 1) * CHUNK, lo]

    blk = 2
    tinv = {u: jnp.where(eye, 1.0, 0.0) - jnp.where((row // blk) == (col // blk), a_low[u], 0.0)
            for u in units}
    while blk < CHUNK:
        in_big = (row // (2 * blk)) == (col // (2 * blk))
        off_mask = in_big & ((row // blk) != (col // blk))
        left = {u: _bdot(tinv[u], jnp.where(off_mask, a_low[u], 0.0)) for u in units}
        tinv = {u: tinv[u] - _bdot(left[u], tinv[u]) for u in units}
        blk *= 2

    y = {u: _bdot(tinv[u], y0[u]) for u in units}
    ky = {u: lax.dot_general(kt[u].astype(BF16), y[u].astype(BF16), (((0,), (0,)), ((), ())),
                             preferred_element_type=F32) for u in units}
    qy = {u: _bdot(qk[u], y[u]) for u in units}

    outs = [[] for _ in range(hg)]
    states = [state_ref[hh] for hh in range(hg)]
    for c in range(n_chunks):
        for hh in range(hg):
            u = (hh, c)
            sb = states[hh].astype(BF16)
            r_mat = qd[u] - qy[u][:, GDN_DV:]
            outs[hh].append(jnp.dot(r_mat.astype(BF16), sb, preferred_element_type=F32)
                            + qy[u][:, :GDN_DV])
            states[hh] = (states[hh] * cd[u]
                          - jnp.dot(ky[u][:, GDN_DV:].astype(BF16), sb,
                                    preferred_element_type=F32)
                          + ky[u][:, :GDN_DV])
    finals = []
    for hh in range(hg):
        state_ref[hh] = states[hh]
        o_all = jnp.concatenate(outs[hh], axis=0)
        finals.append(_rms(o_all, ng_ref[...]) * _silu(heads[hh][1]))
    o_ref[0] = jnp.concatenate(finals, axis=1).astype(o_ref.dtype)


def _gdn(slab, gb, conv_slab, gdn_norm_g, ts, hg):
    b, s, _ = slab.shape
    groups = GDN_HEADS // hg
    r = jnp.arange(ts)
    shifts = jnp.stack([(r[:, None] - r[None, :]) == (CONV_WIDTH - 1 - j)
                        for j in range(CONV_WIDTH - 1)]).astype(BF16)
    li = jnp.arange(GDN_SLAB)
    ind = ((li[:, None] // GDN_DK == li[None, :] // GDN_DK)
           & (li[:, None] < 2 * GDN_DK) & (li[None, :] < 2 * GDN_DK)).astype(BF16)
    src = jnp.arange(LANES)[None, :, None]
    dst = jnp.arange(2 * hg * LANES)[None, None, :]
    grp = jnp.arange(groups)[:, None, None]
    want = jnp.where(dst < hg * LANES, grp * hg + dst // LANES,
                     GDN_HEADS + grp * hg + (dst - hg * LANES) // LANES)
    sel = (src == want).astype(BF16)
    return pl.pallas_call(
        functools.partial(_gdn_kernel, ts=ts, hg=hg),
        out_shape=jax.ShapeDtypeStruct((b, s, GDN_HEADS * GDN_DV), BF16),
        grid=(b, groups, s // ts),
        in_specs=[pl.BlockSpec((1, ts, hg * GDN_SLAB), lambda i, p, j: (i, j, p)),
                  pl.BlockSpec((1, ts, MISC_W), lambda i, p, j: (i, j, 0)),
                  pl.BlockSpec((CONV_WIDTH, hg * GDN_SLAB), lambda i, p, j: (0, p)),
                  pl.BlockSpec((1, GDN_DV), lambda i, p, j: (0, 0)),
                  pl.BlockSpec((CONV_WIDTH - 1, ts, ts), lambda i, p, j: (0, 0, 0)),
                  pl.BlockSpec((GDN_SLAB, GDN_SLAB), lambda i, p, j: (0, 0)),
                  pl.BlockSpec((1, LANES, 2 * hg * LANES), lambda i, p, j: (p, 0, 0))],
        out_specs=pl.BlockSpec((1, ts, hg * GDN_DV), lambda i, p, j: (i, j, p)),
        scratch_shapes=[pltpu.VMEM((16, hg * GDN_SLAB), F32),
                        pltpu.VMEM((hg, GDN_DK, GDN_DV), F32)],
        compiler_params=_cparams(("parallel", "parallel", "arbitrary")),
        name="gated_deltanet",
    )(slab, gb, conv_slab, gdn_norm_g.reshape(1, GDN_DV), shifts, ind, sel)


def _outproj_kernel(x_ref, ao_ref, go_ref, gt_ref, sc_ref, sh_ref, mg_ref, wo_ref, g2_ref,
                    rw_ref, rb_ref, x1_ref, h2_ref, route_ref, cnt_ref, carry_ref, *, tb):
    first = (pl.program_id(0) == 0) & (pl.program_id(1) == 0)

    @pl.when(first)
    def _():
        carry_ref[...] = jnp.zeros(carry_ref.shape, F32)

    mla = _rms(ao_ref[0].astype(F32), mg_ref[...])
    cat = jnp.concatenate([mla.astype(BF16), go_ref[0]], axis=1)
    mix = jnp.dot(cat, wo_ref[...], preferred_element_type=F32)
    x1 = x_ref[0] + gt_ref[0] * mix
    x1_ref[0] = x1
    h2 = _rms(x1, g2_ref[...]) * (1.0 + sc_ref[0]) + sh_ref[0]
    h2_ref[0] = _pack_halves(h2)

    h_hi = h2.astype(BF16)
    h_lo = (h2 - h_hi.astype(F32)).astype(BF16)
    main = jnp.dot(h_hi, rw_ref[...], preferred_element_type=F32)
    logits = (main[:, :LANES] + main[:, LANES:]
              + jnp.dot(h_lo, rw_ref[:, :LANES], preferred_element_type=F32) + rb_ref[...])
    lane = lax.broadcasted_iota(jnp.int32, (tb, LANES), 1).astype(F32)
    work = jnp.where(lane < N_EXPERTS, logits, -jnp.inf)
    vals, idxs = [], []
    onehot = jnp.zeros((tb, LANES), F32)
    for _ in range(TOP_K):
        mx = jnp.max(work, axis=-1, keepdims=True)
        ix = jnp.min(jnp.where(work == mx, lane, float(LANES)), axis=-1, keepdims=True)
        sel = lane == ix
        onehot = jnp.where(sel, 1.0, onehot)
        work = jnp.where(sel, -jnp.inf, work)
        vals.append(mx)
        idxs.append(ix)
    exps = [jnp.exp(v - vals[0]) for v in vals]
    den = exps[0] + exps[1] + exps[2] + exps[3]

    r = lax.broadcasted_iota(jnp.int32, (tb, tb), 0)
    c = lax.broadcasted_iota(jnp.int32, (tb, tb), 1)
    tri = jnp.where(c < r, 1.0, 0.0).astype(BF16)
    before = jnp.dot(tri, onehot.astype(BF16), preferred_element_type=F32) + carry_ref[...]
    route = jnp.zeros((tb, LANES), F32)
    for kk in range(TOP_K):
        rank = jnp.sum(jnp.where(lane == idxs[kk], before, 0.0), axis=-1, keepdims=True)
        route = jnp.where(lane == kk, idxs[kk], route)
        route = jnp.where(lane == TOP_K + kk, rank, route)
        route = jnp.where(lane == 2 * TOP_K + kk, exps[kk] / den, route)
    route_ref[0] = route
    total = carry_ref[...] + jnp.sum(onehot, axis=0, keepdims=True)
    carry_ref[...] = total
    cnt_ref[...] = total


def _out_projection(x, attn_o, gdn_o, gt1, sc2, sh2, mla_out_g, w_out, norm2_g, rw, rb, tb):
    b, s, d = x.shape
    const = lambda shape: pl.BlockSpec(shape, lambda i, j: (0,) * len(shape))
    tok = lambda w: pl.BlockSpec((1, tb, w), lambda i, j: (i, j, 0))
    per_b = pl.BlockSpec((1, 1, d), lambda i, j: (i, 0, 0))
    half = attn_o.shape[-1]
    return pl.pallas_call(
        functools.partial(_outproj_kernel, tb=tb),
        out_shape=(jax.ShapeDtypeStruct((b, s, d), F32),
                   jax.ShapeDtypeStruct((b, s, d // 2), F32),
                   jax.ShapeDtypeStruct((b, s, LANES), F32),
                   jax.ShapeDtypeStruct((1, LANES), F32)),
        grid=(b, s // tb),
        in_specs=[tok(d), tok(half), tok(half), per_b, per_b, per_b, const((1, half)),
                  const(w_out.shape), const((1, d)), const(rw.shape), const((1, LANES))],
        out_specs=(tok(d), tok(d // 2), tok(LANES), const((1, LANES))),
        scratch_shapes=[pltpu.VMEM((1, LANES), F32)],
        compiler_params=_cparams(("arbitrary", "arbitrary")),
        name="out_projection_router",
    )(x, attn_o, gdn_o, gt1, sc2, sh2, mla_out_g.reshape(1, half), w_out,
      norm2_g.reshape(1, d), rw, rb)


def _sc_mesh():
    return plsc.VectorSubcoreMesh(core_axis_name="c", subcore_axis_name="s",
                                  num_cores=SC_CORES, num_subcores=SC_SUBCORES)


def _sc_worker():
    return lax.axis_index("s") * SC_CORES + lax.axis_index("c")


def _dispatch_rows(h2, dest_km, rows):
    t, d = h2.shape
    per_worker = t // SC_WORKERS
    n_chunks = per_worker // SC_CHUNK

    @functools.partial(
        pl.kernel, out_type=jax.ShapeDtypeStruct((rows, d), h2.dtype), mesh=_sc_mesh(),
        scratch_types=[pltpu.VMEM((SC_CHUNK,), jnp.int32), pltpu.VMEM((SC_CHUNK, d), h2.dtype)],
        name="moe_dispatch")
    def run(h2_hbm, dest_hbm, xs_hbm, idx_v, rows_v):
        base_w = _sc_worker() * per_worker

        @pl.loop(0, n_chunks)
        def _(ci):
            base = pl.multiple_of(base_w + ci * SC_CHUNK, SC_CHUNK)
            pltpu.sync_copy(h2_hbm.at[pl.ds(base, SC_CHUNK)], rows_v)
            for kk in range(TOP_K):
                pltpu.sync_copy(dest_hbm.at[kk, pl.ds(base, SC_CHUNK)], idx_v)
                pltpu.sync_copy(rows_v, xs_hbm.at[idx_v])

    return run(h2, dest_km)


def _collect_rows(y_rows, dest_km):
    _, d = y_rows.shape
    t = dest_km.shape[1]
    per_worker = t // SC_WORKERS
    n_chunks = per_worker // SC_CHUNK

    @functools.partial(
        pl.kernel, out_type=jax.ShapeDtypeStruct((TOP_K, t, d), y_rows.dtype), mesh=_sc_mesh(),
        scratch_types=[pltpu.VMEM((SC_CHUNK,), jnp.int32), pltpu.VMEM((SC_CHUNK, d), y_rows.dtype)],
        name="moe_collect")
    def run(y_hbm, dest_hbm, out_hbm, idx_v, rows_v):
        base_w = _sc_worker() * per_worker

        @pl.loop(0, n_chunks)
        def _(ci):
            base = pl.multiple_of(base_w + ci * SC_CHUNK, SC_CHUNK)
            for kk in range(TOP_K):
                pltpu.sync_copy(dest_hbm.at[kk, pl.ds(base, SC_CHUNK)], idx_v)
                pltpu.sync_copy(y_hbm.at[idx_v], rows_v)
                pltpu.sync_copy(rows_v, out_hbm.at[kk, pl.ds(base, SC_CHUNK)])

    return run(y_rows, dest_km)


def _experts_kernel(be_ref, nu_ref, xs_ref, wgu_ref, bgu_ref, wd_ref, bd_ref, y_ref,
                    wgu_bf, wd_bf):
    i = pl.program_id(0)
    new_expert = (i == 0) | (be_ref[i] != be_ref[jnp.maximum(i - 1, 0)])

    @pl.when(new_expert)
    def _():
        wgu_bf[...] = wgu_ref[0].astype(BF16)
        wd_bf[...] = wd_ref[0].astype(BF16)

    @pl.when(i < nu_ref[0])
    def _():
        x_lo, x_hi = _unpack_halves(xs_ref[...])
        half = x_lo.shape[-1]
        gu = (jnp.dot(x_lo.astype(BF16), wgu_bf[:half, :], preferred_element_type=F32)
              + jnp.dot(x_hi.astype(BF16), wgu_bf[half:, :], preferred_element_type=F32)
              + bgu_ref[0])
        gate = jnp.minimum(gu[:, :D_EXPERT], SWIGLU_LIMIT)
        up = jnp.clip(gu[:, D_EXPERT:], -SWIGLU_LIMIT, SWIGLU_LIMIT)
        act = (up + 1.0) * (gate * _sigmoid(SWIGLU_ALPHA * gate))
        y = jnp.dot(act.astype(BF16), wd_bf[...], preferred_element_type=F32) + bd_ref[0]
        y_ref[...] = _pack_halves(y)


def _experts(blk_e, n_used, xs, wgu, bgu, wd, bd, bm):
    rows, half = xs.shape
    d = 2 * half
    n_blocks = rows // bm
    row_map = lambda i, be, nu: (jnp.maximum(jnp.minimum(i, nu[0] - 1), 0), 0)
    exp_map = lambda i, be, nu: (be[i], 0, 0)
    grid_spec = pltpu.PrefetchScalarGridSpec(
        num_scalar_prefetch=2,
        grid=(n_blocks,),
        in_specs=[pl.BlockSpec((bm, half), row_map),
                  pl.BlockSpec((1, d, 2 * D_EXPERT), exp_map),
                  pl.BlockSpec((1, 1, 2 * D_EXPERT), exp_map),
                  pl.BlockSpec((1, D_EXPERT, d), exp_map),
                  pl.BlockSpec((1, 1, d), exp_map)],
        out_specs=pl.BlockSpec((bm, half), row_map),
        scratch_shapes=[pltpu.VMEM((d, 2 * D_EXPERT), BF16),
                        pltpu.VMEM((D_EXPERT, d), BF16)])
    return pl.pallas_call(
        _experts_kernel,
        out_shape=jax.ShapeDtypeStruct((rows, half), F32),
        grid_spec=grid_spec,
        compiler_params=_cparams(("arbitrary",)),
        name="expert_mlp",
    )(blk_e, n_used, xs, wgu, bgu, wd, bd)


def _final_kernel(x1_ref, yk_ref, route_ref, gt_ref, fg_ref, o_ref):
    route = route_ref[0]
    ffn = jnp.zeros(x1_ref.shape[1:], F32)
    for kk in range(TOP_K):
        wk = route[:, 2 * TOP_K + kk:2 * TOP_K + kk + 1]
        y_lo, y_hi = _unpack_halves(yk_ref[kk, 0])
        ffn = ffn + wk * jnp.concatenate([y_lo, y_hi], axis=1)
    x2 = x1_ref[0] + gt_ref[0] * ffn
    o_ref[0] = _rms(x2, fg_ref[...])


def _final(x1, yk, route, gt2, final_g, tb):
    b, s, d = x1.shape
    return pl.pallas_call(
        _final_kernel,
        out_shape=jax.ShapeDtypeStruct((b, s, d), F32),
        grid=(b, s // tb),
        in_specs=[pl.BlockSpec((1, tb, d), lambda i, j: (i, j, 0)),
                  pl.BlockSpec((TOP_K, 1, tb, d // 2), lambda i, j: (0, i, j, 0)),
                  pl.BlockSpec((1, tb, LANES), lambda i, j: (i, j, 0)),
                  pl.BlockSpec((1, 1, d), lambda i, j: (i, 0, 0)),
                  pl.BlockSpec((1, d), lambda i, j: (0, 0))],
        out_specs=pl.BlockSpec((1, tb, d), lambda i, j: (i, j, 0)),
        compiler_params=_cparams(("parallel", "parallel")),
        name="combine_final_norm",
    )(x1, yk, route, gt2, final_g.reshape(1, d))


def _rot_cols(w):
    half = MLA_ROPE // 2
    return jnp.concatenate([-w[..., half:], w[..., :half]], axis=-1)


def _prep_weights(w_in, w_q_b, w_kv_b, conv_w, A_log, dt_bias):
    d = w_in.shape[0]
    cuts = [MLA_Q_LORA, MLA_KV_LORA, MLA_ROPE, GDN_HEADS * GDN_DK, GDN_HEADS * GDN_DK,
            GDN_HEADS * GDN_DV, GDN_HEADS * GDN_DV, GDN_HEADS, GDN_HEADS]
    offs = [0]
    for cw in cuts:
        offs.append(offs[-1] + cw)
    part = lambda i: w_in[:, offs[i]:offs[i + 1]]
    k_pe = part(2)
    misc = jnp.concatenate(
        [part(7), part(8), jnp.zeros((d, MLA_NOPE - 2 * GDN_HEADS), w_in.dtype), k_pe,
         _rot_cols(k_pe)], axis=1)
    heads = lambda w, n: w.reshape(w.shape[0], GDN_HEADS, n)
    slab = jnp.concatenate([heads(part(3), GDN_DK), heads(part(4), GDN_DK),
                            heads(part(5), GDN_DV), heads(part(6), GDN_DV)], axis=-1)
    w1 = jnp.concatenate([part(0), part(1), misc, slab.reshape(d, GDN_HEADS * GDN_SLAB)],
                         axis=1).astype(BF16)

    wq3 = w_q_b.reshape(MLA_Q_LORA, MLA_HEADS, MLA_QK)
    pe = wq3[..., MLA_NOPE:]
    wq = jnp.concatenate([wq3[..., :MLA_NOPE], pe, _rot_cols(pe)], axis=-1)
    wq = wq.reshape(MLA_Q_LORA, MLA_HEADS * HEAD_SLOT).astype(BF16)

    wkv3 = w_kv_b.reshape(MLA_KV_LORA, MLA_HEADS, MLA_NOPE + MLA_V)
    wk = jnp.concatenate([wkv3[..., :MLA_NOPE],
                          jnp.zeros((MLA_KV_LORA, MLA_HEADS, HEAD_SLOT - MLA_NOPE), w_kv_b.dtype)],
                         axis=-1).reshape(MLA_KV_LORA, MLA_HEADS * HEAD_SLOT)
    wv = wkv3[..., MLA_NOPE:].reshape(MLA_KV_LORA, MLA_HEADS * MLA_V)
    wkv = jnp.concatenate([wk, wv], axis=1).astype(BF16)

    nk = GDN_HEADS * GDN_DK
    cheads = lambda w, n: w.reshape(CONV_WIDTH, GDN_HEADS, n)
    conv_slab = jnp.concatenate(
        [cheads(conv_w[:, :nk], GDN_DK), cheads(conv_w[:, nk:2 * nk], GDN_DK),
         cheads(conv_w[:, 2 * nk:], GDN_DV), jnp.zeros((CONV_WIDTH, GDN_HEADS, GDN_DV), conv_w.dtype)],
        axis=-1).reshape(CONV_WIDTH, GDN_HEADS * GDN_SLAB)

    half = MLA_ROPE // 2
    inv_freq = ROPE_THETA ** (-jnp.arange(half, dtype=F32) / half)
    freq = jnp.concatenate([jnp.zeros((MLA_NOPE,), F32), inv_freq, inv_freq,
                            jnp.zeros((LANES - MLA_QK,), F32)]).reshape(1, LANES)
    padl = lambda a: jnp.concatenate([a.astype(F32), jnp.zeros((LANES - a.shape[0],), F32)]).reshape(1, LANES)
    return w1, wq, wkv, conv_slab, freq, padl(A_log), padl(dt_bias)


def _layer(x, mod, positions, w_in, q_norm_g, w_q_b, kv_norm_g, w_kv_b, mla_out_g, conv_w,
           A_log, dt_bias, gdn_norm_g, w_out, norm1_g, norm2_g, router_w, router_b, w_gate_up,
           b_gate_up, w_down, b_down):
    b, s, d = x.shape
    t = b * s
    sh1, sc1, gt1, sh2, sc2, gt2 = [m.reshape(b, 1, d) for m in jnp.split(mod, 6, axis=-1)]
    w1, wq, wkv, conv_slab, freq, alog, dtb = _prep_weights(w_in, w_q_b, w_kv_b, conv_w, A_log,
                                                            dt_bias)
    tb = min(256, s)
    q, k, v, slab, gb = _in_projection(x, positions, sc1, sh1, norm1_g, w1, q_norm_g, wq,
                                       kv_norm_g, wkv, freq, alog, dtb, tb)
    attn_o = _attention(q, k, v, min(512, s))
    gdn_o = _gdn(slab, gb, conv_slab, gdn_norm_g, min(256, s), 4)

    rw = jnp.concatenate([router_w, jnp.zeros((d, LANES - N_EXPERTS), router_w.dtype)], axis=1)
    rw_hi = rw.astype(BF16)
    rw = jnp.concatenate([rw_hi, (rw - rw_hi.astype(F32)).astype(BF16)], axis=1)
    rb = jnp.concatenate([router_b, jnp.zeros((LANES - N_EXPERTS,), router_b.dtype)]).reshape(1, LANES)
    x1, h2, route, counts = _out_projection(x, attn_o, gdn_o, gt1, sc2, sh2, mla_out_g,
                                            w_out.astype(BF16), norm2_g, rw, rb, tb)

    bm = 512
    route2 = route.reshape(t, LANES)
    idx = route2[:, :TOP_K].astype(jnp.int32)
    rank = route2[:, TOP_K:2 * TOP_K].astype(jnp.int32)
    cnt = counts[0, :N_EXPERTS].astype(jnp.int32)
    padded = ((cnt + bm - 1) // bm) * bm
    pend = jnp.cumsum(padded)
    pstart = pend - padded
    dest = pstart[idx] + rank
    n_blocks = (t * TOP_K + N_EXPERTS * (bm - 1) + bm - 1) // bm
    rows = n_blocks * bm
    n_used = (pend[-1] // bm).astype(jnp.int32).reshape(1)
    blk_start = jnp.arange(n_blocks, dtype=jnp.int32) * bm
    blk_e = jnp.minimum(jnp.sum(blk_start[:, None] >= pend[None, :], axis=1), N_EXPERTS - 1)
    last_e = blk_e[jnp.maximum(n_used[0] - 1, 0)]
    blk_e = jnp.where(jnp.arange(n_blocks) < n_used[0], blk_e, last_e).astype(jnp.int32)

    dest_km = dest.T
    xs = _dispatch_rows(h2.reshape(t, d // 2), dest_km, rows)
    y_rows = _experts(blk_e, n_used, xs, w_gate_up, b_gate_up.reshape(N_EXPERTS, 1, -1),
                      w_down, b_down.reshape(N_EXPERTS, 1, -1), bm)
    yk = _collect_rows(y_rows, dest_km).reshape(TOP_K, b, s, d // 2)
    return x1, yk, route, gt2


def kernel(x, c, positions, ada_w, ada_b, norm1_g, w_in, q_norm_g, w_q_b, kv_norm_g, w_kv_b, mla_out_g, conv_w, A_log, dt_bias, gdn_norm_g, w_out, norm2_g, router_w, router_b, w_gate_up, b_gate_up, w_down, b_down, final_g):
    depth = ada_w.shape[0]
    assert depth == 1
    l = 0
    mod = _modulation(c, ada_w[l], ada_b[l])
    x1, yk, route, gt2 = _layer(x, mod, positions, w_in[l], q_norm_g[l], w_q_b[l], kv_norm_g[l],
                         w_kv_b[l], mla_out_g[l], conv_w[l], A_log[l], dt_bias[l], gdn_norm_g[l],
                         w_out[l], norm1_g[l], norm2_g[l], router_w[l], router_b[l],
                         w_gate_up[l], b_gate_up[l], w_down[l], b_down[l])
    return _final(x1, yk, route, gt2, final_g, min(256, x.shape[1]))
```

```python
import functools

import jax
import jax.numpy as jnp
from jax import lax
from jax.experimental import pallas as pl
from jax.experimental.pallas import tpu as pltpu
from jax.experimental.pallas import tpu_sc as plsc

F32 = jnp.float32
BF16 = jnp.bfloat16

D_MODEL = 1024
EPS = 1e-6
MLA_HEADS = 8
MLA_NOPE = 64
MLA_ROPE = 32
MLA_V = 64
MLA_QK = MLA_NOPE + MLA_ROPE
MLA_Q_LORA = 384
MLA_KV_LORA = 256
ROPE_THETA = 10000.0
GDN_HEADS = 8
GDN_DK = 64
GDN_DV = 64
CONV_WIDTH = 4
CHUNK = 64
N_EXPERTS = 32
TOP_K = 4
D_EXPERT = D_MODEL
SWIGLU_ALPHA = 1.702
SWIGLU_LIMIT = 7.0

LANES = 128
HEAD_SLOT = 128
GDN_SLAB = 256
MISC_W = 128
PROJ_W = MLA_Q_LORA + MLA_KV_LORA + MISC_W + GDN_HEADS * GDN_SLAB
VMEM_LIMIT = 56 * 1024 * 1024
SC_CORES = 2
SC_SUBCORES = 16
SC_WORKERS = SC_CORES * SC_SUBCORES
SC_CHUNK = 128
ATT_STRIP = 32
LOG2E = 1.4426950408889634


def _cparams(sem):
    return pltpu.CompilerParams(dimension_semantics=sem, vmem_limit_bytes=VMEM_LIMIT)


def _rms(x, g):
    return x * lax.rsqrt(jnp.mean(x * x, axis=-1, keepdims=True) + EPS) * g


def _sigmoid(x):
    return 1.0 / (1.0 + jnp.exp(-x))


def _silu(x):
    return x * _sigmoid(x)


def _pack_halves(x):
    w = x.shape[-1] // 2
    lo = lax.bitcast_convert_type(x[:, :w].astype(BF16).astype(F32), jnp.uint32) >> 16
    hi = lax.bitcast_convert_type(x[:, w:].astype(BF16).astype(F32), jnp.uint32) & jnp.uint32(0xFFFF0000)
    return lax.bitcast_convert_type(lo | hi, F32)


def _unpack_halves(p):
    u = lax.bitcast_convert_type(p, jnp.uint32)
    return (lax.bitcast_convert_type(u << 16, F32),
            lax.bitcast_convert_type(u & jnp.uint32(0xFFFF0000), F32))


def _mod_kernel(c_ref, w_ref, b_ref, o_ref):
    c = c_ref[...]
    o_ref[...] = jnp.dot(_silu(c), w_ref[...], preferred_element_type=F32,
                         precision=lax.Precision.HIGHEST) + b_ref[...]


def _modulation(c, ada_w, ada_b):
    b, d = c.shape
    n = ada_w.shape[1]
    return pl.pallas_call(
        _mod_kernel,
        out_shape=jax.ShapeDtypeStruct((b, n), F32),
        grid=(n // d,),
        in_specs=[pl.BlockSpec((b, d), lambda j: (0, 0)),
                  pl.BlockSpec((d, d), lambda j: (0, j)),
                  pl.BlockSpec((1, d), lambda j: (0, j))],
        out_specs=pl.BlockSpec((b, d), lambda j: (0, j)),
        compiler_params=_cparams(("arbitrary",)),
        name="adaln_mod",
    )(c, ada_w, ada_b.reshape(1, n))


def _inproj_kernel(x_ref, pos_ref, sc_ref, sh_ref, g1_ref, w1_ref, qg_ref, wq_ref, kvg_ref,
                   wkv_ref, freq_ref, alog_ref, dtb_ref,
                   q_ref, k_ref, v_ref, gdn_ref, gb_ref):
    x = x_ref[0]
    h = _rms(x, g1_ref[...]) * (1.0 + sc_ref[0]) + sh_ref[0]
    proj = jnp.dot(h.astype(BF16), w1_ref[...], preferred_element_type=F32)

    tb = x.shape[0]
    lane = lax.broadcasted_iota(jnp.int32, (tb, LANES), 1)
    ang = pos_ref[0].astype(F32) * freq_ref[...]
    cosv = jnp.cos(ang)
    sinv = jnp.sin(ang)
    in_rope = (lane >= MLA_NOPE) & (lane < MLA_QK)

    scale = (MLA_QK ** -0.5) * LOG2E
    qn = _rms(proj[:, :MLA_Q_LORA], qg_ref[...])
    qa = jnp.dot(qn.astype(BF16), wq_ref[...], preferred_element_type=F32)
    cq = jnp.where(lane < MLA_NOPE, scale, jnp.where(in_rope, cosv * scale, 0.0))
    sq = sinv * scale
    cq_t = jnp.concatenate([cq] * MLA_HEADS, axis=1)
    sq_t = jnp.concatenate([sq] * MLA_HEADS, axis=1)
    width = MLA_HEADS * HEAD_SLOT
    q = qa * cq_t + pltpu.roll(qa, width - MLA_ROPE, axis=1) * sq_t
    q_ref[0] = q.astype(BF16)

    kvn = _rms(proj[:, MLA_Q_LORA:MLA_Q_LORA + MLA_KV_LORA], kvg_ref[...])
    kva = jnp.dot(kvn.astype(BF16), wkv_ref[...], preferred_element_type=F32)
    misc = proj[:, MLA_Q_LORA + MLA_KV_LORA:MLA_Q_LORA + MLA_KV_LORA + MISC_W]
    kp = misc * jnp.where(in_rope, cosv, 0.0) + pltpu.roll(misc, MISC_W - MLA_ROPE, axis=1) * sinv
    k = kva[:, :width] + jnp.concatenate([kp] * MLA_HEADS, axis=1)
    k_ref[0] = k.astype(BF16)
    v_ref[0] = kva[:, width:].astype(BF16)

    z = misc + dtb_ref[...]
    softplus = jnp.maximum(z, 0.0) + jnp.log(1.0 + jnp.exp(-jnp.abs(z)))
    g = -jnp.exp(alog_ref[...]) * softplus
    gb_ref[0] = jnp.where(lane < GDN_HEADS, g, _sigmoid(misc))

    gdn_ref[0] = proj[:, MLA_Q_LORA + MLA_KV_LORA + MISC_W:].astype(BF16)


def _in_projection(x, positions, sc1, sh1, norm1_g, w1, q_norm_g, wq, kv_norm_g, wkv,
                   freq, alog, dtb, tb):
    b, s, d = x.shape
    hw = MLA_HEADS * HEAD_SLOT
    const = lambda shape: pl.BlockSpec(shape, lambda i, j: (0,) * len(shape))
    tok = lambda w: pl.BlockSpec((1, tb, w), lambda i, j: (i, j, 0))
    per_b = pl.BlockSpec((1, 1, d), lambda i, j: (i, 0, 0))
    return pl.pallas_call(
        _inproj_kernel,
        out_shape=(jax.ShapeDtypeStruct((b, s, hw), BF16),
                   jax.ShapeDtypeStruct((b, s, hw), BF16),
                   jax.ShapeDtypeStruct((b, s, MLA_HEADS * MLA_V), BF16),
                   jax.ShapeDtypeStruct((b, s, GDN_HEADS * GDN_SLAB), BF16),
                   jax.ShapeDtypeStruct((b, s, MISC_W), F32)),
        grid=(b, s // tb),
        in_specs=[tok(d), tok(1), per_b, per_b, const((1, d)), const(w1.shape),
                  const((1, MLA_Q_LORA)), const(wq.shape), const((1, MLA_KV_LORA)),
                  const(wkv.shape), const((1, LANES)), const((1, LANES)), const((1, LANES))],
        out_specs=(tok(hw), tok(hw), tok(MLA_HEADS * MLA_V), tok(GDN_HEADS * GDN_SLAB),
                   tok(MISC_W)),
        compiler_params=_cparams(("parallel", "parallel")),
        name="in_projection",
    )(x, positions.reshape(b, s, 1), sc1, sh1, norm1_g.reshape(1, d), w1,
      q_norm_g.reshape(1, -1), wq, kv_norm_g.reshape(1, -1), wkv, freq, alog, dtb)


def _attn_kernel(q_ref, k_ref, v_ref, o_ref, s_ref, p_ref, m_ref, l_ref, acc_ref, *, tq, nh):
    qi = pl.program_id(2)
    m_ref[...] = jnp.full(m_ref.shape, -jnp.inf, F32)
    l_ref[...] = jnp.zeros(l_ref.shape, F32)
    acc_ref[...] = jnp.zeros(acc_ref.shape, F32)

    def step(off, masked):
        for h in range(nh):
            kj = k_ref[0, pl.ds(off, tq), h * HEAD_SLOT:(h + 1) * HEAD_SLOT]
            s_ref[h] = lax.dot_general(q_ref[0, :, h * HEAD_SLOT:(h + 1) * HEAD_SLOT], kj,
                                       (((1,), (1,)), ((), ())), preferred_element_type=F32)
        n_strips = tq // ATT_STRIP

        def strip(h, r):
            rows = slice(r * ATT_STRIP, (r + 1) * ATT_STRIP)
            sc = s_ref[h, rows, :]
            if masked:
                rid = lax.broadcasted_iota(jnp.int32, (ATT_STRIP, tq), 0) + r * ATT_STRIP
                cid = lax.broadcasted_iota(jnp.int32, (ATT_STRIP, tq), 1)
                sc = jnp.where(cid <= rid, sc, -jnp.inf)
            return rows, sc

        rep = lambda col: jnp.broadcast_to(col, (col.shape[0], LANES))
        m_new = []
        for h in range(nh):
            mx = jnp.concatenate([rep(jnp.max(strip(h, r)[1], axis=-1, keepdims=True))
                                  for r in range(n_strips)], axis=0)
            m_new.append(jnp.maximum(m_ref[h], mx))
        for h in range(nh):
            vj = v_ref[0, pl.ds(off, tq), (h // 2) * 2 * MLA_V:(h // 2 + 1) * 2 * MLA_V]
            sums = []
            for r in range(n_strips):
                rows, sc = strip(h, r)
                p = jnp.exp2(sc - jnp.concatenate([m_new[h][rows]] * (tq // LANES), axis=1))
                p_ref[h, rows, :] = p.astype(BF16)
                sums.append(rep(jnp.sum(p, axis=-1, keepdims=True)))
            alpha = jnp.exp2(m_ref[h] - m_new[h])
            l_ref[h] = alpha * l_ref[h] + jnp.concatenate(sums, axis=0)
            m_ref[h] = m_new[h]
            acc_ref[h] = alpha * acc_ref[h] + jnp.dot(p_ref[h], vj, preferred_element_type=F32)

    def body(j, carry):
        step(pl.multiple_of(j * tq, tq), False)
        return carry

    lax.fori_loop(0, qi, body, 0)
    step(pl.multiple_of(qi * tq, tq), True)
    lane = lax.broadcasted_iota(jnp.int32, (tq, 2 * MLA_V), 1)
    outs = []
    for p in range(nh // 2):
        o0 = acc_ref[2 * p] / l_ref[2 * p]
        o1 = acc_ref[2 * p + 1] / l_ref[2 * p + 1]
        outs.append(jnp.where(lane < MLA_V, o0, o1))
    o_ref[0] = jnp.concatenate(outs, axis=1).astype(o_ref.dtype)


def _attention(q, k, v, tq, nh):
    b, s, _ = q.shape
    groups = MLA_HEADS // nh
    return pl.pallas_call(
        functools.partial(_attn_kernel, tq=tq, nh=nh),
        out_shape=jax.ShapeDtypeStruct((b, s, MLA_HEADS * MLA_V), BF16),
        grid=(b, groups, s // tq),
        in_specs=[pl.BlockSpec((1, tq, nh * HEAD_SLOT), lambda i, p, j: (i, j, p)),
                  pl.BlockSpec((1, s, nh * HEAD_SLOT), lambda i, p, j: (i, 0, p)),
                  pl.BlockSpec((1, s, nh * MLA_V), lambda i, p, j: (i, 0, p))],
        out_specs=pl.BlockSpec((1, tq, nh * MLA_V), lambda i, p, j: (i, j, p)),
        scratch_shapes=[pltpu.VMEM((nh, tq, tq), F32), pltpu.VMEM((nh, tq, tq), BF16),
                        pltpu.VMEM((nh, tq, LANES), F32), pltpu.VMEM((nh, tq, LANES), F32),
                        pltpu.VMEM((nh, tq, 2 * MLA_V), F32)],
        compiler_params=_cparams(("parallel", "parallel", "arbitrary")),
        name="mla_attention",
    )(q, k, v)


def _bdot(a, b):
    return jnp.dot(a.astype(BF16), b.astype(BF16), preferred_element_type=F32)


def _gdn_kernel(slab_ref, gb_ref, cw_ref, ng_ref, shf_ref, ind_ref, sel_ref, o_ref, tail_ref,
                state_ref, *, ts, hg):
    si = pl.program_id(2)
    width = hg * GDN_SLAB
    hist_rows = 8

    @pl.when(si == 0)
    def _():
        tail_ref[...] = jnp.zeros(tail_ref.shape, F32)
        state_ref[...] = jnp.zeros(state_ref.shape, F32)

    xs_bf = slab_ref[0]
    xs = xs_bf.astype(F32)
    conv = cw_ref[CONV_WIDTH - 1:CONV_WIDTH, :] * xs
    hist = jnp.zeros((hist_rows, width), F32)
    for j in range(CONV_WIDTH - 1):
        wj = cw_ref[j:j + 1, :]
        conv = conv + wj * jnp.dot(shf_ref[j], xs_bf, preferred_element_type=F32)
        start = hist_rows - (CONV_WIDTH - 1) + j
        hist = hist + wj * tail_ref[start:start + hist_rows, :]
    conv = jnp.concatenate([conv[:hist_rows] + hist, conv[hist_rows:]], axis=0)
    tail_ref[0:hist_rows, :] = xs[ts - hist_rows:ts, :]
    act = _silu(conv)

    gb = gb_ref[0]
    g1 = gb.astype(BF16)
    r1 = gb - g1.astype(F32)
    g2 = r1.astype(BF16)
    g3 = (r1 - g2.astype(F32)).astype(BF16)
    sel = sel_ref[0]
    gate_w = (jnp.dot(g1, sel, preferred_element_type=F32)
              + jnp.dot(g2, sel, preferred_element_type=F32)
              + jnp.dot(g3, sel, preferred_element_type=F32))

    n_chunks = ts // CHUNK
    row = lax.broadcasted_iota(jnp.int32, (CHUNK, CHUNK), 0)
    col = lax.broadcasted_iota(jnp.int32, (CHUNK, CHUNK), 1)
    tri_incl = col <= row
    tri_strict = col < row
    eye = col == row
    crow = lax.broadcasted_iota(jnp.int32, (ts, LANES), 0) % CHUNK
    slab_lane = lax.broadcasted_iota(jnp.int32, (ts, GDN_SLAB), 1)

    heads = []
    for hh in range(hg):
        sl_act = act[:, hh * GDN_SLAB:(hh + 1) * GDN_SLAB]
        sq = sl_act * sl_act
        sq_hi = sq.astype(BF16)
        sq_lo = (sq - sq_hi.astype(F32)).astype(BF16)
        ss = (jnp.dot(sq_hi, ind_ref[...], preferred_element_type=F32)
              + jnp.dot(sq_lo, ind_ref[...], preferred_element_type=F32))
        inv = lax.rsqrt(ss + EPS)
        factor = jnp.where(slab_lane < GDN_DK, inv * (GDN_DK ** -0.5),
                           jnp.where(slab_lane < 2 * GDN_DK, inv, 1.0))
        normed = sl_act * factor
        z_all = xs[:, hh * GDN_SLAB + 3 * GDN_DK:(hh + 1) * GDN_SLAB]
        g_w = gate_w[:, hh * LANES:(hh + 1) * LANES]
        b_w = gate_w[:, (hg + hh) * LANES:(hg + hh + 1) * LANES]
        gc_w = g_w
        shift = 1
        while shift < CHUNK:
            rolled = pltpu.roll(gc_w, shift, axis=0)
            gc_w = gc_w + jnp.where(crow >= shift, rolled, 0.0)
            shift *= 2
        heads.append((normed, z_all, b_w, gc_w, jnp.exp(gc_w)))

    a_low, qk, y0, qd, kt, cd, y, ky, qy = ({} for _ in range(9))
    lo, hi = slice(0, GDN_DK), slice(GDN_DK, 2 * GDN_DK)

    def prep(u):
        hh, c = u
        normed, _, b_w, gc_w, eg_w = heads[hh]
        sl = slice(c * CHUNK, (c + 1) * CHUNK)
        qc, kc, vc = normed[sl, lo], normed[sl, hi], normed[sl, 2 * GDN_DK:2 * GDN_DK + GDN_DV]
        gcol = gc_w[sl, lo]
        grow = jnp.sum(jnp.where(eye, gcol, 0.0), axis=0, keepdims=True)
        diff = gcol - grow
        decay = jnp.where(tri_incl, jnp.exp(jnp.where(tri_incl, diff, 0.0)), 0.0)
        k_beta = kc * b_w[sl, hi]
        kq = jnp.concatenate([k_beta, qc], axis=0).astype(BF16)
        kk = lax.dot_general(kq, kc.astype(BF16), (((1,), (1,)), ((), ())),
                             preferred_element_type=F32)
        a_low[u] = jnp.where(tri_strict, kk[:CHUNK] * decay, 0.0)
        qk[u] = jnp.where(tri_incl, kk[CHUNK:] * decay, 0.0)
        y0[u] = jnp.concatenate([vc * b_w[sl, lo], k_beta * eg_w[sl, hi]], axis=1)
        qd[u] = qc * eg_w[sl, lo]
        glast_hi = gc_w[(c + 1) * CHUNK - 1:(c + 1) * CHUNK, hi]
        kt[u] = kc * jnp.exp(glast_hi - gc_w[sl, hi])
        cd[u] = eg_w[(c + 1) * CHUNK - 1:(c + 1) * CHUNK, lo]

    def solve(units):
        blk = 2
        tinv = {u: jnp.where(eye, 1.0, 0.0)
                - jnp.where((row // blk) == (col // blk), a_low[u], 0.0) for u in units}
        while blk < CHUNK:
            in_big = (row // (2 * blk)) == (col // (2 * blk))
            off_mask = in_big & ((row // blk) != (col // blk))
            left = {u: _bdot(tinv[u], jnp.where(off_mask, a_low[u], 0.0)) for u in units}
            tinv = {u: tinv[u] - _bdot(left[u], tinv[u]) for u in units}
            blk *= 2
        for u in units:
            y[u] = _bdot(tinv[u], y0[u])
        for u in units:
            ky[u] = lax.dot_general(kt[u].astype(BF16), y[u].astype(BF16),
                                    (((0,), (0,)), ((), ())),
                                    preferred_element_type=F32)
        for u in units:
            qy[u] = _bdot(qk[u], y[u])

    units = [(hh, c) for c in range(n_chunks) for hh in range(hg)]
    for u in units:
        prep(u)
    solve(units)

    outs = [[] for _ in range(hg)]
    states = [state_ref[hh] for hh in range(hg)]
    for c in range(n_chunks):
        for hh in range(hg):
            u = (hh, c)
            sb = states[hh].astype(BF16)
            r_mat = qd[u] - qy[u][:, GDN_DV:]
            outs[hh].append(jnp.dot(r_mat.astype(BF16), sb, preferred_element_type=F32)
                            + qy[u][:, :GDN_DV])
            states[hh] = (states[hh] * cd[u]
                          - jnp.dot(ky[u][:, GDN_DV:].astype(BF16), sb,
                                    preferred_element_type=F32)
                          + ky[u][:, :GDN_DV])
    finals = []
    for hh in range(hg):
        state_ref[hh] = states[hh]
        o_all = jnp.concatenate(outs[hh], axis=0)
        finals.append(_rms(o_all, ng_ref[...]) * _silu(heads[hh][1]))
    o_ref[0] = jnp.concatenate(finals, axis=1).astype(o_ref.dtype)


def _gdn(slab, gb, conv_slab, gdn_norm_g, ts, hg):
    b, s, _ = slab.shape
    groups = GDN_HEADS // hg
    r = jnp.arange(ts)
    shifts = jnp.stack([(r[:, None] - r[None, :]) == (CONV_WIDTH - 1 - j)
                        for j in range(CONV_WIDTH - 1)]).astype(BF16)
    li = jnp.arange(GDN_SLAB)
    ind = ((li[:, None] // GDN_DK == li[None, :] // GDN_DK)
           & (li[:, None] < 2 * GDN_DK) & (li[None, :] < 2 * GDN_DK)).astype(BF16)
    src = jnp.arange(LANES)[None, :, None]
    dst = jnp.arange(2 * hg * LANES)[None, None, :]
    grp = jnp.arange(groups)[:, None, None]
    want = jnp.where(dst < hg * LANES, grp * hg + dst // LANES,
                     GDN_HEADS + grp * hg + (dst - hg * LANES) // LANES)
    sel = (src == want).astype(BF16)
    return pl.pallas_call(
        functools.partial(_gdn_kernel, ts=ts, hg=hg),
        out_shape=jax.ShapeDtypeStruct((b, s, GDN_HEADS * GDN_DV), BF16),
        grid=(b, groups, s // ts),
        in_specs=[pl.BlockSpec((1, ts, hg * GDN_SLAB), lambda i, p, j: (i, j, p)),
                  pl.BlockSpec((1, ts, MISC_W), lambda i, p, j: (i, j, 0)),
                  pl.BlockSpec((CONV_WIDTH, hg * GDN_SLAB), lambda i, p, j: (0, p)),
                  pl.BlockSpec((1, GDN_DV), lambda i, p, j: (0, 0)),
                  pl.BlockSpec((CONV_WIDTH - 1, ts, ts), lambda i, p, j: (0, 0, 0)),
                  pl.BlockSpec((GDN_SLAB, GDN_SLAB), lambda i, p, j: (0, 0)),
                  pl.BlockSpec((1, LANES, 2 * hg * LANES), lambda i, p, j: (p, 0, 0))],
        out_specs=pl.BlockSpec((1, ts, hg * GDN_DV), lambda i, p, j: (i, j, p)),
        scratch_shapes=[pltpu.VMEM((16, hg * GDN_SLAB), F32),
                        pltpu.VMEM((hg, GDN_DK, GDN_DV), F32)],
        compiler_params=_cparams(("parallel", "parallel", "arbitrary")),
        name="gated_deltanet",
    )(slab, gb, conv_slab, gdn_norm_g.reshape(1, GDN_DV), shifts, ind, sel)


def _outproj_kernel(x_ref, ao_ref, go_ref, gt_ref, sc_ref, sh_ref, mg_ref, wo_ref, g2_ref,
                    rw_ref, rb_ref, x1_ref, h2_ref, route_ref, cnt_ref, carry_ref, *, tb):
    first = (pl.program_id(0) == 0) & (pl.program_id(1) == 0)

    @pl.when(first)
    def _():
        carry_ref[...] = jnp.zeros(carry_ref.shape, F32)

    mla = _rms(ao_ref[0].astype(F32), mg_ref[...])
    cat = jnp.concatenate([mla.astype(BF16), go_ref[0]], axis=1)
    mix = jnp.dot(cat, wo_ref[...], preferred_element_type=F32)
    x1 = x_ref[0] + gt_ref[0] * mix
    x1_ref[0] = x1
    h2 = _rms(x1, g2_ref[...]) * (1.0 + sc_ref[0]) + sh_ref[0]
    h2_ref[0] = _pack_halves(h2)

    h_hi = h2.astype(BF16)
    h_lo = (h2 - h_hi.astype(F32)).astype(BF16)
    main = jnp.dot(h_hi, rw_ref[...], preferred_element_type=F32)
    logits = (main[:, :LANES] + main[:, LANES:]
              + jnp.dot(h_lo, rw_ref[:, :LANES], preferred_element_type=F32) + rb_ref[...])
    lane = lax.broadcasted_iota(jnp.int32, (tb, LANES), 1).astype(F32)
    work = jnp.where(lane < N_EXPERTS, logits, -jnp.inf)
    vals, idxs = [], []
    onehot = jnp.zeros((tb, LANES), F32)
    for _ in range(TOP_K):
        mx = jnp.max(work, axis=-1, keepdims=True)
        ix = jnp.min(jnp.where(work == mx, lane, float(LANES)), axis=-1, keepdims=True)
        sel = lane == ix
        onehot = jnp.where(sel, 1.0, onehot)
        work = jnp.where(sel, -jnp.inf, work)
        vals.append(mx)
        idxs.append(ix)
    exps = [jnp.exp(v - vals[0]) for v in vals]
    den = exps[0] + exps[1] + exps[2] + exps[3]

    r = lax.broadcasted_iota(jnp.int32, (tb, tb), 0)
    c = lax.broadcasted_iota(jnp.int32, (tb, tb), 1)
    tri = jnp.where(c < r, 1.0, 0.0).astype(BF16)
    before = jnp.dot(tri, onehot.astype(BF16), preferred_element_type=F32) + carry_ref[...]
    route = jnp.zeros((tb, LANES), F32)
    for kk in range(TOP_K):
        rank = jnp.sum(jnp.where(lane == idxs[kk], before, 0.0), axis=-1, keepdims=True)
        route = jnp.where(lane == kk, idxs[kk], route)
        route = jnp.where(lane == TOP_K + kk, rank, route)
        route = jnp.where(lane == 2 * TOP_K + kk, exps[kk] / den, route)
    route_ref[0] = route
    total = carry_ref[...] + jnp.sum(onehot, axis=0, keepdims=True)
    carry_ref[...] = total
    cnt_ref[...] = total


def _out_projection(x, attn_o, gdn_o, gt1, sc2, sh2, mla_out_g, w_out, norm2_g, rw, rb, tb):
    b, s, d = x.shape
    const = lambda shape: pl.BlockSpec(shape, lambda i, j: (0,) * len(shape))
    tok = lambda w: pl.BlockSpec((1, tb, w), lambda i, j: (i, j, 0))
    per_b = pl.BlockSpec((1, 1, d), lambda i, j: (i, 0, 0))
    half = attn_o.shape[-1]
    return pl.pallas_call(
        functools.partial(_outproj_kernel, tb=tb),
        out_shape=(jax.ShapeDtypeStruct((b, s, d), F32),
                   jax.ShapeDtypeStruct((b, s, d // 2), F32),
                   jax.ShapeDtypeStruct((b, s, LANES), F32),
                   jax.ShapeDtypeStruct((1, LANES), F32)),
        grid=(b, s // tb),
        in_specs=[tok(d), tok(half), tok(half), per_b, per_b, per_b, const((1, half)),
                  const(w_out.shape), const((1, d)), const(rw.shape), const((1, LANES))],
        out_specs=(tok(d), tok(d // 2), tok(LANES), const((1, LANES))),
        scratch_shapes=[pltpu.VMEM((1, LANES), F32)],
        compiler_params=_cparams(("arbitrary", "arbitrary")),
        name="out_projection_router",
    )(x, attn_o, gdn_o, gt1, sc2, sh2, mla_out_g.reshape(1, half), w_out,
      norm2_g.reshape(1, d), rw, rb)


def _sc_mesh():
    return plsc.VectorSubcoreMesh(core_axis_name="c", subcore_axis_name="s",
                                  num_cores=SC_CORES, num_subcores=SC_SUBCORES)


def _sc_worker():
    return lax.axis_index("s") * SC_CORES + lax.axis_index("c")


def _dispatch_rows(h2, dest_km, rows):
    t, d = h2.shape
    per_worker = t // SC_WORKERS
    n_chunks = per_worker // SC_CHUNK

    @functools.partial(
        pl.kernel, out_type=jax.ShapeDtypeStruct((rows, d), h2.dtype), mesh=_sc_mesh(),
        scratch_types=[pltpu.VMEM((SC_CHUNK,), jnp.int32), pltpu.VMEM((SC_CHUNK, d), h2.dtype)],
        name="moe_dispatch")
    def run(h2_hbm, dest_hbm, xs_hbm, idx_v, rows_v):
        base_w = _sc_worker() * per_worker

        @pl.loop(0, n_chunks)
        def _(ci):
            base = pl.multiple_of(base_w + ci * SC_CHUNK, SC_CHUNK)
            pltpu.sync_copy(h2_hbm.at[pl.ds(base, SC_CHUNK)], rows_v)
            for kk in range(TOP_K):
                pltpu.sync_copy(dest_hbm.at[kk, pl.ds(base, SC_CHUNK)], idx_v)
                pltpu.sync_copy(rows_v, xs_hbm.at[idx_v])

    return run(h2, dest_km)


def _collect_rows(y_rows, dest_km):
    _, d = y_rows.shape
    t = dest_km.shape[1]
    per_worker = t // SC_WORKERS
    n_chunks = per_worker // SC_CHUNK

    @functools.partial(
        pl.kernel, out_type=jax.ShapeDtypeStruct((TOP_K, t, d), y_rows.dtype), mesh=_sc_mesh(),
        scratch_types=[pltpu.VMEM((SC_CHUNK,), jnp.int32), pltpu.VMEM((SC_CHUNK, d), y_rows.dtype)],
        name="moe_collect")
    def run(y_hbm, dest_hbm, out_hbm, idx_v, rows_v):
        base_w = _sc_worker() * per_worker

        @pl.loop(0, n_chunks)
        def _(ci):
            base = pl.multiple_of(base_w + ci * SC_CHUNK, SC_CHUNK)
            for kk in range(TOP_K):
                pltpu.sync_copy(dest_hbm.at[kk, pl.ds(base, SC_CHUNK)], idx_v)
                pltpu.sync_copy(y_hbm.at[idx_v], rows_v)
                pltpu.sync_copy(rows_v, out_hbm.at[kk, pl.ds(base, SC_CHUNK)])

    return run(y_rows, dest_km)


def _experts_kernel(be_ref, nu_ref, xs_ref, wgu_ref, bgu_ref, wd_ref, bd_ref, y_ref,
                    wgu_bf, wd_bf):
    i = pl.program_id(0)
    new_expert = (i == 0) | (be_ref[i] != be_ref[jnp.maximum(i - 1, 0)])

    @pl.when(new_expert)
    def _():
        wgu_bf[...] = wgu_ref[0].astype(BF16)
        wd_bf[...] = wd_ref[0].astype(BF16)

    @pl.when(i < nu_ref[0])
    def _():
        x_lo, x_hi = _unpack_halves(xs_ref[...])
        half = x_lo.shape[-1]
        gu = (jnp.dot(x_lo.astype(BF16), wgu_bf[:half, :], preferred_element_type=F32)
              + jnp.dot(x_hi.astype(BF16), wgu_bf[half:, :], preferred_element_type=F32)
              + bgu_ref[0])
        gate = jnp.minimum(gu[:, :D_EXPERT], SWIGLU_LIMIT)
        up = jnp.clip(gu[:, D_EXPERT:], -SWIGLU_LIMIT, SWIGLU_LIMIT)
        act = (up + 1.0) * (gate * _sigmoid(SWIGLU_ALPHA * gate))
        y = jnp.dot(act.astype(BF16), wd_bf[...], preferred_element_type=F32) + bd_ref[0]
        y_ref[...] = _pack_halves(y)


def _experts(blk_e, n_used, xs, wgu, bgu, wd, bd, bm):
    rows, half = xs.shape
    d = 2 * half
    n_blocks = rows // bm
    row_map = lambda i, be, nu: (jnp.maximum(jnp.minimum(i, nu[0] - 1), 0), 0)
    exp_map = lambda i, be, nu: (be[i], 0, 0)
    grid_spec = pltpu.PrefetchScalarGridSpec(
        num_scalar_prefetch=2,
        grid=(n_blocks,),
        in_specs=[pl.BlockSpec((bm, half), row_map),
                  pl.BlockSpec((1, d, 2 * D_EXPERT), exp_map),
                  pl.BlockSpec((1, 1, 2 * D_EXPERT), exp_map),
                  pl.BlockSpec((1, D_EXPERT, d), exp_map),
                  pl.BlockSpec((1, 1, d), exp_map)],
        out_specs=pl.BlockSpec((bm, half), row_map),
        scratch_shapes=[pltpu.VMEM((d, 2 * D_EXPERT), BF16),
                        pltpu.VMEM((D_EXPERT, d), BF16)])
    return pl.pallas_call(
        _experts_kernel,
        out_shape=jax.ShapeDtypeStruct((rows, half), F32),
        grid_spec=grid_spec,
        compiler_params=_cparams(("arbitrary",)),
        name="expert_mlp",
    )(blk_e, n_used, xs, wgu, bgu, wd, bd)


def _final_kernel(x1_ref, yk_ref, route_ref, gt_ref, fg_ref, o_ref):
    route = route_ref[0]
    ffn = jnp.zeros(x1_ref.shape[1:], F32)
    for kk in range(TOP_K):
        wk = route[:, 2 * TOP_K + kk:2 * TOP_K + kk + 1]
        y_lo, y_hi = _unpack_halves(yk_ref[kk, 0])
        ffn = ffn + wk * jnp.concatenate([y_lo, y_hi], axis=1)
    x2 = x1_ref[0] + gt_ref[0] * ffn
    o_ref[0] = _rms(x2, fg_ref[...])


def _final(x1, yk, route, gt2, final_g, tb):
    b, s, d = x1.shape
    return pl.pallas_call(
        _final_kernel,
        out_shape=jax.ShapeDtypeStruct((b, s, d), F32),
        grid=(b, s // tb),
        in_specs=[pl.BlockSpec((1, tb, d), lambda i, j: (i, j, 0)),
                  pl.BlockSpec((TOP_K, 1, tb, d // 2), lambda i, j: (0, i, j, 0)),
                  pl.BlockSpec((1, tb, LANES), lambda i, j: (i, j, 0)),
                  pl.BlockSpec((1, 1, d), lambda i, j: (i, 0, 0)),
                  pl.BlockSpec((1, d), lambda i, j: (0, 0))],
        out_specs=pl.BlockSpec((1, tb, d), lambda i, j: (i, j, 0)),
        compiler_params=_cparams(("parallel", "parallel")),
        name="combine_final_norm",
    )(x1, yk, route, gt2, final_g.reshape(1, d))


def _rot_cols(w):
    half = MLA_ROPE // 2
    return jnp.concatenate([-w[..., half:], w[..., :half]], axis=-1)


def _prep_weights(w_in, w_q_b, w_kv_b, conv_w, A_log, dt_bias):
    d = w_in.shape[0]
    cuts = [MLA_Q_LORA, MLA_KV_LORA, MLA_ROPE, GDN_HEADS * GDN_DK, GDN_HEADS * GDN_DK,
            GDN_HEADS * GDN_DV, GDN_HEADS * GDN_DV, GDN_HEADS, GDN_HEADS]
    offs = [0]
    for cw in cuts:
        offs.append(offs[-1] + cw)
    part = lambda i: w_in[:, offs[i]:offs[i + 1]]
    k_pe = part(2)
    misc = jnp.concatenate(
        [part(7), part(8), jnp.zeros((d, MLA_NOPE - 2 * GDN_HEADS), w_in.dtype), k_pe,
         _rot_cols(k_pe)], axis=1)
    heads = lambda w, n: w.reshape(w.shape[0], GDN_HEADS, n)
    slab = jnp.concatenate([heads(part(3), GDN_DK), heads(part(4), GDN_DK),
                            heads(part(5), GDN_DV), heads(part(6), GDN_DV)], axis=-1)
    w1 = jnp.concatenate([part(0), part(1), misc, slab.reshape(d, GDN_HEADS * GDN_SLAB)],
                         axis=1).astype(BF16)

    wq3 = w_q_b.reshape(MLA_Q_LORA, MLA_HEADS, MLA_QK)
    pe = wq3[..., MLA_NOPE:]
    wq = jnp.concatenate([wq3[..., :MLA_NOPE], pe, _rot_cols(pe)], axis=-1)
    wq = wq.reshape(MLA_Q_LORA, MLA_HEADS * HEAD_SLOT).astype(BF16)

    wkv3 = w_kv_b.reshape(MLA_KV_LORA, MLA_HEADS, MLA_NOPE + MLA_V)
    wk = jnp.concatenate([wkv3[..., :MLA_NOPE],
                          jnp.zeros((MLA_KV_LORA, MLA_HEADS, HEAD_SLOT - MLA_NOPE), w_kv_b.dtype)],
                         axis=-1).reshape(MLA_KV_LORA, MLA_HEADS * HEAD_SLOT)
    wv = wkv3[..., MLA_NOPE:].reshape(MLA_KV_LORA, MLA_HEADS * MLA_V)
    wkv = jnp.concatenate([wk, wv], axis=1).astype(BF16)

    nk = GDN_HEADS * GDN_DK
    cheads = lambda w, n: w.reshape(CONV_WIDTH, GDN_HEADS, n)
    conv_slab = jnp.concatenate(
        [cheads(conv_w[:, :nk], GDN_DK), cheads(conv_w[:, nk:2 * nk], GDN_DK),
         cheads(conv_w[:, 2 * nk:], GDN_DV), jnp.zeros((CONV_WIDTH, GDN_HEADS, GDN_DV), conv_w.dtype)],
        axis=-1).reshape(CONV_WIDTH, GDN_HEADS * GDN_SLAB)

    half = MLA_ROPE // 2
    inv_freq = ROPE_THETA ** (-jnp.arange(half, dtype=F32) / half)
    freq = jnp.concatenate([jnp.zeros((MLA_NOPE,), F32), inv_freq, inv_freq,
                            jnp.zeros((LANES - MLA_QK,), F32)]).reshape(1, LANES)
    padl = lambda a: jnp.concatenate([a.astype(F32), jnp.zeros((LANES - a.shape[0],), F32)]).reshape(1, LANES)
    return w1, wq, wkv, conv_slab, freq, padl(A_log), padl(dt_bias)


def _layer(x, mod, positions, w_in, q_norm_g, w_q_b, kv_norm_g, w_kv_b, mla_out_g, conv_w,
           A_log, dt_bias, gdn_norm_g, w_out, norm1_g, norm2_g, router_w, router_b, w_gate_up,
           b_gate_up, w_down, b_down):
    b, s, d = x.shape
    t = b * s
    sh1, sc1, gt1, sh2, sc2, gt2 = [m.reshape(b, 1, d) for m in jnp.split(mod, 6, axis=-1)]
    w1, wq, wkv, conv_slab, freq, alog, dtb = _prep_weights(w_in, w_q_b, w_kv_b, conv_w, A_log,
                                                            dt_bias)
    tb = min(256, s)
    q, k, v, slab, gb = _in_projection(x, positions, sc1, sh1, norm1_g, w1, q_norm_g, wq,
                                       kv_norm_g, wkv, freq, alog, dtb, tb)
    attn_o = _attention(q, k, v, min(512, s), 8)
    gdn_o = _gdn(slab, gb, conv_slab, gdn_norm_g, min(256, s), 4)

    rw = jnp.concatenate([router_w, jnp.zeros((d, LANES - N_EXPERTS), router_w.dtype)], axis=1)
    rw_hi = rw.astype(BF16)
    rw = jnp.concatenate([rw_hi, (rw - rw_hi.astype(F32)).astype(BF16)], axis=1)
    rb = jnp.concatenate([router_b, jnp.zeros((LANES - N_EXPERTS,), router_b.dtype)]).reshape(1, LANES)
    x1, h2, route, counts = _out_projection(x, attn_o, gdn_o, gt1, sc2, sh2, mla_out_g,
                                            w_out.astype(BF16), norm2_g, rw, rb, tb)

    bm = 512
    route2 = route.reshape(t, LANES)
    idx = route2[:, :TOP_K].astype(jnp.int32)
    rank = route2[:, TOP_K:2 * TOP_K].astype(jnp.int32)
    cnt = counts[0, :N_EXPERTS].astype(jnp.int32)
    padded = ((cnt + bm - 1) // bm) * bm
    pend = jnp.cumsum(padded)
    pstart = pend - padded
    dest = pstart[idx] + rank
    n_blocks = (t * TOP_K + N_EXPERTS * (bm - 1) + bm - 1) // bm
    rows = n_blocks * bm
    n_used = (pend[-1] // bm).astype(jnp.int32).reshape(1)
    blk_start = jnp.arange(n_blocks, dtype=jnp.int32) * bm
    blk_e = jnp.minimum(jnp.sum(blk_start[:, None] >= pend[None, :], axis=1), N_EXPERTS - 1)
    last_e = blk_e[jnp.maximum(n_used[0] - 1, 0)]
    blk_e = jnp.where(jnp.arange(n_blocks) < n_used[0], blk_e, last_e).astype(jnp.int32)

    dest_km = dest.T
    xs = _dispatch_rows(h2.reshape(t, d // 2), dest_km, rows)
    y_rows = _experts(blk_e, n_used, xs, w_gate_up, b_gate_up.reshape(N_EXPERTS, 1, -1),
                      w_down, b_down.reshape(N_EXPERTS, 1, -1), bm)
    yk = _collect_rows(y_rows, dest_km).reshape(TOP_K, b, s, d // 2)
    return x1, yk, route, gt2


def kernel(x, c, positions, ada_w, ada_b, norm1_g, w_in, q_norm_g, w_q_b, kv_norm_g, w_kv_b, mla_out_g, conv_w, A_log, dt_bias, gdn_norm_g, w_out, norm2_g, router_w, router_b, w_gate_up, b_gate_up, w_down, b_down, final_g):
    depth = ada_w.shape[0]
    assert depth == 1
    l = 0
    mod = _modulation(c, ada_w[l], ada_b[l])
    x1, yk, route, gt2 = _layer(x, mod, positions, w_in[l], q_norm_g[l], w_q_b[l], kv_norm_g[l],
                         w_kv_b[l], mla_out_g[l], conv_w[l], A_log[l], dt_bias[l], gdn_norm_g[l],
                         w_out[l], norm1_g[l], norm2_g[l], router_w[l], router_b[l],
                         w_gate_up[l], b_gate_up[l], w_down[l], b_down[l])
    return _final(x1, yk, route, gt2, final_g, min(256, x.shape[1]))
```

```python
import functools

import jax
import jax.numpy as jnp
from jax import lax
from jax.experimental import pallas as pl
from jax.experimental.pallas import tpu as pltpu
from jax.experimental.pallas import tpu_sc as plsc

F32 = jnp.float32
BF16 = jnp.bfloat16

D_MODEL = 1024
EPS = 1e-6
MLA_HEADS = 8
MLA_NOPE = 64
MLA_ROPE = 32
MLA_V = 64
MLA_QK = MLA_NOPE + MLA_ROPE
MLA_Q_LORA = 384
MLA_KV_LORA = 256
ROPE_THETA = 10000.0
GDN_HEADS = 8
GDN_DK = 64
GDN_DV = 64
CONV_WIDTH = 4
CHUNK = 64
N_EXPERTS = 32
TOP_K = 4
D_EXPERT = D_MODEL
SWIGLU_ALPHA = 1.702
SWIGLU_LIMIT = 7.0

LANES = 128
HEAD_SLOT = 128
GDN_SLAB = 256
MISC_W = 128
PROJ_W = MLA_Q_LORA + MLA_KV_LORA + MISC_W + GDN_HEADS * GDN_SLAB
VMEM_LIMIT = 56 * 1024 * 1024
MOE_PARTS = 2
EXPERT_ROWS = 512
SC_CORES = 2
SC_SUBCORES = 16
SC_WORKERS = SC_CORES * SC_SUBCORES
SC_CHUNK = 128
ATT_STRIP = 32
LOG2E = 1.4426950408889634


def _cparams(sem):
    return pltpu.CompilerParams(dimension_semantics=sem, vmem_limit_bytes=VMEM_LIMIT)


def _rms(x, g):
    return x * lax.rsqrt(jnp.mean(x * x, axis=-1, keepdims=True) + EPS) * g


def _sigmoid(x):
    return 1.0 / (1.0 + jnp.exp(-x))


def _silu(x):
    return x * _sigmoid(x)


def _pack_halves(x):
    w = x.shape[-1] // 2
    lo = lax.bitcast_convert_type(x[:, :w].astype(BF16).astype(F32), jnp.uint32) >> 16
    hi = lax.bitcast_convert_type(x[:, w:].astype(BF16).astype(F32), jnp.uint32) & jnp.uint32(0xFFFF0000)
    return lax.bitcast_convert_type(lo | hi, F32)


def _unpack_halves(p):
    u = lax.bitcast_convert_type(p, jnp.uint32)
    return (lax.bitcast_convert_type(u << 16, F32),
            lax.bitcast_convert_type(u & jnp.uint32(0xFFFF0000), F32))


def _mod_kernel(c_ref, w_ref, b_ref, o_ref):
    c = c_ref[...]
    o_ref[...] = jnp.dot(_silu(c), w_ref[...], preferred_element_type=F32,
                         precision=lax.Precision.HIGHEST) + b_ref[...]


def _modulation(c, ada_w, ada_b):
    b, d = c.shape
    n = ada_w.shape[1]
    return pl.pallas_call(
        _mod_kernel,
        out_shape=jax.ShapeDtypeStruct((b, n), F32),
        grid=(n // d,),
        in_specs=[pl.BlockSpec((b, d), lambda j: (0, 0)),
                  pl.BlockSpec((d, d), lambda j: (0, j)),
                  pl.BlockSpec((1, d), lambda j: (0, j))],
        out_specs=pl.BlockSpec((b, d), lambda j: (0, j)),
        compiler_params=_cparams(("arbitrary",)),
        name="adaln_mod",
    )(c, ada_w, ada_b.reshape(1, n))


def _inproj_kernel(x_ref, pos_ref, sc_ref, sh_ref, g1_ref, w1_ref, qg_ref, wq_ref, kvg_ref,
                   wkv_ref, freq_ref, alog_ref, dtb_ref,
                   q_ref, k_ref, v_ref, gdn_ref, gb_ref):
    x = x_ref[0]
    h = _rms(x, g1_ref[...]) * (1.0 + sc_ref[0]) + sh_ref[0]
    proj = jnp.dot(h.astype(BF16), w1_ref[...], preferred_element_type=F32)

    tb = x.shape[0]
    lane = lax.broadcasted_iota(jnp.int32, (tb, LANES), 1)
    ang = pos_ref[0].astype(F32) * freq_ref[...]
    cosv = jnp.cos(ang)
    sinv = jnp.sin(ang)
    in_rope = (lane >= MLA_NOPE) & (lane < MLA_QK)

    scale = (MLA_QK ** -0.5) * LOG2E
    qn = _rms(proj[:, :MLA_Q_LORA], qg_ref[...])
    qa = jnp.dot(qn.astype(BF16), wq_ref[...], preferred_element_type=F32)
    cq = jnp.where(lane < MLA_NOPE, scale, jnp.where(in_rope, cosv * scale, 0.0))
    sq = sinv * scale
    cq_t = jnp.concatenate([cq] * MLA_HEADS, axis=1)
    sq_t = jnp.concatenate([sq] * MLA_HEADS, axis=1)
    width = MLA_HEADS * HEAD_SLOT
    q = qa * cq_t + pltpu.roll(qa, width - MLA_ROPE, axis=1) * sq_t
    q_ref[0] = q.astype(BF16)

    kvn = _rms(proj[:, MLA_Q_LORA:MLA_Q_LORA + MLA_KV_LORA], kvg_ref[...])
    kva = jnp.dot(kvn.astype(BF16), wkv_ref[...], preferred_element_type=F32)
    misc = proj[:, MLA_Q_LORA + MLA_KV_LORA:MLA_Q_LORA + MLA_KV_LORA + MISC_W]
    kp = misc * jnp.where(in_rope, cosv, 0.0) + pltpu.roll(misc, MISC_W - MLA_ROPE, axis=1) * sinv
    k = kva[:, :width] + jnp.concatenate([kp] * MLA_HEADS, axis=1)
    k_ref[0] = k.astype(BF16)
    v_ref[0] = kva[:, width:].astype(BF16)

    z = misc + dtb_ref[...]
    softplus = jnp.maximum(z, 0.0) + jnp.log(1.0 + jnp.exp(-jnp.abs(z)))
    g = -jnp.exp(alog_ref[...]) * softplus
    gb_ref[0] = jnp.where(lane < GDN_HEADS, g, _sigmoid(misc))

    gdn_ref[0] = proj[:, MLA_Q_LORA + MLA_KV_LORA + MISC_W:].astype(BF16)


def _in_projection(x, positions, sc1, sh1, norm1_g, w1, q_norm_g, wq, kv_norm_g, wkv,
                   freq, alog, dtb, tb):
    b, s, d = x.shape
    hw = MLA_HEADS * HEAD_SLOT
    const = lambda shape: pl.BlockSpec(shape, lambda i, j: (0,) * len(shape))
    tok = lambda w: pl.BlockSpec((1, tb, w), lambda i, j: (i, j, 0))
    per_b = pl.BlockSpec((1, 1, d), lambda i, j: (i, 0, 0))
    return pl.pallas_call(
        _inproj_kernel,
        out_shape=(jax.ShapeDtypeStruct((b, s, hw), BF16),
                   jax.ShapeDtypeStruct((b, s, hw), BF16),
                   jax.ShapeDtypeStruct((b, s, MLA_HEADS * MLA_V), BF16),
                   jax.ShapeDtypeStruct((b, s, GDN_HEADS * GDN_SLAB), BF16),
                   jax.ShapeDtypeStruct((b, s, MISC_W), F32)),
        grid=(b, s // tb),
        in_specs=[tok(d), tok(1), per_b, per_b, const((1, d)), const(w1.shape),
                  const((1, MLA_Q_LORA)), const(wq.shape), const((1, MLA_KV_LORA)),
                  const(wkv.shape), const((1, LANES)), const((1, LANES)), const((1, LANES))],
        out_specs=(tok(hw), tok(hw), tok(MLA_HEADS * MLA_V), tok(GDN_HEADS * GDN_SLAB),
                   tok(MISC_W)),
        compiler_params=_cparams(("parallel", "parallel")),
        name="in_projection",
    )(x, positions.reshape(b, s, 1), sc1, sh1, norm1_g.reshape(1, d), w1,
      q_norm_g.reshape(1, -1), wq, kv_norm_g.reshape(1, -1), wkv, freq, alog, dtb)


def _attn_kernel(q_ref, k_ref, v_ref, o_ref, s_ref, p_ref, m_ref, l_ref, acc_ref, *, tq, nh):
    qi = pl.program_id(2)
    m_ref[...] = jnp.full(m_ref.shape, -jnp.inf, F32)
    l_ref[...] = jnp.zeros(l_ref.shape, F32)
    acc_ref[...] = jnp.zeros(acc_ref.shape, F32)

    def step(off, masked):
        for h in range(nh):
            kj = k_ref[0, pl.ds(off, tq), h * HEAD_SLOT:(h + 1) * HEAD_SLOT]
            s_ref[h] = lax.dot_general(q_ref[0, :, h * HEAD_SLOT:(h + 1) * HEAD_SLOT], kj,
                                       (((1,), (1,)), ((), ())), preferred_element_type=F32)
        n_strips = tq // ATT_STRIP

        def strip(h, r):
            rows = slice(r * ATT_STRIP, (r + 1) * ATT_STRIP)
            sc = s_ref[h, rows, :]
            if masked:
                rid = lax.broadcasted_iota(jnp.int32, (ATT_STRIP, tq), 0) + r * ATT_STRIP
                cid = lax.broadcasted_iota(jnp.int32, (ATT_STRIP, tq), 1)
                sc = jnp.where(cid <= rid, sc, -jnp.inf)
            return rows, sc

        rep = lambda col: jnp.broadcast_to(col, (col.shape[0], LANES))
        m_new = []
        for h in range(nh):
            mx = jnp.concatenate([rep(jnp.max(strip(h, r)[1], axis=-1, keepdims=True))
                                  for r in range(n_strips)], axis=0)
            m_new.append(jnp.maximum(m_ref[h], mx))
        for h in range(nh):
            vj = v_ref[0, pl.ds(off, tq), (h // 2) * 2 * MLA_V:(h // 2 + 1) * 2 * MLA_V]
            sums = []
            for r in range(n_strips):
                rows, sc = strip(h, r)
                p = jnp.exp2(sc - jnp.concatenate([m_new[h][rows]] * (tq // LANES), axis=1))
                p_ref[h, rows, :] = p.astype(BF16)
                sums.append(rep(jnp.sum(p, axis=-1, keepdims=True)))
            alpha = jnp.exp2(m_ref[h] - m_new[h])
            l_ref[h] = alpha * l_ref[h] + jnp.concatenate(sums, axis=0)
            m_ref[h] = m_new[h]
            acc_ref[h] = alpha * acc_ref[h] + jnp.dot(p_ref[h], vj, preferred_element_type=F32)

    def body(j, carry):
        step(pl.multiple_of(j * tq, tq), False)
        return carry

    lax.fori_loop(0, qi, body, 0)
    step(pl.multiple_of(qi * tq, tq), True)
    lane = lax.broadcasted_iota(jnp.int32, (tq, 2 * MLA_V), 1)
    outs = []
    for p in range(nh // 2):
        o0 = acc_ref[2 * p] / l_ref[2 * p]
        o1 = acc_ref[2 * p + 1] / l_ref[2 * p + 1]
        outs.append(jnp.where(lane < MLA_V, o0, o1))
    o_ref[0] = jnp.concatenate(outs, axis=1).astype(o_ref.dtype)


def _attention(q, k, v, tq, nh):
    b, s, _ = q.shape
    groups = MLA_HEADS // nh
    return pl.pallas_call(
        functools.partial(_attn_kernel, tq=tq, nh=nh),
        out_shape=jax.ShapeDtypeStruct((b, s, MLA_HEADS * MLA_V), BF16),
        grid=(b, groups, s // tq),
        in_specs=[pl.BlockSpec((1, tq, nh * HEAD_SLOT), lambda i, p, j: (i, j, p)),
                  pl.BlockSpec((1, s, nh * HEAD_SLOT), lambda i, p, j: (i, 0, p)),
                  pl.BlockSpec((1, s, nh * MLA_V), lambda i, p, j: (i, 0, p))],
        out_specs=pl.BlockSpec((1, tq, nh * MLA_V), lambda i, p, j: (i, j, p)),
        scratch_shapes=[pltpu.VMEM((nh, tq, tq), F32), pltpu.VMEM((nh, tq, tq), BF16),
                        pltpu.VMEM((nh, tq, LANES), F32), pltpu.VMEM((nh, tq, LANES), F32),
                        pltpu.VMEM((nh, tq, 2 * MLA_V), F32)],
        compiler_params=_cparams(("parallel", "parallel", "arbitrary")),
        name="mla_attention",
    )(q, k, v)


def _bdot(a, b):
    return jnp.dot(a.astype(BF16), b.astype(BF16), preferred_element_type=F32)


def _gdn_kernel(slab_ref, gb_ref, cw_ref, ng_ref, shf_ref, ind_ref, sel_ref, o_ref, tail_ref,
                state_ref, *, ts, hg):
    si = pl.program_id(2)
    width = hg * GDN_SLAB
    hist_rows = 8

    @pl.when(si == 0)
    def _():
        tail_ref[...] = jnp.zeros(tail_ref.shape, F32)
        state_ref[...] = jnp.zeros(state_ref.shape, F32)

    xs_bf = slab_ref[0]
    xs = xs_bf.astype(F32)
    conv = cw_ref[CONV_WIDTH - 1:CONV_WIDTH, :] * xs
    hist = jnp.zeros((hist_rows, width), F32)
    for j in range(CONV_WIDTH - 1):
        wj = cw_ref[j:j + 1, :]
        conv = conv + wj * jnp.dot(shf_ref[j], xs_bf, preferred_element_type=F32)
        start = hist_rows - (CONV_WIDTH - 1) + j
        hist = hist + wj * tail_ref[start:start + hist_rows, :]
    conv = jnp.concatenate([conv[:hist_rows] + hist, conv[hist_rows:]], axis=0)
    tail_ref[0:hist_rows, :] = xs[ts - hist_rows:ts, :]
    act = _silu(conv)

    gb = gb_ref[0]
    g1 = gb.astype(BF16)
    r1 = gb - g1.astype(F32)
    g2 = r1.astype(BF16)
    g3 = (r1 - g2.astype(F32)).astype(BF16)
    sel = sel_ref[0]
    gate_w = (jnp.dot(g1, sel, preferred_element_type=F32)
              + jnp.dot(g2, sel, preferred_element_type=F32)
              + jnp.dot(g3, sel, preferred_element_type=F32))

    n_chunks = ts // CHUNK
    row = lax.broadcasted_iota(jnp.int32, (CHUNK, CHUNK), 0)
    col = lax.broadcasted_iota(jnp.int32, (CHUNK, CHUNK), 1)
    tri_incl = col <= row
    tri_strict = col < row
    eye = col == row
    crow = lax.broadcasted_iota(jnp.int32, (ts, LANES), 0) % CHUNK
    slab_lane = lax.broadcasted_iota(jnp.int32, (ts, GDN_SLAB), 1)

    heads = []
    for hh in range(hg):
        sl_act = act[:, hh * GDN_SLAB:(hh + 1) * GDN_SLAB]
        sq = sl_act * sl_act
        sq_hi = sq.astype(BF16)
        sq_lo = (sq - sq_hi.astype(F32)).astype(BF16)
        ss = (jnp.dot(sq_hi, ind_ref[...], preferred_element_type=F32)
              + jnp.dot(sq_lo, ind_ref[...], preferred_element_type=F32))
        inv = lax.rsqrt(ss + EPS)
        factor = jnp.where(slab_lane < GDN_DK, inv * (GDN_DK ** -0.5),
                           jnp.where(slab_lane < 2 * GDN_DK, inv, 1.0))
        normed = sl_act * factor
        z_all = xs[:, hh * GDN_SLAB + 3 * GDN_DK:(hh + 1) * GDN_SLAB]
        g_w = gate_w[:, hh * LANES:(hh + 1) * LANES]
        b_w = gate_w[:, (hg + hh) * LANES:(hg + hh + 1) * LANES]
        gc_w = g_w
        shift = 1
        while shift < CHUNK:
            rolled = pltpu.roll(gc_w, shift, axis=0)
            gc_w = gc_w + jnp.where(crow >= shift, rolled, 0.0)
            shift *= 2
        heads.append((normed, z_all, b_w, gc_w, jnp.exp(gc_w)))

    a_low, qk, y0, qd, kt, cd, y, ky, qy = ({} for _ in range(9))
    lo, hi = slice(0, GDN_DK), slice(GDN_DK, 2 * GDN_DK)

    def prep(u):
        hh, c = u
        normed, _, b_w, gc_w, eg_w = heads[hh]
        sl = slice(c * CHUNK, (c + 1) * CHUNK)
        qc, kc, vc = normed[sl, lo], normed[sl, hi], normed[sl, 2 * GDN_DK:2 * GDN_DK + GDN_DV]
        gcol = gc_w[sl, lo]
        grow = jnp.sum(jnp.where(eye, gcol, 0.0), axis=0, keepdims=True)
        diff = gcol - grow
        decay = jnp.where(tri_incl, jnp.exp(jnp.where(tri_incl, diff, 0.0)), 0.0)
        k_beta = kc * b_w[sl, hi]
        kq = jnp.concatenate([k_beta, qc], axis=0).astype(BF16)
        kk = lax.dot_general(kq, kc.astype(BF16), (((1,), (1,)), ((), ())),
                             preferred_element_type=F32)
        a_low[u] = jnp.where(tri_strict, kk[:CHUNK] * decay, 0.0)
        qk[u] = jnp.where(tri_incl, kk[CHUNK:] * decay, 0.0)
        y0[u] = jnp.concatenate([vc * b_w[sl, lo], k_beta * eg_w[sl, hi]], axis=1)
        qd[u] = qc * eg_w[sl, lo]
        glast_hi = gc_w[(c + 1) * CHUNK - 1:(c + 1) * CHUNK, hi]
        kt[u] = kc * jnp.exp(glast_hi - gc_w[sl, hi])
        cd[u] = eg_w[(c + 1) * CHUNK - 1:(c + 1) * CHUNK, lo]

    def solve(units):
        blk = 2
        tinv = {u: jnp.where(eye, 1.0, 0.0)
                - jnp.where((row // blk) == (col // blk), a_low[u], 0.0) for u in units}
        while blk < CHUNK:
            in_big = (row // (2 * blk)) == (col // (2 * blk))
            off_mask = in_big & ((row // blk) != (col // blk))
            left = {u: _bdot(tinv[u], jnp.where(off_mask, a_low[u], 0.0)) for u in units}
            tinv = {u: tinv[u] - _bdot(left[u], tinv[u]) for u in units}
            blk *= 2
        for u in units:
            y[u] = _bdot(tinv[u], y0[u])
        for u in units:
            ky[u] = lax.dot_general(kt[u].astype(BF16), y[u].astype(BF16),
                                    (((0,), (0,)), ((), ())),
                                    preferred_element_type=F32)
        for u in units:
            qy[u] = _bdot(qk[u], y[u])

    units = [(hh, c) for c in range(n_chunks) for hh in range(hg)]
    for u in units:
        prep(u)
    solve(units)

    outs = [[] for _ in range(hg)]
    states = [state_ref[hh] for hh in range(hg)]
    for c in range(n_chunks):
        for hh in range(hg):
            u = (hh, c)
            sb = states[hh].astype(BF16)
            r_mat = qd[u] - qy[u][:, GDN_DV:]
            outs[hh].append(jnp.dot(r_mat.astype(BF16), sb, preferred_element_type=F32)
                            + qy[u][:, :GDN_DV])
            states[hh] = (states[hh] * cd[u]
                          - jnp.dot(ky[u][:, GDN_DV:].astype(BF16), sb,
                                    preferred_element_type=F32)
                          + ky[u][:, :GDN_DV])
    finals = []
    for hh in range(hg):
        state_ref[hh] = states[hh]
        o_all = jnp.concatenate(outs[hh], axis=0)
        finals.append(_rms(o_all, ng_ref[...]) * _silu(heads[hh][1]))
    o_ref[0] = jnp.concatenate(finals, axis=1).astype(o_ref.dtype)


def _gdn(slab, gb, conv_slab, gdn_norm_g, ts, hg):
    b, s, _ = slab.shape
    groups = GDN_HEADS // hg
    r = jnp.arange(ts)
    shifts = jnp.stack([(r[:, None] - r[None, :]) == (CONV_WIDTH - 1 - j)
                        for j in range(CONV_WIDTH - 1)]).astype(BF16)
    li = jnp.arange(GDN_SLAB)
    ind = ((li[:, None] // GDN_DK == li[None, :] // GDN_DK)
           & (li[:, None] < 2 * GDN_DK) & (li[None, :] < 2 * GDN_DK)).astype(BF16)
    src = jnp.arange(LANES)[None, :, None]
    dst = jnp.arange(2 * hg * LANES)[None, None, :]
    grp = jnp.arange(groups)[:, None, None]
    want = jnp.where(dst < hg * LANES, grp * hg + dst // LANES,
                     GDN_HEADS + grp * hg + (dst - hg * LANES) // LANES)
    sel = (src == want).astype(BF16)
    return pl.pallas_call(
        functools.partial(_gdn_kernel, ts=ts, hg=hg),
        out_shape=jax.ShapeDtypeStruct((b, s, GDN_HEADS * GDN_DV), BF16),
        grid=(b, groups, s // ts),
        in_specs=[pl.BlockSpec((1, ts, hg * GDN_SLAB), lambda i, p, j: (i, j, p)),
                  pl.BlockSpec((1, ts, MISC_W), lambda i, p, j: (i, j, 0)),
                  pl.BlockSpec((CONV_WIDTH, hg * GDN_SLAB), lambda i, p, j: (0, p)),
                  pl.BlockSpec((1, GDN_DV), lambda i, p, j: (0, 0)),
                  pl.BlockSpec((CONV_WIDTH - 1, ts, ts), lambda i, p, j: (0, 0, 0)),
                  pl.BlockSpec((GDN_SLAB, GDN_SLAB), lambda i, p, j: (0, 0)),
                  pl.BlockSpec((1, LANES, 2 * hg * LANES), lambda i, p, j: (p, 0, 0))],
        out_specs=pl.BlockSpec((1, ts, hg * GDN_DV), lambda i, p, j: (i, j, p)),
        scratch_shapes=[pltpu.VMEM((16, hg * GDN_SLAB), F32),
                        pltpu.VMEM((hg, GDN_DK, GDN_DV), F32)],
        compiler_params=_cparams(("parallel", "parallel", "arbitrary")),
        name="gated_deltanet",
    )(slab, gb, conv_slab, gdn_norm_g.reshape(1, GDN_DV), shifts, ind, sel)


def _outproj_kernel(x_ref, ao_ref, go_ref, gt_ref, sc_ref, sh_ref, mg_ref, wo_ref, g2_ref,
                    rw_ref, rb_ref, x1_ref, h2_ref, route_ref, cnt_ref, carry_ref, *, tb):
    first = (pl.program_id(0) == 0) & (pl.program_id(1) == 0)

    @pl.when(first)
    def _():
        carry_ref[...] = jnp.zeros(carry_ref.shape, F32)

    mla = _rms(ao_ref[0].astype(F32), mg_ref[...])
    cat = jnp.concatenate([mla.astype(BF16), go_ref[0]], axis=1)
    mix = jnp.dot(cat, wo_ref[...], preferred_element_type=F32)
    x1 = x_ref[0] + gt_ref[0] * mix
    x1_ref[0] = x1
    h2 = _rms(x1, g2_ref[...]) * (1.0 + sc_ref[0]) + sh_ref[0]
    h2_ref[0] = _pack_halves(h2)

    h_hi = h2.astype(BF16)
    h_lo = (h2 - h_hi.astype(F32)).astype(BF16)
    main = jnp.dot(h_hi, rw_ref[...], preferred_element_type=F32)
    logits = (main[:, :LANES] + main[:, LANES:]
              + jnp.dot(h_lo, rw_ref[:, :LANES], preferred_element_type=F32) + rb_ref[...])
    lane = lax.broadcasted_iota(jnp.int32, (tb, LANES), 1).astype(F32)
    work = jnp.where(lane < N_EXPERTS, logits, -jnp.inf)
    vals, idxs = [], []
    onehot = jnp.zeros((tb, LANES), F32)
    for _ in range(TOP_K):
        mx = jnp.max(work, axis=-1, keepdims=True)
        ix = jnp.min(jnp.where(work == mx, lane, float(LANES)), axis=-1, keepdims=True)
        sel = lane == ix
        onehot = jnp.where(sel, 1.0, onehot)
        work = jnp.where(sel, -jnp.inf, work)
        vals.append(mx)
        idxs.append(ix)
    exps = [jnp.exp(v - vals[0]) for v in vals]
    den = exps[0] + exps[1] + exps[2] + exps[3]

    r = lax.broadcasted_iota(jnp.int32, (tb, tb), 0)
    c = lax.broadcasted_iota(jnp.int32, (tb, tb), 1)
    tri = jnp.where(c < r, 1.0, 0.0).astype(BF16)
    before = jnp.dot(tri, onehot.astype(BF16), preferred_element_type=F32) + carry_ref[...]
    route = jnp.zeros((tb, LANES), F32)
    for kk in range(TOP_K):
        rank = jnp.sum(jnp.where(lane == idxs[kk], before, 0.0), axis=-1, keepdims=True)
        route = jnp.where(lane == kk, idxs[kk], route)
        route = jnp.where(lane == TOP_K + kk, rank, route)
        route = jnp.where(lane == 2 * TOP_K + kk, exps[kk] / den, route)
    route_ref[0] = route
    total = carry_ref[...] + jnp.sum(onehot, axis=0, keepdims=True)
    carry_ref[...] = total
    cnt_ref[...] = total


def _out_projection(x, attn_o, gdn_o, gt1, sc2, sh2, mla_out_g, w_out, norm2_g, rw, rb, tb, b0, nb):
    _, s, d = x.shape
    const = lambda shape: pl.BlockSpec(shape, lambda i, j: (0,) * len(shape))
    tok_in = lambda w: pl.BlockSpec((1, tb, w), lambda i, j: (i + b0, j, 0))
    tok_out = lambda w: pl.BlockSpec((1, tb, w), lambda i, j: (i, j, 0))
    per_b = pl.BlockSpec((1, 1, d), lambda i, j: (i + b0, 0, 0))
    half = attn_o.shape[-1]
    return pl.pallas_call(
        functools.partial(_outproj_kernel, tb=tb),
        out_shape=(jax.ShapeDtypeStruct((nb, s, d), F32),
                   jax.ShapeDtypeStruct((nb, s, d // 2), F32),
                   jax.ShapeDtypeStruct((nb, s, LANES), F32),
                   jax.ShapeDtypeStruct((1, LANES), F32)),
        grid=(nb, s // tb),
        in_specs=[tok_in(d), tok_in(half), tok_in(half), per_b, per_b, per_b, const((1, half)),
                  const(w_out.shape), const((1, d)), const(rw.shape), const((1, LANES))],
        out_specs=(tok_out(d), tok_out(d // 2), tok_out(LANES), const((1, LANES))),
        scratch_shapes=[pltpu.VMEM((1, LANES), F32)],
        compiler_params=_cparams(("arbitrary", "arbitrary")),
        name="out_projection_router",
    )(x, attn_o, gdn_o, gt1, sc2, sh2, mla_out_g.reshape(1, half), w_out,
      norm2_g.reshape(1, d), rw, rb)


def _sc_mesh():
    return plsc.VectorSubcoreMesh(core_axis_name="c", subcore_axis_name="s",
                                  num_cores=SC_CORES, num_subcores=SC_SUBCORES)


def _sc_worker():
    return lax.axis_index("s") * SC_CORES + lax.axis_index("c")


def _dispatch_rows(h2, dest_km, rows):
    t, d = h2.shape
    per_worker = t // SC_WORKERS
    n_chunks = per_worker // SC_CHUNK

    @functools.partial(
        pl.kernel, out_type=jax.ShapeDtypeStruct((rows, d), h2.dtype), mesh=_sc_mesh(),
        scratch_types=[pltpu.VMEM((SC_CHUNK,), jnp.int32), pltpu.VMEM((SC_CHUNK, d), h2.dtype)],
        name="moe_dispatch")
    def run(h2_hbm, dest_hbm, xs_hbm, idx_v, rows_v):
        base_w = _sc_worker() * per_worker

        @pl.loop(0, n_chunks)
        def _(ci):
            base = pl.multiple_of(base_w + ci * SC_CHUNK, SC_CHUNK)
            pltpu.sync_copy(h2_hbm.at[pl.ds(base, SC_CHUNK)], rows_v)
            for kk in range(TOP_K):
                pltpu.sync_copy(dest_hbm.at[kk, pl.ds(base, SC_CHUNK)], idx_v)
                pltpu.sync_copy(rows_v, xs_hbm.at[idx_v])

    return run(h2, dest_km)


def _collect_rows(y_rows, dest_km):
    _, d = y_rows.shape
    t = dest_km.shape[1]
    per_worker = t // SC_WORKERS
    n_chunks = per_worker // SC_CHUNK

    @functools.partial(
        pl.kernel, out_type=jax.ShapeDtypeStruct((TOP_K, t, d), y_rows.dtype), mesh=_sc_mesh(),
        scratch_types=[pltpu.VMEM((SC_CHUNK,), jnp.int32), pltpu.VMEM((SC_CHUNK, d), y_rows.dtype)],
        name="moe_collect")
    def run(y_hbm, dest_hbm, out_hbm, idx_v, rows_v):
        base_w = _sc_worker() * per_worker

        @pl.loop(0, n_chunks)
        def _(ci):
            base = pl.multiple_of(base_w + ci * SC_CHUNK, SC_CHUNK)
            for kk in range(TOP_K):
                pltpu.sync_copy(dest_hbm.at[kk, pl.ds(base, SC_CHUNK)], idx_v)
                pltpu.sync_copy(y_hbm.at[idx_v], rows_v)
                pltpu.sync_copy(rows_v, out_hbm.at[kk, pl.ds(base, SC_CHUNK)])

    return run(y_rows, dest_km)


def _experts_kernel(be_ref, nu_ref, xs_ref, wgu_ref, bgu_ref, wd_ref, bd_ref, y_ref,
                    wgu_bf, wd_bf):
    i = pl.program_id(0)
    new_expert = (i == 0) | (be_ref[i] != be_ref[jnp.maximum(i - 1, 0)])

    @pl.when(new_expert)
    def _():
        wgu_bf[...] = wgu_ref[0].astype(BF16)
        wd_bf[...] = wd_ref[0].astype(BF16)

    @pl.when(i < nu_ref[0])
    def _():
        x_lo, x_hi = _unpack_halves(xs_ref[...])
        half = x_lo.shape[-1]
        gu = (jnp.dot(x_lo.astype(BF16), wgu_bf[:half, :], preferred_element_type=F32)
              + jnp.dot(x_hi.astype(BF16), wgu_bf[half:, :], preferred_element_type=F32)
              + bgu_ref[0])
        gate = jnp.minimum(gu[:, :D_EXPERT], SWIGLU_LIMIT)
        up = jnp.clip(gu[:, D_EXPERT:], -SWIGLU_LIMIT, SWIGLU_LIMIT)
        act = (up + 1.0) * (gate * _sigmoid(SWIGLU_ALPHA * gate))
        y = jnp.dot(act.astype(BF16), wd_bf[...], preferred_element_type=F32) + bd_ref[0]
        y_ref[...] = _pack_halves(y)


def _experts(blk_e, n_used, xs, wgu, bgu, wd, bd, bm):
    rows, half = xs.shape
    d = 2 * half
    n_blocks = rows // bm
    row_map = lambda i, be, nu: (jnp.maximum(jnp.minimum(i, nu[0] - 1), 0), 0)
    exp_map = lambda i, be, nu: (be[i], 0, 0)
    grid_spec = pltpu.PrefetchScalarGridSpec(
        num_scalar_prefetch=2,
        grid=(n_blocks,),
        in_specs=[pl.BlockSpec((bm, half), row_map),
                  pl.BlockSpec((1, d, 2 * D_EXPERT), exp_map),
                  pl.BlockSpec((1, 1, 2 * D_EXPERT), exp_map),
                  pl.BlockSpec((1, D_EXPERT, d), exp_map),
                  pl.BlockSpec((1, 1, d), exp_map)],
        out_specs=pl.BlockSpec((bm, half), row_map),
        scratch_shapes=[pltpu.VMEM((d, 2 * D_EXPERT), BF16),
                        pltpu.VMEM((D_EXPERT, d), BF16)])
    return pl.pallas_call(
        _experts_kernel,
        out_shape=jax.ShapeDtypeStruct((rows, half), F32),
        grid_spec=grid_spec,
        compiler_params=_cparams(("arbitrary",)),
        name="expert_mlp",
    )(blk_e, n_used, xs, wgu, bgu, wd, bd)


def _final_kernel(x1_ref, yk_ref, route_ref, gt_ref, fg_ref, *rest):
    o_ref = rest[-1]
    route = route_ref[0]
    ffn = jnp.zeros(x1_ref.shape[1:], F32)
    for kk in range(TOP_K):
        wk = route[:, 2 * TOP_K + kk:2 * TOP_K + kk + 1]
        y_lo, y_hi = _unpack_halves(yk_ref[kk, 0])
        ffn = ffn + wk * jnp.concatenate([y_lo, y_hi], axis=1)
    x2 = x1_ref[0] + gt_ref[0] * ffn
    o_ref[0] = _rms(x2, fg_ref[...])


def _final(x1, yk, route, gt2, final_g, tb, b0, b_total, prev):
    nb, s, d = x1.shape
    in_specs = [pl.BlockSpec((1, tb, d), lambda i, j: (i, j, 0)),
                pl.BlockSpec((TOP_K, 1, tb, d // 2), lambda i, j: (0, i, j, 0)),
                pl.BlockSpec((1, tb, LANES), lambda i, j: (i, j, 0)),
                pl.BlockSpec((1, 1, d), lambda i, j: (i + b0, 0, 0)),
                pl.BlockSpec((1, d), lambda i, j: (0, 0))]
    args = [x1, yk, route, gt2, final_g.reshape(1, d)]
    aliases = {}
    if prev is not None:
        in_specs.append(pl.BlockSpec(memory_space=pl.ANY))
        args.append(prev)
        aliases = {len(args) - 1: 0}
    return pl.pallas_call(
        _final_kernel,
        out_shape=jax.ShapeDtypeStruct((b_total, s, d), F32),
        grid=(nb, s // tb),
        in_specs=in_specs,
        out_specs=pl.BlockSpec((1, tb, d), lambda i, j: (i + b0, j, 0)),
        input_output_aliases=aliases,
        compiler_params=_cparams(("parallel", "parallel")),
        name="combine_final_norm",
    )(*args)


def _rot_cols(w):
    half = MLA_ROPE // 2
    return jnp.concatenate([-w[..., half:], w[..., :half]], axis=-1)


def _prep_weights(w_in, w_q_b, w_kv_b, conv_w, A_log, dt_bias):
    d = w_in.shape[0]
    cuts = [MLA_Q_LORA, MLA_KV_LORA, MLA_ROPE, GDN_HEADS * GDN_DK, GDN_HEADS * GDN_DK,
            GDN_HEADS * GDN_DV, GDN_HEADS * GDN_DV, GDN_HEADS, GDN_HEADS]
    offs = [0]
    for cw in cuts:
        offs.append(offs[-1] + cw)
    part = lambda i: w_in[:, offs[i]:offs[i + 1]]
    k_pe = part(2)
    misc = jnp.concatenate(
        [part(7), part(8), jnp.zeros((d, MLA_NOPE - 2 * GDN_HEADS), w_in.dtype), k_pe,
         _rot_cols(k_pe)], axis=1)
    heads = lambda w, n: w.reshape(w.shape[0], GDN_HEADS, n)
    slab = jnp.concatenate([heads(part(3), GDN_DK), heads(part(4), GDN_DK),
                            heads(part(5), GDN_DV), heads(part(6), GDN_DV)], axis=-1)
    w1 = jnp.concatenate([part(0), part(1), misc, slab.reshape(d, GDN_HEADS * GDN_SLAB)],
                         axis=1).astype(BF16)

    wq3 = w_q_b.reshape(MLA_Q_LORA, MLA_HEADS, MLA_QK)
    pe = wq3[..., MLA_NOPE:]
    wq = jnp.concatenate([wq3[..., :MLA_NOPE], pe, _rot_cols(pe)], axis=-1)
    wq = wq.reshape(MLA_Q_LORA, MLA_HEADS * HEAD_SLOT).astype(BF16)

    wkv3 = w_kv_b.reshape(MLA_KV_LORA, MLA_HEADS, MLA_NOPE + MLA_V)
    wk = jnp.concatenate([wkv3[..., :MLA_NOPE],
                          jnp.zeros((MLA_KV_LORA, MLA_HEADS, HEAD_SLOT - MLA_NOPE), w_kv_b.dtype)],
                         axis=-1).reshape(MLA_KV_LORA, MLA_HEADS * HEAD_SLOT)
    wv = wkv3[..., MLA_NOPE:].reshape(MLA_KV_LORA, MLA_HEADS * MLA_V)
    wkv = jnp.concatenate([wk, wv], axis=1).astype(BF16)

    nk = GDN_HEADS * GDN_DK
    cheads = lambda w, n: w.reshape(CONV_WIDTH, GDN_HEADS, n)
    conv_slab = jnp.concatenate(
        [cheads(conv_w[:, :nk], GDN_DK), cheads(conv_w[:, nk:2 * nk], GDN_DK),
         cheads(conv_w[:, 2 * nk:], GDN_DV), jnp.zeros((CONV_WIDTH, GDN_HEADS, GDN_DV), conv_w.dtype)],
        axis=-1).reshape(CONV_WIDTH, GDN_HEADS * GDN_SLAB)

    half = MLA_ROPE // 2
    inv_freq = ROPE_THETA ** (-jnp.arange(half, dtype=F32) / half)
    freq = jnp.concatenate([jnp.zeros((MLA_NOPE,), F32), inv_freq, inv_freq,
                            jnp.zeros((LANES - MLA_QK,), F32)]).reshape(1, LANES)
    padl = lambda a: jnp.concatenate([a.astype(F32), jnp.zeros((LANES - a.shape[0],), F32)]).reshape(1, LANES)
    return w1, wq, wkv, conv_slab, freq, padl(A_log), padl(dt_bias)


def _layer(x, mod, positions, w_in, q_norm_g, w_q_b, kv_norm_g, w_kv_b, mla_out_g, conv_w,
           A_log, dt_bias, gdn_norm_g, w_out, norm1_g, norm2_g, router_w, router_b, w_gate_up,
           b_gate_up, w_down, b_down, final_g):
    b, s, d = x.shape
    t = b * s
    sh1, sc1, gt1, sh2, sc2, gt2 = [m.reshape(b, 1, d) for m in jnp.split(mod, 6, axis=-1)]
    w1, wq, wkv, conv_slab, freq, alog, dtb = _prep_weights(w_in, w_q_b, w_kv_b, conv_w, A_log,
                                                            dt_bias)
    tb = min(256, s)
    q, k, v, slab, gb = _in_projection(x, positions, sc1, sh1, norm1_g, w1, q_norm_g, wq,
                                       kv_norm_g, wkv, freq, alog, dtb, tb)
    attn_o = _attention(q, k, v, min(512, s), 8)
    gdn_o = _gdn(slab, gb, conv_slab, gdn_norm_g, min(256, s), 4)

    rw = jnp.concatenate([router_w, jnp.zeros((d, LANES - N_EXPERTS), router_w.dtype)], axis=1)
    rw_hi = rw.astype(BF16)
    rw = jnp.concatenate([rw_hi, (rw - rw_hi.astype(F32)).astype(BF16)], axis=1)
    rb = jnp.concatenate([router_b, jnp.zeros((LANES - N_EXPERTS,), router_b.dtype)]).reshape(1, LANES)
    w_out_bf = w_out.astype(BF16)
    bgu = b_gate_up.reshape(N_EXPERTS, 1, -1)
    bd = b_down.reshape(N_EXPERTS, 1, -1)
    n_parts = MOE_PARTS if b % MOE_PARTS == 0 else 1
    nb = b // n_parts
    out = None
    for part in range(n_parts):
        b0 = part * nb
        x1, h2, route, counts = _out_projection(x, attn_o, gdn_o, gt1, sc2, sh2, mla_out_g,
                                                w_out_bf, norm2_g, rw, rb, tb, b0, nb)
        yk = _moe(h2, route, counts, w_gate_up, bgu, w_down, bd)
        out = _final(x1, yk, route, gt2, final_g, tb, b0, b, out)
    return out


def _moe(h2, route, counts, w_gate_up, bgu, w_down, bd):
    nb, s, half = h2.shape
    t = nb * s
    bm = EXPERT_ROWS
    route2 = route.reshape(t, LANES)
    idx = route2[:, :TOP_K].astype(jnp.int32)
    rank = route2[:, TOP_K:2 * TOP_K].astype(jnp.int32)
    cnt = counts[0, :N_EXPERTS].astype(jnp.int32)
    padded = ((cnt + bm - 1) // bm) * bm
    pend = jnp.cumsum(padded)
    pstart = pend - padded
    dest = pstart[idx] + rank
    n_blocks = (t * TOP_K + N_EXPERTS * (bm - 1) + bm - 1) // bm
    rows = n_blocks * bm
    n_used = (pend[-1] // bm).astype(jnp.int32).reshape(1)
    blk_start = jnp.arange(n_blocks, dtype=jnp.int32) * bm
    blk_e = jnp.minimum(jnp.sum(blk_start[:, None] >= pend[None, :], axis=1), N_EXPERTS - 1)
    last_e = blk_e[jnp.maximum(n_used[0] - 1, 0)]
    blk_e = jnp.where(jnp.arange(n_blocks) < n_used[0], blk_e, last_e).astype(jnp.int32)

    dest_km = dest.T
    xs = _dispatch_rows(h2.reshape(t, half), dest_km, rows)
    y_rows = _experts(blk_e, n_used, xs, w_gate_up, bgu, w_down, bd, bm)
    return _collect_rows(y_rows, dest_km).reshape(TOP_K, nb, s, half)


def kernel(x, c, positions, ada_w, ada_b, norm1_g, w_in, q_norm_g, w_q_b, kv_norm_g, w_kv_b, mla_out_g, conv_w, A_log, dt_bias, gdn_norm_g, w_out, norm2_g, router_w, router_b, w_gate_up, b_gate_up, w_down, b_down, final_g):
    depth = ada_w.shape[0]
    assert depth == 1
    l = 0
    mod = _modulation(c, ada_w[l], ada_b[l])
    return _layer(x, mod, positions, w_in[l], q_norm_g[l], w_q_b[l], kv_norm_g[l],
                  w_kv_b[l], mla_out_g[l], conv_w[l], A_log[l], dt_bias[l], gdn_norm_g[l],
                  w_out[l], norm1_g[l], norm2_g[l], router_w[l], router_b[l],
                  w_gate_up[l], b_gate_up[l], w_down[l], b_down[l], final_g)
```

```python
import functools

import jax
import jax.numpy as jnp
from jax import lax
from jax.experimental import pallas as pl
from jax.experimental.pallas import tpu as pltpu
from jax.experimental.pallas import tpu_sc as plsc

F32 = jnp.float32
BF16 = jnp.bfloat16

D_MODEL = 1024
EPS = 1e-6
MLA_HEADS = 8
MLA_NOPE = 64
MLA_ROPE = 32
MLA_V = 64
MLA_QK = MLA_NOPE + MLA_ROPE
MLA_Q_LORA = 384
MLA_KV_LORA = 256
ROPE_THETA = 10000.0
GDN_HEADS = 8
GDN_DK = 64
GDN_DV = 64
CONV_WIDTH = 4
CHUNK = 64
N_EXPERTS = 32
TOP_K = 4
D_EXPERT = D_MODEL
SWIGLU_ALPHA = 1.702
SWIGLU_LIMIT = 7.0

LANES = 128
HEAD_SLOT = 128
GDN_SLAB = 256
MISC_W = 128
PROJ_W = MLA_Q_LORA + MLA_KV_LORA + MISC_W + GDN_HEADS * GDN_SLAB
VMEM_LIMIT = 56 * 1024 * 1024
MOE_PARTS = 2
EXPERT_ROWS = 512
SC_CORES = 2
SC_SUBCORES = 16
SC_WORKERS = SC_CORES * SC_SUBCORES
SC_CHUNK = 128
ATT_STRIP = 32
LOG2E = 1.4426950408889634


def _cparams(sem):
    return pltpu.CompilerParams(dimension_semantics=sem, vmem_limit_bytes=VMEM_LIMIT)


def _rms(x, g):
    return x * lax.rsqrt(jnp.mean(x * x, axis=-1, keepdims=True) + EPS) * g


def _sigmoid(x):
    return 1.0 / (1.0 + jnp.exp(-x))


def _silu(x):
    return x * _sigmoid(x)


def _pack_halves(x):
    w = x.shape[-1] // 2
    lo = lax.bitcast_convert_type(x[:, :w].astype(BF16).astype(F32), jnp.uint32) >> 16
    hi = lax.bitcast_convert_type(x[:, w:].astype(BF16).astype(F32), jnp.uint32) & jnp.uint32(0xFFFF0000)
    return lax.bitcast_convert_type(lo | hi, F32)


def _unpack_halves(p):
    u = lax.bitcast_convert_type(p, jnp.uint32)
    return (lax.bitcast_convert_type(u << 16, F32),
            lax.bitcast_convert_type(u & jnp.uint32(0xFFFF0000), F32))


def _mod_kernel(c_ref, w_ref, b_ref, o_ref):
    c = c_ref[...]
    o_ref[...] = jnp.dot(_silu(c), w_ref[...], preferred_element_type=F32,
                         precision=lax.Precision.HIGHEST) + b_ref[...]


def _modulation(c, ada_w, ada_b):
    b, d = c.shape
    n = ada_w.shape[1]
    return pl.pallas_call(
        _mod_kernel,
        out_shape=jax.ShapeDtypeStruct((b, n), F32),
        grid=(n // d,),
        in_specs=[pl.BlockSpec((b, d), lambda j: (0, 0)),
                  pl.BlockSpec((d, d), lambda j: (0, j)),
                  pl.BlockSpec((1, d), lambda j: (0, j))],
        out_specs=pl.BlockSpec((b, d), lambda j: (0, j)),
        compiler_params=_cparams(("arbitrary",)),
        name="adaln_mod",
    )(c, ada_w, ada_b.reshape(1, n))


def _inproj_kernel(x_ref, pos_ref, sc_ref, sh_ref, g1_ref, w1_ref, qg_ref, wq_ref, kvg_ref,
                   wkv_ref, freq_ref, alog_ref, dtb_ref,
                   q_ref, k_ref, v_ref, gdn_ref, gb_ref):
    x = x_ref[0]
    h = _rms(x, g1_ref[...]) * (1.0 + sc_ref[0]) + sh_ref[0]
    proj = jnp.dot(h.astype(BF16), w1_ref[...], preferred_element_type=F32)

    tb = x.shape[0]
    lane = lax.broadcasted_iota(jnp.int32, (tb, LANES), 1)
    ang = pos_ref[0].astype(F32) * freq_ref[...]
    cosv = jnp.cos(ang)
    sinv = jnp.sin(ang)
    in_rope = (lane >= MLA_NOPE) & (lane < MLA_QK)

    scale = (MLA_QK ** -0.5) * LOG2E
    qn = _rms(proj[:, :MLA_Q_LORA], qg_ref[...])
    qa = jnp.dot(qn.astype(BF16), wq_ref[...], preferred_element_type=F32)
    cq = jnp.where(lane < MLA_NOPE, scale, jnp.where(in_rope, cosv * scale, 0.0))
    sq = sinv * scale
    cq_t = jnp.concatenate([cq] * MLA_HEADS, axis=1)
    sq_t = jnp.concatenate([sq] * MLA_HEADS, axis=1)
    width = MLA_HEADS * HEAD_SLOT
    q = qa * cq_t + pltpu.roll(qa, width - MLA_ROPE, axis=1) * sq_t
    q_ref[0] = q.astype(BF16)

    kvn = _rms(proj[:, MLA_Q_LORA:MLA_Q_LORA + MLA_KV_LORA], kvg_ref[...])
    kva = jnp.dot(kvn.astype(BF16), wkv_ref[...], preferred_element_type=F32)
    misc = proj[:, MLA_Q_LORA + MLA_KV_LORA:MLA_Q_LORA + MLA_KV_LORA + MISC_W]
    kp = misc * jnp.where(in_rope, cosv, 0.0) + pltpu.roll(misc, MISC_W - MLA_ROPE, axis=1) * sinv
    k = kva[:, :width] + jnp.concatenate([kp] * MLA_HEADS, axis=1)
    k_ref[0] = k.astype(BF16)
    v_ref[0] = kva[:, width:].astype(BF16)

    z = misc + dtb_ref[...]
    softplus = jnp.maximum(z, 0.0) + jnp.log(1.0 + jnp.exp(-jnp.abs(z)))
    g = -jnp.exp(alog_ref[...]) * softplus
    gb_ref[0] = jnp.where(lane < GDN_HEADS, g, _sigmoid(misc))

    gdn_ref[0] = proj[:, MLA_Q_LORA + MLA_KV_LORA + MISC_W:].astype(BF16)


def _in_projection(x, positions, sc1, sh1, norm1_g, w1, q_norm_g, wq, kv_norm_g, wkv,
                   freq, alog, dtb, tb):
    b, s, d = x.shape
    hw = MLA_HEADS * HEAD_SLOT
    const = lambda shape: pl.BlockSpec(shape, lambda i, j: (0,) * len(shape))
    tok = lambda w: pl.BlockSpec((1, tb, w), lambda i, j: (i, j, 0))
    per_b = pl.BlockSpec((1, 1, d), lambda i, j: (i, 0, 0))
    return pl.pallas_call(
        _inproj_kernel,
        out_shape=(jax.ShapeDtypeStruct((b, s, hw), BF16),
                   jax.ShapeDtypeStruct((b, s, hw), BF16),
                   jax.ShapeDtypeStruct((b, s, MLA_HEADS * MLA_V), BF16),
                   jax.ShapeDtypeStruct((b, s, GDN_HEADS * GDN_SLAB), BF16),
                   jax.ShapeDtypeStruct((b, s, MISC_W), F32)),
        grid=(b, s // tb),
        in_specs=[tok(d), tok(1), per_b, per_b, const((1, d)), const(w1.shape),
                  const((1, MLA_Q_LORA)), const(wq.shape), const((1, MLA_KV_LORA)),
                  const(wkv.shape), const((1, LANES)), const((1, LANES)), const((1, LANES))],
        out_specs=(tok(hw), tok(hw), tok(MLA_HEADS * MLA_V), tok(GDN_HEADS * GDN_SLAB),
                   tok(MISC_W)),
        compiler_params=_cparams(("parallel", "parallel")),
        name="in_projection",
    )(x, positions.reshape(b, s, 1), sc1, sh1, norm1_g.reshape(1, d), w1,
      q_norm_g.reshape(1, -1), wq, kv_norm_g.reshape(1, -1), wkv, freq, alog, dtb)


def _attn_kernel(q_ref, k_ref, v_ref, o_ref, s_ref, p_ref, m_ref, l_ref, acc_ref, *, tq, nh):
    qi = pl.program_id(2)
    m_ref[...] = jnp.full(m_ref.shape, -jnp.inf, F32)
    l_ref[...] = jnp.zeros(l_ref.shape, F32)
    acc_ref[...] = jnp.zeros(acc_ref.shape, F32)

    def step(off, masked):
        for h in range(nh):
            kj = k_ref[0, pl.ds(off, tq), h * HEAD_SLOT:(h + 1) * HEAD_SLOT]
            s_ref[h] = lax.dot_general(q_ref[0, :, h * HEAD_SLOT:(h + 1) * HEAD_SLOT], kj,
                                       (((1,), (1,)), ((), ())), preferred_element_type=F32)
        n_strips = tq // ATT_STRIP

        def strip(h, r):
            rows = slice(r * ATT_STRIP, (r + 1) * ATT_STRIP)
            sc = s_ref[h, rows, :]
            if masked:
                rid = lax.broadcasted_iota(jnp.int32, (ATT_STRIP, tq), 0) + r * ATT_STRIP
                cid = lax.broadcasted_iota(jnp.int32, (ATT_STRIP, tq), 1)
                sc = jnp.where(cid <= rid, sc, -jnp.inf)
            return rows, sc

        rep = lambda col: jnp.broadcast_to(col, (col.shape[0], LANES))
        m_new = []
        for h in range(nh):
            mx = jnp.concatenate([rep(jnp.max(strip(h, r)[1], axis=-1, keepdims=True))
                                  for r in range(n_strips)], axis=0)
            m_new.append(jnp.maximum(m_ref[h], mx))
        for h in range(nh):
            vj = v_ref[0, pl.ds(off, tq), (h // 2) * 2 * MLA_V:(h // 2 + 1) * 2 * MLA_V]
            sums = []
            for r in range(n_strips):
                rows, sc = strip(h, r)
                p = jnp.exp2(sc - jnp.concatenate([m_new[h][rows]] * (tq // LANES), axis=1))
                p_ref[h, rows, :] = p.astype(BF16)
                sums.append(rep(jnp.sum(p, axis=-1, keepdims=True)))
            alpha = jnp.exp2(m_ref[h] - m_new[h])
            l_ref[h] = alpha * l_ref[h] + jnp.concatenate(sums, axis=0)
            m_ref[h] = m_new[h]
            acc_ref[h] = alpha * acc_ref[h] + jnp.dot(p_ref[h], vj, preferred_element_type=F32)

    def body(j, carry):
        step(pl.multiple_of(j * tq, tq), False)
        return carry

    lax.fori_loop(0, qi, body, 0)
    step(pl.multiple_of(qi * tq, tq), True)
    lane = lax.broadcasted_iota(jnp.int32, (tq, 2 * MLA_V), 1)
    outs = []
    for p in range(nh // 2):
        o0 = acc_ref[2 * p] / l_ref[2 * p]
        o1 = acc_ref[2 * p + 1] / l_ref[2 * p + 1]
        outs.append(jnp.where(lane < MLA_V, o0, o1))
    o_ref[0] = jnp.concatenate(outs, axis=1).astype(o_ref.dtype)


def _attention(q, k, v, tq, nh):
    b, s, _ = q.shape
    groups = MLA_HEADS // nh
    return pl.pallas_call(
        functools.partial(_attn_kernel, tq=tq, nh=nh),
        out_shape=jax.ShapeDtypeStruct((b, s, MLA_HEADS * MLA_V), BF16),
        grid=(b, groups, s // tq),
        in_specs=[pl.BlockSpec((1, tq, nh * HEAD_SLOT), lambda i, p, j: (i, j, p)),
                  pl.BlockSpec((1, s, nh * HEAD_SLOT), lambda i, p, j: (i, 0, p)),
                  pl.BlockSpec((1, s, nh * MLA_V), lambda i, p, j: (i, 0, p))],
        out_specs=pl.BlockSpec((1, tq, nh * MLA_V), lambda i, p, j: (i, j, p)),
        scratch_shapes=[pltpu.VMEM((nh, tq, tq), F32), pltpu.VMEM((nh, tq, tq), BF16),
                        pltpu.VMEM((nh, tq, LANES), F32), pltpu.VMEM((nh, tq, LANES), F32),
                        pltpu.VMEM((nh, tq, 2 * MLA_V), F32)],
        compiler_params=_cparams(("parallel", "parallel", "arbitrary")),
        name="mla_attention",
    )(q, k, v)


def _bdot(a, b):
    return jnp.dot(a.astype(BF16), b.astype(BF16), preferred_element_type=F32)


def _gdn_kernel(slab_ref, gb_ref, cw_ref, ng_ref, shf_ref, ind_ref, sel_ref, o_ref, tail_ref,
                state_ref, *, ts, hg):
    si = pl.program_id(2)
    width = hg * GDN_SLAB
    hist_rows = 8

    @pl.when(si == 0)
    def _():
        tail_ref[...] = jnp.zeros(tail_ref.shape, F32)
        state_ref[...] = jnp.zeros(state_ref.shape, F32)

    xs_bf = slab_ref[0]
    xs = xs_bf.astype(F32)
    conv = cw_ref[CONV_WIDTH - 1:CONV_WIDTH, :] * xs
    hist = jnp.zeros((hist_rows, width), F32)
    for j in range(CONV_WIDTH - 1):
        wj = cw_ref[j:j + 1, :]
        conv = conv + wj * jnp.dot(shf_ref[j], xs_bf, preferred_element_type=F32)
        start = hist_rows - (CONV_WIDTH - 1) + j
        hist = hist + wj * tail_ref[start:start + hist_rows, :]
    conv = jnp.concatenate([conv[:hist_rows] + hist, conv[hist_rows:]], axis=0)
    tail_ref[0:hist_rows, :] = xs[ts - hist_rows:ts, :]
    act = _silu(conv)

    gb = gb_ref[0]
    g1 = gb.astype(BF16)
    r1 = gb - g1.astype(F32)
    g2 = r1.astype(BF16)
    g3 = (r1 - g2.astype(F32)).astype(BF16)
    sel = sel_ref[0]
    gate_w = (jnp.dot(g1, sel, preferred_element_type=F32)
              + jnp.dot(g2, sel, preferred_element_type=F32)
              + jnp.dot(g3, sel, preferred_element_type=F32))

    n_chunks = ts // CHUNK
    row = lax.broadcasted_iota(jnp.int32, (CHUNK, CHUNK), 0)
    col = lax.broadcasted_iota(jnp.int32, (CHUNK, CHUNK), 1)
    tri_incl = col <= row
    tri_strict = col < row
    eye = col == row
    crow = lax.broadcasted_iota(jnp.int32, (ts, LANES), 0) % CHUNK
    slab_lane = lax.broadcasted_iota(jnp.int32, (ts, GDN_SLAB), 1)

    heads = []
    for hh in range(hg):
        sl_act = act[:, hh * GDN_SLAB:(hh + 1) * GDN_SLAB]
        sq = sl_act * sl_act
        sq_hi = sq.astype(BF16)
        sq_lo = (sq - sq_hi.astype(F32)).astype(BF16)
        ss = (jnp.dot(sq_hi, ind_ref[...], preferred_element_type=F32)
              + jnp.dot(sq_lo, ind_ref[...], preferred_element_type=F32))
        inv = lax.rsqrt(ss + EPS)
        factor = jnp.where(slab_lane < GDN_DK, inv * (GDN_DK ** -0.5),
                           jnp.where(slab_lane < 2 * GDN_DK, inv, 1.0))
        normed = sl_act * factor
        z_all = xs[:, hh * GDN_SLAB + 3 * GDN_DK:(hh + 1) * GDN_SLAB]
        g_w = gate_w[:, hh * LANES:(hh + 1) * LANES]
        b_w = gate_w[:, (hg + hh) * LANES:(hg + hh + 1) * LANES]
        gc_w = g_w
        shift = 1
        while shift < CHUNK:
            rolled = pltpu.roll(gc_w, shift, axis=0)
            gc_w = gc_w + jnp.where(crow >= shift, rolled, 0.0)
            shift *= 2
        heads.append((normed, z_all, b_w, gc_w, jnp.exp(gc_w)))

    a_low, qk, y0, qd, kt, cd, y, ky, qy = ({} for _ in range(9))
    lo, hi = slice(0, GDN_DK), slice(GDN_DK, 2 * GDN_DK)

    def prep(u):
        hh, c = u
        normed, _, b_w, gc_w, eg_w = heads[hh]
        sl = slice(c * CHUNK, (c + 1) * CHUNK)
        qc, kc, vc = normed[sl, lo], normed[sl, hi], normed[sl, 2 * GDN_DK:2 * GDN_DK + GDN_DV]
        gcol = gc_w[sl, lo]
        grow = jnp.sum(jnp.where(eye, gcol, 0.0), axis=0, keepdims=True)
        diff = gcol - grow
        decay = jnp.where(tri_incl, jnp.exp(jnp.where(tri_incl, diff, 0.0)), 0.0)
        k_beta = kc * b_w[sl, hi]
        kq = jnp.concatenate([k_beta, qc], axis=0).astype(BF16)
        kk = lax.dot_general(kq, kc.astype(BF16), (((1,), (1,)), ((), ())),
                             preferred_element_type=F32)
        a_low[u] = jnp.where(tri_strict, kk[:CHUNK] * decay, 0.0)
        qk[u] = jnp.where(tri_incl, kk[CHUNK:] * decay, 0.0)
        y0[u] = jnp.concatenate([vc * b_w[sl, lo], k_beta * eg_w[sl, hi]], axis=1)
        qd[u] = qc * eg_w[sl, lo]
        glast_hi = gc_w[(c + 1) * CHUNK - 1:(c + 1) * CHUNK, hi]
        kt[u] = kc * jnp.exp(glast_hi - gc_w[sl, hi])
        cd[u] = eg_w[(c + 1) * CHUNK - 1:(c + 1) * CHUNK, lo]

    def solve(units):
        blk = 2
        tinv = {u: jnp.where(eye, 1.0, 0.0)
                - jnp.where((row // blk) == (col // blk), a_low[u], 0.0) for u in units}
        while blk < CHUNK:
            in_big = (row // (2 * blk)) == (col // (2 * blk))
            off_mask = in_big & ((row // blk) != (col // blk))
            left = {u: _bdot(tinv[u], jnp.where(off_mask, a_low[u], 0.0)) for u in units}
            tinv = {u: tinv[u] - _bdot(left[u], tinv[u]) for u in units}
            blk *= 2
        for u in units:
            y[u] = _bdot(tinv[u], y0[u])
        for u in units:
            ky[u] = lax.dot_general(kt[u].astype(BF16), y[u].astype(BF16),
                                    (((0,), (0,)), ((), ())),
                                    preferred_element_type=F32)
        for u in units:
            qy[u] = _bdot(qk[u], y[u])

    units = [(hh, c) for c in range(n_chunks) for hh in range(hg)]
    for u in units:
        prep(u)
    solve(units)

    outs = [[] for _ in range(hg)]
    states = [state_ref[hh] for hh in range(hg)]
    for c in range(n_chunks):
        for hh in range(hg):
            u = (hh, c)
            sb = states[hh].astype(BF16)
            r_mat = qd[u] - qy[u][:, GDN_DV:]
            outs[hh].append(jnp.dot(r_mat.astype(BF16), sb, preferred_element_type=F32)
                            + qy[u][:, :GDN_DV])
            states[hh] = (states[hh] * cd[u]
                          - jnp.dot(ky[u][:, GDN_DV:].astype(BF16), sb,
                                    preferred_element_type=F32)
                          + ky[u][:, :GDN_DV])
    finals = []
    for hh in range(hg):
        state_ref[hh] = states[hh]
        o_all = jnp.concatenate(outs[hh], axis=0)
        finals.append(_rms(o_all, ng_ref[...]) * _silu(heads[hh][1]))
    o_ref[0] = jnp.concatenate(finals, axis=1).astype(o_ref.dtype)


def _gdn(slab, gb, conv_slab, gdn_norm_g, ts, hg):
    b, s, _ = slab.shape
    groups = GDN_HEADS // hg
    r = jnp.arange(ts)
    shifts = jnp.stack([(r[:, None] - r[None, :]) == (CONV_WIDTH - 1 - j)
                        for j in range(CONV_WIDTH - 1)]).astype(BF16)
    li = jnp.arange(GDN_SLAB)
    ind = ((li[:, None] // GDN_DK == li[None, :] // GDN_DK)
           & (li[:, None] < 2 * GDN_DK) & (li[None, :] < 2 * GDN_DK)).astype(BF16)
    src = jnp.arange(LANES)[None, :, None]
    dst = jnp.arange(2 * hg * LANES)[None, None, :]
    grp = jnp.arange(groups)[:, None, None]
    want = jnp.where(dst < hg * LANES, grp * hg + dst // LANES,
                     GDN_HEADS + grp * hg + (dst - hg * LANES) // LANES)
    sel = (src == want).astype(BF16)
    return pl.pallas_call(
        functools.partial(_gdn_kernel, ts=ts, hg=hg),
        out_shape=jax.ShapeDtypeStruct((b, s, GDN_HEADS * GDN_DV), BF16),
        grid=(b, groups, s // ts),
        in_specs=[pl.BlockSpec((1, ts, hg * GDN_SLAB), lambda i, p, j: (i, j, p)),
                  pl.BlockSpec((1, ts, MISC_W), lambda i, p, j: (i, j, 0)),
                  pl.BlockSpec((CONV_WIDTH, hg * GDN_SLAB), lambda i, p, j: (0, p)),
                  pl.BlockSpec((1, GDN_DV), lambda i, p, j: (0, 0)),
                  pl.BlockSpec((CONV_WIDTH - 1, ts, ts), lambda i, p, j: (0, 0, 0)),
                  pl.BlockSpec((GDN_SLAB, GDN_SLAB), lambda i, p, j: (0, 0)),
                  pl.BlockSpec((1, LANES, 2 * hg * LANES), lambda i, p, j: (p, 0, 0))],
        out_specs=pl.BlockSpec((1, ts, hg * GDN_DV), lambda i, p, j: (i, j, p)),
        scratch_shapes=[pltpu.VMEM((16, hg * GDN_SLAB), F32),
                        pltpu.VMEM((hg, GDN_DK, GDN_DV), F32)],
        compiler_params=_cparams(("parallel", "parallel", "arbitrary")),
        name="gated_deltanet",
    )(slab, gb, conv_slab, gdn_norm_g.reshape(1, GDN_DV), shifts, ind, sel)


def _outproj_kernel(x_ref, ao_ref, go_ref, gt_ref, sc_ref, sh_ref, mg_ref, wo_ref, g2_ref,
                    rw_ref, rb_ref, x1_ref, h2_ref, route_ref, cnt_ref, carry_ref, *, tb):
    first = (pl.program_id(0) == 0) & (pl.program_id(1) == 0)

    @pl.when(first)
    def _():
        carry_ref[...] = jnp.zeros(carry_ref.shape, F32)

    mla = _rms(ao_ref[0].astype(F32), mg_ref[...])
    cat = jnp.concatenate([mla.astype(BF16), go_ref[0]], axis=1)
    mix = jnp.dot(cat, wo_ref[...], preferred_element_type=F32)
    x1 = x_ref[0] + gt_ref[0] * mix
    x1_ref[0] = x1
    h2 = _rms(x1, g2_ref[...]) * (1.0 + sc_ref[0]) + sh_ref[0]
    h2_ref[0] = _pack_halves(h2)

    h_hi = h2.astype(BF16)
    h_lo = (h2 - h_hi.astype(F32)).astype(BF16)
    main = jnp.dot(h_hi, rw_ref[...], preferred_element_type=F32)
    logits = (main[:, :LANES] + main[:, LANES:]
              + jnp.dot(h_lo, rw_ref[:, :LANES], preferred_element_type=F32) + rb_ref[...])
    lane = lax.broadcasted_iota(jnp.int32, (tb, LANES), 1).astype(F32)
    work = jnp.where(lane < N_EXPERTS, logits, -jnp.inf)
    vals, idxs = [], []
    onehot = jnp.zeros((tb, LANES), F32)
    for _ in range(TOP_K):
        mx = jnp.max(work, axis=-1, keepdims=True)
        ix = jnp.min(jnp.where(work == mx, lane, float(LANES)), axis=-1, keepdims=True)
        sel = lane == ix
        onehot = jnp.where(sel, 1.0, onehot)
        work = jnp.where(sel, -jnp.inf, work)
        vals.append(mx)
        idxs.append(ix)
    exps = [jnp.exp(v - vals[0]) for v in vals]
    den = exps[0] + exps[1] + exps[2] + exps[3]

    r = lax.broadcasted_iota(jnp.int32, (tb, tb), 0)
    c = lax.broadcasted_iota(jnp.int32, (tb, tb), 1)
    tri = jnp.where(c < r, 1.0, 0.0).astype(BF16)
    before = jnp.dot(tri, onehot.astype(BF16), preferred_element_type=F32) + carry_ref[...]
    route = jnp.zeros((tb, LANES), F32)
    for kk in range(TOP_K):
        rank = jnp.sum(jnp.where(lane == idxs[kk], before, 0.0), axis=-1, keepdims=True)
        route = jnp.where(lane == kk, idxs[kk], route)
        route = jnp.where(lane == TOP_K + kk, rank, route)
        route = jnp.where(lane == 2 * TOP_K + kk, exps[kk] / den, route)
    route_ref[0] = route
    total = carry_ref[...] + jnp.sum(onehot, axis=0, keepdims=True)
    carry_ref[...] = total
    cnt_ref[...] = total


def _out_projection(x, attn_o, gdn_o, gt1, sc2, sh2, mla_out_g, w_out, norm2_g, rw, rb, tb, b0, nb):
    _, s, d = x.shape
    const = lambda shape: pl.BlockSpec(shape, lambda i, j: (0,) * len(shape))
    tok_in = lambda w: pl.BlockSpec((1, tb, w), lambda i, j: (i + b0, j, 0))
    tok_out = lambda w: pl.BlockSpec((1, tb, w), lambda i, j: (i, j, 0))
    per_b = pl.BlockSpec((1, 1, d), lambda i, j: (i + b0, 0, 0))
    half = attn_o.shape[-1]
    return pl.pallas_call(
        functools.partial(_outproj_kernel, tb=tb),
        out_shape=(jax.ShapeDtypeStruct((nb, s, d), F32),
                   jax.ShapeDtypeStruct((nb, s, d // 2), F32),
                   jax.ShapeDtypeStruct((nb, s, LANES), F32),
                   jax.ShapeDtypeStruct((1, LANES), F32)),
        grid=(nb, s // tb),
        in_specs=[tok_in(d), tok_in(half), tok_in(half), per_b, per_b, per_b, const((1, half)),
                  const(w_out.shape), const((1, d)), const(rw.shape), const((1, LANES))],
        out_specs=(tok_out(d), tok_out(d // 2), tok_out(LANES), const((1, LANES))),
        scratch_shapes=[pltpu.VMEM((1, LANES), F32)],
        compiler_params=_cparams(("arbitrary", "arbitrary")),
        name="out_projection_router",
    )(x, attn_o, gdn_o, gt1, sc2, sh2, mla_out_g.reshape(1, half), w_out,
      norm2_g.reshape(1, d), rw, rb)


def _sc_mesh():
    return plsc.VectorSubcoreMesh(core_axis_name="c", subcore_axis_name="s",
                                  num_cores=SC_CORES, num_subcores=SC_SUBCORES)


def _sc_worker():
    return lax.axis_index("s") * SC_CORES + lax.axis_index("c")


def _dispatch_rows(h2, dest_km, rows):
    t, d = h2.shape
    per_worker = t // SC_WORKERS
    n_chunks = per_worker // SC_CHUNK

    @functools.partial(
        pl.kernel, out_type=jax.ShapeDtypeStruct((rows, d), h2.dtype), mesh=_sc_mesh(),
        scratch_types=[pltpu.VMEM((SC_CHUNK,), jnp.int32), pltpu.VMEM((SC_CHUNK, d), h2.dtype)],
        name="moe_dispatch")
    def run(h2_hbm, dest_hbm, xs_hbm, idx_v, rows_v):
        base_w = _sc_worker() * per_worker

        @pl.loop(0, n_chunks)
        def _(ci):
            base = pl.multiple_of(base_w + ci * SC_CHUNK, SC_CHUNK)
            pltpu.sync_copy(h2_hbm.at[pl.ds(base, SC_CHUNK)], rows_v)
            for kk in range(TOP_K):
                pltpu.sync_copy(dest_hbm.at[kk, pl.ds(base, SC_CHUNK)], idx_v)
                pltpu.sync_copy(rows_v, xs_hbm.at[idx_v])

    return run(h2, dest_km)


def _collect_rows(y_rows, dest_km):
    _, d = y_rows.shape
    t = dest_km.shape[1]
    per_worker = t // SC_WORKERS
    n_chunks = per_worker // SC_CHUNK

    @functools.partial(
        pl.kernel, out_type=jax.ShapeDtypeStruct((TOP_K, t, d), y_rows.dtype), mesh=_sc_mesh(),
        scratch_types=[pltpu.VMEM((SC_CHUNK,), jnp.int32), pltpu.VMEM((SC_CHUNK, d), y_rows.dtype)],
        name="moe_collect")
    def run(y_hbm, dest_hbm, out_hbm, idx_v, rows_v):
        base_w = _sc_worker() * per_worker

        @pl.loop(0, n_chunks)
        def _(ci):
            base = pl.multiple_of(base_w + ci * SC_CHUNK, SC_CHUNK)
            for kk in range(TOP_K):
                pltpu.sync_copy(dest_hbm.at[kk, pl.ds(base, SC_CHUNK)], idx_v)
                pltpu.sync_copy(y_hbm.at[idx_v], rows_v)
                pltpu.sync_copy(rows_v, out_hbm.at[kk, pl.ds(base, SC_CHUNK)])

    return run(y_rows, dest_km)


def _experts_kernel(be_ref, nu_ref, nxt_ref, slot_ref, xs_ref, wgu_hbm, bgu_ref, wd_hbm, bd_ref,
                    y_ref, wgu_f32, wd_f32, wgu_bf, wd_bf, sem):
    i = pl.program_id(0)
    e = be_ref[i]
    new_expert = (i == 0) | (e != be_ref[jnp.maximum(i - 1, 0)])

    def fetch(expert, slot):
        return (pltpu.make_async_copy(wgu_hbm.at[expert], wgu_f32.at[slot], sem.at[0, slot]),
                pltpu.make_async_copy(wd_hbm.at[expert], wd_f32.at[slot], sem.at[1, slot]))

    @pl.when(i == 0)
    def _():
        for cp in fetch(e, slot_ref[e]):
            cp.start()

    @pl.when(new_expert)
    def _():
        slot = slot_ref[e]
        for cp in fetch(e, slot):
            cp.wait()
        wgu_bf[...] = wgu_f32[slot].astype(BF16)
        wd_bf[...] = wd_f32[slot].astype(BF16)
        nxt = nxt_ref[e]

        @pl.when(nxt >= 0)
        def _():
            for cp in fetch(nxt, 1 - slot):
                cp.start()

    @pl.when(i < nu_ref[0])
    def _():
        x_lo, x_hi = _unpack_halves(xs_ref[...])
        half = x_lo.shape[-1]
        gu = (jnp.dot(x_lo.astype(BF16), wgu_bf[:half, :], preferred_element_type=F32)
              + jnp.dot(x_hi.astype(BF16), wgu_bf[half:, :], preferred_element_type=F32)
              + bgu_ref[0])
        gate = jnp.minimum(gu[:, :D_EXPERT], SWIGLU_LIMIT)
        up = jnp.clip(gu[:, D_EXPERT:], -SWIGLU_LIMIT, SWIGLU_LIMIT)
        act = (up + 1.0) * (gate * _sigmoid(SWIGLU_ALPHA * gate))
        y = jnp.dot(act.astype(BF16), wd_bf[...], preferred_element_type=F32) + bd_ref[0]
        y_ref[...] = _pack_halves(y)


def _experts(blk_e, n_used, next_e, slot_e, xs, wgu, bgu, wd, bd, bm):
    rows, half = xs.shape
    d = 2 * half
    n_blocks = rows // bm
    row_map = lambda i, be, nu, nx, sl: (jnp.maximum(jnp.minimum(i, nu[0] - 1), 0), 0)
    exp_map = lambda i, be, nu, nx, sl: (be[i], 0, 0)
    grid_spec = pltpu.PrefetchScalarGridSpec(
        num_scalar_prefetch=4,
        grid=(n_blocks,),
        in_specs=[pl.BlockSpec((bm, half), row_map),
                  pl.BlockSpec(memory_space=pl.ANY),
                  pl.BlockSpec((1, 1, 2 * D_EXPERT), exp_map),
                  pl.BlockSpec(memory_space=pl.ANY),
                  pl.BlockSpec((1, 1, d), exp_map)],
        out_specs=pl.BlockSpec((bm, half), row_map),
        scratch_shapes=[pltpu.VMEM((2, d, 2 * D_EXPERT), F32),
                        pltpu.VMEM((2, D_EXPERT, d), F32),
                        pltpu.VMEM((d, 2 * D_EXPERT), BF16),
                        pltpu.VMEM((D_EXPERT, d), BF16),
                        pltpu.SemaphoreType.DMA((2, 2))])
    return pl.pallas_call(
        _experts_kernel,
        out_shape=jax.ShapeDtypeStruct((rows, half), F32),
        grid_spec=grid_spec,
        compiler_params=_cparams(("arbitrary",)),
        name="expert_mlp",
    )(blk_e, n_used, next_e, slot_e, xs, wgu, bgu, wd, bd)


def _final_kernel(x1_ref, yk_ref, route_ref, gt_ref, fg_ref, *rest):
    o_ref = rest[-1]
    route = route_ref[0]
    ffn = jnp.zeros(x1_ref.shape[1:], F32)
    for kk in range(TOP_K):
        wk = route[:, 2 * TOP_K + kk:2 * TOP_K + kk + 1]
        y_lo, y_hi = _unpack_halves(yk_ref[kk, 0])
        ffn = ffn + wk * jnp.concatenate([y_lo, y_hi], axis=1)
    x2 = x1_ref[0] + gt_ref[0] * ffn
    o_ref[0] = _rms(x2, fg_ref[...])


def _final(x1, yk, route, gt2, final_g, tb, b0, b_total, prev):
    nb, s, d = x1.shape
    in_specs = [pl.BlockSpec((1, tb, d), lambda i, j: (i, j, 0)),
                pl.BlockSpec((TOP_K, 1, tb, d // 2), lambda i, j: (0, i, j, 0)),
                pl.BlockSpec((1, tb, LANES), lambda i, j: (i, j, 0)),
                pl.BlockSpec((1, 1, d), lambda i, j: (i + b0, 0, 0)),
                pl.BlockSpec((1, d), lambda i, j: (0, 0))]
    args = [x1, yk, route, gt2, final_g.reshape(1, d)]
    aliases = {}
    if prev is not None:
        in_specs.append(pl.BlockSpec(memory_space=pl.ANY))
        args.append(prev)
        aliases = {len(args) - 1: 0}
    return pl.pallas_call(
        _final_kernel,
        out_shape=jax.ShapeDtypeStruct((b_total, s, d), F32),
        grid=(nb, s // tb),
        in_specs=in_specs,
        out_specs=pl.BlockSpec((1, tb, d), lambda i, j: (i + b0, j, 0)),
        input_output_aliases=aliases,
        compiler_params=_cparams(("parallel", "parallel")),
        name="combine_final_norm",
    )(*args)


def _rot_cols(w):
    half = MLA_ROPE // 2
    return jnp.concatenate([-w[..., half:], w[..., :half]], axis=-1)


def _prep_weights(w_in, w_q_b, w_kv_b, conv_w, A_log, dt_bias):
    d = w_in.shape[0]
    cuts = [MLA_Q_LORA, MLA_KV_LORA, MLA_ROPE, GDN_HEADS * GDN_DK, GDN_HEADS * GDN_DK,
            GDN_HEADS * GDN_DV, GDN_HEADS * GDN_DV, GDN_HEADS, GDN_HEADS]
    offs = [0]
    for cw in cuts:
        offs.append(offs[-1] + cw)
    part = lambda i: w_in[:, offs[i]:offs[i + 1]]
    k_pe = part(2)
    misc = jnp.concatenate(
        [part(7), part(8), jnp.zeros((d, MLA_NOPE - 2 * GDN_HEADS), w_in.dtype), k_pe,
         _rot_cols(k_pe)], axis=1)
    heads = lambda w, n: w.reshape(w.shape[0], GDN_HEADS, n)
    slab = jnp.concatenate([heads(part(3), GDN_DK), heads(part(4), GDN_DK),
                            heads(part(5), GDN_DV), heads(part(6), GDN_DV)], axis=-1)
    w1 = jnp.concatenate([part(0), part(1), misc, slab.reshape(d, GDN_HEADS * GDN_SLAB)],
                         axis=1).astype(BF16)

    wq3 = w_q_b.reshape(MLA_Q_LORA, MLA_HEADS, MLA_QK)
    pe = wq3[..., MLA_NOPE:]
    wq = jnp.concatenate([wq3[..., :MLA_NOPE], pe, _rot_cols(pe)], axis=-1)
    wq = wq.reshape(MLA_Q_LORA, MLA_HEADS * HEAD_SLOT).astype(BF16)

    wkv3 = w_kv_b.reshape(MLA_KV_LORA, MLA_HEADS, MLA_NOPE + MLA_V)
    wk = jnp.concatenate([wkv3[..., :MLA_NOPE],
                          jnp.zeros((MLA_KV_LORA, MLA_HEADS, HEAD_SLOT - MLA_NOPE), w_kv_b.dtype)],
                         axis=-1).reshape(MLA_KV_LORA, MLA_HEADS * HEAD_SLOT)
    wv = wkv3[..., MLA_NOPE:].reshape(MLA_KV_LORA, MLA_HEADS * MLA_V)
    wkv = jnp.concatenate([wk, wv], axis=1).astype(BF16)

    nk = GDN_HEADS * GDN_DK
    cheads = lambda w, n: w.reshape(CONV_WIDTH, GDN_HEADS, n)
    conv_slab = jnp.concatenate(
        [cheads(conv_w[:, :nk], GDN_DK), cheads(conv_w[:, nk:2 * nk], GDN_DK),
         cheads(conv_w[:, 2 * nk:], GDN_DV), jnp.zeros((CONV_WIDTH, GDN_HEADS, GDN_DV), conv_w.dtype)],
        axis=-1).reshape(CONV_WIDTH, GDN_HEADS * GDN_SLAB)

    half = MLA_ROPE // 2
    inv_freq = ROPE_THETA ** (-jnp.arange(half, dtype=F32) / half)
    freq = jnp.concatenate([jnp.zeros((MLA_NOPE,), F32), inv_freq, inv_freq,
                            jnp.zeros((LANES - MLA_QK,), F32)]).reshape(1, LANES)
    padl = lambda a: jnp.concatenate([a.astype(F32), jnp.zeros((LANES - a.shape[0],), F32)]).reshape(1, LANES)
    return w1, wq, wkv, conv_slab, freq, padl(A_log), padl(dt_bias)


def _layer(x, mod, positions, w_in, q_norm_g, w_q_b, kv_norm_g, w_kv_b, mla_out_g, conv_w,
           A_log, dt_bias, gdn_norm_g, w_out, norm1_g, norm2_g, router_w, router_b, w_gate_up,
           b_gate_up, w_down, b_down, final_g):
    b, s, d = x.shape
    t = b * s
    sh1, sc1, gt1, sh2, sc2, gt2 = [m.reshape(b, 1, d) for m in jnp.split(mod, 6, axis=-1)]
    w1, wq, wkv, conv_slab, freq, alog, dtb = _prep_weights(w_in, w_q_b, w_kv_b, conv_w, A_log,
                                                            dt_bias)
    tb = min(256, s)
    q, k, v, slab, gb = _in_projection(x, positions, sc1, sh1, norm1_g, w1, q_norm_g, wq,
                                       kv_norm_g, wkv, freq, alog, dtb, tb)
    attn_o = _attention(q, k, v, min(512, s), 8)
    gdn_o = _gdn(slab, gb, conv_slab, gdn_norm_g, min(256, s), 4)

    rw = jnp.concatenate([router_w, jnp.zeros((d, LANES - N_EXPERTS), router_w.dtype)], axis=1)
    rw_hi = rw.astype(BF16)
    rw = jnp.concatenate([rw_hi, (rw - rw_hi.astype(F32)).astype(BF16)], axis=1)
    rb = jnp.concatenate([router_b, jnp.zeros((LANES - N_EXPERTS,), router_b.dtype)]).reshape(1, LANES)
    w_out_bf = w_out.astype(BF16)
    bgu = b_gate_up.reshape(N_EXPERTS, 1, -1)
    bd = b_down.reshape(N_EXPERTS, 1, -1)
    n_parts = MOE_PARTS if b % MOE_PARTS == 0 else 1
    nb = b // n_parts
    out = None
    for part in range(n_parts):
        b0 = part * nb
        x1, h2, route, counts = _out_projection(x, attn_o, gdn_o, gt1, sc2, sh2, mla_out_g,
                                                w_out_bf, norm2_g, rw, rb, tb, b0, nb)
        yk = _moe(h2, route, counts, w_gate_up, bgu, w_down, bd)
        out = _final(x1, yk, route, gt2, final_g, tb, b0, b, out)
    return out


def _moe(h2, route, counts, w_gate_up, bgu, w_down, bd):
    nb, s, half = h2.shape
    t = nb * s
    bm = EXPERT_ROWS
    route2 = route.reshape(t, LANES)
    idx = route2[:, :TOP_K].astype(jnp.int32)
    rank = route2[:, TOP_K:2 * TOP_K].astype(jnp.int32)
    cnt = counts[0, :N_EXPERTS].astype(jnp.int32)
    padded = ((cnt + bm - 1) // bm) * bm
    pend = jnp.cumsum(padded)
    pstart = pend - padded
    dest = pstart[idx] + rank
    n_blocks = (t * TOP_K + N_EXPERTS * (bm - 1) + bm - 1) // bm
    rows = n_blocks * bm
    n_used = (pend[-1] // bm).astype(jnp.int32).reshape(1)
    blk_start = jnp.arange(n_blocks, dtype=jnp.int32) * bm
    blk_e = jnp.minimum(jnp.sum(blk_start[:, None] >= pend[None, :], axis=1), N_EXPERTS - 1)
    last_e = blk_e[jnp.maximum(n_used[0] - 1, 0)]
    blk_e = jnp.where(jnp.arange(n_blocks) < n_used[0], blk_e, last_e).astype(jnp.int32)

    present = padded > 0
    eidx = jnp.arange(N_EXPERTS, dtype=jnp.int32)
    later = present[None, :] & (eidx[None, :] > eidx[:, None])
    next_e = jnp.where(jnp.any(later, axis=1), jnp.argmax(later, axis=1), -1).astype(jnp.int32)
    slot_e = ((jnp.cumsum(present.astype(jnp.int32)) - 1) % 2).astype(jnp.int32)

    dest_km = dest.T
    xs = _dispatch_rows(h2.reshape(t, half), dest_km, rows)
    y_rows = _experts(blk_e, n_used, next_e, slot_e, xs, w_gate_up, bgu, w_down, bd, bm)
    return _collect_rows(y_rows, dest_km).reshape(TOP_K, nb, s, half)


def kernel(x, c, positions, ada_w, ada_b, norm1_g, w_in, q_norm_g, w_q_b, kv_norm_g, w_kv_b, mla_out_g, conv_w, A_log, dt_bias, gdn_norm_g, w_out, norm2_g, router_w, router_b, w_gate_up, b_gate_up, w_down, b_down, final_g):
    depth = ada_w.shape[0]
    assert depth == 1
    l = 0
    mod = _modulation(c, ada_w[l], ada_b[l])
    return _layer(x, mod, positions, w_in[l], q_norm_g[l], w_q_b[l], kv_norm_g[l],
                  w_kv_b[l], mla_out_g[l], conv_w[l], A_log[l], dt_bias[l], gdn_norm_g[l],
                  w_out[l], norm1_g[l], norm2_g[l], router_w[l], router_b[l],
                  w_gate_up[l], b_gate_up[l], w_down[l], b_down[l], final_g)
```

```python
import functools

import jax
import jax.numpy as jnp
from jax import lax
from jax.experimental import pallas as pl
from jax.experimental.pallas import tpu as pltpu
from jax.experimental.pallas import tpu_sc as plsc

F32 = jnp.float32
BF16 = jnp.bfloat16

D_MODEL = 1024
EPS = 1e-6
MLA_HEADS = 8
MLA_NOPE = 64
MLA_ROPE = 32
MLA_V = 64
MLA_QK = MLA_NOPE + MLA_ROPE
MLA_Q_LORA = 384
MLA_KV_LORA = 256
ROPE_THETA = 10000.0
GDN_HEADS = 8
GDN_DK = 64
GDN_DV = 64
CONV_WIDTH = 4
CHUNK = 64
N_EXPERTS = 32
TOP_K = 4
D_EXPERT = D_MODEL
SWIGLU_ALPHA = 1.702
SWIGLU_LIMIT = 7.0

LANES = 128
HEAD_SLOT = 128
GDN_SLAB = 256
MISC_W = 128
PROJ_W = MLA_Q_LORA + MLA_KV_LORA + MISC_W + GDN_HEADS * GDN_SLAB
VMEM_LIMIT = 56 * 1024 * 1024
MOE_PARTS = 2
EXPERT_ROWS = 512
SC_CORES = 2
SC_SUBCORES = 16
SC_WORKERS = SC_CORES * SC_SUBCORES
SC_CHUNK = 128
ROPE_GROUPS = LANES // MLA_ROPE
ROUTE_ROWS = 16
ATT_STRIP = 32
LOG2E = 1.4426950408889634


def _cparams(sem):
    return pltpu.CompilerParams(dimension_semantics=sem, vmem_limit_bytes=VMEM_LIMIT)


def _rms(x, g):
    return x * lax.rsqrt(jnp.mean(x * x, axis=-1, keepdims=True) + EPS) * g


def _sigmoid(x):
    return 1.0 / (1.0 + jnp.exp(-x))


def _silu(x):
    return x * _sigmoid(x)


def _pack_halves(x):
    w = x.shape[-1] // 2
    lo = lax.bitcast_convert_type(x[:, :w].astype(BF16).astype(F32), jnp.uint32) >> 16
    hi = lax.bitcast_convert_type(x[:, w:].astype(BF16).astype(F32), jnp.uint32) & jnp.uint32(0xFFFF0000)
    return lax.bitcast_convert_type(lo | hi, F32)


def _unpack_halves(p):
    u = lax.bitcast_convert_type(p, jnp.uint32)
    return (lax.bitcast_convert_type(u << 16, F32),
            lax.bitcast_convert_type(u & jnp.uint32(0xFFFF0000), F32))


def _mod_kernel(c_ref, w_ref, b_ref, o_ref):
    c = c_ref[...]
    o_ref[...] = jnp.dot(_silu(c), w_ref[...], preferred_element_type=F32,
                         precision=lax.Precision.HIGHEST) + b_ref[...]


def _modulation(c, ada_w, ada_b):
    b, d = c.shape
    n = ada_w.shape[1]
    return pl.pallas_call(
        _mod_kernel,
        out_shape=jax.ShapeDtypeStruct((b, n), F32),
        grid=(n // d,),
        in_specs=[pl.BlockSpec((b, d), lambda j: (0, 0)),
                  pl.BlockSpec((d, d), lambda j: (0, j)),
                  pl.BlockSpec((1, d), lambda j: (0, j))],
        out_specs=pl.BlockSpec((b, d), lambda j: (0, j)),
        compiler_params=_cparams(("arbitrary",)),
        name="adaln_mod",
    )(c, ada_w, ada_b.reshape(1, n))


def _inproj_kernel(x_ref, pos_ref, sc_ref, sh_ref, g1_ref, w1_ref, qg_ref, wq_ref, kvg_ref,
                   wkv_ref, freq_ref, alog_ref, dtb_ref,
                   q_ref, k_ref, v_ref, gdn_ref, gb_ref):
    x = x_ref[0]
    h = _rms(x, g1_ref[...]) * (1.0 + sc_ref[0]) + sh_ref[0]
    proj = jnp.dot(h.astype(BF16), w1_ref[...], preferred_element_type=F32)

    tb = x.shape[0]
    lane = lax.broadcasted_iota(jnp.int32, (tb, LANES), 1)
    in_rope = (lane >= MLA_NOPE) & (lane < MLA_QK)
    pos = pos_ref[0].astype(F32)
    quarter = tb // ROPE_GROUPS
    ang = pos[0:quarter] * freq_ref[0:1, :]
    for g in range(1, ROPE_GROUPS):
        ang = ang + pos[g * quarter:(g + 1) * quarter] * freq_ref[g:g + 1, :]
    cos_p = jnp.cos(ang)
    sin_p = jnp.sin(ang)
    unpack = lambda t: jnp.concatenate(
        [pltpu.roll(t, (MLA_NOPE - MLA_ROPE * g) % LANES, axis=1) for g in range(ROPE_GROUPS)],
        axis=0)
    cosv = unpack(cos_p)
    sinv = jnp.where(in_rope, unpack(sin_p), 0.0)

    scale = (MLA_QK ** -0.5) * LOG2E
    qn = _rms(proj[:, :MLA_Q_LORA], qg_ref[...])
    qa = jnp.dot(qn.astype(BF16), wq_ref[...], preferred_element_type=F32)
    cq = jnp.where(lane < MLA_NOPE, scale, jnp.where(in_rope, cosv * scale, 0.0))
    sq = sinv * scale
    cq_t = jnp.concatenate([cq] * MLA_HEADS, axis=1)
    sq_t = jnp.concatenate([sq] * MLA_HEADS, axis=1)
    width = MLA_HEADS * HEAD_SLOT
    q = qa * cq_t + pltpu.roll(qa, width - MLA_ROPE, axis=1) * sq_t
    q_ref[0] = q.astype(BF16)

    kvn = _rms(proj[:, MLA_Q_LORA:MLA_Q_LORA + MLA_KV_LORA], kvg_ref[...])
    kva = jnp.dot(kvn.astype(BF16), wkv_ref[...], preferred_element_type=F32)
    misc = proj[:, MLA_Q_LORA + MLA_KV_LORA:MLA_Q_LORA + MLA_KV_LORA + MISC_W]
    kp = misc * jnp.where(in_rope, cosv, 0.0) + pltpu.roll(misc, MISC_W - MLA_ROPE, axis=1) * sinv
    k = kva[:, :width] + jnp.concatenate([kp] * MLA_HEADS, axis=1)
    k_ref[0] = k.astype(BF16)
    v_ref[0] = kva[:, width:].astype(BF16)

    z = misc + dtb_ref[...]
    softplus = jnp.maximum(z, 0.0) + jnp.log(1.0 + jnp.exp(-jnp.abs(z)))
    g = -jnp.exp(alog_ref[...]) * softplus
    gb_ref[0] = jnp.where(lane < GDN_HEADS, g, _sigmoid(misc))

    gdn_ref[0] = proj[:, MLA_Q_LORA + MLA_KV_LORA + MISC_W:].astype(BF16)


def _in_projection(x, positions, sc1, sh1, norm1_g, w1, q_norm_g, wq, kv_norm_g, wkv,
                   freq, alog, dtb, tb):
    b, s, d = x.shape
    hw = MLA_HEADS * HEAD_SLOT
    const = lambda shape: pl.BlockSpec(shape, lambda i, j: (0,) * len(shape))
    tok = lambda w: pl.BlockSpec((1, tb, w), lambda i, j: (i, j, 0))
    per_b = pl.BlockSpec((1, 1, d), lambda i, j: (i, 0, 0))
    return pl.pallas_call(
        _inproj_kernel,
        out_shape=(jax.ShapeDtypeStruct((b, s, hw), BF16),
                   jax.ShapeDtypeStruct((b, s, hw), BF16),
                   jax.ShapeDtypeStruct((b, s, MLA_HEADS * MLA_V), BF16),
                   jax.ShapeDtypeStruct((b, s, GDN_HEADS * GDN_SLAB), BF16),
                   jax.ShapeDtypeStruct((b, s, MISC_W), F32)),
        grid=(b, s // tb),
        in_specs=[tok(d), tok(1), per_b, per_b, const((1, d)), const(w1.shape),
                  const((1, MLA_Q_LORA)), const(wq.shape), const((1, MLA_KV_LORA)),
                  const(wkv.shape), const((ROPE_GROUPS, LANES)), const((1, LANES)),
                  const((1, LANES))],
        out_specs=(tok(hw), tok(hw), tok(MLA_HEADS * MLA_V), tok(GDN_HEADS * GDN_SLAB),
                   tok(MISC_W)),
        compiler_params=_cparams(("parallel", "parallel")),
        name="in_projection",
    )(x, positions.reshape(b, s, 1), sc1, sh1, norm1_g.reshape(1, d), w1,
      q_norm_g.reshape(1, -1), wq, kv_norm_g.reshape(1, -1), wkv, freq, alog, dtb)


def _attn_kernel(q_ref, k_ref, v_ref, o_ref, s_ref, p_ref, m_ref, l_ref, acc_ref, *, tq, nh):
    qi = pl.program_id(2)
    m_ref[...] = jnp.full(m_ref.shape, -jnp.inf, F32)
    l_ref[...] = jnp.zeros(l_ref.shape, F32)
    acc_ref[...] = jnp.zeros(acc_ref.shape, F32)

    def step(off, masked):
        segs = [(0, tq // 2, tq // 2), (tq // 2, tq, tq)] if masked else [(0, tq, tq)]
        width = lambda r: next(nc for r0, r1, nc in segs if r0 <= r * ATT_STRIP < r1)
        for h in range(nh):
            for r0, r1, nc in segs:
                kj = k_ref[0, pl.ds(off, nc), h * HEAD_SLOT:(h + 1) * HEAD_SLOT]
                s_ref[h, r0:r1, 0:nc] = lax.dot_general(
                    q_ref[0, r0:r1, h * HEAD_SLOT:(h + 1) * HEAD_SLOT], kj,
                    (((1,), (1,)), ((), ())), preferred_element_type=F32)
        n_strips = tq // ATT_STRIP

        def strip(h, r):
            rows = slice(r * ATT_STRIP, (r + 1) * ATT_STRIP)
            nc = width(r)
            sc = s_ref[h, rows, 0:nc]
            if masked:
                rid = lax.broadcasted_iota(jnp.int32, (ATT_STRIP, nc), 0) + r * ATT_STRIP
                cid = lax.broadcasted_iota(jnp.int32, (ATT_STRIP, nc), 1)
                sc = jnp.where(cid <= rid, sc, -jnp.inf)
            return rows, sc

        rep = lambda col: jnp.broadcast_to(col, (col.shape[0], LANES))
        m_new = []
        for h in range(nh):
            mx = jnp.concatenate([rep(jnp.max(strip(h, r)[1], axis=-1, keepdims=True))
                                  for r in range(n_strips)], axis=0)
            m_new.append(jnp.maximum(m_ref[h], mx))
        for h in range(nh):
            v_lanes = slice((h // 2) * 2 * MLA_V, (h // 2 + 1) * 2 * MLA_V)
            sums = []
            for r in range(n_strips):
                rows, sc = strip(h, r)
                nc = width(r)
                p = jnp.exp2(sc - jnp.concatenate([m_new[h][rows]] * (nc // LANES), axis=1))
                p_ref[h, rows, 0:nc] = p.astype(BF16)
                sums.append(rep(jnp.sum(p, axis=-1, keepdims=True)))
            alpha = jnp.exp2(m_ref[h] - m_new[h])
            l_ref[h] = alpha * l_ref[h] + jnp.concatenate(sums, axis=0)
            m_ref[h] = m_new[h]
            for r0, r1, nc in segs:
                acc_ref[h, r0:r1, :] = alpha[r0:r1] * acc_ref[h, r0:r1, :] + jnp.dot(
                    p_ref[h, r0:r1, 0:nc], v_ref[0, pl.ds(off, nc), v_lanes],
                    preferred_element_type=F32)

    def body(j, carry):
        step(pl.multiple_of(j * tq, tq), False)
        return carry

    lax.fori_loop(0, qi, body, 0)
    step(pl.multiple_of(qi * tq, tq), True)
    lane = lax.broadcasted_iota(jnp.int32, (tq, 2 * MLA_V), 1)
    outs = []
    for p in range(nh // 2):
        o0 = acc_ref[2 * p] / l_ref[2 * p]
        o1 = acc_ref[2 * p + 1] / l_ref[2 * p + 1]
        outs.append(jnp.where(lane < MLA_V, o0, o1))
    o_ref[0] = jnp.concatenate(outs, axis=1).astype(o_ref.dtype)


def _attention(q, k, v, tq, nh):
    b, s, _ = q.shape
    groups = MLA_HEADS // nh
    return pl.pallas_call(
        functools.partial(_attn_kernel, tq=tq, nh=nh),
        out_shape=jax.ShapeDtypeStruct((b, s, MLA_HEADS * MLA_V), BF16),
        grid=(b, groups, s // tq),
        in_specs=[pl.BlockSpec((1, tq, nh * HEAD_SLOT), lambda i, p, j: (i, j, p)),
                  pl.BlockSpec((1, s, nh * HEAD_SLOT), lambda i, p, j: (i, 0, p)),
                  pl.BlockSpec((1, s, nh * MLA_V), lambda i, p, j: (i, 0, p))],
        out_specs=pl.BlockSpec((1, tq, nh * MLA_V), lambda i, p, j: (i, j, p)),
        scratch_shapes=[pltpu.VMEM((nh, tq, tq), F32), pltpu.VMEM((nh, tq, tq), BF16),
                        pltpu.VMEM((nh, tq, LANES), F32), pltpu.VMEM((nh, tq, LANES), F32),
                        pltpu.VMEM((nh, tq, 2 * MLA_V), F32)],
        compiler_params=_cparams(("parallel", "parallel", "arbitrary")),
        name="mla_attention",
    )(q, k, v)


def _bdot(a, b):
    return jnp.dot(a.astype(BF16), b.astype(BF16), preferred_element_type=F32)


def _gdn_kernel(slab_ref, gb_ref, cw_ref, ng_ref, shf_ref, ind_ref, sel_ref, o_ref, tail_ref,
                state_ref, *, ts, hg):
    si = pl.program_id(2)
    width = hg * GDN_SLAB
    hist_rows = 8

    @pl.when(si == 0)
    def _():
        tail_ref[...] = jnp.zeros(tail_ref.shape, F32)
        state_ref[...] = jnp.zeros(state_ref.shape, F32)

    xs_bf = slab_ref[0]
    xs = xs_bf.astype(F32)
    conv = cw_ref[CONV_WIDTH - 1:CONV_WIDTH, :] * xs
    hist = jnp.zeros((hist_rows, width), F32)
    for j in range(CONV_WIDTH - 1):
        wj = cw_ref[j:j + 1, :]
        conv = conv + wj * jnp.dot(shf_ref[j], xs_bf, preferred_element_type=F32)
        start = hist_rows - (CONV_WIDTH - 1) + j
        hist = hist + wj * tail_ref[start:start + hist_rows, :]
    conv = jnp.concatenate([conv[:hist_rows] + hist, conv[hist_rows:]], axis=0)
    tail_ref[0:hist_rows, :] = xs[ts - hist_rows:ts, :]
    act = _silu(conv)

    gb = gb_ref[0]
    g1 = gb.astype(BF16)
    r1 = gb - g1.astype(F32)
    g2 = r1.astype(BF16)
    g3 = (r1 - g2.astype(F32)).astype(BF16)
    sel = sel_ref[0]
    gate_w = (jnp.dot(g1, sel, preferred_element_type=F32)
              + jnp.dot(g2, sel, preferred_element_type=F32)
              + jnp.dot(g3, sel, preferred_element_type=F32))

    n_chunks = ts // CHUNK
    row = lax.broadcasted_iota(jnp.int32, (CHUNK, CHUNK), 0)
    col = lax.broadcasted_iota(jnp.int32, (CHUNK, CHUNK), 1)
    tri_incl = col <= row
    tri_strict = col < row
    eye = col == row
    crow = lax.broadcasted_iota(jnp.int32, (ts, LANES), 0) % CHUNK
    slab_lane = lax.broadcasted_iota(jnp.int32, (ts, GDN_SLAB), 1)

    heads = []
    for hh in range(hg):
        sl_act = act[:, hh * GDN_SLAB:(hh + 1) * GDN_SLAB]
        sq = sl_act * sl_act
        sq_hi = sq.astype(BF16)
        sq_lo = (sq - sq_hi.astype(F32)).astype(BF16)
        ss = (jnp.dot(sq_hi, ind_ref[...], preferred_element_type=F32)
              + jnp.dot(sq_lo, ind_ref[...], preferred_element_type=F32))
        inv = lax.rsqrt(ss + EPS)
        factor = jnp.where(slab_lane < GDN_DK, inv * (GDN_DK ** -0.5),
                           jnp.where(slab_lane < 2 * GDN_DK, inv, 1.0))
        normed = sl_act * factor
        z_all = xs[:, hh * GDN_SLAB + 3 * GDN_DK:(hh + 1) * GDN_SLAB]
        g_w = gate_w[:, hh * LANES:(hh + 1) * LANES]
        b_w = gate_w[:, (hg + hh) * LANES:(hg + hh + 1) * LANES]
        gc_w = g_w
        shift = 1
        while shift < CHUNK:
            rolled = pltpu.roll(gc_w, shift, axis=0)
            gc_w = gc_w + jnp.where(crow >= shift, rolled, 0.0)
            shift *= 2
        heads.append((normed, z_all, b_w, gc_w, jnp.exp(gc_w)))

    a_low, qk, y0, qd, kt, cd, y, ky, qy = ({} for _ in range(9))
    lo, hi = slice(0, GDN_DK), slice(GDN_DK, 2 * GDN_DK)

    def prep(u):
        hh, c = u
        normed, _, b_w, gc_w, eg_w = heads[hh]
        sl = slice(c * CHUNK, (c + 1) * CHUNK)
        qc, kc, vc = normed[sl, lo], normed[sl, hi], normed[sl, 2 * GDN_DK:2 * GDN_DK + GDN_DV]
        gcol = gc_w[sl, lo]
        grow = jnp.sum(jnp.where(eye, gcol, 0.0), axis=0, keepdims=True)
        diff = gcol - grow
        decay = jnp.where(tri_incl, jnp.exp(jnp.where(tri_incl, diff, 0.0)), 0.0)
        k_beta = kc * b_w[sl, hi]
        kq = jnp.concatenate([k_beta, qc], axis=0).astype(BF16)
        kk = lax.dot_general(kq, kc.astype(BF16), (((1,), (1,)), ((), ())),
                             preferred_element_type=F32)
        a_low[u] = jnp.where(tri_strict, kk[:CHUNK] * decay, 0.0)
        qk[u] = jnp.where(tri_incl, kk[CHUNK:] * decay, 0.0)
        y0[u] = jnp.concatenate([vc * b_w[sl, lo], k_beta * eg_w[sl, hi]], axis=1)
        qd[u] = qc * eg_w[sl, lo]
        glast_hi = gc_w[(c + 1) * CHUNK - 1:(c + 1) * CHUNK, hi]
        kt[u] = kc * jnp.exp(glast_hi - gc_w[sl, hi])
        cd[u] = eg_w[(c + 1) * CHUNK - 1:(c + 1) * CHUNK, lo]

    def solve(units):
        blk = 2
        tinv = {u: jnp.where(eye, 1.0, 0.0)
                - jnp.where((row // blk) == (col // blk), a_low[u], 0.0) for u in units}
        while blk < CHUNK:
            in_big = (row // (2 * blk)) == (col // (2 * blk))
            off_mask = in_big & ((row // blk) != (col // blk))
            left = {u: _bdot(tinv[u], jnp.where(off_mask, a_low[u], 0.0)) for u in units}
            tinv = {u: tinv[u] - _bdot(left[u], tinv[u]) for u in units}
            blk *= 2
        for u in units:
            y[u] = _bdot(tinv[u], y0[u])
        for u in units:
            ky[u] = lax.dot_general(kt[u].astype(BF16), y[u].astype(BF16),
                                    (((0,), (0,)), ((), ())),
                                    preferred_element_type=F32)
        for u in units:
            qy[u] = _bdot(qk[u], y[u])

    units = [(hh, c) for c in range(n_chunks) for hh in range(hg)]
    for u in units:
        prep(u)
    solve(units)

    outs = [[] for _ in range(hg)]
    states = [state_ref[hh] for hh in range(hg)]
    for c in range(n_chunks):
        for hh in range(hg):
            u = (hh, c)
            sb = states[hh].astype(BF16)
            r_mat = qd[u] - qy[u][:, GDN_DV:]
            outs[hh].append(jnp.dot(r_mat.astype(BF16), sb, preferred_element_type=F32)
                            + qy[u][:, :GDN_DV])
            states[hh] = (states[hh] * cd[u]
                          - jnp.dot(ky[u][:, GDN_DV:].astype(BF16), sb,
                                    preferred_element_type=F32)
                          + ky[u][:, :GDN_DV])
    finals = []
    for hh in range(hg):
        state_ref[hh] = states[hh]
        o_all = jnp.concatenate(outs[hh], axis=0)
        finals.append(_rms(o_all, ng_ref[...]) * _silu(heads[hh][1]))
    o_ref[0] = jnp.concatenate(finals, axis=1).astype(o_ref.dtype)


def _gdn(slab, gb, conv_slab, gdn_norm_g, ts, hg):
    b, s, _ = slab.shape
    groups = GDN_HEADS // hg
    r = jnp.arange(ts)
    shifts = jnp.stack([(r[:, None] - r[None, :]) == (CONV_WIDTH - 1 - j)
                        for j in range(CONV_WIDTH - 1)]).astype(BF16)
    li = jnp.arange(GDN_SLAB)
    ind = ((li[:, None] // GDN_DK == li[None, :] // GDN_DK)
           & (li[:, None] < 2 * GDN_DK) & (li[None, :] < 2 * GDN_DK)).astype(BF16)
    src = jnp.arange(LANES)[None, :, None]
    dst = jnp.arange(2 * hg * LANES)[None, None, :]
    grp = jnp.arange(groups)[:, None, None]
    want = jnp.where(dst < hg * LANES, grp * hg + dst // LANES,
                     GDN_HEADS + grp * hg + (dst - hg * LANES) // LANES)
    sel = (src == want).astype(BF16)
    return pl.pallas_call(
        functools.partial(_gdn_kernel, ts=ts, hg=hg),
        out_shape=jax.ShapeDtypeStruct((b, s, GDN_HEADS * GDN_DV), BF16),
        grid=(b, groups, s // ts),
        in_specs=[pl.BlockSpec((1, ts, hg * GDN_SLAB), lambda i, p, j: (i, j, p)),
                  pl.BlockSpec((1, ts, MISC_W), lambda i, p, j: (i, j, 0)),
                  pl.BlockSpec((CONV_WIDTH, hg * GDN_SLAB), lambda i, p, j: (0, p)),
                  pl.BlockSpec((1, GDN_DV), lambda i, p, j: (0, 0)),
                  pl.BlockSpec((CONV_WIDTH - 1, ts, ts), lambda i, p, j: (0, 0, 0)),
                  pl.BlockSpec((GDN_SLAB, GDN_SLAB), lambda i, p, j: (0, 0)),
                  pl.BlockSpec((1, LANES, 2 * hg * LANES), lambda i, p, j: (p, 0, 0))],
        out_specs=pl.BlockSpec((1, ts, hg * GDN_DV), lambda i, p, j: (i, j, p)),
        scratch_shapes=[pltpu.VMEM((16, hg * GDN_SLAB), F32),
                        pltpu.VMEM((hg, GDN_DK, GDN_DV), F32)],
        compiler_params=_cparams(("parallel", "parallel", "arbitrary")),
        name="gated_deltanet",
    )(slab, gb, conv_slab, gdn_norm_g.reshape(1, GDN_DV), shifts, ind, sel)


def _outproj_kernel(x_ref, ao_ref, go_ref, gt_ref, sc_ref, sh_ref, mg_ref, wo_ref, g2_ref,
                    rw_ref, rb_ref, x1_ref, h2_ref, route_ref, rt_ref, cnt_ref, carry_ref, *, tb):
    first = (pl.program_id(0) == 0) & (pl.program_id(1) == 0)

    @pl.when(first)
    def _():
        carry_ref[...] = jnp.zeros(carry_ref.shape, F32)

    mla = _rms(ao_ref[0].astype(F32), mg_ref[...])
    cat = jnp.concatenate([mla.astype(BF16), go_ref[0]], axis=1)
    mix = jnp.dot(cat, wo_ref[...], preferred_element_type=F32)
    x1 = x_ref[0] + gt_ref[0] * mix
    x1_ref[0] = x1
    h2 = _rms(x1, g2_ref[...]) * (1.0 + sc_ref[0]) + sh_ref[0]
    h2_ref[0] = _pack_halves(h2)

    h_hi = h2.astype(BF16)
    h_lo = (h2 - h_hi.astype(F32)).astype(BF16)
    main = jnp.dot(h_hi, rw_ref[...], preferred_element_type=F32)
    logits = (main[:, :LANES] + main[:, LANES:]
              + jnp.dot(h_lo, rw_ref[:, :LANES], preferred_element_type=F32) + rb_ref[...])
    lane = lax.broadcasted_iota(jnp.int32, (tb, LANES), 1).astype(F32)
    work = jnp.where(lane < N_EXPERTS, logits, -jnp.inf)
    vals, idxs = [], []
    onehot = jnp.zeros((tb, LANES), F32)
    for _ in range(TOP_K):
        mx = jnp.max(work, axis=-1, keepdims=True)
        ix = jnp.min(jnp.where(work == mx, lane, float(LANES)), axis=-1, keepdims=True)
        sel = lane == ix
        onehot = jnp.where(sel, 1.0, onehot)
        work = jnp.where(sel, -jnp.inf, work)
        vals.append(mx)
        idxs.append(ix)
    exps = [jnp.exp(v - vals[0]) for v in vals]
    den = exps[0] + exps[1] + exps[2] + exps[3]

    r = lax.broadcasted_iota(jnp.int32, (tb, tb), 0)
    c = lax.broadcasted_iota(jnp.int32, (tb, tb), 1)
    tri = jnp.where(c < r, 1.0, 0.0).astype(BF16)
    before = jnp.dot(tri, onehot.astype(BF16), preferred_element_type=F32) + carry_ref[...]
    route = jnp.zeros((tb, LANES), F32)
    for kk in range(TOP_K):
        rank = jnp.sum(jnp.where(lane == idxs[kk], before, 0.0), axis=-1, keepdims=True)
        route = jnp.where(lane == kk, idxs[kk], route)
        route = jnp.where(lane == TOP_K + kk, rank, route)
        route = jnp.where(lane == 2 * TOP_K + kk, exps[kk] / den, route)
    route_ref[0] = route
    rt_ref[0] = route.T[:ROUTE_ROWS, :]
    total = carry_ref[...] + jnp.sum(onehot, axis=0, keepdims=True)
    carry_ref[...] = total
    cnt_ref[...] = total


def _out_projection(x, attn_o, gdn_o, gt1, sc2, sh2, mla_out_g, w_out, norm2_g, rw, rb, tb, b0, nb):
    _, s, d = x.shape
    const = lambda shape: pl.BlockSpec(shape, lambda i, j: (0,) * len(shape))
    tok_in = lambda w: pl.BlockSpec((1, tb, w), lambda i, j: (i + b0, j, 0))
    tok_out = lambda w: pl.BlockSpec((1, tb, w), lambda i, j: (i, j, 0))
    per_b = pl.BlockSpec((1, 1, d), lambda i, j: (i + b0, 0, 0))
    half = attn_o.shape[-1]
    return pl.pallas_call(
        functools.partial(_outproj_kernel, tb=tb),
        out_shape=(jax.ShapeDtypeStruct((nb, s, d), F32),
                   jax.ShapeDtypeStruct((nb, s, d // 2), F32),
                   jax.ShapeDtypeStruct((nb, s, LANES), F32),
                   jax.ShapeDtypeStruct((nb, ROUTE_ROWS, s), F32),
                   jax.ShapeDtypeStruct((1, LANES), F32)),
        grid=(nb, s // tb),
        in_specs=[tok_in(d), tok_in(half), tok_in(half), per_b, per_b, per_b, const((1, half)),
                  const(w_out.shape), const((1, d)), const(rw.shape), const((1, LANES))],
        out_specs=(tok_out(d), tok_out(d // 2), tok_out(LANES),
                   pl.BlockSpec((1, ROUTE_ROWS, tb), lambda i, j: (i, 0, j)), const((1, LANES))),
        scratch_shapes=[pltpu.VMEM((1, LANES), F32)],
        compiler_params=_cparams(("arbitrary", "arbitrary")),
        name="out_projection_router",
    )(x, attn_o, gdn_o, gt1, sc2, sh2, mla_out_g.reshape(1, half), w_out,
      norm2_g.reshape(1, d), rw, rb)


def _sc_mesh():
    return plsc.VectorSubcoreMesh(core_axis_name="c", subcore_axis_name="s",
                                  num_cores=SC_CORES, num_subcores=SC_SUBCORES)


def _sc_worker():
    return lax.axis_index("s") * SC_CORES + lax.axis_index("c")


def _dispatch_rows(h2, dest_km, rows):
    t, d = h2.shape
    per_worker = t // SC_WORKERS
    n_chunks = per_worker // SC_CHUNK

    @functools.partial(
        pl.kernel, out_type=jax.ShapeDtypeStruct((rows, d), h2.dtype), mesh=_sc_mesh(),
        scratch_types=[pltpu.VMEM((SC_CHUNK,), jnp.int32), pltpu.VMEM((SC_CHUNK, d), h2.dtype)],
        name="moe_dispatch")
    def run(h2_hbm, dest_hbm, xs_hbm, idx_v, rows_v):
        base_w = _sc_worker() * per_worker

        @pl.loop(0, n_chunks)
        def _(ci):
            base = pl.multiple_of(base_w + ci * SC_CHUNK, SC_CHUNK)
            pltpu.sync_copy(h2_hbm.at[pl.ds(base, SC_CHUNK)], rows_v)
            for kk in range(TOP_K):
                pltpu.sync_copy(dest_hbm.at[kk, pl.ds(base, SC_CHUNK)], idx_v)
                pltpu.sync_copy(rows_v, xs_hbm.at[idx_v])

    return run(h2, dest_km)


def _collect_rows(y_rows, dest_km):
    _, d = y_rows.shape
    t = dest_km.shape[1]
    per_worker = t // SC_WORKERS
    n_chunks = per_worker // SC_CHUNK

    @functools.partial(
        pl.kernel, out_type=jax.ShapeDtypeStruct((TOP_K, t, d), y_rows.dtype), mesh=_sc_mesh(),
        scratch_types=[pltpu.VMEM((SC_CHUNK,), jnp.int32), pltpu.VMEM((SC_CHUNK, d), y_rows.dtype)],
        name="moe_collect")
    def run(y_hbm, dest_hbm, out_hbm, idx_v, rows_v):
        base_w = _sc_worker() * per_worker

        @pl.loop(0, n_chunks)
        def _(ci):
            base = pl.multiple_of(base_w + ci * SC_CHUNK, SC_CHUNK)
            for kk in range(TOP_K):
                pltpu.sync_copy(dest_hbm.at[kk, pl.ds(base, SC_CHUNK)], idx_v)
                pltpu.sync_copy(y_hbm.at[idx_v], rows_v)
                pltpu.sync_copy(rows_v, out_hbm.at[kk, pl.ds(base, SC_CHUNK)])

    return run(y_rows, dest_km)


def _experts_kernel(be_ref, nu_ref, nxt_ref, slot_ref, xs_ref, wgu_hbm, bgu_ref, wd_hbm, bd_ref,
                    y_ref, wgu_f32, wd_f32, wgu_bf, wd_bf, sem):
    i = pl.program_id(0)
    e = be_ref[i]
    new_expert = (i == 0) | (e != be_ref[jnp.maximum(i - 1, 0)])

    def fetch(expert, slot):
        return (pltpu.make_async_copy(wgu_hbm.at[expert], wgu_f32.at[slot], sem.at[0, slot]),
                pltpu.make_async_copy(wd_hbm.at[expert], wd_f32.at[slot], sem.at[1, slot]))

    @pl.when(i == 0)
    def _():
        for cp in fetch(e, slot_ref[e]):
            cp.start()

    @pl.when(new_expert)
    def _():
        slot = slot_ref[e]
        for cp in fetch(e, slot):
            cp.wait()
        wgu_bf[...] = wgu_f32[slot].astype(BF16)
        wd_bf[...] = wd_f32[slot].astype(BF16)
        nxt = nxt_ref[e]

        @pl.when(nxt >= 0)
        def _():
            for cp in fetch(nxt, 1 - slot):
                cp.start()

    @pl.when(i < nu_ref[0])
    def _():
        x_lo, x_hi = _unpack_halves(xs_ref[...])
        half = x_lo.shape[-1]
        gu = (jnp.dot(x_lo.astype(BF16), wgu_bf[:half, :], preferred_element_type=F32)
              + jnp.dot(x_hi.astype(BF16), wgu_bf[half:, :], preferred_element_type=F32)
              + bgu_ref[0])
        gate = jnp.minimum(gu[:, :D_EXPERT], SWIGLU_LIMIT)
        up = jnp.clip(gu[:, D_EXPERT:], -SWIGLU_LIMIT, SWIGLU_LIMIT)
        act = (up + 1.0) * (gate * _sigmoid(SWIGLU_ALPHA * gate))
        y = jnp.dot(act.astype(BF16), wd_bf[...], preferred_element_type=F32) + bd_ref[0]
        y_ref[...] = _pack_halves(y)


def _experts(blk_e, n_used, next_e, slot_e, xs, wgu, bgu, wd, bd, bm):
    rows, half = xs.shape
    d = 2 * half
    n_blocks = rows // bm
    row_map = lambda i, be, nu, nx, sl: (jnp.maximum(jnp.minimum(i, nu[0] - 1), 0), 0)
    exp_map = lambda i, be, nu, nx, sl: (be[i], 0, 0)
    grid_spec = pltpu.PrefetchScalarGridSpec(
        num_scalar_prefetch=4,
        grid=(n_blocks,),
        in_specs=[pl.BlockSpec((bm, half), row_map),
                  pl.BlockSpec(memory_space=pl.ANY),
                  pl.BlockSpec((1, 1, 2 * D_EXPERT), exp_map),
                  pl.BlockSpec(memory_space=pl.ANY),
                  pl.BlockSpec((1, 1, d), exp_map)],
        out_specs=pl.BlockSpec((bm, half), row_map),
        scratch_shapes=[pltpu.VMEM((2, d, 2 * D_EXPERT), F32),
                        pltpu.VMEM((2, D_EXPERT, d), F32),
                        pltpu.VMEM((d, 2 * D_EXPERT), BF16),
                        pltpu.VMEM((D_EXPERT, d), BF16),
                        pltpu.SemaphoreType.DMA((2, 2))])
    return pl.pallas_call(
        _experts_kernel,
        out_shape=jax.ShapeDtypeStruct((rows, half), F32),
        grid_spec=grid_spec,
        compiler_params=_cparams(("arbitrary",)),
        name="expert_mlp",
    )(blk_e, n_used, next_e, slot_e, xs, wgu, bgu, wd, bd)


def _final_kernel(x1_ref, yk_ref, route_ref, gt_ref, fg_ref, *rest):
    o_ref = rest[-1]
    route = route_ref[0]
    ffn = jnp.zeros(x1_ref.shape[1:], F32)
    for kk in range(TOP_K):
        wk = route[:, 2 * TOP_K + kk:2 * TOP_K + kk + 1]
        y_lo, y_hi = _unpack_halves(yk_ref[kk, 0])
        ffn = ffn + wk * jnp.concatenate([y_lo, y_hi], axis=1)
    x2 = x1_ref[0] + gt_ref[0] * ffn
    o_ref[0] = _rms(x2, fg_ref[...])


def _final(x1, yk, route, gt2, final_g, tb, b0, b_total, prev):
    nb, s, d = x1.shape
    in_specs = [pl.BlockSpec((1, tb, d), lambda i, j: (i, j, 0)),
                pl.BlockSpec((TOP_K, 1, tb, d // 2), lambda i, j: (0, i, j, 0)),
                pl.BlockSpec((1, tb, LANES), lambda i, j: (i, j, 0)),
                pl.BlockSpec((1, 1, d), lambda i, j: (i + b0, 0, 0)),
                pl.BlockSpec((1, d), lambda i, j: (0, 0))]
    args = [x1, yk, route, gt2, final_g.reshape(1, d)]
    aliases = {}
    if prev is not None:
        in_specs.append(pl.BlockSpec(memory_space=pl.ANY))
        args.append(prev)
        aliases = {len(args) - 1: 0}
    return pl.pallas_call(
        _final_kernel,
        out_shape=jax.ShapeDtypeStruct((b_total, s, d), F32),
        grid=(nb, s // tb),
        in_specs=in_specs,
        out_specs=pl.BlockSpec((1, tb, d), lambda i, j: (i + b0, j, 0)),
        input_output_aliases=aliases,
        compiler_params=_cparams(("parallel", "parallel")),
        name="combine_final_norm",
    )(*args)


def _rot_cols(w):
    half = MLA_ROPE // 2
    return jnp.concatenate([-w[..., half:], w[..., :half]], axis=-1)


def _prep_weights(w_in, w_q_b, w_kv_b, conv_w, A_log, dt_bias):
    d = w_in.shape[0]
    cuts = [MLA_Q_LORA, MLA_KV_LORA, MLA_ROPE, GDN_HEADS * GDN_DK, GDN_HEADS * GDN_DK,
            GDN_HEADS * GDN_DV, GDN_HEADS * GDN_DV, GDN_HEADS, GDN_HEADS]
    offs = [0]
    for cw in cuts:
        offs.append(offs[-1] + cw)
    part = lambda i: w_in[:, offs[i]:offs[i + 1]]
    k_pe = part(2)
    misc = jnp.concatenate(
        [part(7), part(8), jnp.zeros((d, MLA_NOPE - 2 * GDN_HEADS), w_in.dtype), k_pe,
         _rot_cols(k_pe)], axis=1)
    heads = lambda w, n: w.reshape(w.shape[0], GDN_HEADS, n)
    slab = jnp.concatenate([heads(part(3), GDN_DK), heads(part(4), GDN_DK),
                            heads(part(5), GDN_DV), heads(part(6), GDN_DV)], axis=-1)
    w1 = jnp.concatenate([part(0), part(1), misc, slab.reshape(d, GDN_HEADS * GDN_SLAB)],
                         axis=1).astype(BF16)

    wq3 = w_q_b.reshape(MLA_Q_LORA, MLA_HEADS, MLA_QK)
    pe = wq3[..., MLA_NOPE:]
    wq = jnp.concatenate([wq3[..., :MLA_NOPE], pe, _rot_cols(pe)], axis=-1)
    wq = wq.reshape(MLA_Q_LORA, MLA_HEADS * HEAD_SLOT).astype(BF16)

    wkv3 = w_kv_b.reshape(MLA_KV_LORA, MLA_HEADS, MLA_NOPE + MLA_V)
    wk = jnp.concatenate([wkv3[..., :MLA_NOPE],
                          jnp.zeros((MLA_KV_LORA, MLA_HEADS, HEAD_SLOT - MLA_NOPE), w_kv_b.dtype)],
                         axis=-1).reshape(MLA_KV_LORA, MLA_HEADS * HEAD_SLOT)
    wv = wkv3[..., MLA_NOPE:].reshape(MLA_KV_LORA, MLA_HEADS * MLA_V)
    wkv = jnp.concatenate([wk, wv], axis=1).astype(BF16)

    nk = GDN_HEADS * GDN_DK
    cheads = lambda w, n: w.reshape(CONV_WIDTH, GDN_HEADS, n)
    conv_slab = jnp.concatenate(
        [cheads(conv_w[:, :nk], GDN_DK), cheads(conv_w[:, nk:2 * nk], GDN_DK),
         cheads(conv_w[:, 2 * nk:], GDN_DV), jnp.zeros((CONV_WIDTH, GDN_HEADS, GDN_DV), conv_w.dtype)],
        axis=-1).reshape(CONV_WIDTH, GDN_HEADS * GDN_SLAB)

    half = MLA_ROPE // 2
    inv_freq = ROPE_THETA ** (-jnp.arange(half, dtype=F32) / half)
    lane = jnp.arange(LANES)
    freq = jnp.where(lane[None, :] // MLA_ROPE == jnp.arange(ROPE_GROUPS)[:, None],
                     jnp.tile(inv_freq, LANES // half)[None, :], 0.0)
    padl = lambda a: jnp.concatenate([a.astype(F32), jnp.zeros((LANES - a.shape[0],), F32)]).reshape(1, LANES)
    return w1, wq, wkv, conv_slab, freq, padl(A_log), padl(dt_bias)


def _layer(x, mod, positions, w_in, q_norm_g, w_q_b, kv_norm_g, w_kv_b, mla_out_g, conv_w,
           A_log, dt_bias, gdn_norm_g, w_out, norm1_g, norm2_g, router_w, router_b, w_gate_up,
           b_gate_up, w_down, b_down, final_g):
    b, s, d = x.shape
    t = b * s
    sh1, sc1, gt1, sh2, sc2, gt2 = [m.reshape(b, 1, d) for m in jnp.split(mod, 6, axis=-1)]
    w1, wq, wkv, conv_slab, freq, alog, dtb = _prep_weights(w_in, w_q_b, w_kv_b, conv_w, A_log,
                                                            dt_bias)
    tb = min(256, s)
    q, k, v, slab, gb = _in_projection(x, positions, sc1, sh1, norm1_g, w1, q_norm_g, wq,
                                       kv_norm_g, wkv, freq, alog, dtb, tb)
    attn_o = _attention(q, k, v, min(512, s), 8)
    gdn_o = _gdn(slab, gb, conv_slab, gdn_norm_g, min(256, s), 4)

    rw = jnp.concatenate([router_w, jnp.zeros((d, LANES - N_EXPERTS), router_w.dtype)], axis=1)
    rw_hi = rw.astype(BF16)
    rw = jnp.concatenate([rw_hi, (rw - rw_hi.astype(F32)).astype(BF16)], axis=1)
    rb = jnp.concatenate([router_b, jnp.zeros((LANES - N_EXPERTS,), router_b.dtype)]).reshape(1, LANES)
    w_out_bf = w_out.astype(BF16)
    bgu = b_gate_up.reshape(N_EXPERTS, 1, -1)
    bd = b_down.reshape(N_EXPERTS, 1, -1)
    n_parts = MOE_PARTS if b % MOE_PARTS == 0 else 1
    nb = b // n_parts
    out = None
    for part in range(n_parts):
        b0 = part * nb
        x1, h2, route, route_t, counts = _out_projection(x, attn_o, gdn_o, gt1, sc2, sh2,
                                                         mla_out_g, w_out_bf, norm2_g, rw, rb,
                                                         tb, b0, nb)
        yk = _moe(h2, route_t, counts, w_gate_up, bgu, w_down, bd)
        out = _final(x1, yk, route, gt2, final_g, tb, b0, b, out)
    return out


def _moe(h2, route_t, counts, w_gate_up, bgu, w_down, bd):
    nb, s, half = h2.shape
    t = nb * s
    bm = EXPERT_ROWS
    slot_major = lambda r0: jnp.swapaxes(route_t[:, r0:r0 + TOP_K, :], 0, 1).reshape(TOP_K, t)
    idx = slot_major(0).astype(jnp.int32)
    rank = slot_major(TOP_K).astype(jnp.int32)
    cnt = counts[0, :N_EXPERTS].astype(jnp.int32)
    padded = ((cnt + bm - 1) // bm) * bm
    pend = jnp.cumsum(padded)
    pstart = pend - padded
    dest_km = pstart[idx] + rank
    n_blocks = (t * TOP_K + N_EXPERTS * (bm - 1) + bm - 1) // bm
    rows = n_blocks * bm
    n_used = (pend[-1] // bm).astype(jnp.int32).reshape(1)
    blk_start = jnp.arange(n_blocks, dtype=jnp.int32) * bm
    blk_e = jnp.minimum(jnp.sum(blk_start[:, None] >= pend[None, :], axis=1), N_EXPERTS - 1)
    last_e = blk_e[jnp.maximum(n_used[0] - 1, 0)]
    blk_e = jnp.where(jnp.arange(n_blocks) < n_used[0], blk_e, last_e).astype(jnp.int32)

    present = padded > 0
    eidx = jnp.arange(N_EXPERTS, dtype=jnp.int32)
    later = present[None, :] & (eidx[None, :] > eidx[:, None])
    next_e = jnp.where(jnp.any(later, axis=1), jnp.argmax(later, axis=1), -1).astype(jnp.int32)
    slot_e = ((jnp.cumsum(present.astype(jnp.int32)) - 1) % 2).astype(jnp.int32)

    xs = _dispatch_rows(h2.reshape(t, half), dest_km, rows)
    y_rows = _experts(blk_e, n_used, next_e, slot_e, xs, w_gate_up, bgu, w_down, bd, bm)
    return _collect_rows(y_rows, dest_km).reshape(TOP_K, nb, s, half)


def kernel(x, c, positions, ada_w, ada_b, norm1_g, w_in, q_norm_g, w_q_b, kv_norm_g, w_kv_b, mla_out_g, conv_w, A_log, dt_bias, gdn_norm_g, w_out, norm2_g, router_w, router_b, w_gate_up, b_gate_up, w_down, b_down, final_g):
    depth = ada_w.shape[0]
    assert depth == 1
    l = 0
    mod = _modulation(c, ada_w[l], ada_b[l])
    return _layer(x, mod, positions, w_in[l], q_norm_g[l], w_q_b[l], kv_norm_g[l],
                  w_kv_b[l], mla_out_g[l], conv_w[l], A_log[l], dt_bias[l], gdn_norm_g[l],
                  w_out[l], norm1_g[l], norm2_g[l], router_w[l], router_b[l],
                  w_gate_up[l], b_gate_up[l], w_down[l], b_down[l], final_g)
```

```python
import functools

import jax
import jax.numpy as jnp
from jax import lax
from jax.experimental import pallas as pl
from jax.experimental.pallas import tpu as pltpu
from jax.experimental.pallas import tpu_sc as plsc

F32 = jnp.float32
BF16 = jnp.bfloat16

D_MODEL = 1024
EPS = 1e-6
MLA_HEADS = 8
MLA_NOPE = 64
MLA_ROPE = 32
MLA_V = 64
MLA_QK = MLA_NOPE + MLA_ROPE
MLA_Q_LORA = 384
MLA_KV_LORA = 256
ROPE_THETA = 10000.0
GDN_HEADS = 8
GDN_DK = 64
GDN_DV = 64
CONV_WIDTH = 4
CHUNK = 64
N_EXPERTS = 32
TOP_K = 4
D_EXPERT = D_MODEL
SWIGLU_ALPHA = 1.702
SWIGLU_LIMIT = 7.0

LANES = 128
HEAD_SLOT = 128
GDN_SLAB = 256
MISC_W = 128
PROJ_W = MLA_Q_LORA + MLA_KV_LORA + MISC_W + GDN_HEADS * GDN_SLAB
VMEM_LIMIT = 56 * 1024 * 1024
MOE_PARTS = 2
EXPERT_ROWS = 512
SC_CORES = 2
SC_SUBCORES = 16
SC_WORKERS = SC_CORES * SC_SUBCORES
SC_CHUNK = 128
ROPE_GROUPS = LANES // MLA_ROPE
ROUTE_ROWS = 16
ATT_STRIP = 32
LOG2E = 1.4426950408889634


def _cparams(sem):
    return pltpu.CompilerParams(dimension_semantics=sem, vmem_limit_bytes=VMEM_LIMIT)


def _rms(x, g):
    return x * lax.rsqrt(jnp.mean(x * x, axis=-1, keepdims=True) + EPS) * g


def _sigmoid(x):
    return 1.0 / (1.0 + jnp.exp(-x))


def _silu(x):
    return x * _sigmoid(x)


def _pack_halves(x):
    w = x.shape[-1] // 2
    lo = lax.bitcast_convert_type(x[:, :w].astype(BF16).astype(F32), jnp.uint32) >> 16
    hi = lax.bitcast_convert_type(x[:, w:].astype(BF16).astype(F32), jnp.uint32) & jnp.uint32(0xFFFF0000)
    return lax.bitcast_convert_type(lo | hi, F32)


def _unpack_halves(p):
    u = lax.bitcast_convert_type(p, jnp.uint32)
    return (lax.bitcast_convert_type(u << 16, F32),
            lax.bitcast_convert_type(u & jnp.uint32(0xFFFF0000), F32))


def _mod_kernel(c_ref, w_ref, b_ref, o_ref):
    c = c_ref[...]
    o_ref[...] = jnp.dot(_silu(c), w_ref[...], preferred_element_type=F32,
                         precision=lax.Precision.HIGHEST) + b_ref[...]


def _modulation(c, ada_w, ada_b):
    b, d = c.shape
    n = ada_w.shape[1]
    return pl.pallas_call(
        _mod_kernel,
        out_shape=jax.ShapeDtypeStruct((b, n), F32),
        grid=(n // d,),
        in_specs=[pl.BlockSpec((b, d), lambda j: (0, 0)),
                  pl.BlockSpec((d, d), lambda j: (0, j)),
                  pl.BlockSpec((1, d), lambda j: (0, j))],
        out_specs=pl.BlockSpec((b, d), lambda j: (0, j)),
        compiler_params=_cparams(("arbitrary",)),
        name="adaln_mod",
    )(c, ada_w, ada_b.reshape(1, n))


def _inproj_kernel(x_ref, pos_ref, sc_ref, sh_ref, g1_ref, w1_ref, qg_ref, wq_ref, kvg_ref,
                   wkv_ref, freq_ref, alog_ref, dtb_ref,
                   q_ref, k_ref, v_ref, gdn_ref, gb_ref):
    x = x_ref[0]
    h = _rms(x, g1_ref[...]) * (1.0 + sc_ref[0]) + sh_ref[0]
    proj = jnp.dot(h.astype(BF16), w1_ref[...], preferred_element_type=F32)

    tb = x.shape[0]
    lane = lax.broadcasted_iota(jnp.int32, (tb, LANES), 1)
    in_rope = (lane >= MLA_NOPE) & (lane < MLA_QK)
    pos = pos_ref[0].astype(F32)
    quarter = tb // ROPE_GROUPS
    ang = pos[0:quarter] * freq_ref[0:1, :]
    for g in range(1, ROPE_GROUPS):
        ang = ang + pos[g * quarter:(g + 1) * quarter] * freq_ref[g:g + 1, :]
    cos_p = jnp.cos(ang)
    sin_p = jnp.sin(ang)
    unpack = lambda t: jnp.concatenate(
        [pltpu.roll(t, (MLA_NOPE - MLA_ROPE * g) % LANES, axis=1) for g in range(ROPE_GROUPS)],
        axis=0)
    cosv = unpack(cos_p)
    sinv = jnp.where(in_rope, unpack(sin_p), 0.0)

    scale = (MLA_QK ** -0.5) * LOG2E
    qn = _rms(proj[:, :MLA_Q_LORA], qg_ref[...])
    qa = jnp.dot(qn.astype(BF16), wq_ref[...], preferred_element_type=F32)
    cq = jnp.where(lane < MLA_NOPE, scale, jnp.where(in_rope, cosv * scale, 0.0))
    sq = sinv * scale
    cq_t = jnp.concatenate([cq] * MLA_HEADS, axis=1)
    sq_t = jnp.concatenate([sq] * MLA_HEADS, axis=1)
    width = MLA_HEADS * HEAD_SLOT
    q = qa * cq_t + pltpu.roll(qa, width - MLA_ROPE, axis=1) * sq_t
    q_ref[0] = q.astype(BF16)

    kvn = _rms(proj[:, MLA_Q_LORA:MLA_Q_LORA + MLA_KV_LORA], kvg_ref[...])
    kva = jnp.dot(kvn.astype(BF16), wkv_ref[...], preferred_element_type=F32)
    misc = proj[:, MLA_Q_LORA + MLA_KV_LORA:MLA_Q_LORA + MLA_KV_LORA + MISC_W]
    kp = misc * jnp.where(in_rope, cosv, 0.0) + pltpu.roll(misc, MISC_W - MLA_ROPE, axis=1) * sinv
    k = kva[:, :width] + jnp.concatenate([kp] * MLA_HEADS, axis=1)
    k_ref[0] = k.astype(BF16)
    v_ref[0] = kva[:, width:].astype(BF16)

    z = misc + dtb_ref[...]
    softplus = jnp.maximum(z, 0.0) + jnp.log(1.0 + jnp.exp(-jnp.abs(z)))
    g = -jnp.exp(alog_ref[...]) * softplus
    gb_ref[0] = jnp.where(lane < GDN_HEADS, g, _sigmoid(misc))

    gdn_ref[0] = proj[:, MLA_Q_LORA + MLA_KV_LORA + MISC_W:].astype(BF16)


def _in_projection(x, positions, sc1, sh1, norm1_g, w1, q_norm_g, wq, kv_norm_g, wkv,
                   freq, alog, dtb, tb):
    b, s, d = x.shape
    hw = MLA_HEADS * HEAD_SLOT
    const = lambda shape: pl.BlockSpec(shape, lambda i, j: (0,) * len(shape))
    tok = lambda w: pl.BlockSpec((1, tb, w), lambda i, j: (i, j, 0))
    per_b = pl.BlockSpec((1, 1, d), lambda i, j: (i, 0, 0))
    return pl.pallas_call(
        _inproj_kernel,
        out_shape=(jax.ShapeDtypeStruct((b, s, hw), BF16),
                   jax.ShapeDtypeStruct((b, s, hw), BF16),
                   jax.ShapeDtypeStruct((b, s, MLA_HEADS * MLA_V), BF16),
                   jax.ShapeDtypeStruct((b, s, GDN_HEADS * GDN_SLAB), BF16),
                   jax.ShapeDtypeStruct((b, s, MISC_W), F32)),
        grid=(b, s // tb),
        in_specs=[tok(d), tok(1), per_b, per_b, const((1, d)), const(w1.shape),
                  const((1, MLA_Q_LORA)), const(wq.shape), const((1, MLA_KV_LORA)),
                  const(wkv.shape), const((ROPE_GROUPS, LANES)), const((1, LANES)),
                  const((1, LANES))],
        out_specs=(tok(hw), tok(hw), tok(MLA_HEADS * MLA_V), tok(GDN_HEADS * GDN_SLAB),
                   tok(MISC_W)),
        compiler_params=_cparams(("parallel", "parallel")),
        name="in_projection",
    )(x, positions.reshape(b, s, 1), sc1, sh1, norm1_g.reshape(1, d), w1,
      q_norm_g.reshape(1, -1), wq, kv_norm_g.reshape(1, -1), wkv, freq, alog, dtb)


def _attn_kernel(q_ref, k_ref, v_ref, o_ref, s_ref, p_ref, m_ref, l_ref, acc_ref, *, tq, nh):
    qi = pl.program_id(2)
    m_ref[...] = jnp.full(m_ref.shape, -jnp.inf, F32)
    l_ref[...] = jnp.zeros(l_ref.shape, F32)
    acc_ref[...] = jnp.zeros(acc_ref.shape, F32)

    def step(off, masked):
        segs = [(0, tq // 2, tq // 2), (tq // 2, tq, tq)] if masked else [(0, tq, tq)]
        width = lambda r: next(nc for r0, r1, nc in segs if r0 <= r * ATT_STRIP < r1)
        for h in range(nh):
            for r0, r1, nc in segs:
                kj = k_ref[0, pl.ds(off, nc), h * HEAD_SLOT:(h + 1) * HEAD_SLOT]
                s_ref[h, r0:r1, 0:nc] = lax.dot_general(
                    q_ref[0, r0:r1, h * HEAD_SLOT:(h + 1) * HEAD_SLOT], kj,
                    (((1,), (1,)), ((), ())), preferred_element_type=F32)
        n_strips = tq // ATT_STRIP

        def strip(h, r):
            rows = slice(r * ATT_STRIP, (r + 1) * ATT_STRIP)
            nc = width(r)
            sc = s_ref[h, rows, 0:nc]
            if masked:
                rid = lax.broadcasted_iota(jnp.int32, (ATT_STRIP, nc), 0) + r * ATT_STRIP
                cid = lax.broadcasted_iota(jnp.int32, (ATT_STRIP, nc), 1)
                sc = jnp.where(cid <= rid, sc, -jnp.inf)
            return rows, sc

        rep = lambda col: jnp.broadcast_to(col, (col.shape[0], LANES))
        m_new = []
        for h in range(nh):
            mx = jnp.concatenate([rep(jnp.max(strip(h, r)[1], axis=-1, keepdims=True))
                                  for r in range(n_strips)], axis=0)
            m_new.append(jnp.maximum(m_ref[h], mx))
        for h in range(nh):
            v_lanes = slice((h // 2) * 2 * MLA_V, (h // 2 + 1) * 2 * MLA_V)
            sums = []
            for r in range(n_strips):
                rows, sc = strip(h, r)
                nc = width(r)
                p = jnp.exp2(sc - jnp.concatenate([m_new[h][rows]] * (nc // LANES), axis=1))
                p_ref[h, rows, 0:nc] = p.astype(BF16)
                sums.append(rep(jnp.sum(p, axis=-1, keepdims=True)))
            alpha = jnp.exp2(m_ref[h] - m_new[h])
            l_ref[h] = alpha * l_ref[h] + jnp.concatenate(sums, axis=0)
            m_ref[h] = m_new[h]
            for r0, r1, nc in segs:
                acc_ref[h, r0:r1, :] = alpha[r0:r1] * acc_ref[h, r0:r1, :] + jnp.dot(
                    p_ref[h, r0:r1, 0:nc], v_ref[0, pl.ds(off, nc), v_lanes],
                    preferred_element_type=F32)

    def body(j, carry):
        step(pl.multiple_of(j * tq, tq), False)
        return carry

    lax.fori_loop(0, qi, body, 0)
    step(pl.multiple_of(qi * tq, tq), True)
    lane = lax.broadcasted_iota(jnp.int32, (tq, 2 * MLA_V), 1)
    outs = []
    for p in range(nh // 2):
        o0 = acc_ref[2 * p] / l_ref[2 * p]
        o1 = acc_ref[2 * p + 1] / l_ref[2 * p + 1]
        outs.append(jnp.where(lane < MLA_V, o0, o1))
    o_ref[0] = jnp.concatenate(outs, axis=1).astype(o_ref.dtype)


def _attention(q, k, v, tq, nh):
    b, s, _ = q.shape
    groups = MLA_HEADS // nh
    return pl.pallas_call(
        functools.partial(_attn_kernel, tq=tq, nh=nh),
        out_shape=jax.ShapeDtypeStruct((b, s, MLA_HEADS * MLA_V), BF16),
        grid=(b, groups, s // tq),
        in_specs=[pl.BlockSpec((1, tq, nh * HEAD_SLOT), lambda i, p, j: (i, j, p)),
                  pl.BlockSpec((1, s, nh * HEAD_SLOT), lambda i, p, j: (i, 0, p)),
                  pl.BlockSpec((1, s, nh * MLA_V), lambda i, p, j: (i, 0, p))],
        out_specs=pl.BlockSpec((1, tq, nh * MLA_V), lambda i, p, j: (i, j, p)),
        scratch_shapes=[pltpu.VMEM((nh, tq, tq), F32), pltpu.VMEM((nh, tq, tq), BF16),
                        pltpu.VMEM((nh, tq, LANES), F32), pltpu.VMEM((nh, tq, LANES), F32),
                        pltpu.VMEM((nh, tq, 2 * MLA_V), F32)],
        compiler_params=_cparams(("parallel", "parallel", "arbitrary")),
        name="mla_attention",
    )(q, k, v)


def _bdot(a, b):
    return jnp.dot(a.astype(BF16), b.astype(BF16), preferred_element_type=F32)


def _gdn_kernel(slab_ref, gb_ref, cw_ref, ng_ref, shf_ref, ind_ref, sel_ref, o_ref, tail_ref,
                state_ref, *, ts, hg):
    si = pl.program_id(2)
    width = hg * GDN_SLAB
    hist_rows = 8

    @pl.when(si == 0)
    def _():
        tail_ref[...] = jnp.zeros(tail_ref.shape, F32)
        state_ref[...] = jnp.zeros(state_ref.shape, F32)

    xs_bf = slab_ref[0]
    xs = xs_bf.astype(F32)
    conv = cw_ref[CONV_WIDTH - 1:CONV_WIDTH, :] * xs
    hist = jnp.zeros((hist_rows, width), F32)
    for j in range(CONV_WIDTH - 1):
        wj = cw_ref[j:j + 1, :]
        conv = conv + wj * jnp.dot(shf_ref[j], xs_bf, preferred_element_type=F32)
        start = hist_rows - (CONV_WIDTH - 1) + j
        hist = hist + wj * tail_ref[start:start + hist_rows, :]
    conv = jnp.concatenate([conv[:hist_rows] + hist, conv[hist_rows:]], axis=0)
    tail_ref[0:hist_rows, :] = xs[ts - hist_rows:ts, :]
    act = _silu(conv)

    gb = gb_ref[0]
    g1 = gb.astype(BF16)
    r1 = gb - g1.astype(F32)
    g2 = r1.astype(BF16)
    g3 = (r1 - g2.astype(F32)).astype(BF16)
    sel = sel_ref[0]
    gate_w = (jnp.dot(g1, sel, preferred_element_type=F32)
              + jnp.dot(g2, sel, preferred_element_type=F32)
              + jnp.dot(g3, sel, preferred_element_type=F32))

    n_chunks = ts // CHUNK
    row = lax.broadcasted_iota(jnp.int32, (CHUNK, CHUNK), 0)
    col = lax.broadcasted_iota(jnp.int32, (CHUNK, CHUNK), 1)
    tri_incl = col <= row
    tri_strict = col < row
    eye = col == row
    crow = lax.broadcasted_iota(jnp.int32, (ts, LANES), 0) % CHUNK
    slab_lane = lax.broadcasted_iota(jnp.int32, (ts, GDN_SLAB), 1)

    heads = []
    for hh in range(hg):
        sl_act = act[:, hh * GDN_SLAB:(hh + 1) * GDN_SLAB]
        sq = sl_act * sl_act
        sq_hi = sq.astype(BF16)
        sq_lo = (sq - sq_hi.astype(F32)).astype(BF16)
        ss = (jnp.dot(sq_hi, ind_ref[...], preferred_element_type=F32)
              + jnp.dot(sq_lo, ind_ref[...], preferred_element_type=F32))
        inv = lax.rsqrt(ss + EPS)
        factor = jnp.where(slab_lane < GDN_DK, inv * (GDN_DK ** -0.5),
                           jnp.where(slab_lane < 2 * GDN_DK, inv, 1.0))
        normed = sl_act * factor
        z_all = xs[:, hh * GDN_SLAB + 3 * GDN_DK:(hh + 1) * GDN_SLAB]
        g_w = gate_w[:, hh * LANES:(hh + 1) * LANES]
        b_w = gate_w[:, (hg + hh) * LANES:(hg + hh + 1) * LANES]
        gc_w = g_w
        shift = 1
        while shift < CHUNK:
            rolled = pltpu.roll(gc_w, shift, axis=0)
            gc_w = gc_w + jnp.where(crow >= shift, rolled, 0.0)
            shift *= 2
        heads.append((normed, z_all, b_w, gc_w, jnp.exp(gc_w)))

    a_low, qk, y0, qd, kt, cd, y, ky, qy = ({} for _ in range(9))
    lo, hi = slice(0, GDN_DK), slice(GDN_DK, 2 * GDN_DK)

    def prep(u):
        hh, c = u
        normed, _, b_w, gc_w, eg_w = heads[hh]
        sl = slice(c * CHUNK, (c + 1) * CHUNK)
        qc, kc, vc = normed[sl, lo], normed[sl, hi], normed[sl, 2 * GDN_DK:2 * GDN_DK + GDN_DV]
        gcol = gc_w[sl, lo]
        grow = jnp.sum(jnp.where(eye, gcol, 0.0), axis=0, keepdims=True)
        diff = gcol - grow
        decay = jnp.where(tri_incl, jnp.exp(jnp.where(tri_incl, diff, 0.0)), 0.0)
        k_beta = kc * b_w[sl, hi]
        kq = jnp.concatenate([k_beta, qc], axis=0).astype(BF16)
        kk = lax.dot_general(kq, kc.astype(BF16), (((1,), (1,)), ((), ())),
                             preferred_element_type=F32)
        a_low[u] = jnp.where(tri_strict, kk[:CHUNK] * decay, 0.0)
        qk[u] = jnp.where(tri_incl, kk[CHUNK:] * decay, 0.0)
        y0[u] = jnp.concatenate([vc * b_w[sl, lo], k_beta * eg_w[sl, hi]], axis=1)
        qd[u] = qc * eg_w[sl, lo]
        glast_hi = gc_w[(c + 1) * CHUNK - 1:(c + 1) * CHUNK, hi]
        kt[u] = kc * jnp.exp(glast_hi - gc_w[sl, hi])
        cd[u] = eg_w[(c + 1) * CHUNK - 1:(c + 1) * CHUNK, lo]

    def solve(units):
        blk = 2
        tinv = {u: jnp.where(eye, 1.0, 0.0)
                - jnp.where((row // blk) == (col // blk), a_low[u], 0.0) for u in units}
        while blk < CHUNK:
            in_big = (row // (2 * blk)) == (col // (2 * blk))
            off_mask = in_big & ((row // blk) != (col // blk))
            left = {u: _bdot(tinv[u], jnp.where(off_mask, a_low[u], 0.0)) for u in units}
            tinv = {u: tinv[u] - _bdot(left[u], tinv[u]) for u in units}
            blk *= 2
        for u in units:
            y[u] = _bdot(tinv[u], y0[u])
        for u in units:
            ky[u] = lax.dot_general(kt[u].astype(BF16), y[u].astype(BF16),
                                    (((0,), (0,)), ((), ())),
                                    preferred_element_type=F32)
        for u in units:
            qy[u] = _bdot(qk[u], y[u])

    units = [(hh, c) for c in range(n_chunks) for hh in range(hg)]
    for u in units:
        prep(u)
    solve(units)

    outs = [[] for _ in range(hg)]
    states = [state_ref[hh] for hh in range(hg)]
    for c in range(n_chunks):
        for hh in range(hg):
            u = (hh, c)
            sb = states[hh].astype(BF16)
            r_mat = qd[u] - qy[u][:, GDN_DV:]
            outs[hh].append(jnp.dot(r_mat.astype(BF16), sb, preferred_element_type=F32)
                            + qy[u][:, :GDN_DV])
            states[hh] = (states[hh] * cd[u]
                          - jnp.dot(ky[u][:, GDN_DV:].astype(BF16), sb,
                                    preferred_element_type=F32)
                          + ky[u][:, :GDN_DV])
    finals = []
    for hh in range(hg):
        state_ref[hh] = states[hh]
        o_all = jnp.concatenate(outs[hh], axis=0)
        finals.append(_rms(o_all, ng_ref[...]) * _silu(heads[hh][1]))
    o_ref[0] = jnp.concatenate(finals, axis=1).astype(o_ref.dtype)


def _gdn(slab, gb, conv_slab, gdn_norm_g, ts, hg):
    b, s, _ = slab.shape
    groups = GDN_HEADS // hg
    r = jnp.arange(ts)
    shifts = jnp.stack([(r[:, None] - r[None, :]) == (CONV_WIDTH - 1 - j)
                        for j in range(CONV_WIDTH - 1)]).astype(BF16)
    li = jnp.arange(GDN_SLAB)
    ind = ((li[:, None] // GDN_DK == li[None, :] // GDN_DK)
           & (li[:, None] < 2 * GDN_DK) & (li[None, :] < 2 * GDN_DK)).astype(BF16)
    src = jnp.arange(LANES)[None, :, None]
    dst = jnp.arange(2 * hg * LANES)[None, None, :]
    grp = jnp.arange(groups)[:, None, None]
    want = jnp.where(dst < hg * LANES, grp * hg + dst // LANES,
                     GDN_HEADS + grp * hg + (dst - hg * LANES) // LANES)
    sel = (src == want).astype(BF16)
    return pl.pallas_call(
        functools.partial(_gdn_kernel, ts=ts, hg=hg),
        out_shape=jax.ShapeDtypeStruct((b, s, GDN_HEADS * GDN_DV), BF16),
        grid=(b, groups, s // ts),
        in_specs=[pl.BlockSpec((1, ts, hg * GDN_SLAB), lambda i, p, j: (i, j, p)),
                  pl.BlockSpec((1, ts, MISC_W), lambda i, p, j: (i, j, 0)),
                  pl.BlockSpec((CONV_WIDTH, hg * GDN_SLAB), lambda i, p, j: (0, p)),
                  pl.BlockSpec((1, GDN_DV), lambda i, p, j: (0, 0)),
                  pl.BlockSpec((CONV_WIDTH - 1, ts, ts), lambda i, p, j: (0, 0, 0)),
                  pl.BlockSpec((GDN_SLAB, GDN_SLAB), lambda i, p, j: (0, 0)),
                  pl.BlockSpec((1, LANES, 2 * hg * LANES), lambda i, p, j: (p, 0, 0))],
        out_specs=pl.BlockSpec((1, ts, hg * GDN_DV), lambda i, p, j: (i, j, p)),
        scratch_shapes=[pltpu.VMEM((16, hg * GDN_SLAB), F32),
                        pltpu.VMEM((hg, GDN_DK, GDN_DV), F32)],
        compiler_params=_cparams(("parallel", "parallel", "arbitrary")),
        name="gated_deltanet",
    )(slab, gb, conv_slab, gdn_norm_g.reshape(1, GDN_DV), shifts, ind, sel)


def _outproj_kernel(x_ref, ao_ref, go_ref, gt_ref, sc_ref, sh_ref, mg_ref, wo_ref, g2_ref,
                    rw_ref, rb_ref, x1_ref, h2_ref, route_ref, rt_ref, cnt_ref, carry_ref, *, tb):
    first = (pl.program_id(0) == 0) & (pl.program_id(1) == 0)

    @pl.when(first)
    def _():
        carry_ref[...] = jnp.zeros(carry_ref.shape, F32)

    mla = _rms(ao_ref[0].astype(F32), mg_ref[...])
    cat = jnp.concatenate([mla.astype(BF16), go_ref[0]], axis=1)
    mix = jnp.dot(cat, wo_ref[...], preferred_element_type=F32)
    x1 = x_ref[0] + gt_ref[0] * mix
    x1_ref[0] = x1
    h2 = _rms(x1, g2_ref[...]) * (1.0 + sc_ref[0]) + sh_ref[0]
    h2_ref[0] = _pack_halves(h2)

    h_hi = h2.astype(BF16)
    h_lo = (h2 - h_hi.astype(F32)).astype(BF16)
    main = jnp.dot(h_hi, rw_ref[...], preferred_element_type=F32)
    logits = (main[:, :LANES] + main[:, LANES:]
              + jnp.dot(h_lo, rw_ref[:, :LANES], preferred_element_type=F32) + rb_ref[...])
    lane = lax.broadcasted_iota(jnp.int32, (tb, LANES), 1).astype(F32)
    work = jnp.where(lane < N_EXPERTS, logits, -jnp.inf)
    vals, idxs = [], []
    onehot = jnp.zeros((tb, LANES), F32)
    for _ in range(TOP_K):
        mx = jnp.max(work, axis=-1, keepdims=True)
        ix = jnp.min(jnp.where(work == mx, lane, float(LANES)), axis=-1, keepdims=True)
        sel = lane == ix
        onehot = jnp.where(sel, 1.0, onehot)
        work = jnp.where(sel, -jnp.inf, work)
        vals.append(mx)
        idxs.append(ix)
    exps = [jnp.exp(v - vals[0]) for v in vals]
    den = exps[0] + exps[1] + exps[2] + exps[3]

    r = lax.broadcasted_iota(jnp.int32, (tb, tb), 0)
    c = lax.broadcasted_iota(jnp.int32, (tb, tb), 1)
    tri = jnp.where(c < r, 1.0, 0.0).astype(BF16)
    before = jnp.dot(tri, onehot.astype(BF16), preferred_element_type=F32) + carry_ref[...]
    route = jnp.zeros((tb, LANES), F32)
    for kk in range(TOP_K):
        rank = jnp.sum(jnp.where(lane == idxs[kk], before, 0.0), axis=-1, keepdims=True)
        route = jnp.where(lane == kk, idxs[kk], route)
        route = jnp.where(lane == TOP_K + kk, rank, route)
        route = jnp.where(lane == 2 * TOP_K + kk, exps[kk] / den, route)
    route_ref[0] = route
    rt_ref[0] = route.T[:ROUTE_ROWS, :]
    total = carry_ref[...] + jnp.sum(onehot, axis=0, keepdims=True)
    carry_ref[...] = total
    cnt_ref[...] = total


def _out_projection(x, attn_o, gdn_o, gt1, sc2, sh2, mla_out_g, w_out, norm2_g, rw, rb, tb, b0, nb):
    _, s, d = x.shape
    const = lambda shape: pl.BlockSpec(shape, lambda i, j: (0,) * len(shape))
    tok_in = lambda w: pl.BlockSpec((1, tb, w), lambda i, j: (i + b0, j, 0))
    tok_out = lambda w: pl.BlockSpec((1, tb, w), lambda i, j: (i, j, 0))
    per_b = pl.BlockSpec((1, 1, d), lambda i, j: (i + b0, 0, 0))
    half = attn_o.shape[-1]
    return pl.pallas_call(
        functools.partial(_outproj_kernel, tb=tb),
        out_shape=(jax.ShapeDtypeStruct((nb, s, d), F32),
                   jax.ShapeDtypeStruct((nb, s, d // 2), F32),
                   jax.ShapeDtypeStruct((nb, s, LANES), F32),
                   jax.ShapeDtypeStruct((nb, ROUTE_ROWS, s), F32),
                   jax.ShapeDtypeStruct((1, LANES), F32)),
        grid=(nb, s // tb),
        in_specs=[tok_in(d), tok_in(half), tok_in(half), per_b, per_b, per_b, const((1, half)),
                  const(w_out.shape), const((1, d)), const(rw.shape), const((1, LANES))],
        out_specs=(tok_out(d), tok_out(d // 2), tok_out(LANES),
                   pl.BlockSpec((1, ROUTE_ROWS, tb), lambda i, j: (i, 0, j)), const((1, LANES))),
        scratch_shapes=[pltpu.VMEM((1, LANES), F32)],
        compiler_params=_cparams(("arbitrary", "arbitrary")),
        name="out_projection_router",
    )(x, attn_o, gdn_o, gt1, sc2, sh2, mla_out_g.reshape(1, half), w_out,
      norm2_g.reshape(1, d), rw, rb)


def _sc_mesh():
    return plsc.VectorSubcoreMesh(core_axis_name="c", subcore_axis_name="s",
                                  num_cores=SC_CORES, num_subcores=SC_SUBCORES)


def _sc_worker():
    return lax.axis_index("s") * SC_CORES + lax.axis_index("c")


def _dispatch_rows(h2, dest_km, rows):
    t, d = h2.shape
    per_worker = t // SC_WORKERS
    n_chunks = per_worker // SC_CHUNK

    @functools.partial(
        pl.kernel, out_type=jax.ShapeDtypeStruct((rows, d), h2.dtype), mesh=_sc_mesh(),
        scratch_types=[pltpu.VMEM((SC_CHUNK,), jnp.int32), pltpu.VMEM((SC_CHUNK, d), h2.dtype)],
        name="moe_dispatch")
    def run(h2_hbm, dest_hbm, xs_hbm, idx_v, rows_v):
        base_w = _sc_worker() * per_worker

        @pl.loop(0, n_chunks)
        def _(ci):
            base = pl.multiple_of(base_w + ci * SC_CHUNK, SC_CHUNK)
            pltpu.sync_copy(h2_hbm.at[pl.ds(base, SC_CHUNK)], rows_v)
            for kk in range(TOP_K):
                pltpu.sync_copy(dest_hbm.at[kk, pl.ds(base, SC_CHUNK)], idx_v)
                pltpu.sync_copy(rows_v, xs_hbm.at[idx_v])

    return run(h2, dest_km)


def _collect_rows(y_rows, dest_km):
    _, d = y_rows.shape
    t = dest_km.shape[1]
    per_worker = t // SC_WORKERS
    n_chunks = per_worker // SC_CHUNK

    @functools.partial(
        pl.kernel, out_type=jax.ShapeDtypeStruct((TOP_K, t, d), y_rows.dtype), mesh=_sc_mesh(),
        scratch_types=[pltpu.VMEM((SC_CHUNK,), jnp.int32), pltpu.VMEM((SC_CHUNK, d), y_rows.dtype)],
        name="moe_collect")
    def run(y_hbm, dest_hbm, out_hbm, idx_v, rows_v):
        base_w = _sc_worker() * per_worker

        @pl.loop(0, n_chunks)
        def _(ci):
            base = pl.multiple_of(base_w + ci * SC_CHUNK, SC_CHUNK)
            for kk in range(TOP_K):
                pltpu.sync_copy(dest_hbm.at[kk, pl.ds(base, SC_CHUNK)], idx_v)
                pltpu.sync_copy(y_hbm.at[idx_v], rows_v)
                pltpu.sync_copy(rows_v, out_hbm.at[kk, pl.ds(base, SC_CHUNK)])

    return run(y_rows, dest_km)


def _experts_kernel(be_ref, nu_ref, nxt_ref, slot_ref, xs_ref, wgu_hbm, bgu_ref, wd_hbm, bd_ref,
                    y_ref, wgu_f32, wd_f32, wgu_bf, wd_bf, sem):
    i = pl.program_id(0)
    e = be_ref[i]
    new_expert = (i == 0) | (e != be_ref[jnp.maximum(i - 1, 0)])

    def fetch(expert, slot):
        return (pltpu.make_async_copy(wgu_hbm.at[expert], wgu_f32.at[slot], sem.at[0, slot]),
                pltpu.make_async_copy(wd_hbm.at[expert], wd_f32.at[slot], sem.at[1, slot]))

    @pl.when(i == 0)
    def _():
        for cp in fetch(e, slot_ref[e]):
            cp.start()

    @pl.when(new_expert)
    def _():
        slot = slot_ref[e]
        for cp in fetch(e, slot):
            cp.wait()
        wgu_bf[...] = wgu_f32[slot].astype(BF16)
        wd_bf[...] = wd_f32[slot].astype(BF16)
        nxt = nxt_ref[e]

        @pl.when(nxt >= 0)
        def _():
            for cp in fetch(nxt, 1 - slot):
                cp.start()

    @pl.when(i < nu_ref[0])
    def _():
        x_lo, x_hi = _unpack_halves(xs_ref[...])
        half = x_lo.shape[-1]
        gu = (jnp.dot(x_lo.astype(BF16), wgu_bf[:half, :], preferred_element_type=F32)
              + jnp.dot(x_hi.astype(BF16), wgu_bf[half:, :], preferred_element_type=F32)
              + bgu_ref[0])
        gate = jnp.minimum(gu[:, :D_EXPERT], SWIGLU_LIMIT)
        up = jnp.clip(gu[:, D_EXPERT:], -SWIGLU_LIMIT, SWIGLU_LIMIT)
        act = (up + 1.0) * (gate * _sigmoid(SWIGLU_ALPHA * gate))
        y = jnp.dot(act.astype(BF16), wd_bf[...], preferred_element_type=F32) + bd_ref[0]
        y_ref[...] = _pack_halves(y)


def _experts(blk_e, n_used, next_e, slot_e, xs, wgu, bgu, wd, bd, bm):
    rows, half = xs.shape
    d = 2 * half
    n_blocks = rows // bm
    row_map = lambda i, be, nu, nx, sl: (jnp.maximum(jnp.minimum(i, nu[0] - 1), 0), 0)
    exp_map = lambda i, be, nu, nx, sl: (be[i], 0, 0)
    grid_spec = pltpu.PrefetchScalarGridSpec(
        num_scalar_prefetch=4,
        grid=(n_blocks,),
        in_specs=[pl.BlockSpec((bm, half), row_map),
                  pl.BlockSpec(memory_space=pl.ANY),
                  pl.BlockSpec((1, 1, 2 * D_EXPERT), exp_map),
                  pl.BlockSpec(memory_space=pl.ANY),
                  pl.BlockSpec((1, 1, d), exp_map)],
        out_specs=pl.BlockSpec((bm, half), row_map),
        scratch_shapes=[pltpu.VMEM((2, d, 2 * D_EXPERT), F32),
                        pltpu.VMEM((2, D_EXPERT, d), F32),
                        pltpu.VMEM((d, 2 * D_EXPERT), BF16),
                        pltpu.VMEM((D_EXPERT, d), BF16),
                        pltpu.SemaphoreType.DMA((2, 2))])
    return pl.pallas_call(
        _experts_kernel,
        out_shape=jax.ShapeDtypeStruct((rows, half), F32),
        grid_spec=grid_spec,
        compiler_params=_cparams(("arbitrary",)),
        name="expert_mlp",
    )(blk_e, n_used, next_e, slot_e, xs, wgu, bgu, wd, bd)


def _final_kernel(x1_ref, yk_ref, route_ref, gt_ref, fg_ref, *rest):
    o_ref = rest[-1]
    route = route_ref[0]
    ffn = jnp.zeros(x1_ref.shape[1:], F32)
    for kk in range(TOP_K):
        wk = route[:, 2 * TOP_K + kk:2 * TOP_K + kk + 1]
        y_lo, y_hi = _unpack_halves(yk_ref[kk, 0])
        ffn = ffn + wk * jnp.concatenate([y_lo, y_hi], axis=1)
    x2 = x1_ref[0] + gt_ref[0] * ffn
    o_ref[0] = _rms(x2, fg_ref[...])


def _final(x1, yk, route, gt2, final_g, tb, b0, b_total, prev):
    nb, s, d = x1.shape
    in_specs = [pl.BlockSpec((1, tb, d), lambda i, j: (i, j, 0)),
                pl.BlockSpec((TOP_K, 1, tb, d // 2), lambda i, j: (0, i, j, 0)),
                pl.BlockSpec((1, tb, LANES), lambda i, j: (i, j, 0)),
                pl.BlockSpec((1, 1, d), lambda i, j: (i + b0, 0, 0)),
                pl.BlockSpec((1, d), lambda i, j: (0, 0))]
    args = [x1, yk, route, gt2, final_g.reshape(1, d)]
    aliases = {}
    if prev is not None:
        in_specs.append(pl.BlockSpec(memory_space=pl.ANY))
        args.append(prev)
        aliases = {len(args) - 1: 0}
    return pl.pallas_call(
        _final_kernel,
        out_shape=jax.ShapeDtypeStruct((b_total, s, d), F32),
        grid=(nb, s // tb),
        in_specs=in_specs,
        out_specs=pl.BlockSpec((1, tb, d), lambda i, j: (i + b0, j, 0)),
        input_output_aliases=aliases,
        compiler_params=_cparams(("parallel", "parallel")),
        name="combine_final_norm",
    )(*args)


def _rot_cols(w):
    half = MLA_ROPE // 2
    return jnp.concatenate([-w[..., half:], w[..., :half]], axis=-1)


def _prep_weights(w_in, w_q_b, w_kv_b, conv_w, A_log, dt_bias):
    d = w_in.shape[0]
    cuts = [MLA_Q_LORA, MLA_KV_LORA, MLA_ROPE, GDN_HEADS * GDN_DK, GDN_HEADS * GDN_DK,
            GDN_HEADS * GDN_DV, GDN_HEADS * GDN_DV, GDN_HEADS, GDN_HEADS]
    offs = [0]
    for cw in cuts:
        offs.append(offs[-1] + cw)
    part = lambda i: w_in[:, offs[i]:offs[i + 1]]
    k_pe = part(2)
    misc = jnp.concatenate(
        [part(7), part(8), jnp.zeros((d, MLA_NOPE - 2 * GDN_HEADS), w_in.dtype), k_pe,
         _rot_cols(k_pe)], axis=1)
    heads = lambda w, n: w.reshape(w.shape[0], GDN_HEADS, n)
    slab = jnp.concatenate([heads(part(3), GDN_DK), heads(part(4), GDN_DK),
                            heads(part(5), GDN_DV), heads(part(6), GDN_DV)], axis=-1)
    w1 = jnp.concatenate([part(0), part(1), misc, slab.reshape(d, GDN_HEADS * GDN_SLAB)],
                         axis=1).astype(BF16)

    wq3 = w_q_b.reshape(MLA_Q_LORA, MLA_HEADS, MLA_QK)
    pe = wq3[..., MLA_NOPE:]
    wq = jnp.concatenate([wq3[..., :MLA_NOPE], pe, _rot_cols(pe)], axis=-1)
    wq = wq.reshape(MLA_Q_LORA, MLA_HEADS * HEAD_SLOT).astype(BF16)

    wkv3 = w_kv_b.reshape(MLA_KV_LORA, MLA_HEADS, MLA_NOPE + MLA_V)
    wk = jnp.concatenate([wkv3[..., :MLA_NOPE],
                          jnp.zeros((MLA_KV_LORA, MLA_HEADS, HEAD_SLOT - MLA_NOPE), w_kv_b.dtype)],
                         axis=-1).reshape(MLA_KV_LORA, MLA_HEADS * HEAD_SLOT)
    wv = wkv3[..., MLA_NOPE:].reshape(MLA_KV_LORA, MLA_HEADS * MLA_V)
    wkv = jnp.concatenate([wk, wv], axis=1).astype(BF16)

    nk = GDN_HEADS * GDN_DK
    cheads = lambda w, n: w.reshape(CONV_WIDTH, GDN_HEADS, n)
    conv_slab = jnp.concatenate(
        [cheads(conv_w[:, :nk], GDN_DK), cheads(conv_w[:, nk:2 * nk], GDN_DK),
         cheads(conv_w[:, 2 * nk:], GDN_DV), jnp.zeros((CONV_WIDTH, GDN_HEADS, GDN_DV), conv_w.dtype)],
        axis=-1).reshape(CONV_WIDTH, GDN_HEADS * GDN_SLAB)

    half = MLA_ROPE // 2
    inv_freq = ROPE_THETA ** (-jnp.arange(half, dtype=F32) / half)
    lane = jnp.arange(LANES)
    freq = jnp.where(lane[None, :] // MLA_ROPE == jnp.arange(ROPE_GROUPS)[:, None],
                     jnp.tile(inv_freq, LANES // half)[None, :], 0.0)
    padl = lambda a: jnp.concatenate([a.astype(F32), jnp.zeros((LANES - a.shape[0],), F32)]).reshape(1, LANES)
    return w1, wq, wkv, conv_slab, freq, padl(A_log), padl(dt_bias)


def _layer(x, mod, positions, w_in, q_norm_g, w_q_b, kv_norm_g, w_kv_b, mla_out_g, conv_w,
           A_log, dt_bias, gdn_norm_g, w_out, norm1_g, norm2_g, router_w, router_b, w_gate_up,
           b_gate_up, w_down, b_down, final_g):
    b, s, d = x.shape
    t = b * s
    sh1, sc1, gt1, sh2, sc2, gt2 = [m.reshape(b, 1, d) for m in jnp.split(mod, 6, axis=-1)]
    w1, wq, wkv, conv_slab, freq, alog, dtb = _prep_weights(w_in, w_q_b, w_kv_b, conv_w, A_log,
                                                            dt_bias)
    tb = min(256, s)
    q, k, v, slab, gb = _in_projection(x, positions, sc1, sh1, norm1_g, w1, q_norm_g, wq,
                                       kv_norm_g, wkv, freq, alog, dtb, tb)
    attn_o = _attention(q, k, v, min(512, s), 8)
    gdn_o = _gdn(slab, gb, conv_slab, gdn_norm_g, min(256, s), 4)

    rw = jnp.concatenate([router_w, jnp.zeros((d, LANES - N_EXPERTS), router_w.dtype)], axis=1)
    rw_hi = rw.astype(BF16)
    rw = jnp.concatenate([rw_hi, (rw - rw_hi.astype(F32)).astype(BF16)], axis=1)
    rb = jnp.concatenate([router_b, jnp.zeros((LANES - N_EXPERTS,), router_b.dtype)]).reshape(1, LANES)
    w_out_bf = w_out.astype(BF16)
    bgu = b_gate_up.reshape(N_EXPERTS, 1, -1)
    bd = b_down.reshape(N_EXPERTS, 1, -1)
    n_parts = MOE_PARTS if b % MOE_PARTS == 0 else 1
    nb = b // n_parts
    out = None
    for part in range(n_parts):
        b0 = part * nb
        x1, h2, route, route_t, counts = _out_projection(x, attn_o, gdn_o, gt1, sc2, sh2,
                                                         mla_out_g, w_out_bf, norm2_g, rw, rb,
                                                         tb, b0, nb)
        yk = _moe(h2, route_t, counts, w_gate_up, bgu, w_down, bd)
        out = _final(x1, yk, route, gt2, final_g, tb, b0, b, out)
    return out


def _moe(h2, route_t, counts, w_gate_up, bgu, w_down, bd):
    nb, s, half = h2.shape
    t = nb * s
    bm = EXPERT_ROWS
    slot_major = lambda r0: jnp.swapaxes(route_t[:, r0:r0 + TOP_K, :], 0, 1).reshape(TOP_K, t)
    idx = slot_major(0).astype(jnp.int32)
    rank = slot_major(TOP_K).astype(jnp.int32)
    cnt = counts[0, :N_EXPERTS].astype(jnp.int32)
    padded = ((cnt + bm - 1) // bm) * bm
    pend = jnp.cumsum(padded)
    pstart = pend - padded
    base = jnp.zeros_like(idx)
    for e in range(N_EXPERTS):
        base = jnp.where(idx == e, pstart[e], base)
    dest_km = base + rank
    n_blocks = (t * TOP_K + N_EXPERTS * (bm - 1) + bm - 1) // bm
    rows = n_blocks * bm
    n_used = (pend[-1] // bm).astype(jnp.int32).reshape(1)
    blk_start = jnp.arange(n_blocks, dtype=jnp.int32) * bm
    blk_e = jnp.minimum(jnp.sum(blk_start[:, None] >= pend[None, :], axis=1), N_EXPERTS - 1)
    last_e = blk_e[jnp.maximum(n_used[0] - 1, 0)]
    blk_e = jnp.where(jnp.arange(n_blocks) < n_used[0], blk_e, last_e).astype(jnp.int32)

    present = padded > 0
    eidx = jnp.arange(N_EXPERTS, dtype=jnp.int32)
    later = present[None, :] & (eidx[None, :] > eidx[:, None])
    next_e = jnp.where(jnp.any(later, axis=1), jnp.argmax(later, axis=1), -1).astype(jnp.int32)
    slot_e = ((jnp.cumsum(present.astype(jnp.int32)) - 1) % 2).astype(jnp.int32)

    xs = _dispatch_rows(h2.reshape(t, half), dest_km, rows)
    y_rows = _experts(blk_e, n_used, next_e, slot_e, xs, w_gate_up, bgu, w_down, bd, bm)
    return _collect_rows(y_rows, dest_km).reshape(TOP_K, nb, s, half)


def kernel(x, c, positions, ada_w, ada_b, norm1_g, w_in, q_norm_g, w_q_b, kv_norm_g, w_kv_b, mla_out_g, conv_w, A_log, dt_bias, gdn_norm_g, w_out, norm2_g, router_w, router_b, w_gate_up, b_gate_up, w_down, b_down, final_g):
    depth = ada_w.shape[0]
    assert depth == 1
    l = 0
    mod = _modulation(c, ada_w[l], ada_b[l])
    return _layer(x, mod, positions, w_in[l], q_norm_g[l], w_q_b[l], kv_norm_g[l],
                  w_kv_b[l], mla_out_g[l], conv_w[l], A_log[l], dt_bias[l], gdn_norm_g[l],
                  w_out[l], norm1_g[l], norm2_g[l], router_w[l], router_b[l],
                  w_gate_up[l], b_gate_up[l], w_down[l], b_down[l], final_g)
```

```python
import functools

import jax
import jax.numpy as jnp
from jax import lax
from jax.experimental import pallas as pl
from jax.experimental.pallas import tpu as pltpu
from jax.experimental.pallas import tpu_sc as plsc

F32 = jnp.float32
BF16 = jnp.bfloat16

D_MODEL = 1024
EPS = 1e-6
MLA_HEADS = 8
MLA_NOPE = 64
MLA_ROPE = 32
MLA_V = 64
MLA_QK = MLA_NOPE + MLA_ROPE
MLA_Q_LORA = 384
MLA_KV_LORA = 256
ROPE_THETA = 10000.0
GDN_HEADS = 8
GDN_DK = 64
GDN_DV = 64
CONV_WIDTH = 4
CHUNK = 64
N_EXPERTS = 32
TOP_K = 4
D_EXPERT = D_MODEL
SWIGLU_ALPHA = 1.702
SWIGLU_LIMIT = 7.0

LANES = 128
HEAD_SLOT = 128
GDN_SLAB = 256
GDN_GW = 4
MISC_W = 128
PROJ_W = MLA_Q_LORA + MLA_KV_LORA + MISC_W + GDN_HEADS * GDN_SLAB
VMEM_LIMIT = 56 * 1024 * 1024
MOE_PARTS = 2
EXPERT_ROWS = 512
SC_CORES = 2
SC_SUBCORES = 16
SC_WORKERS = SC_CORES * SC_SUBCORES
SC_CHUNK = 128
ROPE_GROUPS = LANES // MLA_ROPE
ROUTE_ROWS = 16
ATT_STRIP = 32
LOG2E = 1.4426950408889634


def _cparams(sem):
    return pltpu.CompilerParams(dimension_semantics=sem, vmem_limit_bytes=VMEM_LIMIT)


def _rms(x, g):
    return x * lax.rsqrt(jnp.mean(x * x, axis=-1, keepdims=True) + EPS) * g


def _sigmoid(x):
    return 1.0 / (1.0 + jnp.exp(-x))


def _silu(x):
    return x * _sigmoid(x)


def _pack_halves(x):
    w = x.shape[-1] // 2
    lo = lax.bitcast_convert_type(x[:, :w].astype(BF16).astype(F32), jnp.uint32) >> 16
    hi = lax.bitcast_convert_type(x[:, w:].astype(BF16).astype(F32), jnp.uint32) & jnp.uint32(0xFFFF0000)
    return lax.bitcast_convert_type(lo | hi, F32)


def _unpack_halves(p):
    u = lax.bitcast_convert_type(p, jnp.uint32)
    return (lax.bitcast_convert_type(u << 16, F32),
            lax.bitcast_convert_type(u & jnp.uint32(0xFFFF0000), F32))


def _mod_kernel(c_ref, w_ref, b_ref, o_ref):
    c = c_ref[...]
    o_ref[...] = jnp.dot(_silu(c), w_ref[...], preferred_element_type=F32,
                         precision=lax.Precision.HIGHEST) + b_ref[...]


def _modulation(c, ada_w, ada_b):
    b, d = c.shape
    n = ada_w.shape[1]
    return pl.pallas_call(
        _mod_kernel,
        out_shape=jax.ShapeDtypeStruct((b, n), F32),
        grid=(n // d,),
        in_specs=[pl.BlockSpec((b, d), lambda j: (0, 0)),
                  pl.BlockSpec((d, d), lambda j: (0, j)),
                  pl.BlockSpec((1, d), lambda j: (0, j))],
        out_specs=pl.BlockSpec((b, d), lambda j: (0, j)),
        compiler_params=_cparams(("arbitrary",)),
        name="adaln_mod",
    )(c, ada_w, ada_b.reshape(1, n))


def _inproj_kernel(x_ref, pos_ref, sc_ref, sh_ref, g1_ref, w1_ref, qg_ref, wq_ref, kvg_ref,
                   wkv_ref, freq_ref, alog_ref, dtb_ref,
                   q_ref, k_ref, v_ref, gdn_ref, gb_ref):
    x = x_ref[0]
    h = _rms(x, g1_ref[...]) * (1.0 + sc_ref[0]) + sh_ref[0]
    proj = jnp.dot(h.astype(BF16), w1_ref[...], preferred_element_type=F32)

    tb = x.shape[0]
    lane = lax.broadcasted_iota(jnp.int32, (tb, LANES), 1)
    in_rope = (lane >= MLA_NOPE) & (lane < MLA_QK)
    pos = pos_ref[0].astype(F32)
    quarter = tb // ROPE_GROUPS
    ang = pos[0:quarter] * freq_ref[0:1, :]
    for g in range(1, ROPE_GROUPS):
        ang = ang + pos[g * quarter:(g + 1) * quarter] * freq_ref[g:g + 1, :]
    cos_p = jnp.cos(ang)
    sin_p = jnp.sin(ang)
    unpack = lambda t: jnp.concatenate(
        [pltpu.roll(t, (MLA_NOPE - MLA_ROPE * g) % LANES, axis=1) for g in range(ROPE_GROUPS)],
        axis=0)
    cosv = unpack(cos_p)
    sinv = jnp.where(in_rope, unpack(sin_p), 0.0)

    scale = (MLA_QK ** -0.5) * LOG2E
    qn = _rms(proj[:, :MLA_Q_LORA], qg_ref[...])
    qa = jnp.dot(qn.astype(BF16), wq_ref[...], preferred_element_type=F32)
    cq = jnp.where(lane < MLA_NOPE, scale, jnp.where(in_rope, cosv * scale, 0.0))
    sq = sinv * scale
    cq_t = jnp.concatenate([cq] * MLA_HEADS, axis=1)
    sq_t = jnp.concatenate([sq] * MLA_HEADS, axis=1)
    width = MLA_HEADS * HEAD_SLOT
    q = qa * cq_t + pltpu.roll(qa, width - MLA_ROPE, axis=1) * sq_t
    q_ref[0] = q.astype(BF16)

    kvn = _rms(proj[:, MLA_Q_LORA:MLA_Q_LORA + MLA_KV_LORA], kvg_ref[...])
    kva = jnp.dot(kvn.astype(BF16), wkv_ref[...], preferred_element_type=F32)
    misc = proj[:, MLA_Q_LORA + MLA_KV_LORA:MLA_Q_LORA + MLA_KV_LORA + MISC_W]
    kp = misc * jnp.where(in_rope, cosv, 0.0) + pltpu.roll(misc, MISC_W - MLA_ROPE, axis=1) * sinv
    k = kva[:, :width] + jnp.concatenate([kp] * MLA_HEADS, axis=1)
    k_ref[0] = k.astype(BF16)
    v_ref[0] = kva[:, width:].astype(BF16)

    z = misc + dtb_ref[...]
    softplus = jnp.maximum(z, 0.0) + jnp.log(1.0 + jnp.exp(-jnp.abs(z)))
    g = -jnp.exp(alog_ref[...]) * softplus
    gb_ref[0] = jnp.where(lane < GDN_HEADS, g, _sigmoid(misc))

    gdn_ref[0] = proj[:, MLA_Q_LORA + MLA_KV_LORA + MISC_W:].astype(BF16)


def _in_projection(x, positions, sc1, sh1, norm1_g, w1, q_norm_g, wq, kv_norm_g, wkv,
                   freq, alog, dtb, tb):
    b, s, d = x.shape
    hw = MLA_HEADS * HEAD_SLOT
    const = lambda shape: pl.BlockSpec(shape, lambda i, j: (0,) * len(shape))
    tok = lambda w: pl.BlockSpec((1, tb, w), lambda i, j: (i, j, 0))
    per_b = pl.BlockSpec((1, 1, d), lambda i, j: (i, 0, 0))
    return pl.pallas_call(
        _inproj_kernel,
        out_shape=(jax.ShapeDtypeStruct((b, s, hw), BF16),
                   jax.ShapeDtypeStruct((b, s, hw), BF16),
                   jax.ShapeDtypeStruct((b, s, MLA_HEADS * MLA_V), BF16),
                   jax.ShapeDtypeStruct((b, s, GDN_HEADS * GDN_SLAB), BF16),
                   jax.ShapeDtypeStruct((b, s, MISC_W), F32)),
        grid=(b, s // tb),
        in_specs=[tok(d), tok(1), per_b, per_b, const((1, d)), const(w1.shape),
                  const((1, MLA_Q_LORA)), const(wq.shape), const((1, MLA_KV_LORA)),
                  const(wkv.shape), const((ROPE_GROUPS, LANES)), const((1, LANES)),
                  const((1, LANES))],
        out_specs=(tok(hw), tok(hw), tok(MLA_HEADS * MLA_V), tok(GDN_HEADS * GDN_SLAB),
                   tok(MISC_W)),
        compiler_params=_cparams(("parallel", "parallel")),
        name="in_projection",
    )(x, positions.reshape(b, s, 1), sc1, sh1, norm1_g.reshape(1, d), w1,
      q_norm_g.reshape(1, -1), wq, kv_norm_g.reshape(1, -1), wkv, freq, alog, dtb)


def _attn_kernel(q_ref, k_ref, v_ref, o_ref, s_ref, p_ref, m_ref, l_ref, acc_ref, *, tq, nh):
    qi = pl.program_id(2)
    m_ref[...] = jnp.full(m_ref.shape, -jnp.inf, F32)
    l_ref[...] = jnp.zeros(l_ref.shape, F32)
    acc_ref[...] = jnp.zeros(acc_ref.shape, F32)

    def step(off, masked):
        segs = [(0, tq // 2, tq // 2), (tq // 2, tq, tq)] if masked else [(0, tq, tq)]
        width = lambda r: next(nc for r0, r1, nc in segs if r0 <= r * ATT_STRIP < r1)
        for h in range(nh):
            for r0, r1, nc in segs:
                kj = k_ref[0, pl.ds(off, nc), h * HEAD_SLOT:(h + 1) * HEAD_SLOT]
                s_ref[h, r0:r1, 0:nc] = lax.dot_general(
                    q_ref[0, r0:r1, h * HEAD_SLOT:(h + 1) * HEAD_SLOT], kj,
                    (((1,), (1,)), ((), ())), preferred_element_type=F32)
        n_strips = tq // ATT_STRIP

        def strip(h, r):
            rows = slice(r * ATT_STRIP, (r + 1) * ATT_STRIP)
            nc = width(r)
            sc = s_ref[h, rows, 0:nc]
            if masked:
                rid = lax.broadcasted_iota(jnp.int32, (ATT_STRIP, nc), 0) + r * ATT_STRIP
                cid = lax.broadcasted_iota(jnp.int32, (ATT_STRIP, nc), 1)
                sc = jnp.where(cid <= rid, sc, -jnp.inf)
            return rows, sc

        rep = lambda col: jnp.broadcast_to(col, (col.shape[0], LANES))
        m_new = []
        for h in range(nh):
            mx = jnp.concatenate([rep(jnp.max(strip(h, r)[1], axis=-1, keepdims=True))
                                  for r in range(n_strips)], axis=0)
            m_new.append(jnp.maximum(m_ref[h], mx))
        for h in range(nh):
            v_lanes = slice((h // 2) * 2 * MLA_V, (h // 2 + 1) * 2 * MLA_V)
            sums = []
            for r in range(n_strips):
                rows, sc = strip(h, r)
                nc = width(r)
                p = jnp.exp2(sc - jnp.concatenate([m_new[h][rows]] * (nc // LANES), axis=1))
                p_ref[h, rows, 0:nc] = p.astype(BF16)
                sums.append(rep(jnp.sum(p, axis=-1, keepdims=True)))
            alpha = jnp.exp2(m_ref[h] - m_new[h])
            l_ref[h] = alpha * l_ref[h] + jnp.concatenate(sums, axis=0)
            m_ref[h] = m_new[h]
            for r0, r1, nc in segs:
                acc_ref[h, r0:r1, :] = alpha[r0:r1] * acc_ref[h, r0:r1, :] + jnp.dot(
                    p_ref[h, r0:r1, 0:nc], v_ref[0, pl.ds(off, nc), v_lanes],
                    preferred_element_type=F32)

    def body(j, carry):
        step(pl.multiple_of(j * tq, tq), False)
        return carry

    lax.fori_loop(0, qi, body, 0)
    step(pl.multiple_of(qi * tq, tq), True)
    lane = lax.broadcasted_iota(jnp.int32, (tq, 2 * MLA_V), 1)
    outs = []
    for p in range(nh // 2):
        o0 = acc_ref[2 * p] / l_ref[2 * p]
        o1 = acc_ref[2 * p + 1] / l_ref[2 * p + 1]
        outs.append(jnp.where(lane < MLA_V, o0, o1))
    o_ref[0] = jnp.concatenate(outs, axis=1).astype(o_ref.dtype)


def _attention(q, k, v, tq, nh):
    b, s, _ = q.shape
    groups = MLA_HEADS // nh
    return pl.pallas_call(
        functools.partial(_attn_kernel, tq=tq, nh=nh),
        out_shape=jax.ShapeDtypeStruct((b, s, MLA_HEADS * MLA_V), BF16),
        grid=(b, groups, s // tq),
        in_specs=[pl.BlockSpec((1, tq, nh * HEAD_SLOT), lambda i, p, j: (i, j, p)),
                  pl.BlockSpec((1, s, nh * HEAD_SLOT), lambda i, p, j: (i, 0, p)),
                  pl.BlockSpec((1, s, nh * MLA_V), lambda i, p, j: (i, 0, p))],
        out_specs=pl.BlockSpec((1, tq, nh * MLA_V), lambda i, p, j: (i, j, p)),
        scratch_shapes=[pltpu.VMEM((nh, tq, tq), F32), pltpu.VMEM((nh, tq, tq), BF16),
                        pltpu.VMEM((nh, tq, LANES), F32), pltpu.VMEM((nh, tq, LANES), F32),
                        pltpu.VMEM((nh, tq, 2 * MLA_V), F32)],
        compiler_params=_cparams(("parallel", "parallel", "arbitrary")),
        name="mla_attention",
    )(q, k, v)


def _gdn_kernel(slab_ref, gb_ref, cw_ref, ng_ref, shf_ref, ind_ref, sel_ref, o_ref, tail_ref,
                state_ref, *, ts, groups):
    si = pl.program_id(1)
    gw = GDN_GW * GDN_DK
    hist_rows = 8

    @pl.when(si == 0)
    def _():
        tail_ref[...] = jnp.zeros(tail_ref.shape, F32)
        state_ref[...] = jnp.zeros(state_ref.shape, F32)

    xs_bf = slab_ref[0]
    xs = xs_bf.astype(F32)
    width = xs.shape[1]
    conv = cw_ref[CONV_WIDTH - 1:CONV_WIDTH, :] * xs
    hist = jnp.zeros((hist_rows, width), F32)
    for j in range(CONV_WIDTH - 1):
        wj = cw_ref[j:j + 1, :]
        conv = conv + wj * jnp.dot(shf_ref[j], xs_bf, preferred_element_type=F32)
        start = hist_rows - (CONV_WIDTH - 1) + j
        hist = hist + wj * tail_ref[start:start + hist_rows, :]
    conv = jnp.concatenate([conv[:hist_rows] + hist, conv[hist_rows:]], axis=0)
    tail_ref[0:hist_rows, :] = xs[ts - hist_rows:ts, :]
    act = _silu(conv)

    gb = gb_ref[0]
    g1 = gb.astype(BF16)
    r1 = gb - g1.astype(F32)
    g2 = r1.astype(BF16)
    g3 = (r1 - g2.astype(F32)).astype(BF16)
    sel = sel_ref[...]
    gate_w = (jnp.dot(g1, sel, preferred_element_type=F32)
              + jnp.dot(g2, sel, preferred_element_type=F32)
              + jnp.dot(g3, sel, preferred_element_type=F32))

    n_chunks = ts // CHUNK
    row = lax.broadcasted_iota(jnp.int32, (CHUNK, gw), 0)
    col = lax.broadcasted_iota(jnp.int32, (CHUNK, gw), 1) % GDN_DK
    tri_incl = col <= row
    tri_strict = col < row
    eye = col == row
    brow = lax.broadcasted_iota(jnp.int32, (gw, gw), 0) // GDN_DK
    bcol = lax.broadcasted_iota(jnp.int32, (gw, gw), 1) // GDN_DK
    same_head = brow == bcol
    crow = lax.broadcasted_iota(jnp.int32, (ts, gw), 0) % CHUNK

    def split2(x):
        hi = x.astype(BF16)
        return hi, (x - hi.astype(F32)).astype(BF16)

    def head_sums(x):
        hi, lo = split2(x)
        return (jnp.dot(hi, ind_ref[...], preferred_element_type=F32)
                + jnp.dot(lo, ind_ref[...], preferred_element_type=F32))

    def blockdiag(x):
        return jnp.where(same_head, jnp.concatenate([x.astype(BF16)] * GDN_GW, axis=0), 0)

    def wdot(a, b):
        return jnp.dot(a.astype(BF16), blockdiag(b), preferred_element_type=F32)

    prep = []
    for gi in range(groups):
        base = gi * GDN_GW * GDN_SLAB
        q_raw = act[:, base:base + gw]
        k_raw = act[:, base + gw:base + 2 * gw]
        v_all = act[:, base + 2 * gw:base + 3 * gw]
        z_all = xs[:, base + 3 * gw:base + 4 * gw]
        q_all = q_raw * (lax.rsqrt(head_sums(q_raw * q_raw) + EPS) * (GDN_DK ** -0.5))
        k_all = k_raw * lax.rsqrt(head_sums(k_raw * k_raw) + EPS)
        g_w = gate_w[:, gi * gw:(gi + 1) * gw]
        b_w = gate_w[:, (groups + gi) * gw:(groups + gi + 1) * gw]
        gc_w = g_w
        shift = 1
        while shift < CHUNK:
            rolled = pltpu.roll(gc_w, shift, axis=0)
            gc_w = gc_w + jnp.where(crow >= shift, rolled, 0.0)
            shift *= 2
        prep.append((q_all, k_all, v_all, z_all, b_w, gc_w, jnp.exp(gc_w)))

    units = [(gi, c) for c in range(n_chunks) for gi in range(groups)]
    a_low, qk, yu0, yw0, qd, kt, cd = ({} for _ in range(7))
    for u in units:
        gi, c = u
        q_all, k_all, v_all, _, b_w, gc_w, eg_w = prep[gi]
        sl = slice(c * CHUNK, (c + 1) * CHUNK)
        qc, kc, vc, bc, gcum, eg = q_all[sl], k_all[sl], v_all[sl], b_w[sl], gc_w[sl], eg_w[sl]
        grow = jnp.sum(jnp.where(eye, gcum, 0.0), axis=0, keepdims=True)
        diff = gcum - grow
        decay = jnp.where(tri_incl, jnp.exp(jnp.where(tri_incl, diff, 0.0)), 0.0)
        k_beta = kc * bc
        kq = jnp.concatenate([k_beta, qc], axis=0).astype(BF16)
        kk = lax.dot_general(kq, blockdiag(kc), (((1,), (1,)), ((), ())),
                             preferred_element_type=F32)
        a_low[u] = jnp.where(tri_strict, kk[:CHUNK] * decay, 0.0)
        qk[u] = jnp.where(tri_incl, kk[CHUNK:] * decay, 0.0)
        yu0[u] = vc * bc
        yw0[u] = k_beta * eg
        qd[u] = qc * eg
        glast = gcum[CHUNK - 1:CHUNK, :]
        kt[u] = kc * jnp.exp(glast - gcum)
        cd[u] = eg[CHUNK - 1:CHUNK, :]

    blk = 2
    tinv = {u: jnp.where(eye, 1.0, 0.0) - jnp.where((row // blk) == (col // blk), a_low[u], 0.0)
            for u in units}
    while blk < CHUNK:
        in_big = (row // (2 * blk)) == (col // (2 * blk))
        off_mask = in_big & ((row // blk) != (col // blk))
        left = {u: wdot(tinv[u], jnp.where(off_mask, a_low[u], 0.0)) for u in units}
        tinv = {u: tinv[u] - wdot(left[u], tinv[u]) for u in units}
        blk *= 2

    yu = {u: wdot(tinv[u], yu0[u]) for u in units}
    yw = {u: wdot(tinv[u], yw0[u]) for u in units}
    tdot = lambda a, b: jnp.where(same_head, lax.dot_general(
        a.astype(BF16), b.astype(BF16), (((0,), (0,)), ((), ())),
        preferred_element_type=F32), 0.0)
    q_bd = {u: tdot(kt[u], yu[u]) for u in units}
    p_bd = {u: tdot(kt[u], yw[u]) for u in units}
    qu = {u: wdot(qk[u], yu[u]) for u in units}
    qw = {u: wdot(qk[u], yw[u]) for u in units}

    outs = [[] for _ in range(groups)]
    states = [state_ref[gi] for gi in range(groups)]
    for c in range(n_chunks):
        for gi in range(groups):
            u = (gi, c)
            sb = states[gi].astype(BF16)
            r_mat = qd[u] - qw[u]
            outs[gi].append(jnp.dot(r_mat.astype(BF16), sb, preferred_element_type=F32) + qu[u])
            states[gi] = (states[gi] * cd[u]
                          - jnp.dot(p_bd[u].astype(BF16), sb, preferred_element_type=F32)
                          + q_bd[u])
    finals = []
    for gi in range(groups):
        state_ref[gi] = states[gi]
        o_all = jnp.concatenate(outs[gi], axis=0)
        inv = lax.rsqrt(head_sums(o_all * o_all) * (1.0 / GDN_DV) + EPS)
        finals.append(o_all * inv * ng_ref[...] * _silu(prep[gi][3]))
    o_ref[0] = jnp.concatenate(finals, axis=1).astype(o_ref.dtype)


def _gdn(slab, gb, conv_slab, gdn_norm_g, ts):
    b, s, width = slab.shape
    groups = GDN_HEADS // GDN_GW
    gw = GDN_GW * GDN_DK
    r = jnp.arange(ts)
    shifts = jnp.stack([(r[:, None] - r[None, :]) == (CONV_WIDTH - 1 - j)
                        for j in range(CONV_WIDTH - 1)]).astype(BF16)
    li = jnp.arange(gw)
    ind = (li[:, None] // GDN_DK == li[None, :] // GDN_DK).astype(BF16)
    src = jnp.arange(LANES)[:, None]
    dst = jnp.arange(2 * GDN_HEADS * GDN_DK)[None, :]
    sel = (src == dst // GDN_DK).astype(BF16)
    return pl.pallas_call(
        functools.partial(_gdn_kernel, ts=ts, groups=groups),
        out_shape=jax.ShapeDtypeStruct((b, s, GDN_HEADS * GDN_DV), BF16),
        grid=(b, s // ts),
        in_specs=[pl.BlockSpec((1, ts, width), lambda i, j: (i, j, 0)),
                  pl.BlockSpec((1, ts, MISC_W), lambda i, j: (i, j, 0)),
                  pl.BlockSpec((CONV_WIDTH, width), lambda i, j: (0, 0)),
                  pl.BlockSpec((1, gw), lambda i, j: (0, 0)),
                  pl.BlockSpec((CONV_WIDTH - 1, ts, ts), lambda i, j: (0, 0, 0)),
                  pl.BlockSpec((gw, gw), lambda i, j: (0, 0)),
                  pl.BlockSpec((LANES, 2 * GDN_HEADS * GDN_DK), lambda i, j: (0, 0))],
        out_specs=pl.BlockSpec((1, ts, GDN_HEADS * GDN_DV), lambda i, j: (i, j, 0)),
        scratch_shapes=[pltpu.VMEM((16, width), F32),
                        pltpu.VMEM((groups, gw, gw), F32)],
        compiler_params=_cparams(("parallel", "arbitrary")),
        name="gated_deltanet",
    )(slab, gb, conv_slab, jnp.tile(gdn_norm_g, GDN_GW).reshape(1, gw), shifts, ind, sel)


def _outproj_kernel(x_ref, ao_ref, go_ref, gt_ref, sc_ref, sh_ref, mg_ref, wo_ref, g2_ref,
                    rw_ref, rb_ref, x1_ref, h2_ref, route_ref, rt_ref, cnt_ref, carry_ref, *, tb):
    first = (pl.program_id(0) == 0) & (pl.program_id(1) == 0)

    @pl.when(first)
    def _():
        carry_ref[...] = jnp.zeros(carry_ref.shape, F32)

    mla = _rms(ao_ref[0].astype(F32), mg_ref[...])
    cat = jnp.concatenate([mla.astype(BF16), go_ref[0]], axis=1)
    mix = jnp.dot(cat, wo_ref[...], preferred_element_type=F32)
    x1 = x_ref[0] + gt_ref[0] * mix
    x1_ref[0] = x1
    h2 = _rms(x1, g2_ref[...]) * (1.0 + sc_ref[0]) + sh_ref[0]
    h2_ref[0] = _pack_halves(h2)

    h_hi = h2.astype(BF16)
    h_lo = (h2 - h_hi.astype(F32)).astype(BF16)
    main = jnp.dot(h_hi, rw_ref[...], preferred_element_type=F32)
    logits = (main[:, :LANES] + main[:, LANES:]
              + jnp.dot(h_lo, rw_ref[:, :LANES], preferred_element_type=F32) + rb_ref[...])
    lane = lax.broadcasted_iota(jnp.int32, (tb, LANES), 1).astype(F32)
    work = jnp.where(lane < N_EXPERTS, logits, -jnp.inf)
    vals, idxs = [], []
    onehot = jnp.zeros((tb, LANES), F32)
    for _ in range(TOP_K):
        mx = jnp.max(work, axis=-1, keepdims=True)
        ix = jnp.min(jnp.where(work == mx, lane, float(LANES)), axis=-1, keepdims=True)
        sel = lane == ix
        onehot = jnp.where(sel, 1.0, onehot)
        work = jnp.where(sel, -jnp.inf, work)
        vals.append(mx)
        idxs.append(ix)
    exps = [jnp.exp(v - vals[0]) for v in vals]
    den = exps[0] + exps[1] + exps[2] + exps[3]

    r = lax.broadcasted_iota(jnp.int32, (tb, tb), 0)
    c = lax.broadcasted_iota(jnp.int32, (tb, tb), 1)
    tri = jnp.where(c < r, 1.0, 0.0).astype(BF16)
    before = jnp.dot(tri, onehot.astype(BF16), preferred_element_type=F32) + carry_ref[...]
    route = jnp.zeros((tb, LANES), F32)
    for kk in range(TOP_K):
        rank = jnp.sum(jnp.where(lane == idxs[kk], before, 0.0), axis=-1, keepdims=True)
        route = jnp.where(lane == kk, idxs[kk], route)
        route = jnp.where(lane == TOP_K + kk, rank, route)
        route = jnp.where(lane == 2 * TOP_K + kk, exps[kk] / den, route)
    route_ref[0] = route
    rt_ref[0] = route.T[:ROUTE_ROWS, :]
    total = carry_ref[...] + jnp.sum(onehot, axis=0, keepdims=True)
    carry_ref[...] = total
    cnt_ref[...] = total


def _out_projection(x, attn_o, gdn_o, gt1, sc2, sh2, mla_out_g, w_out, norm2_g, rw, rb, tb, b0, nb):
    _, s, d = x.shape
    const = lambda shape: pl.BlockSpec(shape, lambda i, j: (0,) * len(shape))
    tok_in = lambda w: pl.BlockSpec((1, tb, w), lambda i, j: (i + b0, j, 0))
    tok_out = lambda w: pl.BlockSpec((1, tb, w), lambda i, j: (i, j, 0))
    per_b = pl.BlockSpec((1, 1, d), lambda i, j: (i + b0, 0, 0))
    half = attn_o.shape[-1]
    return pl.pallas_call(
        functools.partial(_outproj_kernel, tb=tb),
        out_shape=(jax.ShapeDtypeStruct((nb, s, d), F32),
                   jax.ShapeDtypeStruct((nb, s, d // 2), F32),
                   jax.ShapeDtypeStruct((nb, s, LANES), F32),
                   jax.ShapeDtypeStruct((nb, ROUTE_ROWS, s), F32),
                   jax.ShapeDtypeStruct((1, LANES), F32)),
        grid=(nb, s // tb),
        in_specs=[tok_in(d), tok_in(half), tok_in(half), per_b, per_b, per_b, const((1, half)),
                  const(w_out.shape), const((1, d)), const(rw.shape), const((1, LANES))],
        out_specs=(tok_out(d), tok_out(d // 2), tok_out(LANES),
                   pl.BlockSpec((1, ROUTE_ROWS, tb), lambda i, j: (i, 0, j)), const((1, LANES))),
        scratch_shapes=[pltpu.VMEM((1, LANES), F32)],
        compiler_params=_cparams(("arbitrary", "arbitrary")),
        name="out_projection_router",
    )(x, attn_o, gdn_o, gt1, sc2, sh2, mla_out_g.reshape(1, half), w_out,
      norm2_g.reshape(1, d), rw, rb)


def _sc_mesh():
    return plsc.VectorSubcoreMesh(core_axis_name="c", subcore_axis_name="s",
                                  num_cores=SC_CORES, num_subcores=SC_SUBCORES)


def _sc_worker():
    return lax.axis_index("s") * SC_CORES + lax.axis_index("c")


def _dispatch_rows(h2, dest_km, rows):
    t, d = h2.shape
    per_worker = t // SC_WORKERS
    n_chunks = per_worker // SC_CHUNK

    @functools.partial(
        pl.kernel, out_type=jax.ShapeDtypeStruct((rows, d), h2.dtype), mesh=_sc_mesh(),
        scratch_types=[pltpu.VMEM((SC_CHUNK,), jnp.int32), pltpu.VMEM((SC_CHUNK, d), h2.dtype)],
        name="moe_dispatch")
    def run(h2_hbm, dest_hbm, xs_hbm, idx_v, rows_v):
        base_w = _sc_worker() * per_worker

        @pl.loop(0, n_chunks)
        def _(ci):
            base = pl.multiple_of(base_w + ci * SC_CHUNK, SC_CHUNK)
            pltpu.sync_copy(h2_hbm.at[pl.ds(base, SC_CHUNK)], rows_v)
            for kk in range(TOP_K):
                pltpu.sync_copy(dest_hbm.at[kk, pl.ds(base, SC_CHUNK)], idx_v)
                pltpu.sync_copy(rows_v, xs_hbm.at[idx_v])

    return run(h2, dest_km)


def _collect_rows(y_rows, dest_km):
    _, d = y_rows.shape
    t = dest_km.shape[1]
    per_worker = t // SC_WORKERS
    n_chunks = per_worker // SC_CHUNK

    @functools.partial(
        pl.kernel, out_type=jax.ShapeDtypeStruct((TOP_K, t, d), y_rows.dtype), mesh=_sc_mesh(),
        scratch_types=[pltpu.VMEM((SC_CHUNK,), jnp.int32), pltpu.VMEM((SC_CHUNK, d), y_rows.dtype)],
        name="moe_collect")
    def run(y_hbm, dest_hbm, out_hbm, idx_v, rows_v):
        base_w = _sc_worker() * per_worker

        @pl.loop(0, n_chunks)
        def _(ci):
            base = pl.multiple_of(base_w + ci * SC_CHUNK, SC_CHUNK)
            for kk in range(TOP_K):
                pltpu.sync_copy(dest_hbm.at[kk, pl.ds(base, SC_CHUNK)], idx_v)
                pltpu.sync_copy(y_hbm.at[idx_v], rows_v)
                pltpu.sync_copy(rows_v, out_hbm.at[kk, pl.ds(base, SC_CHUNK)])

    return run(y_rows, dest_km)


def _experts_kernel(be_ref, nu_ref, nxt_ref, slot_ref, xs_ref, wgu_hbm, bgu_ref, wd_hbm, bd_ref,
                    y_ref, wgu_f32, wd_f32, wgu_bf, wd_bf, sem):
    i = pl.program_id(0)
    e = be_ref[i]
    new_expert = (i == 0) | (e != be_ref[jnp.maximum(i - 1, 0)])

    def fetch(expert, slot):
        return (pltpu.make_async_copy(wgu_hbm.at[expert], wgu_f32.at[slot], sem.at[0, slot]),
                pltpu.make_async_copy(wd_hbm.at[expert], wd_f32.at[slot], sem.at[1, slot]))

    @pl.when(i == 0)
    def _():
        for cp in fetch(e, slot_ref[e]):
            cp.start()

    @pl.when(new_expert)
    def _():
        slot = slot_ref[e]
        for cp in fetch(e, slot):
            cp.wait()
        wgu_bf[...] = wgu_f32[slot].astype(BF16)
        wd_bf[...] = wd_f32[slot].astype(BF16)
        nxt = nxt_ref[e]

        @pl.when(nxt >= 0)
        def _():
            for cp in fetch(nxt, 1 - slot):
                cp.start()

    @pl.when(i < nu_ref[0])
    def _():
        x_lo, x_hi = _unpack_halves(xs_ref[...])
        half = x_lo.shape[-1]
        gu = (jnp.dot(x_lo.astype(BF16), wgu_bf[:half, :], preferred_element_type=F32)
              + jnp.dot(x_hi.astype(BF16), wgu_bf[half:, :], preferred_element_type=F32)
              + bgu_ref[0])
        gate = jnp.minimum(gu[:, :D_EXPERT], SWIGLU_LIMIT)
        up = jnp.clip(gu[:, D_EXPERT:], -SWIGLU_LIMIT, SWIGLU_LIMIT)
        act = (up + 1.0) * (gate * _sigmoid(SWIGLU_ALPHA * gate))
        y = jnp.dot(act.astype(BF16), wd_bf[...], preferred_element_type=F32) + bd_ref[0]
        y_ref[...] = _pack_halves(y)


def _experts(blk_e, n_used, next_e, slot_e, xs, wgu, bgu, wd, bd, bm):
    rows, half = xs.shape
    d = 2 * half
    n_blocks = rows // bm
    row_map = lambda i, be, nu, nx, sl: (jnp.maximum(jnp.minimum(i, nu[0] - 1), 0), 0)
    exp_map = lambda i, be, nu, nx, sl: (be[i], 0, 0)
    grid_spec = pltpu.PrefetchScalarGridSpec(
        num_scalar_prefetch=4,
        grid=(n_blocks,),
        in_specs=[pl.BlockSpec((bm, half), row_map),
                  pl.BlockSpec(memory_space=pl.ANY),
                  pl.BlockSpec((1, 1, 2 * D_EXPERT), exp_map),
                  pl.BlockSpec(memory_space=pl.ANY),
                  pl.BlockSpec((1, 1, d), exp_map)],
        out_specs=pl.BlockSpec((bm, half), row_map),
        scratch_shapes=[pltpu.VMEM((2, d, 2 * D_EXPERT), F32),
                        pltpu.VMEM((2, D_EXPERT, d), F32),
                        pltpu.VMEM((d, 2 * D_EXPERT), BF16),
                        pltpu.VMEM((D_EXPERT, d), BF16),
                        pltpu.SemaphoreType.DMA((2, 2))])
    return pl.pallas_call(
        _experts_kernel,
        out_shape=jax.ShapeDtypeStruct((rows, half), F32),
        grid_spec=grid_spec,
        compiler_params=_cparams(("arbitrary",)),
        name="expert_mlp",
    )(blk_e, n_used, next_e, slot_e, xs, wgu, bgu, wd, bd)


def _final_kernel(x1_ref, yk_ref, route_ref, gt_ref, fg_ref, *rest):
    o_ref = rest[-1]
    route = route_ref[0]
    ffn = jnp.zeros(x1_ref.shape[1:], F32)
    for kk in range(TOP_K):
        wk = route[:, 2 * TOP_K + kk:2 * TOP_K + kk + 1]
        y_lo, y_hi = _unpack_halves(yk_ref[kk, 0])
        ffn = ffn + wk * jnp.concatenate([y_lo, y_hi], axis=1)
    x2 = x1_ref[0] + gt_ref[0] * ffn
    o_ref[0] = _rms(x2, fg_ref[...])


def _final(x1, yk, route, gt2, final_g, tb, b0, b_total, prev):
    nb, s, d = x1.shape
    in_specs = [pl.BlockSpec((1, tb, d), lambda i, j: (i, j, 0)),
                pl.BlockSpec((TOP_K, 1, tb, d // 2), lambda i, j: (0, i, j, 0)),
                pl.BlockSpec((1, tb, LANES), lambda i, j: (i, j, 0)),
                pl.BlockSpec((1, 1, d), lambda i, j: (i + b0, 0, 0)),
                pl.BlockSpec((1, d), lambda i, j: (0, 0))]
    args = [x1, yk, route, gt2, final_g.reshape(1, d)]
    aliases = {}
    if prev is not None:
        in_specs.append(pl.BlockSpec(memory_space=pl.ANY))
        args.append(prev)
        aliases = {len(args) - 1: 0}
    return pl.pallas_call(
        _final_kernel,
        out_shape=jax.ShapeDtypeStruct((b_total, s, d), F32),
        grid=(nb, s // tb),
        in_specs=in_specs,
        out_specs=pl.BlockSpec((1, tb, d), lambda i, j: (i + b0, j, 0)),
        input_output_aliases=aliases,
        compiler_params=_cparams(("parallel", "parallel")),
        name="combine_final_norm",
    )(*args)


def _rot_cols(w):
    half = MLA_ROPE // 2
    return jnp.concatenate([-w[..., half:], w[..., :half]], axis=-1)


def _prep_weights(w_in, w_q_b, w_kv_b, conv_w, A_log, dt_bias):
    d = w_in.shape[0]
    cuts = [MLA_Q_LORA, MLA_KV_LORA, MLA_ROPE, GDN_HEADS * GDN_DK, GDN_HEADS * GDN_DK,
            GDN_HEADS * GDN_DV, GDN_HEADS * GDN_DV, GDN_HEADS, GDN_HEADS]
    offs = [0]
    for cw in cuts:
        offs.append(offs[-1] + cw)
    part = lambda i: w_in[:, offs[i]:offs[i + 1]]
    k_pe = part(2)
    misc = jnp.concatenate(
        [part(7), part(8), jnp.zeros((d, MLA_NOPE - 2 * GDN_HEADS), w_in.dtype), k_pe,
         _rot_cols(k_pe)], axis=1)
    heads = lambda w, n: w.reshape(w.shape[0], GDN_HEADS // GDN_GW, GDN_GW * n)
    slab = jnp.concatenate([heads(part(3), GDN_DK), heads(part(4), GDN_DK),
                            heads(part(5), GDN_DV), heads(part(6), GDN_DV)], axis=-1)
    w1 = jnp.concatenate([part(0), part(1), misc, slab.reshape(d, GDN_HEADS * GDN_SLAB)],
                         axis=1).astype(BF16)

    wq3 = w_q_b.reshape(MLA_Q_LORA, MLA_HEADS, MLA_QK)
    pe = wq3[..., MLA_NOPE:]
    wq = jnp.concatenate([wq3[..., :MLA_NOPE], pe, _rot_cols(pe)], axis=-1)
    wq = wq.reshape(MLA_Q_LORA, MLA_HEADS * HEAD_SLOT).astype(BF16)

    wkv3 = w_kv_b.reshape(MLA_KV_LORA, MLA_HEADS, MLA_NOPE + MLA_V)
    wk = jnp.concatenate([wkv3[..., :MLA_NOPE],
                          jnp.zeros((MLA_KV_LORA, MLA_HEADS, HEAD_SLOT - MLA_NOPE), w_kv_b.dtype)],
                         axis=-1).reshape(MLA_KV_LORA, MLA_HEADS * HEAD_SLOT)
    wv = wkv3[..., MLA_NOPE:].reshape(MLA_KV_LORA, MLA_HEADS * MLA_V)
    wkv = jnp.concatenate([wk, wv], axis=1).astype(BF16)

    nk = GDN_HEADS * GDN_DK
    cheads = lambda w, n: w.reshape(CONV_WIDTH, GDN_HEADS // GDN_GW, GDN_GW * n)
    conv_slab = jnp.concatenate(
        [cheads(conv_w[:, :nk], GDN_DK), cheads(conv_w[:, nk:2 * nk], GDN_DK),
         cheads(conv_w[:, 2 * nk:], GDN_DV),
         jnp.zeros((CONV_WIDTH, GDN_HEADS // GDN_GW, GDN_GW * GDN_DV), conv_w.dtype)],
        axis=-1).reshape(CONV_WIDTH, GDN_HEADS * GDN_SLAB)

    half = MLA_ROPE // 2
    inv_freq = ROPE_THETA ** (-jnp.arange(half, dtype=F32) / half)
    lane = jnp.arange(LANES)
    freq = jnp.where(lane[None, :] // MLA_ROPE == jnp.arange(ROPE_GROUPS)[:, None],
                     jnp.tile(inv_freq, LANES // half)[None, :], 0.0)
    padl = lambda a: jnp.concatenate([a.astype(F32), jnp.zeros((LANES - a.shape[0],), F32)]).reshape(1, LANES)
    return w1, wq, wkv, conv_slab, freq, padl(A_log), padl(dt_bias)


def _layer(x, mod, positions, w_in, q_norm_g, w_q_b, kv_norm_g, w_kv_b, mla_out_g, conv_w,
           A_log, dt_bias, gdn_norm_g, w_out, norm1_g, norm2_g, router_w, router_b, w_gate_up,
           b_gate_up, w_down, b_down, final_g):
    b, s, d = x.shape
    t = b * s
    sh1, sc1, gt1, sh2, sc2, gt2 = [m.reshape(b, 1, d) for m in jnp.split(mod, 6, axis=-1)]
    w1, wq, wkv, conv_slab, freq, alog, dtb = _prep_weights(w_in, w_q_b, w_kv_b, conv_w, A_log,
                                                            dt_bias)
    tb = min(256, s)
    q, k, v, slab, gb = _in_projection(x, positions, sc1, sh1, norm1_g, w1, q_norm_g, wq,
                                       kv_norm_g, wkv, freq, alog, dtb, tb)
    attn_o = _attention(q, k, v, min(512, s), 8)
    gdn_o = _gdn(slab, gb, conv_slab, gdn_norm_g, min(256, s))

    rw = jnp.concatenate([router_w, jnp.zeros((d, LANES - N_EXPERTS), router_w.dtype)], axis=1)
    rw_hi = rw.astype(BF16)
    rw = jnp.concatenate([rw_hi, (rw - rw_hi.astype(F32)).astype(BF16)], axis=1)
    rb = jnp.concatenate([router_b, jnp.zeros((LANES - N_EXPERTS,), router_b.dtype)]).reshape(1, LANES)
    w_out_bf = w_out.astype(BF16)
    bgu = b_gate_up.reshape(N_EXPERTS, 1, -1)
    bd = b_down.reshape(N_EXPERTS, 1, -1)
    n_parts = MOE_PARTS if b % MOE_PARTS == 0 else 1
    nb = b // n_parts
    out = None
    for part in range(n_parts):
        b0 = part * nb
        x1, h2, route, route_t, counts = _out_projection(x, attn_o, gdn_o, gt1, sc2, sh2,
                                                         mla_out_g, w_out_bf, norm2_g, rw, rb,
                                                         tb, b0, nb)
        yk = _moe(h2, route_t, counts, w_gate_up, bgu, w_down, bd)
        out = _final(x1, yk, route, gt2, final_g, tb, b0, b, out)
    return out


def _moe(h2, route_t, counts, w_gate_up, bgu, w_down, bd):
    nb, s, half = h2.shape
    t = nb * s
    bm = EXPERT_ROWS
    slot_major = lambda r0: jnp.swapaxes(route_t[:, r0:r0 + TOP_K, :], 0, 1).reshape(TOP_K, t)
    idx = slot_major(0).astype(jnp.int32)
    rank = slot_major(TOP_K).astype(jnp.int32)
    cnt = counts[0, :N_EXPERTS].astype(jnp.int32)
    padded = ((cnt + bm - 1) // bm) * bm
    pend = jnp.cumsum(padded)
    pstart = pend - padded
    base = jnp.zeros_like(idx)
    for e in range(N_EXPERTS):
        base = jnp.where(idx == e, pstart[e], base)
    dest_km = base + rank
    n_blocks = (t * TOP_K + N_EXPERTS * (bm - 1) + bm - 1) // bm
    rows = n_blocks * bm
    n_used = (pend[-1] // bm).astype(jnp.int32).reshape(1)
    blk_start = jnp.arange(n_blocks, dtype=jnp.int32) * bm
    blk_e = jnp.minimum(jnp.sum(blk_start[:, None] >= pend[None, :], axis=1), N_EXPERTS - 1)
    last_e = blk_e[jnp.maximum(n_used[0] - 1, 0)]
    blk_e = jnp.where(jnp.arange(n_blocks) < n_used[0], blk_e, last_e).astype(jnp.int32)

    present = padded > 0
    eidx = jnp.arange(N_EXPERTS, dtype=jnp.int32)
    later = present[None, :] & (eidx[None, :] > eidx[:, None])
    next_e = jnp.where(jnp.any(later, axis=1), jnp.argmax(later, axis=1), -1).astype(jnp.int32)
    slot_e = ((jnp.cumsum(present.astype(jnp.int32)) - 1) % 2).astype(jnp.int32)

    xs = _dispatch_rows(h2.reshape(t, half), dest_km, rows)
    y_rows = _experts(blk_e, n_used, next_e, slot_e, xs, w_gate_up, bgu, w_down, bd, bm)
    return _collect_rows(y_rows, dest_km).reshape(TOP_K, nb, s, half)


def kernel(x, c, positions, ada_w, ada_b, norm1_g, w_in, q_norm_g, w_q_b, kv_norm_g, w_kv_b, mla_out_g, conv_w, A_log, dt_bias, gdn_norm_g, w_out, norm2_g, router_w, router_b, w_gate_up, b_gate_up, w_down, b_down, final_g):
    depth = ada_w.shape[0]
    assert depth == 1
    l = 0
    mod = _modulation(c, ada_w[l], ada_b[l])
    return _layer(x, mod, positions, w_in[l], q_norm_g[l], w_q_b[l], kv_norm_g[l],
                  w_kv_b[l], mla_out_g[l], conv_w[l], A_log[l], dt_bias[l], gdn_norm_g[l],
                  w_out[l], norm1_g[l], norm2_g[l], router_w[l], router_b[l],
                  w_gate_up[l], b_gate_up[l], w_down[l], b_down[l], final_g)
```

```python
import functools

import jax
import jax.numpy as jnp
from jax import lax
from jax.experimental import pallas as pl
from jax.experimental.pallas import tpu as pltpu
from jax.experimental.pallas import tpu_sc as plsc

F32 = jnp.float32
BF16 = jnp.bfloat16

D_MODEL = 1024
EPS = 1e-6
MLA_HEADS = 8
MLA_NOPE = 64
MLA_ROPE = 32
MLA_V = 64
MLA_QK = MLA_NOPE + MLA_ROPE
MLA_Q_LORA = 384
MLA_KV_LORA = 256
ROPE_THETA = 10000.0
GDN_HEADS = 8
GDN_DK = 64
GDN_DV = 64
CONV_WIDTH = 4
CHUNK = 64
N_EXPERTS = 32
TOP_K = 4
D_EXPERT = D_MODEL
SWIGLU_ALPHA = 1.702
SWIGLU_LIMIT = 7.0

LANES = 128
HEAD_SLOT = 128
GDN_SLAB = 256
GDN_GW = 4
CONV_BLOCK = 256
MISC_W = 128
PROJ_W = MLA_Q_LORA + MLA_KV_LORA + MISC_W + GDN_HEADS * GDN_SLAB
VMEM_LIMIT = 56 * 1024 * 1024
MOE_PARTS = 2
EXPERT_ROWS = 512
SC_CORES = 2
SC_SUBCORES = 16
SC_WORKERS = SC_CORES * SC_SUBCORES
SC_CHUNK = 128
ROPE_GROUPS = LANES // MLA_ROPE
ROUTE_ROWS = 16
ATT_STRIP = 32
LOG2E = 1.4426950408889634


def _cparams(sem):
    return pltpu.CompilerParams(dimension_semantics=sem, vmem_limit_bytes=VMEM_LIMIT)


def _rms(x, g):
    return x * lax.rsqrt(jnp.mean(x * x, axis=-1, keepdims=True) + EPS) * g


def _sigmoid(x):
    return 1.0 / (1.0 + jnp.exp(-x))


def _silu(x):
    return x * _sigmoid(x)


def _pack_halves(x):
    w = x.shape[-1] // 2
    lo = lax.bitcast_convert_type(x[:, :w].astype(BF16).astype(F32), jnp.uint32) >> 16
    hi = lax.bitcast_convert_type(x[:, w:].astype(BF16).astype(F32), jnp.uint32) & jnp.uint32(0xFFFF0000)
    return lax.bitcast_convert_type(lo | hi, F32)


def _unpack_halves(p):
    u = lax.bitcast_convert_type(p, jnp.uint32)
    return (lax.bitcast_convert_type(u << 16, F32),
            lax.bitcast_convert_type(u & jnp.uint32(0xFFFF0000), F32))


def _mod_kernel(c_ref, w_ref, b_ref, o_ref):
    c = c_ref[...]
    o_ref[...] = jnp.dot(_silu(c), w_ref[...], preferred_element_type=F32,
                         precision=lax.Precision.HIGHEST) + b_ref[...]


def _modulation(c, ada_w, ada_b):
    b, d = c.shape
    n = ada_w.shape[1]
    return pl.pallas_call(
        _mod_kernel,
        out_shape=jax.ShapeDtypeStruct((b, n), F32),
        grid=(n // d,),
        in_specs=[pl.BlockSpec((b, d), lambda j: (0, 0)),
                  pl.BlockSpec((d, d), lambda j: (0, j)),
                  pl.BlockSpec((1, d), lambda j: (0, j))],
        out_specs=pl.BlockSpec((b, d), lambda j: (0, j)),
        compiler_params=_cparams(("arbitrary",)),
        name="adaln_mod",
    )(c, ada_w, ada_b.reshape(1, n))


def _inproj_kernel(x_ref, pos_ref, sc_ref, sh_ref, g1_ref, w1_ref, qg_ref, wq_ref, kvg_ref,
                   wkv_ref, freq_ref, alog_ref, dtb_ref,
                   q_ref, k_ref, v_ref, gdn_ref, gb_ref):
    x = x_ref[0]
    h = _rms(x, g1_ref[...]) * (1.0 + sc_ref[0]) + sh_ref[0]
    proj = jnp.dot(h.astype(BF16), w1_ref[...], preferred_element_type=F32)

    tb = x.shape[0]
    lane = lax.broadcasted_iota(jnp.int32, (tb, LANES), 1)
    in_rope = (lane >= MLA_NOPE) & (lane < MLA_QK)
    pos = pos_ref[0].astype(F32)
    quarter = tb // ROPE_GROUPS
    ang = pos[0:quarter] * freq_ref[0:1, :]
    for g in range(1, ROPE_GROUPS):
        ang = ang + pos[g * quarter:(g + 1) * quarter] * freq_ref[g:g + 1, :]
    cos_p = jnp.cos(ang)
    sin_p = jnp.sin(ang)
    unpack = lambda t: jnp.concatenate(
        [pltpu.roll(t, (MLA_NOPE - MLA_ROPE * g) % LANES, axis=1) for g in range(ROPE_GROUPS)],
        axis=0)
    cosv = unpack(cos_p)
    sinv = jnp.where(in_rope, unpack(sin_p), 0.0)

    scale = (MLA_QK ** -0.5) * LOG2E
    qn = _rms(proj[:, :MLA_Q_LORA], qg_ref[...])
    qa = jnp.dot(qn.astype(BF16), wq_ref[...], preferred_element_type=F32)
    cq = jnp.where(lane < MLA_NOPE, scale, jnp.where(in_rope, cosv * scale, 0.0))
    sq = sinv * scale
    cq_t = jnp.concatenate([cq] * MLA_HEADS, axis=1)
    sq_t = jnp.concatenate([sq] * MLA_HEADS, axis=1)
    width = MLA_HEADS * HEAD_SLOT
    q = qa * cq_t + pltpu.roll(qa, width - MLA_ROPE, axis=1) * sq_t
    q_ref[0] = q.astype(BF16)

    kvn = _rms(proj[:, MLA_Q_LORA:MLA_Q_LORA + MLA_KV_LORA], kvg_ref[...])
    kva = jnp.dot(kvn.astype(BF16), wkv_ref[...], preferred_element_type=F32)
    misc = proj[:, MLA_Q_LORA + MLA_KV_LORA:MLA_Q_LORA + MLA_KV_LORA + MISC_W]
    kp = misc * jnp.where(in_rope, cosv, 0.0) + pltpu.roll(misc, MISC_W - MLA_ROPE, axis=1) * sinv
    k = kva[:, :width] + jnp.concatenate([kp] * MLA_HEADS, axis=1)
    k_ref[0] = k.astype(BF16)
    v_ref[0] = kva[:, width:].astype(BF16)

    z = misc + dtb_ref[...]
    softplus = jnp.maximum(z, 0.0) + jnp.log(1.0 + jnp.exp(-jnp.abs(z)))
    g = -jnp.exp(alog_ref[...]) * softplus
    gb_ref[0] = jnp.where(lane < GDN_HEADS, g, _sigmoid(misc))

    gdn_ref[0] = proj[:, MLA_Q_LORA + MLA_KV_LORA + MISC_W:].astype(BF16)


def _in_projection(x, positions, sc1, sh1, norm1_g, w1, q_norm_g, wq, kv_norm_g, wkv,
                   freq, alog, dtb, tb):
    b, s, d = x.shape
    hw = MLA_HEADS * HEAD_SLOT
    const = lambda shape: pl.BlockSpec(shape, lambda i, j: (0,) * len(shape))
    tok = lambda w: pl.BlockSpec((1, tb, w), lambda i, j: (i, j, 0))
    per_b = pl.BlockSpec((1, 1, d), lambda i, j: (i, 0, 0))
    return pl.pallas_call(
        _inproj_kernel,
        out_shape=(jax.ShapeDtypeStruct((b, s, hw), BF16),
                   jax.ShapeDtypeStruct((b, s, hw), BF16),
                   jax.ShapeDtypeStruct((b, s, MLA_HEADS * MLA_V), BF16),
                   jax.ShapeDtypeStruct((b, s, GDN_HEADS * GDN_SLAB), BF16),
                   jax.ShapeDtypeStruct((b, s, MISC_W), F32)),
        grid=(b, s // tb),
        in_specs=[tok(d), tok(1), per_b, per_b, const((1, d)), const(w1.shape),
                  const((1, MLA_Q_LORA)), const(wq.shape), const((1, MLA_KV_LORA)),
                  const(wkv.shape), const((ROPE_GROUPS, LANES)), const((1, LANES)),
                  const((1, LANES))],
        out_specs=(tok(hw), tok(hw), tok(MLA_HEADS * MLA_V), tok(GDN_HEADS * GDN_SLAB),
                   tok(MISC_W)),
        compiler_params=_cparams(("parallel", "parallel")),
        name="in_projection",
    )(x, positions.reshape(b, s, 1), sc1, sh1, norm1_g.reshape(1, d), w1,
      q_norm_g.reshape(1, -1), wq, kv_norm_g.reshape(1, -1), wkv, freq, alog, dtb)


def _attn_kernel(q_ref, k_ref, v_ref, o_ref, s_ref, p_ref, m_ref, l_ref, acc_ref, *, tq, nh):
    qi = pl.program_id(2)
    m_ref[...] = jnp.full(m_ref.shape, -jnp.inf, F32)
    l_ref[...] = jnp.zeros(l_ref.shape, F32)
    acc_ref[...] = jnp.zeros(acc_ref.shape, F32)

    def step(off, masked):
        segs = [(0, tq // 2, tq // 2), (tq // 2, tq, tq)] if masked else [(0, tq, tq)]
        width = lambda r: next(nc for r0, r1, nc in segs if r0 <= r * ATT_STRIP < r1)
        for h in range(nh):
            for r0, r1, nc in segs:
                kj = k_ref[0, pl.ds(off, nc), h * HEAD_SLOT:(h + 1) * HEAD_SLOT]
                s_ref[h, r0:r1, 0:nc] = lax.dot_general(
                    q_ref[0, r0:r1, h * HEAD_SLOT:(h + 1) * HEAD_SLOT], kj,
                    (((1,), (1,)), ((), ())), preferred_element_type=F32)
        n_strips = tq // ATT_STRIP

        def strip(h, r):
            rows = slice(r * ATT_STRIP, (r + 1) * ATT_STRIP)
            nc = width(r)
            sc = s_ref[h, rows, 0:nc]
            if masked:
                rid = lax.broadcasted_iota(jnp.int32, (ATT_STRIP, nc), 0) + r * ATT_STRIP
                cid = lax.broadcasted_iota(jnp.int32, (ATT_STRIP, nc), 1)
                sc = jnp.where(cid <= rid, sc, -jnp.inf)
            return rows, sc

        rep = lambda col: jnp.broadcast_to(col, (col.shape[0], LANES))
        m_new = []
        for h in range(nh):
            mx = jnp.concatenate([rep(jnp.max(strip(h, r)[1], axis=-1, keepdims=True))
                                  for r in range(n_strips)], axis=0)
            m_new.append(jnp.maximum(m_ref[h], mx))
        for h in range(nh):
            v_lanes = slice((h // 2) * 2 * MLA_V, (h // 2 + 1) * 2 * MLA_V)
            sums = []
            for r in range(n_strips):
                rows, sc = strip(h, r)
                nc = width(r)
                p = jnp.exp2(sc - jnp.concatenate([m_new[h][rows]] * (nc // LANES), axis=1))
                p_ref[h, rows, 0:nc] = p.astype(BF16)
                sums.append(rep(jnp.sum(p, axis=-1, keepdims=True)))
            alpha = jnp.exp2(m_ref[h] - m_new[h])
            l_ref[h] = alpha * l_ref[h] + jnp.concatenate(sums, axis=0)
            m_ref[h] = m_new[h]
            for r0, r1, nc in segs:
                acc_ref[h, r0:r1, :] = alpha[r0:r1] * acc_ref[h, r0:r1, :] + jnp.dot(
                    p_ref[h, r0:r1, 0:nc], v_ref[0, pl.ds(off, nc), v_lanes],
                    preferred_element_type=F32)

    def body(j, carry):
        step(pl.multiple_of(j * tq, tq), False)
        return carry

    lax.fori_loop(0, qi, body, 0)
    step(pl.multiple_of(qi * tq, tq), True)
    lane = lax.broadcasted_iota(jnp.int32, (tq, 2 * MLA_V), 1)
    outs = []
    for p in range(nh // 2):
        o0 = acc_ref[2 * p] / l_ref[2 * p]
        o1 = acc_ref[2 * p + 1] / l_ref[2 * p + 1]
        outs.append(jnp.where(lane < MLA_V, o0, o1))
    o_ref[0] = jnp.concatenate(outs, axis=1).astype(o_ref.dtype)


def _attention(q, k, v, tq, nh):
    b, s, _ = q.shape
    groups = MLA_HEADS // nh
    return pl.pallas_call(
        functools.partial(_attn_kernel, tq=tq, nh=nh),
        out_shape=jax.ShapeDtypeStruct((b, s, MLA_HEADS * MLA_V), BF16),
        grid=(b, groups, s // tq),
        in_specs=[pl.BlockSpec((1, tq, nh * HEAD_SLOT), lambda i, p, j: (i, j, p)),
                  pl.BlockSpec((1, s, nh * HEAD_SLOT), lambda i, p, j: (i, 0, p)),
                  pl.BlockSpec((1, s, nh * MLA_V), lambda i, p, j: (i, 0, p))],
        out_specs=pl.BlockSpec((1, tq, nh * MLA_V), lambda i, p, j: (i, j, p)),
        scratch_shapes=[pltpu.VMEM((nh, tq, tq), F32), pltpu.VMEM((nh, tq, tq), BF16),
                        pltpu.VMEM((nh, tq, LANES), F32), pltpu.VMEM((nh, tq, LANES), F32),
                        pltpu.VMEM((nh, tq, 2 * MLA_V), F32)],
        compiler_params=_cparams(("parallel", "parallel", "arbitrary")),
        name="mla_attention",
    )(q, k, v)


def _gdn_kernel(slab_ref, gb_ref, cw_ref, ng_ref, shf_ref, ind_ref, sel_ref, o_ref, tail_ref,
                state_ref, *, ts, groups):
    si = pl.program_id(1)
    gw = GDN_GW * GDN_DK
    hist_rows = 8

    @pl.when(si == 0)
    def _():
        tail_ref[...] = jnp.zeros(tail_ref.shape, F32)
        state_ref[...] = jnp.zeros(state_ref.shape, F32)

    xs_bf = slab_ref[0]
    xs = xs_bf.astype(F32)
    width = xs.shape[1]
    sub = shf_ref.shape[1]
    conv_parts = []
    for r0 in range(0, ts, sub):
        xs_sub = xs[r0:r0 + sub]
        conv = cw_ref[CONV_WIDTH - 1:CONV_WIDTH, :] * xs_sub
        hist = jnp.zeros((hist_rows, width), F32)
        for j in range(CONV_WIDTH - 1):
            wj = cw_ref[j:j + 1, :]
            conv = conv + wj * jnp.dot(shf_ref[j], xs_bf[r0:r0 + sub],
                                       preferred_element_type=F32)
            start = hist_rows - (CONV_WIDTH - 1) + j
            hist = hist + wj * tail_ref[start:start + hist_rows, :]
        conv_parts += [conv[:hist_rows] + hist, conv[hist_rows:]]
        tail_ref[0:hist_rows, :] = xs_sub[sub - hist_rows:sub, :]
    act = _silu(jnp.concatenate(conv_parts, axis=0))

    gb = gb_ref[0]
    g1 = gb.astype(BF16)
    r1 = gb - g1.astype(F32)
    g2 = r1.astype(BF16)
    g3 = (r1 - g2.astype(F32)).astype(BF16)
    sel = sel_ref[...]
    gate_w = (jnp.dot(g1, sel, preferred_element_type=F32)
              + jnp.dot(g2, sel, preferred_element_type=F32)
              + jnp.dot(g3, sel, preferred_element_type=F32))

    n_chunks = ts // CHUNK
    row = lax.broadcasted_iota(jnp.int32, (CHUNK, gw), 0)
    col = lax.broadcasted_iota(jnp.int32, (CHUNK, gw), 1) % GDN_DK
    tri_incl = col <= row
    tri_strict = col < row
    eye = col == row
    brow = lax.broadcasted_iota(jnp.int32, (gw, gw), 0) // GDN_DK
    bcol = lax.broadcasted_iota(jnp.int32, (gw, gw), 1) // GDN_DK
    same_head = brow == bcol
    crow = lax.broadcasted_iota(jnp.int32, (ts, gw), 0) % CHUNK

    def split2(x):
        hi = x.astype(BF16)
        return hi, (x - hi.astype(F32)).astype(BF16)

    def head_sums(x):
        hi, lo = split2(x)
        return (jnp.dot(hi, ind_ref[...], preferred_element_type=F32)
                + jnp.dot(lo, ind_ref[...], preferred_element_type=F32))

    def blockdiag(x):
        return jnp.where(same_head, jnp.concatenate([x.astype(BF16)] * GDN_GW, axis=0), 0)

    def wdot(a, b):
        return jnp.dot(a.astype(BF16), blockdiag(b), preferred_element_type=F32)

    prep = []
    for gi in range(groups):
        base = gi * GDN_GW * GDN_SLAB
        q_raw = act[:, base:base + gw]
        k_raw = act[:, base + gw:base + 2 * gw]
        v_all = act[:, base + 2 * gw:base + 3 * gw]
        z_all = xs[:, base + 3 * gw:base + 4 * gw]
        q_all = q_raw * (lax.rsqrt(head_sums(q_raw * q_raw) + EPS) * (GDN_DK ** -0.5))
        k_all = k_raw * lax.rsqrt(head_sums(k_raw * k_raw) + EPS)
        g_w = gate_w[:, gi * gw:(gi + 1) * gw]
        b_w = gate_w[:, (groups + gi) * gw:(groups + gi + 1) * gw]
        gc_w = g_w
        shift = 1
        while shift < CHUNK:
            rolled = pltpu.roll(gc_w, shift, axis=0)
            gc_w = gc_w + jnp.where(crow >= shift, rolled, 0.0)
            shift *= 2
        prep.append((q_all, k_all, v_all, z_all, b_w, gc_w, jnp.exp(gc_w)))

    units = [(gi, c) for c in range(n_chunks) for gi in range(groups)]
    a_low, qk, yu0, yw0, qd, kt, cd = ({} for _ in range(7))
    for u in units:
        gi, c = u
        q_all, k_all, v_all, _, b_w, gc_w, eg_w = prep[gi]
        sl = slice(c * CHUNK, (c + 1) * CHUNK)
        qc, kc, vc, bc, gcum, eg = q_all[sl], k_all[sl], v_all[sl], b_w[sl], gc_w[sl], eg_w[sl]
        grow = jnp.sum(jnp.where(eye, gcum, 0.0), axis=0, keepdims=True)
        diff = gcum - grow
        decay = jnp.where(tri_incl, jnp.exp(jnp.where(tri_incl, diff, 0.0)), 0.0)
        k_beta = kc * bc
        kq = jnp.concatenate([k_beta, qc], axis=0).astype(BF16)
        kk = lax.dot_general(kq, blockdiag(kc), (((1,), (1,)), ((), ())),
                             preferred_element_type=F32)
        a_low[u] = jnp.where(tri_strict, kk[:CHUNK] * decay, 0.0)
        qk[u] = jnp.where(tri_incl, kk[CHUNK:] * decay, 0.0)
        yu0[u] = vc * bc
        yw0[u] = k_beta * eg
        qd[u] = qc * eg
        glast = gcum[CHUNK - 1:CHUNK, :]
        kt[u] = kc * jnp.exp(glast - gcum)
        cd[u] = eg[CHUNK - 1:CHUNK, :]

    blk = 2
    tinv = {u: jnp.where(eye, 1.0, 0.0) - jnp.where((row // blk) == (col // blk), a_low[u], 0.0)
            for u in units}
    while blk < CHUNK:
        in_big = (row // (2 * blk)) == (col // (2 * blk))
        off_mask = in_big & ((row // blk) != (col // blk))
        left = {u: wdot(tinv[u], jnp.where(off_mask, a_low[u], 0.0)) for u in units}
        tinv = {u: tinv[u] - wdot(left[u], tinv[u]) for u in units}
        blk *= 2

    yu = {u: wdot(tinv[u], yu0[u]) for u in units}
    yw = {u: wdot(tinv[u], yw0[u]) for u in units}
    tdot = lambda a, b: jnp.where(same_head, lax.dot_general(
        a.astype(BF16), b.astype(BF16), (((0,), (0,)), ((), ())),
        preferred_element_type=F32), 0.0)
    q_bd = {u: tdot(kt[u], yu[u]) for u in units}
    p_bd = {u: tdot(kt[u], yw[u]) for u in units}
    qu = {u: wdot(qk[u], yu[u]) for u in units}
    qw = {u: wdot(qk[u], yw[u]) for u in units}

    outs = [[] for _ in range(groups)]
    states = [state_ref[gi] for gi in range(groups)]
    for c in range(n_chunks):
        for gi in range(groups):
            u = (gi, c)
            sb = states[gi].astype(BF16)
            r_mat = qd[u] - qw[u]
            outs[gi].append(jnp.dot(r_mat.astype(BF16), sb, preferred_element_type=F32) + qu[u])
            states[gi] = (states[gi] * cd[u]
                          - jnp.dot(p_bd[u].astype(BF16), sb, preferred_element_type=F32)
                          + q_bd[u])
    finals = []
    for gi in range(groups):
        state_ref[gi] = states[gi]
        o_all = jnp.concatenate(outs[gi], axis=0)
        inv = lax.rsqrt(head_sums(o_all * o_all) * (1.0 / GDN_DV) + EPS)
        finals.append(o_all * inv * ng_ref[...] * _silu(prep[gi][3]))
    o_ref[0] = jnp.concatenate(finals, axis=1).astype(o_ref.dtype)


def _gdn(slab, gb, conv_slab, gdn_norm_g, ts):
    b, s, width = slab.shape
    groups = GDN_HEADS // GDN_GW
    gw = GDN_GW * GDN_DK
    sub = min(ts, CONV_BLOCK)
    r = jnp.arange(sub)
    shifts = jnp.stack([(r[:, None] - r[None, :]) == (CONV_WIDTH - 1 - j)
                        for j in range(CONV_WIDTH - 1)]).astype(BF16)
    li = jnp.arange(gw)
    ind = (li[:, None] // GDN_DK == li[None, :] // GDN_DK).astype(BF16)
    src = jnp.arange(LANES)[:, None]
    dst = jnp.arange(2 * GDN_HEADS * GDN_DK)[None, :]
    sel = (src == dst // GDN_DK).astype(BF16)
    return pl.pallas_call(
        functools.partial(_gdn_kernel, ts=ts, groups=groups),
        out_shape=jax.ShapeDtypeStruct((b, s, GDN_HEADS * GDN_DV), BF16),
        grid=(b, s // ts),
        in_specs=[pl.BlockSpec((1, ts, width), lambda i, j: (i, j, 0)),
                  pl.BlockSpec((1, ts, MISC_W), lambda i, j: (i, j, 0)),
                  pl.BlockSpec((CONV_WIDTH, width), lambda i, j: (0, 0)),
                  pl.BlockSpec((1, gw), lambda i, j: (0, 0)),
                  pl.BlockSpec((CONV_WIDTH - 1, sub, sub), lambda i, j: (0, 0, 0)),
                  pl.BlockSpec((gw, gw), lambda i, j: (0, 0)),
                  pl.BlockSpec((LANES, 2 * GDN_HEADS * GDN_DK), lambda i, j: (0, 0))],
        out_specs=pl.BlockSpec((1, ts, GDN_HEADS * GDN_DV), lambda i, j: (i, j, 0)),
        scratch_shapes=[pltpu.VMEM((16, width), F32),
                        pltpu.VMEM((groups, gw, gw), F32)],
        compiler_params=_cparams(("parallel", "arbitrary")),
        name="gated_deltanet",
    )(slab, gb, conv_slab, jnp.tile(gdn_norm_g, GDN_GW).reshape(1, gw), shifts, ind, sel)


def _outproj_kernel(x_ref, ao_ref, go_ref, gt_ref, sc_ref, sh_ref, mg_ref, wo_ref, g2_ref,
                    rw_ref, rb_ref, x1_ref, h2_ref, route_ref, rt_ref, cnt_ref, carry_ref, *, tb):
    first = (pl.program_id(0) == 0) & (pl.program_id(1) == 0)

    @pl.when(first)
    def _():
        carry_ref[...] = jnp.zeros(carry_ref.shape, F32)

    mla = _rms(ao_ref[0].astype(F32), mg_ref[...])
    cat = jnp.concatenate([mla.astype(BF16), go_ref[0]], axis=1)
    mix = jnp.dot(cat, wo_ref[...], preferred_element_type=F32)
    x1 = x_ref[0] + gt_ref[0] * mix
    x1_ref[0] = x1
    h2 = _rms(x1, g2_ref[...]) * (1.0 + sc_ref[0]) + sh_ref[0]
    h2_ref[0] = _pack_halves(h2)

    h_hi = h2.astype(BF16)
    h_lo = (h2 - h_hi.astype(F32)).astype(BF16)
    main = jnp.dot(h_hi, rw_ref[...], preferred_element_type=F32)
    logits = (main[:, :LANES] + main[:, LANES:]
              + jnp.dot(h_lo, rw_ref[:, :LANES], preferred_element_type=F32) + rb_ref[...])
    lane = lax.broadcasted_iota(jnp.int32, (tb, LANES), 1).astype(F32)
    work = jnp.where(lane < N_EXPERTS, logits, -jnp.inf)
    vals, idxs = [], []
    onehot = jnp.zeros((tb, LANES), F32)
    for _ in range(TOP_K):
        mx = jnp.max(work, axis=-1, keepdims=True)
        ix = jnp.min(jnp.where(work == mx, lane, float(LANES)), axis=-1, keepdims=True)
        sel = lane == ix
        onehot = jnp.where(sel, 1.0, onehot)
        work = jnp.where(sel, -jnp.inf, work)
        vals.append(mx)
        idxs.append(ix)
    exps = [jnp.exp(v - vals[0]) for v in vals]
    den = exps[0] + exps[1] + exps[2] + exps[3]

    r = lax.broadcasted_iota(jnp.int32, (tb, tb), 0)
    c = lax.broadcasted_iota(jnp.int32, (tb, tb), 1)
    tri = jnp.where(c < r, 1.0, 0.0).astype(BF16)
    before = jnp.dot(tri, onehot.astype(BF16), preferred_element_type=F32) + carry_ref[...]
    route = jnp.zeros((tb, LANES), F32)
    for kk in range(TOP_K):
        rank = jnp.sum(jnp.where(lane == idxs[kk], before, 0.0), axis=-1, keepdims=True)
        route = jnp.where(lane == kk, idxs[kk], route)
        route = jnp.where(lane == TOP_K + kk, rank, route)
        route = jnp.where(lane == 2 * TOP_K + kk, exps[kk] / den, route)
    route_ref[0] = route
    rt_ref[0] = route.T[:ROUTE_ROWS, :]
    total = carry_ref[...] + jnp.sum(onehot, axis=0, keepdims=True)
    carry_ref[...] = total
    cnt_ref[...] = total


def _out_projection(x, attn_o, gdn_o, gt1, sc2, sh2, mla_out_g, w_out, norm2_g, rw, rb, tb, b0, nb):
    _, s, d = x.shape
    const = lambda shape: pl.BlockSpec(shape, lambda i, j: (0,) * len(shape))
    tok_in = lambda w: pl.BlockSpec((1, tb, w), lambda i, j: (i + b0, j, 0))
    tok_out = lambda w: pl.BlockSpec((1, tb, w), lambda i, j: (i, j, 0))
    per_b = pl.BlockSpec((1, 1, d), lambda i, j: (i + b0, 0, 0))
    half = attn_o.shape[-1]
    return pl.pallas_call(
        functools.partial(_outproj_kernel, tb=tb),
        out_shape=(jax.ShapeDtypeStruct((nb, s, d), F32),
                   jax.ShapeDtypeStruct((nb, s, d // 2), F32),
                   jax.ShapeDtypeStruct((nb, s, LANES), F32),
                   jax.ShapeDtypeStruct((nb, ROUTE_ROWS, s), F32),
                   jax.ShapeDtypeStruct((1, LANES), F32)),
        grid=(nb, s // tb),
        in_specs=[tok_in(d), tok_in(half), tok_in(half), per_b, per_b, per_b, const((1, half)),
                  const(w_out.shape), const((1, d)), const(rw.shape), const((1, LANES))],
        out_specs=(tok_out(d), tok_out(d // 2), tok_out(LANES),
                   pl.BlockSpec((1, ROUTE_ROWS, tb), lambda i, j: (i, 0, j)), const((1, LANES))),
        scratch_shapes=[pltpu.VMEM((1, LANES), F32)],
        compiler_params=_cparams(("arbitrary", "arbitrary")),
        name="out_projection_router",
    )(x, attn_o, gdn_o, gt1, sc2, sh2, mla_out_g.reshape(1, half), w_out,
      norm2_g.reshape(1, d), rw, rb)


def _sc_mesh():
    return plsc.VectorSubcoreMesh(core_axis_name="c", subcore_axis_name="s",
                                  num_cores=SC_CORES, num_subcores=SC_SUBCORES)


def _sc_worker():
    return lax.axis_index("s") * SC_CORES + lax.axis_index("c")


def _dispatch_rows(h2, dest_km, rows):
    t, d = h2.shape
    per_worker = t // SC_WORKERS
    n_chunks = per_worker // SC_CHUNK

    @functools.partial(
        pl.kernel, out_type=jax.ShapeDtypeStruct((rows, d), h2.dtype), mesh=_sc_mesh(),
        scratch_types=[pltpu.VMEM((SC_CHUNK,), jnp.int32), pltpu.VMEM((SC_CHUNK, d), h2.dtype)],
        name="moe_dispatch")
    def run(h2_hbm, dest_hbm, xs_hbm, idx_v, rows_v):
        base_w = _sc_worker() * per_worker

        @pl.loop(0, n_chunks)
        def _(ci):
            base = pl.multiple_of(base_w + ci * SC_CHUNK, SC_CHUNK)
            pltpu.sync_copy(h2_hbm.at[pl.ds(base, SC_CHUNK)], rows_v)
            for kk in range(TOP_K):
                pltpu.sync_copy(dest_hbm.at[kk, pl.ds(base, SC_CHUNK)], idx_v)
                pltpu.sync_copy(rows_v, xs_hbm.at[idx_v])

    return run(h2, dest_km)


def _collect_rows(y_rows, dest_km):
    _, d = y_rows.shape
    t = dest_km.shape[1]
    per_worker = t // SC_WORKERS
    n_chunks = per_worker // SC_CHUNK

    @functools.partial(
        pl.kernel, out_type=jax.ShapeDtypeStruct((TOP_K, t, d), y_rows.dtype), mesh=_sc_mesh(),
        scratch_types=[pltpu.VMEM((SC_CHUNK,), jnp.int32), pltpu.VMEM((SC_CHUNK, d), y_rows.dtype)],
        name="moe_collect")
    def run(y_hbm, dest_hbm, out_hbm, idx_v, rows_v):
        base_w = _sc_worker() * per_worker

        @pl.loop(0, n_chunks)
        def _(ci):
            base = pl.multiple_of(base_w + ci * SC_CHUNK, SC_CHUNK)
            for kk in range(TOP_K):
                pltpu.sync_copy(dest_hbm.at[kk, pl.ds(base, SC_CHUNK)], idx_v)
                pltpu.sync_copy(y_hbm.at[idx_v], rows_v)
                pltpu.sync_copy(rows_v, out_hbm.at[kk, pl.ds(base, SC_CHUNK)])

    return run(y_rows, dest_km)


def _experts_kernel(be_ref, nu_ref, nxt_ref, slot_ref, xs_ref, wgu_hbm, bgu_ref, wd_hbm, bd_ref,
                    y_ref, wgu_f32, wd_f32, wgu_bf, wd_bf, sem):
    i = pl.program_id(0)
    e = be_ref[i]
    new_expert = (i == 0) | (e != be_ref[jnp.maximum(i - 1, 0)])

    def fetch(expert, slot):
        return (pltpu.make_async_copy(wgu_hbm.at[expert], wgu_f32.at[slot], sem.at[0, slot]),
                pltpu.make_async_copy(wd_hbm.at[expert], wd_f32.at[slot], sem.at[1, slot]))

    @pl.when(i == 0)
    def _():
        for cp in fetch(e, slot_ref[e]):
            cp.start()

    @pl.when(new_expert)
    def _():
        slot = slot_ref[e]
        for cp in fetch(e, slot):
            cp.wait()
        wgu_bf[...] = wgu_f32[slot].astype(BF16)
        wd_bf[...] = wd_f32[slot].astype(BF16)
        nxt = nxt_ref[e]

        @pl.when(nxt >= 0)
        def _():
            for cp in fetch(nxt, 1 - slot):
                cp.start()

    @pl.when(i < nu_ref[0])
    def _():
        x_lo, x_hi = _unpack_halves(xs_ref[...])
        half = x_lo.shape[-1]
        gu = (jnp.dot(x_lo.astype(BF16), wgu_bf[:half, :], preferred_element_type=F32)
              + jnp.dot(x_hi.astype(BF16), wgu_bf[half:, :], preferred_element_type=F32)
              + bgu_ref[0])
        gate = jnp.minimum(gu[:, :D_EXPERT], SWIGLU_LIMIT)
        up = jnp.clip(gu[:, D_EXPERT:], -SWIGLU_LIMIT, SWIGLU_LIMIT)
        act = (up + 1.0) * (gate * _sigmoid(SWIGLU_ALPHA * gate))
        y = jnp.dot(act.astype(BF16), wd_bf[...], preferred_element_type=F32) + bd_ref[0]
        y_ref[...] = _pack_halves(y)


def _experts(blk_e, n_used, next_e, slot_e, xs, wgu, bgu, wd, bd, bm):
    rows, half = xs.shape
    d = 2 * half
    n_blocks = rows // bm
    row_map = lambda i, be, nu, nx, sl: (jnp.maximum(jnp.minimum(i, nu[0] - 1), 0), 0)
    exp_map = lambda i, be, nu, nx, sl: (be[i], 0, 0)
    grid_spec = pltpu.PrefetchScalarGridSpec(
        num_scalar_prefetch=4,
        grid=(n_blocks,),
        in_specs=[pl.BlockSpec((bm, half), row_map),
                  pl.BlockSpec(memory_space=pl.ANY),
                  pl.BlockSpec((1, 1, 2 * D_EXPERT), exp_map),
                  pl.BlockSpec(memory_space=pl.ANY),
                  pl.BlockSpec((1, 1, d), exp_map)],
        out_specs=pl.BlockSpec((bm, half), row_map),
        scratch_shapes=[pltpu.VMEM((2, d, 2 * D_EXPERT), F32),
                        pltpu.VMEM((2, D_EXPERT, d), F32),
                        pltpu.VMEM((d, 2 * D_EXPERT), BF16),
                        pltpu.VMEM((D_EXPERT, d), BF16),
                        pltpu.SemaphoreType.DMA((2, 2))])
    return pl.pallas_call(
        _experts_kernel,
        out_shape=jax.ShapeDtypeStruct((rows, half), F32),
        grid_spec=grid_spec,
        compiler_params=_cparams(("arbitrary",)),
        name="expert_mlp",
    )(blk_e, n_used, next_e, slot_e, xs, wgu, bgu, wd, bd)


def _final_kernel(x1_ref, yk_ref, route_ref, gt_ref, fg_ref, *rest):
    o_ref = rest[-1]
    route = route_ref[0]
    ffn = jnp.zeros(x1_ref.shape[1:], F32)
    for kk in range(TOP_K):
        wk = route[:, 2 * TOP_K + kk:2 * TOP_K + kk + 1]
        y_lo, y_hi = _unpack_halves(yk_ref[kk, 0])
        ffn = ffn + wk * jnp.concatenate([y_lo, y_hi], axis=1)
    x2 = x1_ref[0] + gt_ref[0] * ffn
    o_ref[0] = _rms(x2, fg_ref[...])


def _final(x1, yk, route, gt2, final_g, tb, b0, b_total, prev):
    nb, s, d = x1.shape
    in_specs = [pl.BlockSpec((1, tb, d), lambda i, j: (i, j, 0)),
                pl.BlockSpec((TOP_K, 1, tb, d // 2), lambda i, j: (0, i, j, 0)),
                pl.BlockSpec((1, tb, LANES), lambda i, j: (i, j, 0)),
                pl.BlockSpec((1, 1, d), lambda i, j: (i + b0, 0, 0)),
                pl.BlockSpec((1, d), lambda i, j: (0, 0))]
    args = [x1, yk, route, gt2, final_g.reshape(1, d)]
    aliases = {}
    if prev is not None:
        in_specs.append(pl.BlockSpec(memory_space=pl.ANY))
        args.append(prev)
        aliases = {len(args) - 1: 0}
    return pl.pallas_call(
        _final_kernel,
        out_shape=jax.ShapeDtypeStruct((b_total, s, d), F32),
        grid=(nb, s // tb),
        in_specs=in_specs,
        out_specs=pl.BlockSpec((1, tb, d), lambda i, j: (i + b0, j, 0)),
        input_output_aliases=aliases,
        compiler_params=_cparams(("parallel", "parallel")),
        name="combine_final_norm",
    )(*args)


def _rot_cols(w):
    half = MLA_ROPE // 2
    return jnp.concatenate([-w[..., half:], w[..., :half]], axis=-1)


def _prep_weights(w_in, w_q_b, w_kv_b, conv_w, A_log, dt_bias):
    d = w_in.shape[0]
    cuts = [MLA_Q_LORA, MLA_KV_LORA, MLA_ROPE, GDN_HEADS * GDN_DK, GDN_HEADS * GDN_DK,
            GDN_HEADS * GDN_DV, GDN_HEADS * GDN_DV, GDN_HEADS, GDN_HEADS]
    offs = [0]
    for cw in cuts:
        offs.append(offs[-1] + cw)
    part = lambda i: w_in[:, offs[i]:offs[i + 1]]
    k_pe = part(2)
    misc = jnp.concatenate(
        [part(7), part(8), jnp.zeros((d, MLA_NOPE - 2 * GDN_HEADS), w_in.dtype), k_pe,
         _rot_cols(k_pe)], axis=1)
    heads = lambda w, n: w.reshape(w.shape[0], GDN_HEADS // GDN_GW, GDN_GW * n)
    slab = jnp.concatenate([heads(part(3), GDN_DK), heads(part(4), GDN_DK),
                            heads(part(5), GDN_DV), heads(part(6), GDN_DV)], axis=-1)
    w1 = jnp.concatenate([part(0), part(1), misc, slab.reshape(d, GDN_HEADS * GDN_SLAB)],
                         axis=1).astype(BF16)

    wq3 = w_q_b.reshape(MLA_Q_LORA, MLA_HEADS, MLA_QK)
    pe = wq3[..., MLA_NOPE:]
    wq = jnp.concatenate([wq3[..., :MLA_NOPE], pe, _rot_cols(pe)], axis=-1)
    wq = wq.reshape(MLA_Q_LORA, MLA_HEADS * HEAD_SLOT).astype(BF16)

    wkv3 = w_kv_b.reshape(MLA_KV_LORA, MLA_HEADS, MLA_NOPE + MLA_V)
    wk = jnp.concatenate([wkv3[..., :MLA_NOPE],
                          jnp.zeros((MLA_KV_LORA, MLA_HEADS, HEAD_SLOT - MLA_NOPE), w_kv_b.dtype)],
                         axis=-1).reshape(MLA_KV_LORA, MLA_HEADS * HEAD_SLOT)
    wv = wkv3[..., MLA_NOPE:].reshape(MLA_KV_LORA, MLA_HEADS * MLA_V)
    wkv = jnp.concatenate([wk, wv], axis=1).astype(BF16)

    nk = GDN_HEADS * GDN_DK
    cheads = lambda w, n: w.reshape(CONV_WIDTH, GDN_HEADS // GDN_GW, GDN_GW * n)
    conv_slab = jnp.concatenate(
        [cheads(conv_w[:, :nk], GDN_DK), cheads(conv_w[:, nk:2 * nk], GDN_DK),
         cheads(conv_w[:, 2 * nk:], GDN_DV),
         jnp.zeros((CONV_WIDTH, GDN_HEADS // GDN_GW, GDN_GW * GDN_DV), conv_w.dtype)],
        axis=-1).reshape(CONV_WIDTH, GDN_HEADS * GDN_SLAB)

    half = MLA_ROPE // 2
    inv_freq = ROPE_THETA ** (-jnp.arange(half, dtype=F32) / half)
    lane = jnp.arange(LANES)
    freq = jnp.where(lane[None, :] // MLA_ROPE == jnp.arange(ROPE_GROUPS)[:, None],
                     jnp.tile(inv_freq, LANES // half)[None, :], 0.0)
    padl = lambda a: jnp.concatenate([a.astype(F32), jnp.zeros((LANES - a.shape[0],), F32)]).reshape(1, LANES)
    return w1, wq, wkv, conv_slab, freq, padl(A_log), padl(dt_bias)


def _layer(x, mod, positions, w_in, q_norm_g, w_q_b, kv_norm_g, w_kv_b, mla_out_g, conv_w,
           A_log, dt_bias, gdn_norm_g, w_out, norm1_g, norm2_g, router_w, router_b, w_gate_up,
           b_gate_up, w_down, b_down, final_g):
    b, s, d = x.shape
    t = b * s
    sh1, sc1, gt1, sh2, sc2, gt2 = [m.reshape(b, 1, d) for m in jnp.split(mod, 6, axis=-1)]
    w1, wq, wkv, conv_slab, freq, alog, dtb = _prep_weights(w_in, w_q_b, w_kv_b, conv_w, A_log,
                                                            dt_bias)
    tb = min(256, s)
    q, k, v, slab, gb = _in_projection(x, positions, sc1, sh1, norm1_g, w1, q_norm_g, wq,
                                       kv_norm_g, wkv, freq, alog, dtb, min(512, s))
    attn_o = _attention(q, k, v, min(512, s), 8)
    gdn_o = _gdn(slab, gb, conv_slab, gdn_norm_g, min(512, s))

    rw = jnp.concatenate([router_w, jnp.zeros((d, LANES - N_EXPERTS), router_w.dtype)], axis=1)
    rw_hi = rw.astype(BF16)
    rw = jnp.concatenate([rw_hi, (rw - rw_hi.astype(F32)).astype(BF16)], axis=1)
    rb = jnp.concatenate([router_b, jnp.zeros((LANES - N_EXPERTS,), router_b.dtype)]).reshape(1, LANES)
    w_out_bf = w_out.astype(BF16)
    bgu = b_gate_up.reshape(N_EXPERTS, 1, -1)
    bd = b_down.reshape(N_EXPERTS, 1, -1)
    n_parts = MOE_PARTS if b % MOE_PARTS == 0 else 1
    nb = b // n_parts
    out = None
    for part in range(n_parts):
        b0 = part * nb
        x1, h2, route, route_t, counts = _out_projection(x, attn_o, gdn_o, gt1, sc2, sh2,
                                                         mla_out_g, w_out_bf, norm2_g, rw, rb,
                                                         min(512, s), b0, nb)
        yk = _moe(h2, route_t, counts, w_gate_up, bgu, w_down, bd)
        out = _final(x1, yk, route, gt2, final_g, tb, b0, b, out)
    return out


def _moe(h2, route_t, counts, w_gate_up, bgu, w_down, bd):
    nb, s, half = h2.shape
    t = nb * s
    bm = EXPERT_ROWS
    slot_major = lambda r0: jnp.swapaxes(route_t[:, r0:r0 + TOP_K, :], 0, 1).reshape(TOP_K, t)
    idx = slot_major(0).astype(jnp.int32)
    rank = slot_major(TOP_K).astype(jnp.int32)
    cnt = counts[0, :N_EXPERTS].astype(jnp.int32)
    padded = ((cnt + bm - 1) // bm) * bm
    pend = jnp.cumsum(padded)
    pstart = pend - padded
    base = jnp.zeros_like(idx)
    for e in range(N_EXPERTS):
        base = jnp.where(idx == e, pstart[e], base)
    dest_km = base + rank
    n_blocks = (t * TOP_K + N_EXPERTS * (bm - 1) + bm - 1) // bm
    rows = n_blocks * bm
    n_used = (pend[-1] // bm).astype(jnp.int32).reshape(1)
    blk_start = jnp.arange(n_blocks, dtype=jnp.int32) * bm
    blk_e = jnp.minimum(jnp.sum(blk_start[:, None] >= pend[None, :], axis=1), N_EXPERTS - 1)
    last_e = blk_e[jnp.maximum(n_used[0] - 1, 0)]
    blk_e = jnp.where(jnp.arange(n_blocks) < n_used[0], blk_e, last_e).astype(jnp.int32)

    present = padded > 0
    eidx = jnp.arange(N_EXPERTS, dtype=jnp.int32)
    later = present[None, :] & (eidx[None, :] > eidx[:, None])
    next_e = jnp.where(jnp.any(later, axis=1), jnp.argmax(later, axis=1), -1).astype(jnp.int32)
    slot_e = ((jnp.cumsum(present.astype(jnp.int32)) - 1) % 2).astype(jnp.int32)

    xs = _dispatch_rows(h2.reshape(t, half), dest_km, rows)
    y_rows = _experts(blk_e, n_used, next_e, slot_e, xs, w_gate_up, bgu, w_down, bd, bm)
    return _collect_rows(y_rows, dest_km).reshape(TOP_K, nb, s, half)


def kernel(x, c, positions, ada_w, ada_b, norm1_g, w_in, q_norm_g, w_q_b, kv_norm_g, w_kv_b, mla_out_g, conv_w, A_log, dt_bias, gdn_norm_g, w_out, norm2_g, router_w, router_b, w_gate_up, b_gate_up, w_down, b_down, final_g):
    depth = ada_w.shape[0]
    assert depth == 1
    l = 0
    mod = _modulation(c, ada_w[l], ada_b[l])
    return _layer(x, mod, positions, w_in[l], q_norm_g[l], w_q_b[l], kv_norm_g[l],
                  w_kv_b[l], mla_out_g[l], conv_w[l], A_log[l], dt_bias[l], gdn_norm_g[l],
                  w_out[l], norm1_g[l], norm2_g[l], router_w[l], router_b[l],
                  w_gate_up[l], b_gate_up[l], w_down[l], b_down[l], final_g)
```

```python
import functools

import jax
import jax.numpy as jnp
from jax import lax
from jax.experimental import pallas as pl
from jax.experimental.pallas import tpu as pltpu
from jax.experimental.pallas import tpu_sc as plsc

F32 = jnp.float32
BF16 = jnp.bfloat16

D_MODEL = 1024
EPS = 1e-6
MLA_HEADS = 8
MLA_NOPE = 64
MLA_ROPE = 32
MLA_V = 64
MLA_QK = MLA_NOPE + MLA_ROPE
MLA_Q_LORA = 384
MLA_KV_LORA = 256
ROPE_THETA = 10000.0
GDN_HEADS = 8
GDN_DK = 64
GDN_DV = 64
CONV_WIDTH = 4
CHUNK = 64
N_EXPERTS = 32
TOP_K = 4
D_EXPERT = D_MODEL
SWIGLU_ALPHA = 1.702
SWIGLU_LIMIT = 7.0

LANES = 128
HEAD_SLOT = 128
GDN_SLAB = 256
GDN_GW = 4
CONV_BLOCK = 256
MISC_W = 128
PROJ_W = MLA_Q_LORA + MLA_KV_LORA + MISC_W + GDN_HEADS * GDN_SLAB
VMEM_LIMIT = 56 * 1024 * 1024
MOE_PARTS = 2
EXPERT_ROWS = 512
ROW_QUARTERS = 4
SC_CORES = 2
SC_SUBCORES = 16
SC_WORKERS = SC_CORES * SC_SUBCORES
SC_CHUNK = 128
ROPE_GROUPS = LANES // MLA_ROPE
ROUTE_ROWS = 16
ATT_STRIP = 32
LOG2E = 1.4426950408889634


def _cparams(sem):
    return pltpu.CompilerParams(dimension_semantics=sem, vmem_limit_bytes=VMEM_LIMIT)


def _rms(x, g):
    return x * lax.rsqrt(jnp.mean(x * x, axis=-1, keepdims=True) + EPS) * g


def _sigmoid(x):
    return 1.0 / (1.0 + jnp.exp(-x))


def _silu(x):
    return x * _sigmoid(x)


def _pack_halves(x):
    w = x.shape[-1] // 2
    lo = lax.bitcast_convert_type(x[:, :w].astype(BF16).astype(F32), jnp.uint32) >> 16
    hi = lax.bitcast_convert_type(x[:, w:].astype(BF16).astype(F32), jnp.uint32) & jnp.uint32(0xFFFF0000)
    return lax.bitcast_convert_type(lo | hi, F32)


def _unpack_halves(p):
    u = lax.bitcast_convert_type(p, jnp.uint32)
    return (lax.bitcast_convert_type(u << 16, F32),
            lax.bitcast_convert_type(u & jnp.uint32(0xFFFF0000), F32))


def _mod_kernel(c_ref, w_ref, b_ref, o_ref):
    c = c_ref[...]
    o_ref[...] = jnp.dot(_silu(c), w_ref[...], preferred_element_type=F32,
                         precision=lax.Precision.HIGHEST) + b_ref[...]


def _modulation(c, ada_w, ada_b):
    b, d = c.shape
    n = ada_w.shape[1]
    return pl.pallas_call(
        _mod_kernel,
        out_shape=jax.ShapeDtypeStruct((b, n), F32),
        grid=(n // d,),
        in_specs=[pl.BlockSpec((b, d), lambda j: (0, 0)),
                  pl.BlockSpec((d, d), lambda j: (0, j)),
                  pl.BlockSpec((1, d), lambda j: (0, j))],
        out_specs=pl.BlockSpec((b, d), lambda j: (0, j)),
        compiler_params=_cparams(("arbitrary",)),
        name="adaln_mod",
    )(c, ada_w, ada_b.reshape(1, n))


def _inproj_kernel(x_ref, pos_ref, sc_ref, sh_ref, g1_ref, w1_ref, qg_ref, wq_ref, kvg_ref,
                   wkv_ref, freq_ref, alog_ref, dtb_ref,
                   q_ref, k_ref, v_ref, gdn_ref, gb_ref):
    x = x_ref[0]
    h = _rms(x, g1_ref[...]) * (1.0 + sc_ref[0]) + sh_ref[0]
    proj = jnp.dot(h.astype(BF16), w1_ref[...], preferred_element_type=F32)

    tb = x.shape[0]
    lane = lax.broadcasted_iota(jnp.int32, (tb, LANES), 1)
    in_rope = (lane >= MLA_NOPE) & (lane < MLA_QK)
    pos = pos_ref[0].astype(F32)
    quarter = tb // ROPE_GROUPS
    ang = pos[0:quarter] * freq_ref[0:1, :]
    for g in range(1, ROPE_GROUPS):
        ang = ang + pos[g * quarter:(g + 1) * quarter] * freq_ref[g:g + 1, :]
    cos_p = jnp.cos(ang)
    sin_p = jnp.sin(ang)
    unpack = lambda t: jnp.concatenate(
        [pltpu.roll(t, (MLA_NOPE - MLA_ROPE * g) % LANES, axis=1) for g in range(ROPE_GROUPS)],
        axis=0)
    cosv = unpack(cos_p)
    sinv = jnp.where(in_rope, unpack(sin_p), 0.0)

    scale = (MLA_QK ** -0.5) * LOG2E
    qn = _rms(proj[:, :MLA_Q_LORA], qg_ref[...])
    qa = jnp.dot(qn.astype(BF16), wq_ref[...], preferred_element_type=F32)
    cq = jnp.where(lane < MLA_NOPE, scale, jnp.where(in_rope, cosv * scale, 0.0))
    sq = sinv * scale
    cq_t = jnp.concatenate([cq] * MLA_HEADS, axis=1)
    sq_t = jnp.concatenate([sq] * MLA_HEADS, axis=1)
    width = MLA_HEADS * HEAD_SLOT
    q = qa * cq_t + pltpu.roll(qa, width - MLA_ROPE, axis=1) * sq_t
    q_ref[0] = q.astype(BF16)

    kvn = _rms(proj[:, MLA_Q_LORA:MLA_Q_LORA + MLA_KV_LORA], kvg_ref[...])
    kva = jnp.dot(kvn.astype(BF16), wkv_ref[...], preferred_element_type=F32)
    misc = proj[:, MLA_Q_LORA + MLA_KV_LORA:MLA_Q_LORA + MLA_KV_LORA + MISC_W]
    kp = misc * jnp.where(in_rope, cosv, 0.0) + pltpu.roll(misc, MISC_W - MLA_ROPE, axis=1) * sinv
    k = kva[:, :width] + jnp.concatenate([kp] * MLA_HEADS, axis=1)
    k_ref[0] = k.astype(BF16)
    v_ref[0] = kva[:, width:].astype(BF16)

    z = misc + dtb_ref[...]
    softplus = jnp.maximum(z, 0.0) + jnp.log(1.0 + jnp.exp(-jnp.abs(z)))
    g = -jnp.exp(alog_ref[...]) * softplus
    gb_ref[0] = jnp.where(lane < GDN_HEADS, g, _sigmoid(misc))

    gdn_ref[0] = proj[:, MLA_Q_LORA + MLA_KV_LORA + MISC_W:].astype(BF16)


def _in_projection(x, positions, sc1, sh1, norm1_g, w1, q_norm_g, wq, kv_norm_g, wkv,
                   freq, alog, dtb, tb):
    b, s, d = x.shape
    hw = MLA_HEADS * HEAD_SLOT
    const = lambda shape: pl.BlockSpec(shape, lambda i, j: (0,) * len(shape))
    tok = lambda w: pl.BlockSpec((1, tb, w), lambda i, j: (i, j, 0))
    per_b = pl.BlockSpec((1, 1, d), lambda i, j: (i, 0, 0))
    return pl.pallas_call(
        _inproj_kernel,
        out_shape=(jax.ShapeDtypeStruct((b, s, hw), BF16),
                   jax.ShapeDtypeStruct((b, s, hw), BF16),
                   jax.ShapeDtypeStruct((b, s, MLA_HEADS * MLA_V), BF16),
                   jax.ShapeDtypeStruct((b, s, GDN_HEADS * GDN_SLAB), BF16),
                   jax.ShapeDtypeStruct((b, s, MISC_W), F32)),
        grid=(b, s // tb),
        in_specs=[tok(d), tok(1), per_b, per_b, const((1, d)), const(w1.shape),
                  const((1, MLA_Q_LORA)), const(wq.shape), const((1, MLA_KV_LORA)),
                  const(wkv.shape), const((ROPE_GROUPS, LANES)), const((1, LANES)),
                  const((1, LANES))],
        out_specs=(tok(hw), tok(hw), tok(MLA_HEADS * MLA_V), tok(GDN_HEADS * GDN_SLAB),
                   tok(MISC_W)),
        compiler_params=_cparams(("parallel", "parallel")),
        name="in_projection",
    )(x, positions.reshape(b, s, 1), sc1, sh1, norm1_g.reshape(1, d), w1,
      q_norm_g.reshape(1, -1), wq, kv_norm_g.reshape(1, -1), wkv, freq, alog, dtb)


def _attn_kernel(q_ref, k_ref, v_ref, o_ref, s_ref, p_ref, m_ref, l_ref, acc_ref, *, tq, nh):
    qi = pl.program_id(2)
    m_ref[...] = jnp.full(m_ref.shape, -jnp.inf, F32)
    l_ref[...] = jnp.zeros(l_ref.shape, F32)
    acc_ref[...] = jnp.zeros(acc_ref.shape, F32)

    def step(off, masked):
        segs = [(0, tq // 2, tq // 2), (tq // 2, tq, tq)] if masked else [(0, tq, tq)]
        width = lambda r: next(nc for r0, r1, nc in segs if r0 <= r * ATT_STRIP < r1)
        for h in range(nh):
            for r0, r1, nc in segs:
                kj = k_ref[0, pl.ds(off, nc), h * HEAD_SLOT:(h + 1) * HEAD_SLOT]
                s_ref[h, r0:r1, 0:nc] = lax.dot_general(
                    q_ref[0, r0:r1, h * HEAD_SLOT:(h + 1) * HEAD_SLOT], kj,
                    (((1,), (1,)), ((), ())), preferred_element_type=F32)
        n_strips = tq // ATT_STRIP

        def strip(h, r):
            rows = slice(r * ATT_STRIP, (r + 1) * ATT_STRIP)
            nc = width(r)
            sc = s_ref[h, rows, 0:nc]
            if masked:
                rid = lax.broadcasted_iota(jnp.int32, (ATT_STRIP, nc), 0) + r * ATT_STRIP
                cid = lax.broadcasted_iota(jnp.int32, (ATT_STRIP, nc), 1)
                sc = jnp.where(cid <= rid, sc, -jnp.inf)
            return rows, sc

        rep = lambda col: jnp.broadcast_to(col, (col.shape[0], LANES))
        m_new = []
        for h in range(nh):
            mx = jnp.concatenate([rep(jnp.max(strip(h, r)[1], axis=-1, keepdims=True))
                                  for r in range(n_strips)], axis=0)
            m_new.append(jnp.maximum(m_ref[h], mx))
        for h in range(nh):
            v_lanes = slice((h // 2) * 2 * MLA_V, (h // 2 + 1) * 2 * MLA_V)
            sums = []
            for r in range(n_strips):
                rows, sc = strip(h, r)
                nc = width(r)
                p = jnp.exp2(sc - jnp.concatenate([m_new[h][rows]] * (nc // LANES), axis=1))
                p_ref[h, rows, 0:nc] = p.astype(BF16)
                sums.append(rep(jnp.sum(p, axis=-1, keepdims=True)))
            alpha = jnp.exp2(m_ref[h] - m_new[h])
            l_ref[h] = alpha * l_ref[h] + jnp.concatenate(sums, axis=0)
            m_ref[h] = m_new[h]
            for r0, r1, nc in segs:
                acc_ref[h, r0:r1, :] = alpha[r0:r1] * acc_ref[h, r0:r1, :] + jnp.dot(
                    p_ref[h, r0:r1, 0:nc], v_ref[0, pl.ds(off, nc), v_lanes],
                    preferred_element_type=F32)

    def body(j, carry):
        step(pl.multiple_of(j * tq, tq), False)
        return carry

    lax.fori_loop(0, qi, body, 0)
    step(pl.multiple_of(qi * tq, tq), True)
    lane = lax.broadcasted_iota(jnp.int32, (tq, 2 * MLA_V), 1)
    outs = []
    for p in range(nh // 2):
        o0 = acc_ref[2 * p] / l_ref[2 * p]
        o1 = acc_ref[2 * p + 1] / l_ref[2 * p + 1]
        outs.append(jnp.where(lane < MLA_V, o0, o1))
    o_ref[0] = jnp.concatenate(outs, axis=1).astype(o_ref.dtype)


def _attention(q, k, v, tq, nh):
    b, s, _ = q.shape
    groups = MLA_HEADS // nh
    return pl.pallas_call(
        functools.partial(_attn_kernel, tq=tq, nh=nh),
        out_shape=jax.ShapeDtypeStruct((b, s, MLA_HEADS * MLA_V), BF16),
        grid=(b, groups, s // tq),
        in_specs=[pl.BlockSpec((1, tq, nh * HEAD_SLOT), lambda i, p, j: (i, j, p)),
                  pl.BlockSpec((1, s, nh * HEAD_SLOT), lambda i, p, j: (i, 0, p)),
                  pl.BlockSpec((1, s, nh * MLA_V), lambda i, p, j: (i, 0, p))],
        out_specs=pl.BlockSpec((1, tq, nh * MLA_V), lambda i, p, j: (i, j, p)),
        scratch_shapes=[pltpu.VMEM((nh, tq, tq), F32), pltpu.VMEM((nh, tq, tq), BF16),
                        pltpu.VMEM((nh, tq, LANES), F32), pltpu.VMEM((nh, tq, LANES), F32),
                        pltpu.VMEM((nh, tq, 2 * MLA_V), F32)],
        compiler_params=_cparams(("parallel", "parallel", "arbitrary")),
        name="mla_attention",
    )(q, k, v)


def _gdn_kernel(slab_ref, gb_ref, cw_ref, ng_ref, shf_ref, ind_ref, sel_ref, o_ref, tail_ref,
                state_ref, *, ts, groups):
    si = pl.program_id(1)
    gw = GDN_GW * GDN_DK
    hist_rows = 8

    @pl.when(si == 0)
    def _():
        tail_ref[...] = jnp.zeros(tail_ref.shape, F32)
        state_ref[...] = jnp.zeros(state_ref.shape, F32)

    xs_bf = slab_ref[0]
    xs = xs_bf.astype(F32)
    width = xs.shape[1]
    sub = shf_ref.shape[1]
    conv_parts = []
    for r0 in range(0, ts, sub):
        xs_sub = xs[r0:r0 + sub]
        conv = cw_ref[CONV_WIDTH - 1:CONV_WIDTH, :] * xs_sub
        hist = jnp.zeros((hist_rows, width), F32)
        for j in range(CONV_WIDTH - 1):
            wj = cw_ref[j:j + 1, :]
            conv = conv + wj * jnp.dot(shf_ref[j], xs_bf[r0:r0 + sub],
                                       preferred_element_type=F32)
            start = hist_rows - (CONV_WIDTH - 1) + j
            hist = hist + wj * tail_ref[start:start + hist_rows, :]
        conv_parts += [conv[:hist_rows] + hist, conv[hist_rows:]]
        tail_ref[0:hist_rows, :] = xs_sub[sub - hist_rows:sub, :]
    act = _silu(jnp.concatenate(conv_parts, axis=0))

    gb = gb_ref[0]
    g1 = gb.astype(BF16)
    r1 = gb - g1.astype(F32)
    g2 = r1.astype(BF16)
    g3 = (r1 - g2.astype(F32)).astype(BF16)
    sel = sel_ref[...]
    gate_w = (jnp.dot(g1, sel, preferred_element_type=F32)
              + jnp.dot(g2, sel, preferred_element_type=F32)
              + jnp.dot(g3, sel, preferred_element_type=F32))

    n_chunks = ts // CHUNK
    row = lax.broadcasted_iota(jnp.int32, (CHUNK, gw), 0)
    col = lax.broadcasted_iota(jnp.int32, (CHUNK, gw), 1) % GDN_DK
    tri_incl = col <= row
    tri_strict = col < row
    eye = col == row
    brow = lax.broadcasted_iota(jnp.int32, (gw, gw), 0) // GDN_DK
    bcol = lax.broadcasted_iota(jnp.int32, (gw, gw), 1) // GDN_DK
    same_head = brow == bcol
    crow = lax.broadcasted_iota(jnp.int32, (ts, gw), 0) % CHUNK

    def split2(x):
        hi = x.astype(BF16)
        return hi, (x - hi.astype(F32)).astype(BF16)

    def head_sums(x):
        hi, lo = split2(x)
        return (jnp.dot(hi, ind_ref[...], preferred_element_type=F32)
                + jnp.dot(lo, ind_ref[...], preferred_element_type=F32))

    def blockdiag(x):
        return jnp.where(same_head, jnp.concatenate([x.astype(BF16)] * GDN_GW, axis=0), 0)

    def wdot(a, b):
        return jnp.dot(a.astype(BF16), blockdiag(b), preferred_element_type=F32)

    prep = []
    for gi in range(groups):
        base = gi * GDN_GW * GDN_SLAB
        q_raw = act[:, base:base + gw]
        k_raw = act[:, base + gw:base + 2 * gw]
        v_all = act[:, base + 2 * gw:base + 3 * gw]
        z_all = xs[:, base + 3 * gw:base + 4 * gw]
        q_all = q_raw * (lax.rsqrt(head_sums(q_raw * q_raw) + EPS) * (GDN_DK ** -0.5))
        k_all = k_raw * lax.rsqrt(head_sums(k_raw * k_raw) + EPS)
        g_w = gate_w[:, gi * gw:(gi + 1) * gw]
        b_w = gate_w[:, (groups + gi) * gw:(groups + gi + 1) * gw]
        gc_w = g_w
        shift = 1
        while shift < CHUNK:
            rolled = pltpu.roll(gc_w, shift, axis=0)
            gc_w = gc_w + jnp.where(crow >= shift, rolled, 0.0)
            shift *= 2
        prep.append((q_all, k_all, v_all, z_all, b_w, gc_w, jnp.exp(gc_w)))

    units = [(gi, c) for c in range(n_chunks) for gi in range(groups)]
    a_low, qk, yu0, yw0, qd, kt, cd = ({} for _ in range(7))
    for u in units:
        gi, c = u
        q_all, k_all, v_all, _, b_w, gc_w, eg_w = prep[gi]
        sl = slice(c * CHUNK, (c + 1) * CHUNK)
        qc, kc, vc, bc, gcum, eg = q_all[sl], k_all[sl], v_all[sl], b_w[sl], gc_w[sl], eg_w[sl]
        grow = jnp.sum(jnp.where(eye, gcum, 0.0), axis=0, keepdims=True)
        diff = gcum - grow
        decay = jnp.where(tri_incl, jnp.exp(jnp.where(tri_incl, diff, 0.0)), 0.0)
        k_beta = kc * bc
        kq = jnp.concatenate([k_beta, qc], axis=0).astype(BF16)
        kk = lax.dot_general(kq, blockdiag(kc), (((1,), (1,)), ((), ())),
                             preferred_element_type=F32)
        a_low[u] = jnp.where(tri_strict, kk[:CHUNK] * decay, 0.0)
        qk[u] = jnp.where(tri_incl, kk[CHUNK:] * decay, 0.0)
        yu0[u] = vc * bc
        yw0[u] = k_beta * eg
        qd[u] = qc * eg
        glast = gcum[CHUNK - 1:CHUNK, :]
        kt[u] = kc * jnp.exp(glast - gcum)
        cd[u] = eg[CHUNK - 1:CHUNK, :]

    blk = 2
    tinv = {u: jnp.where(eye, 1.0, 0.0) - jnp.where((row // blk) == (col // blk), a_low[u], 0.0)
            for u in units}
    while blk < CHUNK:
        in_big = (row // (2 * blk)) == (col // (2 * blk))
        off_mask = in_big & ((row // blk) != (col // blk))
        left = {u: wdot(tinv[u], jnp.where(off_mask, a_low[u], 0.0)) for u in units}
        tinv = {u: tinv[u] - wdot(left[u], tinv[u]) for u in units}
        blk *= 2

    yu = {u: wdot(tinv[u], yu0[u]) for u in units}
    yw = {u: wdot(tinv[u], yw0[u]) for u in units}
    tdot = lambda a, b: jnp.where(same_head, lax.dot_general(
        a.astype(BF16), b.astype(BF16), (((0,), (0,)), ((), ())),
        preferred_element_type=F32), 0.0)
    q_bd = {u: tdot(kt[u], yu[u]) for u in units}
    p_bd = {u: tdot(kt[u], yw[u]) for u in units}
    qu = {u: wdot(qk[u], yu[u]) for u in units}
    qw = {u: wdot(qk[u], yw[u]) for u in units}

    outs = [[] for _ in range(groups)]
    states = [state_ref[gi] for gi in range(groups)]
    for c in range(n_chunks):
        for gi in range(groups):
            u = (gi, c)
            sb = states[gi].astype(BF16)
            r_mat = qd[u] - qw[u]
            outs[gi].append(jnp.dot(r_mat.astype(BF16), sb, preferred_element_type=F32) + qu[u])
            states[gi] = (states[gi] * cd[u]
                          - jnp.dot(p_bd[u].astype(BF16), sb, preferred_element_type=F32)
                          + q_bd[u])
    finals = []
    for gi in range(groups):
        state_ref[gi] = states[gi]
        o_all = jnp.concatenate(outs[gi], axis=0)
        inv = lax.rsqrt(head_sums(o_all * o_all) * (1.0 / GDN_DV) + EPS)
        finals.append(o_all * inv * ng_ref[...] * _silu(prep[gi][3]))
    o_ref[0] = jnp.concatenate(finals, axis=1).astype(o_ref.dtype)


def _gdn(slab, gb, conv_slab, gdn_norm_g, ts):
    b, s, width = slab.shape
    groups = GDN_HEADS // GDN_GW
    gw = GDN_GW * GDN_DK
    sub = min(ts, CONV_BLOCK)
    r = jnp.arange(sub)
    shifts = jnp.stack([(r[:, None] - r[None, :]) == (CONV_WIDTH - 1 - j)
                        for j in range(CONV_WIDTH - 1)]).astype(BF16)
    li = jnp.arange(gw)
    ind = (li[:, None] // GDN_DK == li[None, :] // GDN_DK).astype(BF16)
    src = jnp.arange(LANES)[:, None]
    dst = jnp.arange(2 * GDN_HEADS * GDN_DK)[None, :]
    sel = (src == dst // GDN_DK).astype(BF16)
    return pl.pallas_call(
        functools.partial(_gdn_kernel, ts=ts, groups=groups),
        out_shape=jax.ShapeDtypeStruct((b, s, GDN_HEADS * GDN_DV), BF16),
        grid=(b, s // ts),
        in_specs=[pl.BlockSpec((1, ts, width), lambda i, j: (i, j, 0)),
                  pl.BlockSpec((1, ts, MISC_W), lambda i, j: (i, j, 0)),
                  pl.BlockSpec((CONV_WIDTH, width), lambda i, j: (0, 0)),
                  pl.BlockSpec((1, gw), lambda i, j: (0, 0)),
                  pl.BlockSpec((CONV_WIDTH - 1, sub, sub), lambda i, j: (0, 0, 0)),
                  pl.BlockSpec((gw, gw), lambda i, j: (0, 0)),
                  pl.BlockSpec((LANES, 2 * GDN_HEADS * GDN_DK), lambda i, j: (0, 0))],
        out_specs=pl.BlockSpec((1, ts, GDN_HEADS * GDN_DV), lambda i, j: (i, j, 0)),
        scratch_shapes=[pltpu.VMEM((16, width), F32),
                        pltpu.VMEM((groups, gw, gw), F32)],
        compiler_params=_cparams(("parallel", "arbitrary")),
        name="gated_deltanet",
    )(slab, gb, conv_slab, jnp.tile(gdn_norm_g, GDN_GW).reshape(1, gw), shifts, ind, sel)


def _outproj_kernel(x_ref, ao_ref, go_ref, gt_ref, sc_ref, sh_ref, mg_ref, wo_ref, g2_ref,
                    rw_ref, rb_ref, x1_ref, h2_ref, route_ref, rt_ref, cnt_ref, carry_ref, *, tb):
    first = (pl.program_id(0) == 0) & (pl.program_id(1) == 0)

    @pl.when(first)
    def _():
        carry_ref[...] = jnp.zeros(carry_ref.shape, F32)

    mla = _rms(ao_ref[0].astype(F32), mg_ref[...])
    cat = jnp.concatenate([mla.astype(BF16), go_ref[0]], axis=1)
    mix = jnp.dot(cat, wo_ref[...], preferred_element_type=F32)
    x1 = x_ref[0] + gt_ref[0] * mix
    x1_ref[0] = x1
    h2 = _rms(x1, g2_ref[...]) * (1.0 + sc_ref[0]) + sh_ref[0]
    h2_ref[0] = _pack_halves(h2)

    h_hi = h2.astype(BF16)
    h_lo = (h2 - h_hi.astype(F32)).astype(BF16)
    main = jnp.dot(h_hi, rw_ref[...], preferred_element_type=F32)
    logits = (main[:, :LANES] + main[:, LANES:]
              + jnp.dot(h_lo, rw_ref[:, :LANES], preferred_element_type=F32) + rb_ref[...])
    lane = lax.broadcasted_iota(jnp.int32, (tb, LANES), 1).astype(F32)
    work = jnp.where(lane < N_EXPERTS, logits, -jnp.inf)
    vals, idxs = [], []
    onehot = jnp.zeros((tb, LANES), F32)
    for _ in range(TOP_K):
        mx = jnp.max(work, axis=-1, keepdims=True)
        ix = jnp.min(jnp.where(work == mx, lane, float(LANES)), axis=-1, keepdims=True)
        sel = lane == ix
        onehot = jnp.where(sel, 1.0, onehot)
        work = jnp.where(sel, -jnp.inf, work)
        vals.append(mx)
        idxs.append(ix)
    exps = [jnp.exp(v - vals[0]) for v in vals]
    den = exps[0] + exps[1] + exps[2] + exps[3]

    r = lax.broadcasted_iota(jnp.int32, (tb, tb), 0)
    c = lax.broadcasted_iota(jnp.int32, (tb, tb), 1)
    tri = jnp.where(c < r, 1.0, 0.0).astype(BF16)
    before = jnp.dot(tri, onehot.astype(BF16), preferred_element_type=F32) + carry_ref[...]
    route = jnp.zeros((tb, LANES), F32)
    for kk in range(TOP_K):
        rank = jnp.sum(jnp.where(lane == idxs[kk], before, 0.0), axis=-1, keepdims=True)
        route = jnp.where(lane == kk, idxs[kk], route)
        route = jnp.where(lane == TOP_K + kk, rank, route)
        route = jnp.where(lane == 2 * TOP_K + kk, exps[kk] / den, route)
    route_ref[0] = route
    rt_ref[0] = route.T[:ROUTE_ROWS, :]
    total = carry_ref[...] + jnp.sum(onehot, axis=0, keepdims=True)
    carry_ref[...] = total
    cnt_ref[...] = total


def _out_projection(x, attn_o, gdn_o, gt1, sc2, sh2, mla_out_g, w_out, norm2_g, rw, rb, tb, b0, nb):
    _, s, d = x.shape
    const = lambda shape: pl.BlockSpec(shape, lambda i, j: (0,) * len(shape))
    tok_in = lambda w: pl.BlockSpec((1, tb, w), lambda i, j: (i + b0, j, 0))
    tok_out = lambda w: pl.BlockSpec((1, tb, w), lambda i, j: (i, j, 0))
    per_b = pl.BlockSpec((1, 1, d), lambda i, j: (i + b0, 0, 0))
    half = attn_o.shape[-1]
    return pl.pallas_call(
        functools.partial(_outproj_kernel, tb=tb),
        out_shape=(jax.ShapeDtypeStruct((nb, s, d), F32),
                   jax.ShapeDtypeStruct((nb, s, d // 2), F32),
                   jax.ShapeDtypeStruct((nb, s, LANES), F32),
                   jax.ShapeDtypeStruct((nb, ROUTE_ROWS, s), F32),
                   jax.ShapeDtypeStruct((1, LANES), F32)),
        grid=(nb, s // tb),
        in_specs=[tok_in(d), tok_in(half), tok_in(half), per_b, per_b, per_b, const((1, half)),
                  const(w_out.shape), const((1, d)), const(rw.shape), const((1, LANES))],
        out_specs=(tok_out(d), tok_out(d // 2), tok_out(LANES),
                   pl.BlockSpec((1, ROUTE_ROWS, tb), lambda i, j: (i, 0, j)), const((1, LANES))),
        scratch_shapes=[pltpu.VMEM((1, LANES), F32)],
        compiler_params=_cparams(("arbitrary", "arbitrary")),
        name="out_projection_router",
    )(x, attn_o, gdn_o, gt1, sc2, sh2, mla_out_g.reshape(1, half), w_out,
      norm2_g.reshape(1, d), rw, rb)


def _sc_mesh():
    return plsc.VectorSubcoreMesh(core_axis_name="c", subcore_axis_name="s",
                                  num_cores=SC_CORES, num_subcores=SC_SUBCORES)


def _sc_worker():
    return lax.axis_index("s") * SC_CORES + lax.axis_index("c")


def _dispatch_rows(h2, dest_km, rows):
    t, d = h2.shape
    per_worker = t // SC_WORKERS
    n_chunks = per_worker // SC_CHUNK

    @functools.partial(
        pl.kernel, out_type=jax.ShapeDtypeStruct((rows, d), h2.dtype), mesh=_sc_mesh(),
        scratch_types=[pltpu.VMEM((SC_CHUNK,), jnp.int32), pltpu.VMEM((SC_CHUNK, d), h2.dtype)],
        name="moe_dispatch")
    def run(h2_hbm, dest_hbm, xs_hbm, idx_v, rows_v):
        base_w = _sc_worker() * per_worker

        @pl.loop(0, n_chunks)
        def _(ci):
            base = pl.multiple_of(base_w + ci * SC_CHUNK, SC_CHUNK)
            pltpu.sync_copy(h2_hbm.at[pl.ds(base, SC_CHUNK)], rows_v)
            for kk in range(TOP_K):
                pltpu.sync_copy(dest_hbm.at[kk, pl.ds(base, SC_CHUNK)], idx_v)
                pltpu.sync_copy(rows_v, xs_hbm.at[idx_v])

    return run(h2, dest_km)


def _collect_rows(y_rows, dest_km):
    _, d = y_rows.shape
    t = dest_km.shape[1]
    per_worker = t // SC_WORKERS
    n_chunks = per_worker // SC_CHUNK

    @functools.partial(
        pl.kernel, out_type=jax.ShapeDtypeStruct((TOP_K, t, d), y_rows.dtype), mesh=_sc_mesh(),
        scratch_types=[pltpu.VMEM((SC_CHUNK,), jnp.int32), pltpu.VMEM((SC_CHUNK, d), y_rows.dtype)],
        name="moe_collect")
    def run(y_hbm, dest_hbm, out_hbm, idx_v, rows_v):
        base_w = _sc_worker() * per_worker

        @pl.loop(0, n_chunks)
        def _(ci):
            base = pl.multiple_of(base_w + ci * SC_CHUNK, SC_CHUNK)
            for kk in range(TOP_K):
                pltpu.sync_copy(dest_hbm.at[kk, pl.ds(base, SC_CHUNK)], idx_v)
                pltpu.sync_copy(y_hbm.at[idx_v], rows_v)
                pltpu.sync_copy(rows_v, out_hbm.at[kk, pl.ds(base, SC_CHUNK)])

    return run(y_rows, dest_km)


def _experts_kernel(be_ref, nu_ref, nxt_ref, slot_ref, live_ref, xs_ref, wgu_hbm, bgu_ref, wd_hbm,
                    bd_ref, y_ref, wgu_f32, wd_f32, wgu_bf, wd_bf, sem):
    i = pl.program_id(0)
    e = be_ref[i]
    new_expert = (i == 0) | (e != be_ref[jnp.maximum(i - 1, 0)])

    def fetch(expert, slot):
        return (pltpu.make_async_copy(wgu_hbm.at[expert], wgu_f32.at[slot], sem.at[0, slot]),
                pltpu.make_async_copy(wd_hbm.at[expert], wd_f32.at[slot], sem.at[1, slot]))

    @pl.when(i == 0)
    def _():
        for cp in fetch(e, slot_ref[e]):
            cp.start()

    @pl.when(new_expert)
    def _():
        slot = slot_ref[e]
        for cp in fetch(e, slot):
            cp.wait()
        wgu_bf[...] = wgu_f32[slot].astype(BF16)
        wd_bf[...] = wd_f32[slot].astype(BF16)
        nxt = nxt_ref[e]

        @pl.when(nxt >= 0)
        def _():
            for cp in fetch(nxt, 1 - slot):
                cp.start()

    def mlp(m):
        x_lo, x_hi = _unpack_halves(xs_ref[0:m, :])
        half = x_lo.shape[-1]
        gu = (jnp.dot(x_lo.astype(BF16), wgu_bf[:half, :], preferred_element_type=F32)
              + jnp.dot(x_hi.astype(BF16), wgu_bf[half:, :], preferred_element_type=F32)
              + bgu_ref[0])
        gate = jnp.minimum(gu[:, :D_EXPERT], SWIGLU_LIMIT)
        up = jnp.clip(gu[:, D_EXPERT:], -SWIGLU_LIMIT, SWIGLU_LIMIT)
        act = (up + 1.0) * (gate * _sigmoid(SWIGLU_ALPHA * gate))
        y = jnp.dot(act.astype(BF16), wd_bf[...], preferred_element_type=F32) + bd_ref[0]
        y_ref[0:m, :] = _pack_halves(y)

    quarter = xs_ref.shape[0] // ROW_QUARTERS
    for nq in range(1, ROW_QUARTERS + 1):
        @pl.when((i < nu_ref[0]) & (live_ref[i] == nq))
        def _(nq=nq):
            mlp(nq * quarter)


def _experts(blk_e, n_used, next_e, slot_e, live, xs, wgu, bgu, wd, bd, bm):
    rows, half = xs.shape
    d = 2 * half
    n_blocks = rows // bm
    row_map = lambda i, be, nu, nx, sl, lv: (jnp.maximum(jnp.minimum(i, nu[0] - 1), 0), 0)
    exp_map = lambda i, be, nu, nx, sl, lv: (be[i], 0, 0)
    grid_spec = pltpu.PrefetchScalarGridSpec(
        num_scalar_prefetch=5,
        grid=(n_blocks,),
        in_specs=[pl.BlockSpec((bm, half), row_map),
                  pl.BlockSpec(memory_space=pl.ANY),
                  pl.BlockSpec((1, 1, 2 * D_EXPERT), exp_map),
                  pl.BlockSpec(memory_space=pl.ANY),
                  pl.BlockSpec((1, 1, d), exp_map)],
        out_specs=pl.BlockSpec((bm, half), row_map),
        scratch_shapes=[pltpu.VMEM((2, d, 2 * D_EXPERT), F32),
                        pltpu.VMEM((2, D_EXPERT, d), F32),
                        pltpu.VMEM((d, 2 * D_EXPERT), BF16),
                        pltpu.VMEM((D_EXPERT, d), BF16),
                        pltpu.SemaphoreType.DMA((2, 2))])
    return pl.pallas_call(
        _experts_kernel,
        out_shape=jax.ShapeDtypeStruct((rows, half), F32),
        grid_spec=grid_spec,
        compiler_params=_cparams(("arbitrary",)),
        name="expert_mlp",
    )(blk_e, n_used, next_e, slot_e, live, xs, wgu, bgu, wd, bd)


def _final_kernel(x1_ref, yk_ref, route_ref, gt_ref, fg_ref, *rest):
    o_ref = rest[-1]
    route = route_ref[0]
    ffn = jnp.zeros(x1_ref.shape[1:], F32)
    for kk in range(TOP_K):
        wk = route[:, 2 * TOP_K + kk:2 * TOP_K + kk + 1]
        y_lo, y_hi = _unpack_halves(yk_ref[kk, 0])
        ffn = ffn + wk * jnp.concatenate([y_lo, y_hi], axis=1)
    x2 = x1_ref[0] + gt_ref[0] * ffn
    o_ref[0] = _rms(x2, fg_ref[...])


def _final(x1, yk, route, gt2, final_g, tb, b0, b_total, prev):
    nb, s, d = x1.shape
    in_specs = [pl.BlockSpec((1, tb, d), lambda i, j: (i, j, 0)),
                pl.BlockSpec((TOP_K, 1, tb, d // 2), lambda i, j: (0, i, j, 0)),
                pl.BlockSpec((1, tb, LANES), lambda i, j: (i, j, 0)),
                pl.BlockSpec((1, 1, d), lambda i, j: (i + b0, 0, 0)),
                pl.BlockSpec((1, d), lambda i, j: (0, 0))]
    args = [x1, yk, route, gt2, final_g.reshape(1, d)]
    aliases = {}
    if prev is not None:
        in_specs.append(pl.BlockSpec(memory_space=pl.ANY))
        args.append(prev)
        aliases = {len(args) - 1: 0}
    return pl.pallas_call(
        _final_kernel,
        out_shape=jax.ShapeDtypeStruct((b_total, s, d), F32),
        grid=(nb, s // tb),
        in_specs=in_specs,
        out_specs=pl.BlockSpec((1, tb, d), lambda i, j: (i + b0, j, 0)),
        input_output_aliases=aliases,
        compiler_params=_cparams(("parallel", "parallel")),
        name="combine_final_norm",
    )(*args)


def _rot_cols(w):
    half = MLA_ROPE // 2
    return jnp.concatenate([-w[..., half:], w[..., :half]], axis=-1)


def _prep_weights(w_in, w_q_b, w_kv_b, conv_w, A_log, dt_bias):
    d = w_in.shape[0]
    cuts = [MLA_Q_LORA, MLA_KV_LORA, MLA_ROPE, GDN_HEADS * GDN_DK, GDN_HEADS * GDN_DK,
            GDN_HEADS * GDN_DV, GDN_HEADS * GDN_DV, GDN_HEADS, GDN_HEADS]
    offs = [0]
    for cw in cuts:
        offs.append(offs[-1] + cw)
    part = lambda i: w_in[:, offs[i]:offs[i + 1]]
    k_pe = part(2)
    misc = jnp.concatenate(
        [part(7), part(8), jnp.zeros((d, MLA_NOPE - 2 * GDN_HEADS), w_in.dtype), k_pe,
         _rot_cols(k_pe)], axis=1)
    heads = lambda w, n: w.reshape(w.shape[0], GDN_HEADS // GDN_GW, GDN_GW * n)
    slab = jnp.concatenate([heads(part(3), GDN_DK), heads(part(4), GDN_DK),
                            heads(part(5), GDN_DV), heads(part(6), GDN_DV)], axis=-1)
    w1 = jnp.concatenate([part(0), part(1), misc, slab.reshape(d, GDN_HEADS * GDN_SLAB)],
                         axis=1).astype(BF16)

    wq3 = w_q_b.reshape(MLA_Q_LORA, MLA_HEADS, MLA_QK)
    pe = wq3[..., MLA_NOPE:]
    wq = jnp.concatenate([wq3[..., :MLA_NOPE], pe, _rot_cols(pe)], axis=-1)
    wq = wq.reshape(MLA_Q_LORA, MLA_HEADS * HEAD_SLOT).astype(BF16)

    wkv3 = w_kv_b.reshape(MLA_KV_LORA, MLA_HEADS, MLA_NOPE + MLA_V)
    wk = jnp.concatenate([wkv3[..., :MLA_NOPE],
                          jnp.zeros((MLA_KV_LORA, MLA_HEADS, HEAD_SLOT - MLA_NOPE), w_kv_b.dtype)],
                         axis=-1).reshape(MLA_KV_LORA, MLA_HEADS * HEAD_SLOT)
    wv = wkv3[..., MLA_NOPE:].reshape(MLA_KV_LORA, MLA_HEADS * MLA_V)
    wkv = jnp.concatenate([wk, wv], axis=1).astype(BF16)

    nk = GDN_HEADS * GDN_DK
    cheads = lambda w, n: w.reshape(CONV_WIDTH, GDN_HEADS // GDN_GW, GDN_GW * n)
    conv_slab = jnp.concatenate(
        [cheads(conv_w[:, :nk], GDN_DK), cheads(conv_w[:, nk:2 * nk], GDN_DK),
         cheads(conv_w[:, 2 * nk:], GDN_DV),
         jnp.zeros((CONV_WIDTH, GDN_HEADS // GDN_GW, GDN_GW * GDN_DV), conv_w.dtype)],
        axis=-1).reshape(CONV_WIDTH, GDN_HEADS * GDN_SLAB)

    half = MLA_ROPE // 2
    inv_freq = ROPE_THETA ** (-jnp.arange(half, dtype=F32) / half)
    lane = jnp.arange(LANES)
    freq = jnp.where(lane[None, :] // MLA_ROPE == jnp.arange(ROPE_GROUPS)[:, None],
                     jnp.tile(inv_freq, LANES // half)[None, :], 0.0)
    padl = lambda a: jnp.concatenate([a.astype(F32), jnp.zeros((LANES - a.shape[0],), F32)]).reshape(1, LANES)
    return w1, wq, wkv, conv_slab, freq, padl(A_log), padl(dt_bias)


def _layer(x, mod, positions, w_in, q_norm_g, w_q_b, kv_norm_g, w_kv_b, mla_out_g, conv_w,
           A_log, dt_bias, gdn_norm_g, w_out, norm1_g, norm2_g, router_w, router_b, w_gate_up,
           b_gate_up, w_down, b_down, final_g):
    b, s, d = x.shape
    t = b * s
    sh1, sc1, gt1, sh2, sc2, gt2 = [m.reshape(b, 1, d) for m in jnp.split(mod, 6, axis=-1)]
    w1, wq, wkv, conv_slab, freq, alog, dtb = _prep_weights(w_in, w_q_b, w_kv_b, conv_w, A_log,
                                                            dt_bias)
    tb = min(256, s)
    q, k, v, slab, gb = _in_projection(x, positions, sc1, sh1, norm1_g, w1, q_norm_g, wq,
                                       kv_norm_g, wkv, freq, alog, dtb, min(512, s))
    attn_o = _attention(q, k, v, min(512, s), 8)
    gdn_o = _gdn(slab, gb, conv_slab, gdn_norm_g, min(512, s))

    rw = jnp.concatenate([router_w, jnp.zeros((d, LANES - N_EXPERTS), router_w.dtype)], axis=1)
    rw_hi = rw.astype(BF16)
    rw = jnp.concatenate([rw_hi, (rw - rw_hi.astype(F32)).astype(BF16)], axis=1)
    rb = jnp.concatenate([router_b, jnp.zeros((LANES - N_EXPERTS,), router_b.dtype)]).reshape(1, LANES)
    w_out_bf = w_out.astype(BF16)
    bgu = b_gate_up.reshape(N_EXPERTS, 1, -1)
    bd = b_down.reshape(N_EXPERTS, 1, -1)
    n_parts = MOE_PARTS if b % MOE_PARTS == 0 else 1
    nb = b // n_parts
    out = None
    for part in range(n_parts):
        b0 = part * nb
        x1, h2, route, route_t, counts = _out_projection(x, attn_o, gdn_o, gt1, sc2, sh2,
                                                         mla_out_g, w_out_bf, norm2_g, rw, rb,
                                                         min(512, s), b0, nb)
        yk = _moe(h2, route_t, counts, w_gate_up, bgu, w_down, bd)
        out = _final(x1, yk, route, gt2, final_g, tb, b0, b, out)
    return out


def _moe(h2, route_t, counts, w_gate_up, bgu, w_down, bd):
    nb, s, half = h2.shape
    t = nb * s
    bm = EXPERT_ROWS
    slot_major = lambda r0: jnp.swapaxes(route_t[:, r0:r0 + TOP_K, :], 0, 1).reshape(TOP_K, t)
    idx = slot_major(0).astype(jnp.int32)
    rank = slot_major(TOP_K).astype(jnp.int32)
    cnt = counts[0, :N_EXPERTS].astype(jnp.int32)
    padded = ((cnt + bm - 1) // bm) * bm
    pend = jnp.cumsum(padded)
    pstart = pend - padded
    base = jnp.zeros_like(idx)
    for e in range(N_EXPERTS):
        base = jnp.where(idx == e, pstart[e], base)
    dest_km = base + rank
    n_blocks = (t * TOP_K + N_EXPERTS * (bm - 1) + bm - 1) // bm
    rows = n_blocks * bm
    n_used = (pend[-1] // bm).astype(jnp.int32).reshape(1)
    blk_start = jnp.arange(n_blocks, dtype=jnp.int32) * bm
    blk_e = jnp.minimum(jnp.sum(blk_start[:, None] >= pend[None, :], axis=1), N_EXPERTS - 1)
    last_e = blk_e[jnp.maximum(n_used[0] - 1, 0)]
    blk_e = jnp.where(jnp.arange(n_blocks) < n_used[0], blk_e, last_e).astype(jnp.int32)

    present = padded > 0
    eidx = jnp.arange(N_EXPERTS, dtype=jnp.int32)
    later = present[None, :] & (eidx[None, :] > eidx[:, None])
    next_e = jnp.where(jnp.any(later, axis=1), jnp.argmax(later, axis=1), -1).astype(jnp.int32)
    slot_e = ((jnp.cumsum(present.astype(jnp.int32)) - 1) % 2).astype(jnp.int32)

    in_blk = jnp.clip(cnt[blk_e] - (blk_start - pstart[blk_e]), 0, bm)
    quarter = bm // ROW_QUARTERS
    live = jnp.clip((in_blk + quarter - 1) // quarter, 1, ROW_QUARTERS).astype(jnp.int32)

    xs = _dispatch_rows(h2.reshape(t, half), dest_km, rows)
    y_rows = _experts(blk_e, n_used, next_e, slot_e, live, xs, w_gate_up, bgu, w_down, bd, bm)
    return _collect_rows(y_rows, dest_km).reshape(TOP_K, nb, s, half)


def kernel(x, c, positions, ada_w, ada_b, norm1_g, w_in, q_norm_g, w_q_b, kv_norm_g, w_kv_b, mla_out_g, conv_w, A_log, dt_bias, gdn_norm_g, w_out, norm2_g, router_w, router_b, w_gate_up, b_gate_up, w_down, b_down, final_g):
    depth = ada_w.shape[0]
    assert depth == 1
    l = 0
    mod = _modulation(c, ada_w[l], ada_b[l])
    return _layer(x, mod, positions, w_in[l], q_norm_g[l], w_q_b[l], kv_norm_g[l],
                  w_kv_b[l], mla_out_g[l], conv_w[l], A_log[l], dt_bias[l], gdn_norm_g[l],
                  w_out[l], norm1_g[l], norm2_g[l], router_w[l], router_b[l],
                  w_gate_up[l], b_gate_up[l], w_down[l], b_down[l], final_g)
```

```python
import functools

import jax
import jax.numpy as jnp
from jax import lax
from jax.experimental import pallas as pl
from jax.experimental.pallas import tpu as pltpu
from jax.experimental.pallas import tpu_sc as plsc

F32 = jnp.float32
BF16 = jnp.bfloat16

D_MODEL = 1024
EPS = 1e-6
MLA_HEADS = 8
MLA_NOPE = 64
MLA_ROPE = 32
MLA_V = 64
MLA_QK = MLA_NOPE + MLA_ROPE
MLA_Q_LORA = 384
MLA_KV_LORA = 256
ROPE_THETA = 10000.0
GDN_HEADS = 8
GDN_DK = 64
GDN_DV = 64
CONV_WIDTH = 4
CHUNK = 64
N_EXPERTS = 32
TOP_K = 4
D_EXPERT = D_MODEL
SWIGLU_ALPHA = 1.702
SWIGLU_LIMIT = 7.0

LANES = 128
HEAD_SLOT = 128
GDN_SLAB = 256
GDN_GW = 4
CONV_BLOCK = 256
MISC_W = 128
PROJ_W = MLA_Q_LORA + MLA_KV_LORA + MISC_W + GDN_HEADS * GDN_SLAB
VMEM_LIMIT = 56 * 1024 * 1024
MOE_PARTS = 2
EXPERT_ROWS = 512
ROW_QUARTERS = 4
SC_CORES = 2
SC_SUBCORES = 16
SC_WORKERS = SC_CORES * SC_SUBCORES
SC_CHUNK = 128
ROPE_GROUPS = LANES // MLA_ROPE
ROUTE_ROWS = 16
ATT_STRIP = 32
LOG2E = 1.4426950408889634


def _cparams(sem):
    return pltpu.CompilerParams(dimension_semantics=sem, vmem_limit_bytes=VMEM_LIMIT)


def _rms(x, g):
    return x * lax.rsqrt(jnp.mean(x * x, axis=-1, keepdims=True) + EPS) * g


def _sigmoid(x):
    return 1.0 / (1.0 + jnp.exp(-x))


def _silu(x):
    return x * _sigmoid(x)


def _pack_halves(x):
    w = x.shape[-1] // 2
    lo = lax.bitcast_convert_type(x[:, :w].astype(BF16).astype(F32), jnp.uint32) >> 16
    hi = lax.bitcast_convert_type(x[:, w:].astype(BF16).astype(F32), jnp.uint32) & jnp.uint32(0xFFFF0000)
    return lax.bitcast_convert_type(lo | hi, F32)


def _unpack_halves(p):
    u = lax.bitcast_convert_type(p, jnp.uint32)
    return (lax.bitcast_convert_type(u << 16, F32),
            lax.bitcast_convert_type(u & jnp.uint32(0xFFFF0000), F32))


def _mod_kernel(c_ref, w_ref, b_ref, o_ref):
    c = c_ref[...]
    o_ref[...] = jnp.dot(_silu(c), w_ref[...], preferred_element_type=F32,
                         precision=lax.Precision.HIGHEST) + b_ref[...]


def _modulation(c, ada_w, ada_b):
    b, d = c.shape
    n = ada_w.shape[1]
    return pl.pallas_call(
        _mod_kernel,
        out_shape=jax.ShapeDtypeStruct((b, n), F32),
        grid=(n // d,),
        in_specs=[pl.BlockSpec((b, d), lambda j: (0, 0)),
                  pl.BlockSpec((d, d), lambda j: (0, j)),
                  pl.BlockSpec((1, d), lambda j: (0, j))],
        out_specs=pl.BlockSpec((b, d), lambda j: (0, j)),
        compiler_params=_cparams(("arbitrary",)),
        name="adaln_mod",
    )(c, ada_w, ada_b.reshape(1, n))


def _inproj_kernel(x_ref, pos_ref, sc_ref, sh_ref, g1_ref, w1_ref, qg_ref, wq_ref, kvg_ref,
                   wkv_ref, freq_ref, alog_ref, dtb_ref,
                   q_ref, k_ref, v_ref, gdn_ref, gb_ref):
    x = x_ref[0]
    h = _rms(x, g1_ref[...]) * (1.0 + sc_ref[0]) + sh_ref[0]
    proj = jnp.dot(h.astype(BF16), w1_ref[...], preferred_element_type=F32)

    tb = x.shape[0]
    lane = lax.broadcasted_iota(jnp.int32, (tb, LANES), 1)
    in_rope = (lane >= MLA_NOPE) & (lane < MLA_QK)
    pos = pos_ref[0].astype(F32)
    quarter = tb // ROPE_GROUPS
    ang = pos[0:quarter] * freq_ref[0:1, :]
    for g in range(1, ROPE_GROUPS):
        ang = ang + pos[g * quarter:(g + 1) * quarter] * freq_ref[g:g + 1, :]
    cos_p = jnp.cos(ang)
    sin_p = jnp.sin(ang)
    unpack = lambda t: jnp.concatenate(
        [pltpu.roll(t, (MLA_NOPE - MLA_ROPE * g) % LANES, axis=1) for g in range(ROPE_GROUPS)],
        axis=0)
    cosv = unpack(cos_p)
    sinv = jnp.where(in_rope, unpack(sin_p), 0.0)

    scale = (MLA_QK ** -0.5) * LOG2E
    qn = _rms(proj[:, :MLA_Q_LORA], qg_ref[...])
    qa = jnp.dot(qn.astype(BF16), wq_ref[...], preferred_element_type=F32)
    cq = jnp.where(lane < MLA_NOPE, scale, jnp.where(in_rope, cosv * scale, 0.0))
    sq = sinv * scale
    cq_t = jnp.concatenate([cq] * MLA_HEADS, axis=1)
    sq_t = jnp.concatenate([sq] * MLA_HEADS, axis=1)
    width = MLA_HEADS * HEAD_SLOT
    q = qa * cq_t + pltpu.roll(qa, width - MLA_ROPE, axis=1) * sq_t
    q_ref[0] = q.astype(BF16)

    kvn = _rms(proj[:, MLA_Q_LORA:MLA_Q_LORA + MLA_KV_LORA], kvg_ref[...])
    kva = jnp.dot(kvn.astype(BF16), wkv_ref[...], preferred_element_type=F32)
    misc = proj[:, MLA_Q_LORA + MLA_KV_LORA:MLA_Q_LORA + MLA_KV_LORA + MISC_W]
    kp = misc * jnp.where(in_rope, cosv, 0.0) + pltpu.roll(misc, MISC_W - MLA_ROPE, axis=1) * sinv
    k = kva[:, :width] + jnp.concatenate([kp] * MLA_HEADS, axis=1)
    k_ref[0] = k.astype(BF16)
    v_ref[0] = kva[:, width:].astype(BF16)

    z = misc + dtb_ref[...]
    softplus = jnp.maximum(z, 0.0) + jnp.log(1.0 + jnp.exp(-jnp.abs(z)))
    g = -jnp.exp(alog_ref[...]) * softplus
    gb_ref[0] = jnp.where(lane < GDN_HEADS, g, _sigmoid(misc))

    gdn_ref[0] = proj[:, MLA_Q_LORA + MLA_KV_LORA + MISC_W:].astype(BF16)


def _in_projection(x, positions, sc1, sh1, norm1_g, w1, q_norm_g, wq, kv_norm_g, wkv,
                   freq, alog, dtb, tb):
    b, s, d = x.shape
    hw = MLA_HEADS * HEAD_SLOT
    const = lambda shape: pl.BlockSpec(shape, lambda i, j: (0,) * len(shape))
    tok = lambda w: pl.BlockSpec((1, tb, w), lambda i, j: (i, j, 0))
    per_b = pl.BlockSpec((1, 1, d), lambda i, j: (i, 0, 0))
    return pl.pallas_call(
        _inproj_kernel,
        out_shape=(jax.ShapeDtypeStruct((b, s, hw), BF16),
                   jax.ShapeDtypeStruct((b, s, hw), BF16),
                   jax.ShapeDtypeStruct((b, s, MLA_HEADS * MLA_V), BF16),
                   jax.ShapeDtypeStruct((b, s, GDN_HEADS * GDN_SLAB), BF16),
                   jax.ShapeDtypeStruct((b, s, MISC_W), F32)),
        grid=(b, s // tb),
        in_specs=[tok(d), tok(1), per_b, per_b, const((1, d)), const(w1.shape),
                  const((1, MLA_Q_LORA)), const(wq.shape), const((1, MLA_KV_LORA)),
                  const(wkv.shape), const((ROPE_GROUPS, LANES)), const((1, LANES)),
                  const((1, LANES))],
        out_specs=(tok(hw), tok(hw), tok(MLA_HEADS * MLA_V), tok(GDN_HEADS * GDN_SLAB),
                   tok(MISC_W)),
        compiler_params=_cparams(("parallel", "parallel")),
        name="in_projection",
    )(x, positions.reshape(b, s, 1), sc1, sh1, norm1_g.reshape(1, d), w1,
      q_norm_g.reshape(1, -1), wq, kv_norm_g.reshape(1, -1), wkv, freq, alog, dtb)


def _attn_kernel(q_ref, k_ref, v_ref, o_ref, s_ref, p_ref, m_ref, l_ref, acc_ref, *, tq, nh):
    qi = pl.program_id(2)
    m_ref[...] = jnp.full(m_ref.shape, -jnp.inf, F32)
    l_ref[...] = jnp.zeros(l_ref.shape, F32)
    acc_ref[...] = jnp.zeros(acc_ref.shape, F32)

    def step(off, masked):
        segs = [(0, tq // 2, tq // 2), (tq // 2, tq, tq)] if masked else [(0, tq, tq)]
        width = lambda r: next(nc for r0, r1, nc in segs if r0 <= r * ATT_STRIP < r1)
        for h in range(nh):
            for r0, r1, nc in segs:
                kj = k_ref[0, pl.ds(off, nc), h * HEAD_SLOT:(h + 1) * HEAD_SLOT]
                s_ref[h, r0:r1, 0:nc] = lax.dot_general(
                    q_ref[0, r0:r1, h * HEAD_SLOT:(h + 1) * HEAD_SLOT], kj,
                    (((1,), (1,)), ((), ())), preferred_element_type=F32)
        n_strips = tq // ATT_STRIP

        def strip(h, r):
            rows = slice(r * ATT_STRIP, (r + 1) * ATT_STRIP)
            nc = width(r)
            sc = s_ref[h, rows, 0:nc]
            if masked:
                rid = lax.broadcasted_iota(jnp.int32, (ATT_STRIP, nc), 0) + r * ATT_STRIP
                cid = lax.broadcasted_iota(jnp.int32, (ATT_STRIP, nc), 1)
                sc = jnp.where(cid <= rid, sc, -jnp.inf)
            return rows, sc

        rep = lambda col: jnp.broadcast_to(col, (col.shape[0], LANES))
        m_new = []
        for h in range(nh):
            mx = jnp.concatenate([rep(jnp.max(strip(h, r)[1], axis=-1, keepdims=True))
                                  for r in range(n_strips)], axis=0)
            m_new.append(jnp.maximum(m_ref[h], mx))
        for h in range(nh):
            v_lanes = slice((h // 2) * 2 * MLA_V, (h // 2 + 1) * 2 * MLA_V)
            sums = []
            for r in range(n_strips):
                rows, sc = strip(h, r)
                nc = width(r)
                p = jnp.exp2(sc - jnp.concatenate([m_new[h][rows]] * (nc // LANES), axis=1))
                p_ref[h, rows, 0:nc] = p.astype(BF16)
                sums.append(rep(jnp.sum(p, axis=-1, keepdims=True)))
            alpha = jnp.exp2(m_ref[h] - m_new[h])
            l_ref[h] = alpha * l_ref[h] + jnp.concatenate(sums, axis=0)
            m_ref[h] = m_new[h]
            for r0, r1, nc in segs:
                acc_ref[h, r0:r1, :] = alpha[r0:r1] * acc_ref[h, r0:r1, :] + jnp.dot(
                    p_ref[h, r0:r1, 0:nc], v_ref[0, pl.ds(off, nc), v_lanes],
                    preferred_element_type=F32)

    def body(j, carry):
        step(pl.multiple_of(j * tq, tq), False)
        return carry

    lax.fori_loop(0, qi, body, 0)
    step(pl.multiple_of(qi * tq, tq), True)
    lane = lax.broadcasted_iota(jnp.int32, (tq, 2 * MLA_V), 1)
    outs = []
    for p in range(nh // 2):
        o0 = acc_ref[2 * p] / l_ref[2 * p]
        o1 = acc_ref[2 * p + 1] / l_ref[2 * p + 1]
        outs.append(jnp.where(lane < MLA_V, o0, o1))
    o_ref[0] = jnp.concatenate(outs, axis=1).astype(o_ref.dtype)


def _attention(q, k, v, tq, nh):
    b, s, _ = q.shape
    groups = MLA_HEADS // nh
    return pl.pallas_call(
        functools.partial(_attn_kernel, tq=tq, nh=nh),
        out_shape=jax.ShapeDtypeStruct((b, s, MLA_HEADS * MLA_V), BF16),
        grid=(b, groups, s // tq),
        in_specs=[pl.BlockSpec((1, tq, nh * HEAD_SLOT), lambda i, p, j: (i, j, p)),
                  pl.BlockSpec((1, s, nh * HEAD_SLOT), lambda i, p, j: (i, 0, p)),
                  pl.BlockSpec((1, s, nh * MLA_V), lambda i, p, j: (i, 0, p))],
        out_specs=pl.BlockSpec((1, tq, nh * MLA_V), lambda i, p, j: (i, j, p)),
        scratch_shapes=[pltpu.VMEM((nh, tq, tq), F32), pltpu.VMEM((nh, tq, tq), BF16),
                        pltpu.VMEM((nh, tq, LANES), F32), pltpu.VMEM((nh, tq, LANES), F32),
                        pltpu.VMEM((nh, tq, 2 * MLA_V), F32)],
        compiler_params=_cparams(("parallel", "parallel", "arbitrary")),
        name="mla_attention",
    )(q, k, v)


def _gdn_kernel(slab_ref, gb_ref, cw_ref, ng_ref, shf_ref, ind_ref, sel_ref, o_ref, tail_ref,
                state_ref, *, ts, groups):
    si = pl.program_id(1)
    gw = GDN_GW * GDN_DK
    hist_rows = 8

    @pl.when(si == 0)
    def _():
        tail_ref[...] = jnp.zeros(tail_ref.shape, F32)
        state_ref[...] = jnp.zeros(state_ref.shape, F32)

    xs_bf = slab_ref[0]
    xs = xs_bf.astype(F32)
    width = xs.shape[1]
    sub = shf_ref.shape[1]
    conv_parts = []
    for r0 in range(0, ts, sub):
        xs_sub = xs[r0:r0 + sub]
        conv = cw_ref[CONV_WIDTH - 1:CONV_WIDTH, :] * xs_sub
        hist = jnp.zeros((hist_rows, width), F32)
        for j in range(CONV_WIDTH - 1):
            wj = cw_ref[j:j + 1, :]
            conv = conv + wj * jnp.dot(shf_ref[j], xs_bf[r0:r0 + sub],
                                       preferred_element_type=F32)
            start = hist_rows - (CONV_WIDTH - 1) + j
            hist = hist + wj * tail_ref[start:start + hist_rows, :]
        conv_parts += [conv[:hist_rows] + hist, conv[hist_rows:]]
        tail_ref[0:hist_rows, :] = xs_sub[sub - hist_rows:sub, :]
    act = _silu(jnp.concatenate(conv_parts, axis=0))

    gb = gb_ref[0]
    g1 = gb.astype(BF16)
    r1 = gb - g1.astype(F32)
    g2 = r1.astype(BF16)
    g3 = (r1 - g2.astype(F32)).astype(BF16)
    sel = sel_ref[...]
    gate_w = (jnp.dot(g1, sel, preferred_element_type=F32)
              + jnp.dot(g2, sel, preferred_element_type=F32)
              + jnp.dot(g3, sel, preferred_element_type=F32))

    n_chunks = ts // CHUNK
    row = lax.broadcasted_iota(jnp.int32, (CHUNK, gw), 0)
    col = lax.broadcasted_iota(jnp.int32, (CHUNK, gw), 1) % GDN_DK
    tri_incl = col <= row
    tri_strict = col < row
    eye = col == row
    brow = lax.broadcasted_iota(jnp.int32, (gw, gw), 0) // GDN_DK
    bcol = lax.broadcasted_iota(jnp.int32, (gw, gw), 1) // GDN_DK
    same_head = brow == bcol
    crow = lax.broadcasted_iota(jnp.int32, (ts, gw), 0) % CHUNK

    def split2(x):
        hi = x.astype(BF16)
        return hi, (x - hi.astype(F32)).astype(BF16)

    def head_sums(x):
        hi, lo = split2(x)
        return (jnp.dot(hi, ind_ref[...], preferred_element_type=F32)
                + jnp.dot(lo, ind_ref[...], preferred_element_type=F32))

    def blockdiag(x):
        return jnp.where(same_head, jnp.concatenate([x.astype(BF16)] * GDN_GW, axis=0), 0)

    def wdot(a, b):
        return jnp.dot(a.astype(BF16), blockdiag(b), preferred_element_type=F32)

    prep = []
    for gi in range(groups):
        base = gi * GDN_GW * GDN_SLAB
        q_raw = act[:, base:base + gw]
        k_raw = act[:, base + gw:base + 2 * gw]
        v_all = act[:, base + 2 * gw:base + 3 * gw]
        z_all = xs[:, base + 3 * gw:base + 4 * gw]
        q_all = q_raw * (lax.rsqrt(head_sums(q_raw * q_raw) + EPS) * (GDN_DK ** -0.5))
        k_all = k_raw * lax.rsqrt(head_sums(k_raw * k_raw) + EPS)
        g_w = gate_w[:, gi * gw:(gi + 1) * gw]
        b_w = gate_w[:, (groups + gi) * gw:(groups + gi + 1) * gw]
        gc_w = g_w
        shift = 1
        while shift < CHUNK:
            rolled = pltpu.roll(gc_w, shift, axis=0)
            gc_w = gc_w + jnp.where(crow >= shift, rolled, 0.0)
            shift *= 2
        prep.append((q_all, k_all, v_all, z_all, b_w, gc_w, jnp.exp(gc_w)))

    units = [(gi, c) for c in range(n_chunks) for gi in range(groups)]
    a_low, qk, yu0, yw0, qd, kt, cd = ({} for _ in range(7))
    for u in units:
        gi, c = u
        q_all, k_all, v_all, _, b_w, gc_w, eg_w = prep[gi]
        sl = slice(c * CHUNK, (c + 1) * CHUNK)
        qc, kc, vc, bc, gcum, eg = q_all[sl], k_all[sl], v_all[sl], b_w[sl], gc_w[sl], eg_w[sl]
        grow = jnp.sum(jnp.where(eye, gcum, 0.0), axis=0, keepdims=True)
        diff = gcum - grow
        decay = jnp.where(tri_incl, jnp.exp(jnp.where(tri_incl, diff, 0.0)), 0.0)
        k_beta = kc * bc
        kq = jnp.concatenate([k_beta, qc], axis=0).astype(BF16)
        kk = lax.dot_general(kq, blockdiag(kc), (((1,), (1,)), ((), ())),
                             preferred_element_type=F32)
        a_low[u] = jnp.where(tri_strict, kk[:CHUNK] * decay, 0.0)
        qk[u] = jnp.where(tri_incl, kk[CHUNK:] * decay, 0.0)
        yu0[u] = vc * bc
        yw0[u] = k_beta * eg
        qd[u] = qc * eg
        glast = gcum[CHUNK - 1:CHUNK, :]
        kt[u] = kc * jnp.exp(glast - gcum)
        cd[u] = eg[CHUNK - 1:CHUNK, :]

    blk = 2
    tinv = {u: jnp.where(eye, 1.0, 0.0) - jnp.where((row // blk) == (col // blk), a_low[u], 0.0)
            for u in units}
    while blk < CHUNK:
        in_big = (row // (2 * blk)) == (col // (2 * blk))
        off_mask = in_big & ((row // blk) != (col // blk))
        left = {u: wdot(tinv[u], jnp.where(off_mask, a_low[u], 0.0)) for u in units}
        tinv = {u: tinv[u] - wdot(left[u], tinv[u]) for u in units}
        blk *= 2

    yu = {u: wdot(tinv[u], yu0[u]) for u in units}
    yw = {u: wdot(tinv[u], yw0[u]) for u in units}
    tdot = lambda a, b: jnp.where(same_head, lax.dot_general(
        a.astype(BF16), b.astype(BF16), (((0,), (0,)), ((), ())),
        preferred_element_type=F32), 0.0)
    q_bd = {u: tdot(kt[u], yu[u]) for u in units}
    p_bd = {u: tdot(kt[u], yw[u]) for u in units}
    qu = {u: wdot(qk[u], yu[u]) for u in units}
    qw = {u: wdot(qk[u], yw[u]) for u in units}

    outs = [[] for _ in range(groups)]
    states = [state_ref[gi] for gi in range(groups)]
    for c in range(n_chunks):
        for gi in range(groups):
            u = (gi, c)
            sb = states[gi].astype(BF16)
            r_mat = qd[u] - qw[u]
            outs[gi].append(jnp.dot(r_mat.astype(BF16), sb, preferred_element_type=F32) + qu[u])
            states[gi] = (states[gi] * cd[u]
                          - jnp.dot(p_bd[u].astype(BF16), sb, preferred_element_type=F32)
                          + q_bd[u])
    finals = []
    for gi in range(groups):
        state_ref[gi] = states[gi]
        o_all = jnp.concatenate(outs[gi], axis=0)
        inv = lax.rsqrt(head_sums(o_all * o_all) * (1.0 / GDN_DV) + EPS)
        finals.append(o_all * inv * ng_ref[...] * _silu(prep[gi][3]))
    o_ref[0] = jnp.concatenate(finals, axis=1).astype(o_ref.dtype)


def _gdn(slab, gb, conv_slab, gdn_norm_g, ts):
    b, s, width = slab.shape
    groups = GDN_HEADS // GDN_GW
    gw = GDN_GW * GDN_DK
    sub = min(ts, CONV_BLOCK)
    r = jnp.arange(sub)
    shifts = jnp.stack([(r[:, None] - r[None, :]) == (CONV_WIDTH - 1 - j)
                        for j in range(CONV_WIDTH - 1)]).astype(BF16)
    li = jnp.arange(gw)
    ind = (li[:, None] // GDN_DK == li[None, :] // GDN_DK).astype(BF16)
    src = jnp.arange(LANES)[:, None]
    dst = jnp.arange(2 * GDN_HEADS * GDN_DK)[None, :]
    sel = (src == dst // GDN_DK).astype(BF16)
    return pl.pallas_call(
        functools.partial(_gdn_kernel, ts=ts, groups=groups),
        out_shape=jax.ShapeDtypeStruct((b, s, GDN_HEADS * GDN_DV), BF16),
        grid=(b, s // ts),
        in_specs=[pl.BlockSpec((1, ts, width), lambda i, j: (i, j, 0)),
                  pl.BlockSpec((1, ts, MISC_W), lambda i, j: (i, j, 0)),
                  pl.BlockSpec((CONV_WIDTH, width), lambda i, j: (0, 0)),
                  pl.BlockSpec((1, gw), lambda i, j: (0, 0)),
                  pl.BlockSpec((CONV_WIDTH - 1, sub, sub), lambda i, j: (0, 0, 0)),
                  pl.BlockSpec((gw, gw), lambda i, j: (0, 0)),
                  pl.BlockSpec((LANES, 2 * GDN_HEADS * GDN_DK), lambda i, j: (0, 0))],
        out_specs=pl.BlockSpec((1, ts, GDN_HEADS * GDN_DV), lambda i, j: (i, j, 0)),
        scratch_shapes=[pltpu.VMEM((16, width), F32),
                        pltpu.VMEM((groups, gw, gw), F32)],
        compiler_params=_cparams(("parallel", "arbitrary")),
        name="gated_deltanet",
    )(slab, gb, conv_slab, jnp.tile(gdn_norm_g, GDN_GW).reshape(1, gw), shifts, ind, sel)


def _outproj_kernel(x_ref, ao_ref, go_ref, gt_ref, sc_ref, sh_ref, mg_ref, wo_ref, g2_ref,
                    rw_ref, rb_ref, x1_ref, h2_ref, route_ref, rt_ref, cnt_ref, carry_ref, *, tb):
    first = (pl.program_id(0) == 0) & (pl.program_id(1) == 0)

    @pl.when(first)
    def _():
        carry_ref[...] = jnp.zeros(carry_ref.shape, F32)

    mla = _rms(ao_ref[0].astype(F32), mg_ref[...])
    cat = jnp.concatenate([mla.astype(BF16), go_ref[0]], axis=1)
    mix = jnp.dot(cat, wo_ref[...], preferred_element_type=F32)
    x1 = x_ref[0] + gt_ref[0] * mix
    x1_ref[0] = x1
    h2 = _rms(x1, g2_ref[...]) * (1.0 + sc_ref[0]) + sh_ref[0]
    h2_ref[0] = _pack_halves(h2)

    h_hi = h2.astype(BF16)
    h_lo = (h2 - h_hi.astype(F32)).astype(BF16)
    main = jnp.dot(h_hi, rw_ref[...], preferred_element_type=F32)
    logits = (main[:, :LANES] + main[:, LANES:]
              + jnp.dot(h_lo, rw_ref[:, :LANES], preferred_element_type=F32) + rb_ref[...])
    lane = lax.broadcasted_iota(jnp.int32, (tb, LANES), 1).astype(F32)
    work = jnp.where(lane < N_EXPERTS, logits, -jnp.inf)
    vals, idxs = [], []
    onehot = jnp.zeros((tb, LANES), F32)
    for _ in range(TOP_K):
        mx = jnp.max(work, axis=-1, keepdims=True)
        ix = jnp.min(jnp.where(work == mx, lane, float(LANES)), axis=-1, keepdims=True)
        sel = lane == ix
        onehot = jnp.where(sel, 1.0, onehot)
        work = jnp.where(sel, -jnp.inf, work)
        vals.append(mx)
        idxs.append(ix)
    exps = [jnp.exp(v - vals[0]) for v in vals]
    den = exps[0] + exps[1] + exps[2] + exps[3]

    r = lax.broadcasted_iota(jnp.int32, (tb, tb), 0)
    c = lax.broadcasted_iota(jnp.int32, (tb, tb), 1)
    tri = jnp.where(c < r, 1.0, 0.0).astype(BF16)
    before = jnp.dot(tri, onehot.astype(BF16), preferred_element_type=F32) + carry_ref[...]
    route = jnp.zeros((tb, LANES), F32)
    for kk in range(TOP_K):
        rank = jnp.sum(jnp.where(lane == idxs[kk], before, 0.0), axis=-1, keepdims=True)
        route = jnp.where(lane == kk, idxs[kk], route)
        route = jnp.where(lane == TOP_K + kk, rank, route)
        route = jnp.where(lane == 2 * TOP_K + kk, exps[kk] / den, route)
    route_ref[0] = route
    rt_ref[0] = route.T[:ROUTE_ROWS, :]
    total = carry_ref[...] + jnp.sum(onehot, axis=0, keepdims=True)
    carry_ref[...] = total
    cnt_ref[...] = total


def _out_projection(x, attn_o, gdn_o, gt1, sc2, sh2, mla_out_g, w_out, norm2_g, rw, rb, tb, b0, nb):
    _, s, d = x.shape
    const = lambda shape: pl.BlockSpec(shape, lambda i, j: (0,) * len(shape))
    tok_in = lambda w: pl.BlockSpec((1, tb, w), lambda i, j: (i + b0, j, 0))
    tok_out = lambda w: pl.BlockSpec((1, tb, w), lambda i, j: (i, j, 0))
    per_b = pl.BlockSpec((1, 1, d), lambda i, j: (i + b0, 0, 0))
    half = attn_o.shape[-1]
    return pl.pallas_call(
        functools.partial(_outproj_kernel, tb=tb),
        out_shape=(jax.ShapeDtypeStruct((nb, s, d), F32),
                   jax.ShapeDtypeStruct((nb, s, d // 2), F32),
                   jax.ShapeDtypeStruct((nb, s, LANES), F32),
                   jax.ShapeDtypeStruct((nb, ROUTE_ROWS, s), F32),
                   jax.ShapeDtypeStruct((1, LANES), F32)),
        grid=(nb, s // tb),
        in_specs=[tok_in(d), tok_in(half), tok_in(half), per_b, per_b, per_b, const((1, half)),
                  const(w_out.shape), const((1, d)), const(rw.shape), const((1, LANES))],
        out_specs=(tok_out(d), tok_out(d // 2), tok_out(LANES),
                   pl.BlockSpec((1, ROUTE_ROWS, tb), lambda i, j: (i, 0, j)), const((1, LANES))),
        scratch_shapes=[pltpu.VMEM((1, LANES), F32)],
        compiler_params=_cparams(("arbitrary", "arbitrary")),
        name="out_projection_router",
    )(x, attn_o, gdn_o, gt1, sc2, sh2, mla_out_g.reshape(1, half), w_out,
      norm2_g.reshape(1, d), rw, rb)


def _sc_mesh():
    return plsc.VectorSubcoreMesh(core_axis_name="c", subcore_axis_name="s",
                                  num_cores=SC_CORES, num_subcores=SC_SUBCORES)


def _sc_worker():
    return lax.axis_index("s") * SC_CORES + lax.axis_index("c")


def _dispatch_rows(h2, dest_km, rows):
    t, d = h2.shape
    per_worker = t // SC_WORKERS
    n_chunks = per_worker // SC_CHUNK

    @functools.partial(
        pl.kernel, out_type=jax.ShapeDtypeStruct((rows, d), h2.dtype), mesh=_sc_mesh(),
        scratch_types=[pltpu.VMEM((SC_CHUNK,), jnp.int32), pltpu.VMEM((SC_CHUNK, d), h2.dtype)],
        name="moe_dispatch")
    def run(h2_hbm, dest_hbm, xs_hbm, idx_v, rows_v):
        base_w = _sc_worker() * per_worker

        @pl.loop(0, n_chunks)
        def _(ci):
            base = pl.multiple_of(base_w + ci * SC_CHUNK, SC_CHUNK)
            pltpu.sync_copy(h2_hbm.at[pl.ds(base, SC_CHUNK)], rows_v)
            for kk in range(TOP_K):
                pltpu.sync_copy(dest_hbm.at[kk, pl.ds(base, SC_CHUNK)], idx_v)
                pltpu.sync_copy(rows_v, xs_hbm.at[idx_v])

    return run(h2, dest_km)


def _collect_rows(y_rows, dest_km):
    _, d = y_rows.shape
    t = dest_km.shape[1]
    per_worker = t // SC_WORKERS
    n_chunks = per_worker // SC_CHUNK

    @functools.partial(
        pl.kernel, out_type=jax.ShapeDtypeStruct((TOP_K, t, d), y_rows.dtype), mesh=_sc_mesh(),
        scratch_types=[pltpu.VMEM((SC_CHUNK,), jnp.int32), pltpu.VMEM((SC_CHUNK, d), y_rows.dtype)],
        name="moe_collect")
    def run(y_hbm, dest_hbm, out_hbm, idx_v, rows_v):
        base_w = _sc_worker() * per_worker

        @pl.loop(0, n_chunks)
        def _(ci):
            base = pl.multiple_of(base_w + ci * SC_CHUNK, SC_CHUNK)
            for kk in range(TOP_K):
                pltpu.sync_copy(dest_hbm.at[kk, pl.ds(base, SC_CHUNK)], idx_v)
                pltpu.sync_copy(y_hbm.at[idx_v], rows_v)
                pltpu.sync_copy(rows_v, out_hbm.at[kk, pl.ds(base, SC_CHUNK)])

    return run(y_rows, dest_km)


def _experts_kernel(be_ref, nu_ref, nxt_ref, slot_ref, live_ref, xs_ref, wgu_hbm, bgu_ref, wd_hbm,
                    bd_ref, y_ref, wgu_f32, wd_f32, wgu_bf, wd_bf, sem):
    i = pl.program_id(0)
    e = be_ref[i]
    new_expert = (i == 0) | (e != be_ref[jnp.maximum(i - 1, 0)])

    def fetch(expert, slot):
        return (pltpu.make_async_copy(wgu_hbm.at[expert], wgu_f32.at[slot], sem.at[0, slot]),
                pltpu.make_async_copy(wd_hbm.at[expert], wd_f32.at[slot], sem.at[1, slot]))

    @pl.when(i == 0)
    def _():
        for cp in fetch(e, slot_ref[e]):
            cp.start()

    @pl.when(new_expert)
    def _():
        slot = slot_ref[e]
        for cp in fetch(e, slot):
            cp.wait()
        wgu_bf[...] = wgu_f32[slot].astype(BF16)
        wd_bf[...] = wd_f32[slot].astype(BF16)
        nxt = nxt_ref[e]

        @pl.when(nxt >= 0)
        def _():
            for cp in fetch(nxt, 1 - slot):
                cp.start()

    def mlp(m):
        x_lo, x_hi = _unpack_halves(xs_ref[0:m, :])
        half = x_lo.shape[-1]
        gu = (jnp.dot(x_lo.astype(BF16), wgu_bf[:half, :], preferred_element_type=F32)
              + jnp.dot(x_hi.astype(BF16), wgu_bf[half:, :], preferred_element_type=F32)
              + bgu_ref[0])
        gate = jnp.minimum(gu[:, :D_EXPERT], SWIGLU_LIMIT)
        up = jnp.clip(gu[:, D_EXPERT:], -SWIGLU_LIMIT, SWIGLU_LIMIT)
        act = (up + 1.0) * (gate * _sigmoid(SWIGLU_ALPHA * gate))
        y = jnp.dot(act.astype(BF16), wd_bf[...], preferred_element_type=F32) + bd_ref[0]
        y_ref[0:m, :] = _pack_halves(y)

    quarter = xs_ref.shape[0] // ROW_QUARTERS
    for nq in range(1, ROW_QUARTERS + 1):
        @pl.when((i < nu_ref[0]) & (live_ref[i] == nq))
        def _(nq=nq):
            mlp(nq * quarter)


def _experts(blk_e, n_used, next_e, slot_e, live, xs, wgu, bgu, wd, bd, bm):
    rows, half = xs.shape
    d = 2 * half
    n_blocks = rows // bm
    row_map = lambda i, be, nu, nx, sl, lv: (jnp.maximum(jnp.minimum(i, nu[0] - 1), 0), 0)
    exp_map = lambda i, be, nu, nx, sl, lv: (be[i], 0, 0)
    grid_spec = pltpu.PrefetchScalarGridSpec(
        num_scalar_prefetch=5,
        grid=(n_blocks,),
        in_specs=[pl.BlockSpec((bm, half), row_map),
                  pl.BlockSpec(memory_space=pl.ANY),
                  pl.BlockSpec((1, 1, 2 * D_EXPERT), exp_map),
                  pl.BlockSpec(memory_space=pl.ANY),
                  pl.BlockSpec((1, 1, d), exp_map)],
        out_specs=pl.BlockSpec((bm, half), row_map),
        scratch_shapes=[pltpu.VMEM((2, d, 2 * D_EXPERT), F32),
                        pltpu.VMEM((2, D_EXPERT, d), F32),
                        pltpu.VMEM((d, 2 * D_EXPERT), BF16),
                        pltpu.VMEM((D_EXPERT, d), BF16),
                        pltpu.SemaphoreType.DMA((2, 2))])
    return pl.pallas_call(
        _experts_kernel,
        out_shape=jax.ShapeDtypeStruct((rows, half), F32),
        grid_spec=grid_spec,
        compiler_params=_cparams(("arbitrary",)),
        name="expert_mlp",
    )(blk_e, n_used, next_e, slot_e, live, xs, wgu, bgu, wd, bd)


def _final_kernel(x1_ref, yk_ref, route_ref, gt_ref, fg_ref, *rest):
    o_ref = rest[-1]
    route = route_ref[0]
    ffn = jnp.zeros(x1_ref.shape[1:], F32)
    for kk in range(TOP_K):
        wk = route[:, 2 * TOP_K + kk:2 * TOP_K + kk + 1]
        y_lo, y_hi = _unpack_halves(yk_ref[kk, 0])
        ffn = ffn + wk * jnp.concatenate([y_lo, y_hi], axis=1)
    x2 = x1_ref[0] + gt_ref[0] * ffn
    o_ref[0] = _rms(x2, fg_ref[...])


def _final(x1, yk, route, gt2, final_g, tb, b0, b_total, prev):
    nb, s, d = x1.shape
    in_specs = [pl.BlockSpec((1, tb, d), lambda i, j: (i, j, 0)),
                pl.BlockSpec((TOP_K, 1, tb, d // 2), lambda i, j: (0, i, j, 0)),
                pl.BlockSpec((1, tb, LANES), lambda i, j: (i, j, 0)),
                pl.BlockSpec((1, 1, d), lambda i, j: (i + b0, 0, 0)),
                pl.BlockSpec((1, d), lambda i, j: (0, 0))]
    args = [x1, yk, route, gt2, final_g.reshape(1, d)]
    aliases = {}
    if prev is not None:
        in_specs.append(pl.BlockSpec(memory_space=pl.ANY))
        args.append(prev)
        aliases = {len(args) - 1: 0}
    return pl.pallas_call(
        _final_kernel,
        out_shape=jax.ShapeDtypeStruct((b_total, s, d), F32),
        grid=(nb, s // tb),
        in_specs=in_specs,
        out_specs=pl.BlockSpec((1, tb, d), lambda i, j: (i + b0, j, 0)),
        input_output_aliases=aliases,
        compiler_params=_cparams(("parallel", "parallel")),
        name="combine_final_norm",
    )(*args)


def _rot_cols(w):
    half = MLA_ROPE // 2
    return jnp.concatenate([-w[..., half:], w[..., :half]], axis=-1)


def _prep_weights(w_in, w_q_b, w_kv_b, conv_w, A_log, dt_bias):
    d = w_in.shape[0]
    cuts = [MLA_Q_LORA, MLA_KV_LORA, MLA_ROPE, GDN_HEADS * GDN_DK, GDN_HEADS * GDN_DK,
            GDN_HEADS * GDN_DV, GDN_HEADS * GDN_DV, GDN_HEADS, GDN_HEADS]
    offs = [0]
    for cw in cuts:
        offs.append(offs[-1] + cw)
    part = lambda i: w_in[:, offs[i]:offs[i + 1]]
    k_pe = part(2)
    misc = jnp.concatenate(
        [part(7), part(8), jnp.zeros((d, MLA_NOPE - 2 * GDN_HEADS), w_in.dtype), k_pe,
         _rot_cols(k_pe)], axis=1)
    heads = lambda w, n: w.reshape(w.shape[0], GDN_HEADS // GDN_GW, GDN_GW * n)
    slab = jnp.concatenate([heads(part(3), GDN_DK), heads(part(4), GDN_DK),
                            heads(part(5), GDN_DV), heads(part(6), GDN_DV)], axis=-1)
    w1 = jnp.concatenate([part(0), part(1), misc, slab.reshape(d, GDN_HEADS * GDN_SLAB)],
                         axis=1).astype(BF16)

    wq3 = w_q_b.reshape(MLA_Q_LORA, MLA_HEADS, MLA_QK)
    pe = wq3[..., MLA_NOPE:]
    wq = jnp.concatenate([wq3[..., :MLA_NOPE], pe, _rot_cols(pe)], axis=-1)
    wq = wq.reshape(MLA_Q_LORA, MLA_HEADS * HEAD_SLOT).astype(BF16)

    wkv3 = w_kv_b.reshape(MLA_KV_LORA, MLA_HEADS, MLA_NOPE + MLA_V)
    wk = jnp.concatenate([wkv3[..., :MLA_NOPE],
                          jnp.zeros((MLA_KV_LORA, MLA_HEADS, HEAD_SLOT - MLA_NOPE), w_kv_b.dtype)],
                         axis=-1).reshape(MLA_KV_LORA, MLA_HEADS * HEAD_SLOT)
    wv = wkv3[..., MLA_NOPE:].reshape(MLA_KV_LORA, MLA_HEADS * MLA_V)
    wkv = jnp.concatenate([wk, wv], axis=1).astype(BF16)

    nk = GDN_HEADS * GDN_DK
    cheads = lambda w, n: w.reshape(CONV_WIDTH, GDN_HEADS // GDN_GW, GDN_GW * n)
    conv_slab = jnp.concatenate(
        [cheads(conv_w[:, :nk], GDN_DK), cheads(conv_w[:, nk:2 * nk], GDN_DK),
         cheads(conv_w[:, 2 * nk:], GDN_DV),
         jnp.zeros((CONV_WIDTH, GDN_HEADS // GDN_GW, GDN_GW * GDN_DV), conv_w.dtype)],
        axis=-1).reshape(CONV_WIDTH, GDN_HEADS * GDN_SLAB)

    half = MLA_ROPE // 2
    inv_freq = ROPE_THETA ** (-jnp.arange(half, dtype=F32) / half)
    lane = jnp.arange(LANES)
    freq = jnp.where(lane[None, :] // MLA_ROPE == jnp.arange(ROPE_GROUPS)[:, None],
                     jnp.tile(inv_freq, LANES // half)[None, :], 0.0)
    padl = lambda a: jnp.concatenate([a.astype(F32), jnp.zeros((LANES - a.shape[0],), F32)]).reshape(1, LANES)
    return w1, wq, wkv, conv_slab, freq, padl(A_log), padl(dt_bias)


def _layer(x, mod, positions, w_in, q_norm_g, w_q_b, kv_norm_g, w_kv_b, mla_out_g, conv_w,
           A_log, dt_bias, gdn_norm_g, w_out, norm1_g, norm2_g, router_w, router_b, w_gate_up,
           b_gate_up, w_down, b_down, final_g):
    b, s, d = x.shape
    t = b * s
    sh1, sc1, gt1, sh2, sc2, gt2 = [m.reshape(b, 1, d) for m in jnp.split(mod, 6, axis=-1)]
    w1, wq, wkv, conv_slab, freq, alog, dtb = _prep_weights(w_in, w_q_b, w_kv_b, conv_w, A_log,
                                                            dt_bias)
    tb = min(256, s)
    q, k, v, slab, gb = _in_projection(x, positions, sc1, sh1, norm1_g, w1, q_norm_g, wq,
                                       kv_norm_g, wkv, freq, alog, dtb, min(512, s))
    attn_o = _attention(q, k, v, min(512, s), 8)
    gdn_o = _gdn(slab, gb, conv_slab, gdn_norm_g, min(512, s))

    rw = jnp.concatenate([router_w, jnp.zeros((d, LANES - N_EXPERTS), router_w.dtype)], axis=1)
    rw_hi = rw.astype(BF16)
    rw = jnp.concatenate([rw_hi, (rw - rw_hi.astype(F32)).astype(BF16)], axis=1)
    rb = jnp.concatenate([router_b, jnp.zeros((LANES - N_EXPERTS,), router_b.dtype)]).reshape(1, LANES)
    w_out_bf = w_out.astype(BF16)
    bgu = b_gate_up.reshape(N_EXPERTS, 1, -1)
    bd = b_down.reshape(N_EXPERTS, 1, -1)
    n_parts = MOE_PARTS if b % MOE_PARTS == 0 else 1
    nb = b // n_parts
    out = None
    for part in range(n_parts):
        b0 = part * nb
        x1, h2, route, route_t, counts = _out_projection(x, attn_o, gdn_o, gt1, sc2, sh2,
                                                         mla_out_g, w_out_bf, norm2_g, rw, rb,
                                                         min(512, s), b0, nb)
        yk = _moe(h2, route_t, counts, w_gate_up, bgu, w_down, bd)
        out = _final(x1, yk, route, gt2, final_g, tb, b0, b, out)
    return out


def _moe(h2, route_t, counts, w_gate_up, bgu, w_down, bd):
    nb, s, half = h2.shape
    t = nb * s
    bm = EXPERT_ROWS
    slot_major = lambda r0: jnp.swapaxes(route_t[:, r0:r0 + TOP_K, :], 0, 1).reshape(TOP_K, t)
    idx = slot_major(0).astype(jnp.int32)
    rank = slot_major(TOP_K).astype(jnp.int32)
    cnt = counts[0, :N_EXPERTS].astype(jnp.int32)
    padded = ((cnt + bm - 1) // bm) * bm
    pend = jnp.cumsum(padded)
    pstart = pend - padded
    base = jnp.zeros_like(idx)
    for e in range(N_EXPERTS):
        base = jnp.where(idx == e, pstart[e], base)
    dest_km = base + rank
    n_blocks = (t * TOP_K + N_EXPERTS * (bm - 1) + bm - 1) // bm
    rows = n_blocks * bm
    n_used = (pend[-1] // bm).astype(jnp.int32).reshape(1)
    blk_start = jnp.arange(n_blocks, dtype=jnp.int32) * bm
    blk_e = jnp.minimum(jnp.sum(blk_start[:, None] >= pend[None, :], axis=1), N_EXPERTS - 1)
    last_e = blk_e[jnp.maximum(n_used[0] - 1, 0)]
    blk_e = jnp.where(jnp.arange(n_blocks) < n_used[0], blk_e, last_e).astype(jnp.int32)

    present = padded > 0
    eidx = jnp.arange(N_EXPERTS, dtype=jnp.int32)
    later = present[None, :] & (eidx[None, :] > eidx[:, None])
    next_e = jnp.where(jnp.any(later, axis=1), jnp.argmax(later, axis=1), -1).astype(jnp.int32)
    slot_e = ((jnp.cumsum(present.astype(jnp.int32)) - 1) % 2).astype(jnp.int32)

    own = blk_e[:, None] == jnp.arange(N_EXPERTS, dtype=jnp.int32)[None, :]
    pick = lambda tab: jnp.sum(jnp.where(own, tab[None, :], 0), axis=1)
    in_blk = jnp.clip(pick(cnt) - (blk_start - pick(pstart)), 0, bm)
    quarter = bm // ROW_QUARTERS
    live = jnp.clip((in_blk + quarter - 1) // quarter, 1, ROW_QUARTERS).astype(jnp.int32)

    xs = _dispatch_rows(h2.reshape(t, half), dest_km, rows)
    y_rows = _experts(blk_e, n_used, next_e, slot_e, live, xs, w_gate_up, bgu, w_down, bd, bm)
    return _collect_rows(y_rows, dest_km).reshape(TOP_K, nb, s, half)


def kernel(x, c, positions, ada_w, ada_b, norm1_g, w_in, q_norm_g, w_q_b, kv_norm_g, w_kv_b, mla_out_g, conv_w, A_log, dt_bias, gdn_norm_g, w_out, norm2_g, router_w, router_b, w_gate_up, b_gate_up, w_down, b_down, final_g):
    depth = ada_w.shape[0]
    assert depth == 1
    l = 0
    mod = _modulation(c, ada_w[l], ada_b[l])
    return _layer(x, mod, positions, w_in[l], q_norm_g[l], w_q_b[l], kv_norm_g[l],
                  w_kv_b[l], mla_out_g[l], conv_w[l], A_log[l], dt_bias[l], gdn_norm_g[l],
                  w_out[l], norm1_g[l], norm2_g[l], router_w[l], router_b[l],
                  w_gate_up[l], b_gate_up[l], w_down[l], b_down[l], final_g)
```

```python
import functools

import jax
import jax.numpy as jnp
from jax import lax
from jax.experimental import pallas as pl
from jax.experimental.pallas import tpu as pltpu
from jax.experimental.pallas import tpu_sc as plsc

F32 = jnp.float32
BF16 = jnp.bfloat16

D_MODEL = 1024
EPS = 1e-6
MLA_HEADS = 8
MLA_NOPE = 64
MLA_ROPE = 32
MLA_V = 64
MLA_QK = MLA_NOPE + MLA_ROPE
MLA_Q_LORA = 384
MLA_KV_LORA = 256
ROPE_THETA = 10000.0
GDN_HEADS = 8
GDN_DK = 64
GDN_DV = 64
CONV_WIDTH = 4
CHUNK = 64
N_EXPERTS = 32
TOP_K = 4
D_EXPERT = D_MODEL
SWIGLU_ALPHA = 1.702
SWIGLU_LIMIT = 7.0

LANES = 128
HEAD_SLOT = 128
GDN_SLAB = 256
GDN_GW = 4
CONV_BLOCK = 256
MISC_W = 128
PROJ_W = MLA_Q_LORA + MLA_KV_LORA + MISC_W + GDN_HEADS * GDN_SLAB
VMEM_LIMIT = 56 * 1024 * 1024
TOKEN_BLOCK = 512
MOE_PARTS = 2
EXPERT_ROWS = 512
ROW_QUARTERS = 4
SC_CORES = 2
SC_SUBCORES = 16
SC_WORKERS = SC_CORES * SC_SUBCORES
SC_CHUNK = 128
ROPE_GROUPS = LANES // MLA_ROPE
ROUTE_ROWS = 16
ATT_STRIP = 32
LOG2E = 1.4426950408889634


def _cparams(sem):
    return pltpu.CompilerParams(dimension_semantics=sem, vmem_limit_bytes=VMEM_LIMIT)


def _rms(x, g):
    return x * lax.rsqrt(jnp.mean(x * x, axis=-1, keepdims=True) + EPS) * g


def _sigmoid(x):
    return 1.0 / (1.0 + jnp.exp(-x))


def _silu(x):
    return x * _sigmoid(x)


def _pack_halves(x):
    w = x.shape[-1] // 2
    lo = lax.bitcast_convert_type(x[:, :w].astype(BF16).astype(F32), jnp.uint32) >> 16
    hi = lax.bitcast_convert_type(x[:, w:].astype(BF16).astype(F32), jnp.uint32) & jnp.uint32(0xFFFF0000)
    return lax.bitcast_convert_type(lo | hi, F32)


def _unpack_halves(p):
    u = lax.bitcast_convert_type(p, jnp.uint32)
    return (lax.bitcast_convert_type(u << 16, F32),
            lax.bitcast_convert_type(u & jnp.uint32(0xFFFF0000), F32))


def _mod_kernel(c_ref, w_ref, b_ref, o_ref):
    c = c_ref[...]
    o_ref[...] = jnp.dot(_silu(c), w_ref[...], preferred_element_type=F32,
                         precision=lax.Precision.HIGHEST) + b_ref[...]


def _modulation(c, ada_w, ada_b):
    b, d = c.shape
    n = ada_w.shape[1]
    return pl.pallas_call(
        _mod_kernel,
        out_shape=jax.ShapeDtypeStruct((b, n), F32),
        grid=(n // d,),
        in_specs=[pl.BlockSpec((b, d), lambda j: (0, 0)),
                  pl.BlockSpec((d, d), lambda j: (0, j)),
                  pl.BlockSpec((1, d), lambda j: (0, j))],
        out_specs=pl.BlockSpec((b, d), lambda j: (0, j)),
        compiler_params=_cparams(("arbitrary",)),
        name="adaln_mod",
    )(c, ada_w, ada_b.reshape(1, n))


def _inproj_kernel(x_ref, pos_ref, sc_ref, sh_ref, g1_ref, w1_ref, qg_ref, wq_ref, kvg_ref,
                   wkv_ref, freq_ref, alog_ref, dtb_ref,
                   q_ref, k_ref, v_ref, gdn_ref, gb_ref):
    x = x_ref[0]
    h = _rms(x, g1_ref[...]) * (1.0 + sc_ref[0]) + sh_ref[0]
    proj = jnp.dot(h.astype(BF16), w1_ref[...], preferred_element_type=F32)

    tb = x.shape[0]
    lane = lax.broadcasted_iota(jnp.int32, (tb, LANES), 1)
    in_rope = (lane >= MLA_NOPE) & (lane < MLA_QK)
    pos = pos_ref[0].astype(F32)
    quarter = tb // ROPE_GROUPS
    ang = pos[0:quarter] * freq_ref[0:1, :]
    for g in range(1, ROPE_GROUPS):
        ang = ang + pos[g * quarter:(g + 1) * quarter] * freq_ref[g:g + 1, :]
    cos_p = jnp.cos(ang)
    sin_p = jnp.sin(ang)
    unpack = lambda t: jnp.concatenate(
        [pltpu.roll(t, (MLA_NOPE - MLA_ROPE * g) % LANES, axis=1) for g in range(ROPE_GROUPS)],
        axis=0)
    cosv = unpack(cos_p)
    sinv = jnp.where(in_rope, unpack(sin_p), 0.0)

    scale = (MLA_QK ** -0.5) * LOG2E
    qn = _rms(proj[:, :MLA_Q_LORA], qg_ref[...])
    qa = jnp.dot(qn.astype(BF16), wq_ref[...], preferred_element_type=F32)
    cq = jnp.where(lane < MLA_NOPE, scale, jnp.where(in_rope, cosv * scale, 0.0))
    sq = sinv * scale
    cq_t = jnp.concatenate([cq] * MLA_HEADS, axis=1)
    sq_t = jnp.concatenate([sq] * MLA_HEADS, axis=1)
    width = MLA_HEADS * HEAD_SLOT
    q = qa * cq_t + pltpu.roll(qa, width - MLA_ROPE, axis=1) * sq_t
    q_ref[0] = q.astype(BF16)

    kvn = _rms(proj[:, MLA_Q_LORA:MLA_Q_LORA + MLA_KV_LORA], kvg_ref[...])
    kva = jnp.dot(kvn.astype(BF16), wkv_ref[...], preferred_element_type=F32)
    misc = proj[:, MLA_Q_LORA + MLA_KV_LORA:MLA_Q_LORA + MLA_KV_LORA + MISC_W]
    kp = misc * jnp.where(in_rope, cosv, 0.0) + pltpu.roll(misc, MISC_W - MLA_ROPE, axis=1) * sinv
    k = kva[:, :width] + jnp.concatenate([kp] * MLA_HEADS, axis=1)
    k_ref[0] = k.astype(BF16)
    v_ref[0] = kva[:, width:].astype(BF16)

    z = misc + dtb_ref[...]
    softplus = jnp.maximum(z, 0.0) + jnp.log(1.0 + jnp.exp(-jnp.abs(z)))
    g = -jnp.exp(alog_ref[...]) * softplus
    gb_ref[0] = jnp.where(lane < GDN_HEADS, g, _sigmoid(misc))

    gdn_ref[0] = proj[:, MLA_Q_LORA + MLA_KV_LORA + MISC_W:].astype(BF16)


def _in_projection(x, positions, sc1, sh1, norm1_g, w1, q_norm_g, wq, kv_norm_g, wkv,
                   freq, alog, dtb, tb):
    b, s, d = x.shape
    hw = MLA_HEADS * HEAD_SLOT
    const = lambda shape: pl.BlockSpec(shape, lambda i, j: (0,) * len(shape))
    tok = lambda w: pl.BlockSpec((1, tb, w), lambda i, j: (i, j, 0))
    per_b = pl.BlockSpec((1, 1, d), lambda i, j: (i, 0, 0))
    return pl.pallas_call(
        _inproj_kernel,
        out_shape=(jax.ShapeDtypeStruct((b, s, hw), BF16),
                   jax.ShapeDtypeStruct((b, s, hw), BF16),
                   jax.ShapeDtypeStruct((b, s, MLA_HEADS * MLA_V), BF16),
                   jax.ShapeDtypeStruct((b, s, GDN_HEADS * GDN_SLAB), BF16),
                   jax.ShapeDtypeStruct((b, s, MISC_W), F32)),
        grid=(b, s // tb),
        in_specs=[tok(d), tok(1), per_b, per_b, const((1, d)), const(w1.shape),
                  const((1, MLA_Q_LORA)), const(wq.shape), const((1, MLA_KV_LORA)),
                  const(wkv.shape), const((ROPE_GROUPS, LANES)), const((1, LANES)),
                  const((1, LANES))],
        out_specs=(tok(hw), tok(hw), tok(MLA_HEADS * MLA_V), tok(GDN_HEADS * GDN_SLAB),
                   tok(MISC_W)),
        compiler_params=_cparams(("parallel", "parallel")),
        name="in_projection",
    )(x, positions.reshape(b, s, 1), sc1, sh1, norm1_g.reshape(1, d), w1,
      q_norm_g.reshape(1, -1), wq, kv_norm_g.reshape(1, -1), wkv, freq, alog, dtb)


def _attn_kernel(q_ref, k_ref, v_ref, o_ref, s_ref, p_ref, m_ref, l_ref, acc_ref, *, tq, nh):
    qi = pl.program_id(2)
    m_ref[...] = jnp.full(m_ref.shape, -jnp.inf, F32)
    l_ref[...] = jnp.zeros(l_ref.shape, F32)
    acc_ref[...] = jnp.zeros(acc_ref.shape, F32)

    def step(off, masked):
        segs = [(0, tq // 2, tq // 2), (tq // 2, tq, tq)] if masked else [(0, tq, tq)]
        width = lambda r: next(nc for r0, r1, nc in segs if r0 <= r * ATT_STRIP < r1)
        for h in range(nh):
            for r0, r1, nc in segs:
                kj = k_ref[0, pl.ds(off, nc), h * HEAD_SLOT:(h + 1) * HEAD_SLOT]
                s_ref[h, r0:r1, 0:nc] = lax.dot_general(
                    q_ref[0, r0:r1, h * HEAD_SLOT:(h + 1) * HEAD_SLOT], kj,
                    (((1,), (1,)), ((), ())), preferred_element_type=F32)
        n_strips = tq // ATT_STRIP

        def strip(h, r):
            rows = slice(r * ATT_STRIP, (r + 1) * ATT_STRIP)
            nc = width(r)
            sc = s_ref[h, rows, 0:nc]
            if masked:
                rid = lax.broadcasted_iota(jnp.int32, (ATT_STRIP, nc), 0) + r * ATT_STRIP
                cid = lax.broadcasted_iota(jnp.int32, (ATT_STRIP, nc), 1)
                sc = jnp.where(cid <= rid, sc, -jnp.inf)
            return rows, sc

        rep = lambda col: jnp.broadcast_to(col, (col.shape[0], LANES))
        m_new = []
        for h in range(nh):
            mx = jnp.concatenate([rep(jnp.max(strip(h, r)[1], axis=-1, keepdims=True))
                                  for r in range(n_strips)], axis=0)
            m_new.append(jnp.maximum(m_ref[h], mx))
        for h in range(nh):
            v_lanes = slice((h // 2) * 2 * MLA_V, (h // 2 + 1) * 2 * MLA_V)
            sums = []
            for r in range(n_strips):
                rows, sc = strip(h, r)
                nc = width(r)
                p = jnp.exp2(sc - jnp.concatenate([m_new[h][rows]] * (nc // LANES), axis=1))
                p_ref[h, rows, 0:nc] = p.astype(BF16)
                sums.append(rep(jnp.sum(p, axis=-1, keepdims=True)))
            alpha = jnp.exp2(m_ref[h] - m_new[h])
            l_ref[h] = alpha * l_ref[h] + jnp.concatenate(sums, axis=0)
            m_ref[h] = m_new[h]
            for r0, r1, nc in segs:
                acc_ref[h, r0:r1, :] = alpha[r0:r1] * acc_ref[h, r0:r1, :] + jnp.dot(
                    p_ref[h, r0:r1, 0:nc], v_ref[0, pl.ds(off, nc), v_lanes],
                    preferred_element_type=F32)

    def body(j, carry):
        step(pl.multiple_of(j * tq, tq), False)
        return carry

    lax.fori_loop(0, qi, body, 0)
    step(pl.multiple_of(qi * tq, tq), True)
    lane = lax.broadcasted_iota(jnp.int32, (tq, 2 * MLA_V), 1)
    outs = []
    for p in range(nh // 2):
        o0 = acc_ref[2 * p] / l_ref[2 * p]
        o1 = acc_ref[2 * p + 1] / l_ref[2 * p + 1]
        outs.append(jnp.where(lane < MLA_V, o0, o1))
    o_ref[0] = jnp.concatenate(outs, axis=1).astype(o_ref.dtype)


def _attention(q, k, v, tq, nh):
    b, s, _ = q.shape
    groups = MLA_HEADS // nh
    return pl.pallas_call(
        functools.partial(_attn_kernel, tq=tq, nh=nh),
        out_shape=jax.ShapeDtypeStruct((b, s, MLA_HEADS * MLA_V), BF16),
        grid=(b, groups, s // tq),
        in_specs=[pl.BlockSpec((1, tq, nh * HEAD_SLOT), lambda i, p, j: (i, j, p)),
                  pl.BlockSpec((1, s, nh * HEAD_SLOT), lambda i, p, j: (i, 0, p)),
                  pl.BlockSpec((1, s, nh * MLA_V), lambda i, p, j: (i, 0, p))],
        out_specs=pl.BlockSpec((1, tq, nh * MLA_V), lambda i, p, j: (i, j, p)),
        scratch_shapes=[pltpu.VMEM((nh, tq, tq), F32), pltpu.VMEM((nh, tq, tq), BF16),
                        pltpu.VMEM((nh, tq, LANES), F32), pltpu.VMEM((nh, tq, LANES), F32),
                        pltpu.VMEM((nh, tq, 2 * MLA_V), F32)],
        compiler_params=_cparams(("parallel", "parallel", "arbitrary")),
        name="mla_attention",
    )(q, k, v)


def _gdn_kernel(slab_ref, gb_ref, cw_ref, ng_ref, shf_ref, ind_ref, sel_ref, o_ref, tail_ref,
                state_ref, *, ts, groups):
    si = pl.program_id(1)
    gw = GDN_GW * GDN_DK
    hist_rows = 8

    @pl.when(si == 0)
    def _():
        tail_ref[...] = jnp.zeros(tail_ref.shape, F32)
        state_ref[...] = jnp.zeros(state_ref.shape, F32)

    xs_bf = slab_ref[0]
    xs = xs_bf.astype(F32)
    width = xs.shape[1]
    sub = shf_ref.shape[1]
    conv_parts = []
    for r0 in range(0, ts, sub):
        xs_sub = xs[r0:r0 + sub]
        conv = cw_ref[CONV_WIDTH - 1:CONV_WIDTH, :] * xs_sub
        hist = jnp.zeros((hist_rows, width), F32)
        for j in range(CONV_WIDTH - 1):
            wj = cw_ref[j:j + 1, :]
            conv = conv + wj * jnp.dot(shf_ref[j], xs_bf[r0:r0 + sub],
                                       preferred_element_type=F32)
            start = hist_rows - (CONV_WIDTH - 1) + j
            hist = hist + wj * tail_ref[start:start + hist_rows, :]
        conv_parts += [conv[:hist_rows] + hist, conv[hist_rows:]]
        tail_ref[0:hist_rows, :] = xs_sub[sub - hist_rows:sub, :]
    act = _silu(jnp.concatenate(conv_parts, axis=0))

    gb = gb_ref[0]
    g1 = gb.astype(BF16)
    r1 = gb - g1.astype(F32)
    g2 = r1.astype(BF16)
    g3 = (r1 - g2.astype(F32)).astype(BF16)
    sel = sel_ref[...]
    gate_w = (jnp.dot(g1, sel, preferred_element_type=F32)
              + jnp.dot(g2, sel, preferred_element_type=F32)
              + jnp.dot(g3, sel, preferred_element_type=F32))

    n_chunks = ts // CHUNK
    row = lax.broadcasted_iota(jnp.int32, (CHUNK, gw), 0)
    col = lax.broadcasted_iota(jnp.int32, (CHUNK, gw), 1) % GDN_DK
    tri_incl = col <= row
    tri_strict = col < row
    eye = col == row
    brow = lax.broadcasted_iota(jnp.int32, (gw, gw), 0) // GDN_DK
    bcol = lax.broadcasted_iota(jnp.int32, (gw, gw), 1) // GDN_DK
    same_head = brow == bcol
    crow = lax.broadcasted_iota(jnp.int32, (ts, gw), 0) % CHUNK

    def split2(x):
        hi = x.astype(BF16)
        return hi, (x - hi.astype(F32)).astype(BF16)

    def head_sums(x):
        hi, lo = split2(x)
        return (jnp.dot(hi, ind_ref[...], preferred_element_type=F32)
                + jnp.dot(lo, ind_ref[...], preferred_element_type=F32))

    def blockdiag(x):
        return jnp.where(same_head, jnp.concatenate([x.astype(BF16)] * GDN_GW, axis=0), 0)

    def wdot(a, b):
        return jnp.dot(a.astype(BF16), blockdiag(b), preferred_element_type=F32)

    prep = []
    for gi in range(groups):
        base = gi * GDN_GW * GDN_SLAB
        q_raw = act[:, base:base + gw]
        k_raw = act[:, base + gw:base + 2 * gw]
        v_all = act[:, base + 2 * gw:base + 3 * gw]
        z_all = xs[:, base + 3 * gw:base + 4 * gw]
        q_all = q_raw * (lax.rsqrt(head_sums(q_raw * q_raw) + EPS) * (GDN_DK ** -0.5))
        k_all = k_raw * lax.rsqrt(head_sums(k_raw * k_raw) + EPS)
        g_w = gate_w[:, gi * gw:(gi + 1) * gw]
        b_w = gate_w[:, (groups + gi) * gw:(groups + gi + 1) * gw]
        gc_w = g_w
        shift = 1
        while shift < CHUNK:
            rolled = pltpu.roll(gc_w, shift, axis=0)
            gc_w = gc_w + jnp.where(crow >= shift, rolled, 0.0)
            shift *= 2
        prep.append((q_all, k_all, v_all, z_all, b_w, gc_w, jnp.exp(gc_w)))

    units = [(gi, c) for c in range(n_chunks) for gi in range(groups)]
    a_low, qk, yu0, yw0, qd, kt, cd = ({} for _ in range(7))
    for u in units:
        gi, c = u
        q_all, k_all, v_all, _, b_w, gc_w, eg_w = prep[gi]
        sl = slice(c * CHUNK, (c + 1) * CHUNK)
        qc, kc, vc, bc, gcum, eg = q_all[sl], k_all[sl], v_all[sl], b_w[sl], gc_w[sl], eg_w[sl]
        grow = jnp.sum(jnp.where(eye, gcum, 0.0), axis=0, keepdims=True)
        diff = gcum - grow
        decay = jnp.where(tri_incl, jnp.exp(jnp.where(tri_incl, diff, 0.0)), 0.0)
        k_beta = kc * bc
        kq = jnp.concatenate([k_beta, qc], axis=0).astype(BF16)
        kk = lax.dot_general(kq, blockdiag(kc), (((1,), (1,)), ((), ())),
                             preferred_element_type=F32)
        a_low[u] = jnp.where(tri_strict, kk[:CHUNK] * decay, 0.0)
        qk[u] = jnp.where(tri_incl, kk[CHUNK:] * decay, 0.0)
        yu0[u] = vc * bc
        yw0[u] = k_beta * eg
        qd[u] = qc * eg
        glast = gcum[CHUNK - 1:CHUNK, :]
        kt[u] = kc * jnp.exp(glast - gcum)
        cd[u] = eg[CHUNK - 1:CHUNK, :]

    blk = 2
    tinv = {u: jnp.where(eye, 1.0, 0.0) - jnp.where((row // blk) == (col // blk), a_low[u], 0.0)
            for u in units}
    while blk < CHUNK:
        in_big = (row // (2 * blk)) == (col // (2 * blk))
        off_mask = in_big & ((row // blk) != (col // blk))
        left = {u: wdot(tinv[u], jnp.where(off_mask, a_low[u], 0.0)) for u in units}
        tinv = {u: tinv[u] - wdot(left[u], tinv[u]) for u in units}
        blk *= 2

    yu = {u: wdot(tinv[u], yu0[u]) for u in units}
    yw = {u: wdot(tinv[u], yw0[u]) for u in units}
    tdot = lambda a, b: jnp.where(same_head, lax.dot_general(
        a.astype(BF16), b.astype(BF16), (((0,), (0,)), ((), ())),
        preferred_element_type=F32), 0.0)
    q_bd = {u: tdot(kt[u], yu[u]) for u in units}
    p_bd = {u: tdot(kt[u], yw[u]) for u in units}
    qu = {u: wdot(qk[u], yu[u]) for u in units}
    qw = {u: wdot(qk[u], yw[u]) for u in units}

    outs = [[] for _ in range(groups)]
    states = [state_ref[gi] for gi in range(groups)]
    for c in range(n_chunks):
        for gi in range(groups):
            u = (gi, c)
            sb = states[gi].astype(BF16)
            r_mat = qd[u] - qw[u]
            outs[gi].append(jnp.dot(r_mat.astype(BF16), sb, preferred_element_type=F32) + qu[u])
            states[gi] = (states[gi] * cd[u]
                          - jnp.dot(p_bd[u].astype(BF16), sb, preferred_element_type=F32)
                          + q_bd[u])
    finals = []
    for gi in range(groups):
        state_ref[gi] = states[gi]
        o_all = jnp.concatenate(outs[gi], axis=0)
        inv = lax.rsqrt(head_sums(o_all * o_all) * (1.0 / GDN_DV) + EPS)
        finals.append(o_all * inv * ng_ref[...] * _silu(prep[gi][3]))
    o_ref[0] = jnp.concatenate(finals, axis=1).astype(o_ref.dtype)


def _gdn(slab, gb, conv_slab, gdn_norm_g, ts):
    b, s, width = slab.shape
    groups = GDN_HEADS // GDN_GW
    gw = GDN_GW * GDN_DK
    sub = min(ts, CONV_BLOCK)
    r = jnp.arange(sub)
    shifts = jnp.stack([(r[:, None] - r[None, :]) == (CONV_WIDTH - 1 - j)
                        for j in range(CONV_WIDTH - 1)]).astype(BF16)
    li = jnp.arange(gw)
    ind = (li[:, None] // GDN_DK == li[None, :] // GDN_DK).astype(BF16)
    src = jnp.arange(LANES)[:, None]
    dst = jnp.arange(2 * GDN_HEADS * GDN_DK)[None, :]
    sel = (src == dst // GDN_DK).astype(BF16)
    return pl.pallas_call(
        functools.partial(_gdn_kernel, ts=ts, groups=groups),
        out_shape=jax.ShapeDtypeStruct((b, s, GDN_HEADS * GDN_DV), BF16),
        grid=(b, s // ts),
        in_specs=[pl.BlockSpec((1, ts, width), lambda i, j: (i, j, 0)),
                  pl.BlockSpec((1, ts, MISC_W), lambda i, j: (i, j, 0)),
                  pl.BlockSpec((CONV_WIDTH, width), lambda i, j: (0, 0)),
                  pl.BlockSpec((1, gw), lambda i, j: (0, 0)),
                  pl.BlockSpec((CONV_WIDTH - 1, sub, sub), lambda i, j: (0, 0, 0)),
                  pl.BlockSpec((gw, gw), lambda i, j: (0, 0)),
                  pl.BlockSpec((LANES, 2 * GDN_HEADS * GDN_DK), lambda i, j: (0, 0))],
        out_specs=pl.BlockSpec((1, ts, GDN_HEADS * GDN_DV), lambda i, j: (i, j, 0)),
        scratch_shapes=[pltpu.VMEM((16, width), F32),
                        pltpu.VMEM((groups, gw, gw), F32)],
        compiler_params=_cparams(("parallel", "arbitrary")),
        name="gated_deltanet",
    )(slab, gb, conv_slab, jnp.tile(gdn_norm_g, GDN_GW).reshape(1, gw), shifts, ind, sel)


def _outproj_kernel(x_ref, ao_ref, go_ref, gt_ref, sc_ref, sh_ref, mg_ref, wo_ref, g2_ref,
                    rw_ref, rb_ref, x1_ref, h2_ref, route_ref, rt_ref, cnt_ref, carry_ref, *, tb):
    first = (pl.program_id(0) == 0) & (pl.program_id(1) == 0)

    @pl.when(first)
    def _():
        carry_ref[...] = jnp.zeros(carry_ref.shape, F32)

    mla = _rms(ao_ref[0].astype(F32), mg_ref[...])
    cat = jnp.concatenate([mla.astype(BF16), go_ref[0]], axis=1)
    mix = jnp.dot(cat, wo_ref[...], preferred_element_type=F32)
    x1 = x_ref[0] + gt_ref[0] * mix
    x1_ref[0] = x1
    h2 = _rms(x1, g2_ref[...]) * (1.0 + sc_ref[0]) + sh_ref[0]
    h2_ref[0] = _pack_halves(h2)

    h_hi = h2.astype(BF16)
    h_lo = (h2 - h_hi.astype(F32)).astype(BF16)
    main = jnp.dot(h_hi, rw_ref[...], preferred_element_type=F32)
    logits = (main[:, :LANES] + main[:, LANES:]
              + jnp.dot(h_lo, rw_ref[:, :LANES], preferred_element_type=F32) + rb_ref[...])
    lane = lax.broadcasted_iota(jnp.int32, (tb, LANES), 1).astype(F32)
    work = jnp.where(lane < N_EXPERTS, logits, -jnp.inf)
    vals, idxs = [], []
    onehot = jnp.zeros((tb, LANES), F32)
    for _ in range(TOP_K):
        mx = jnp.max(work, axis=-1, keepdims=True)
        ix = jnp.min(jnp.where(work == mx, lane, float(LANES)), axis=-1, keepdims=True)
        sel = lane == ix
        onehot = jnp.where(sel, 1.0, onehot)
        work = jnp.where(sel, -jnp.inf, work)
        vals.append(mx)
        idxs.append(ix)
    exps = [jnp.exp(v - vals[0]) for v in vals]
    den = exps[0] + exps[1] + exps[2] + exps[3]

    r = lax.broadcasted_iota(jnp.int32, (tb, tb), 0)
    c = lax.broadcasted_iota(jnp.int32, (tb, tb), 1)
    tri = jnp.where(c < r, 1.0, 0.0).astype(BF16)
    before = jnp.dot(tri, onehot.astype(BF16), preferred_element_type=F32) + carry_ref[...]
    route = jnp.zeros((tb, LANES), F32)
    for kk in range(TOP_K):
        rank = jnp.sum(jnp.where(lane == idxs[kk], before, 0.0), axis=-1, keepdims=True)
        route = jnp.where(lane == kk, idxs[kk], route)
        route = jnp.where(lane == TOP_K + kk, rank, route)
        route = jnp.where(lane == 2 * TOP_K + kk, exps[kk] / den, route)
    route_ref[0] = route
    rt_ref[0] = route.T[:ROUTE_ROWS, :]
    total = carry_ref[...] + jnp.sum(onehot, axis=0, keepdims=True)
    carry_ref[...] = total
    cnt_ref[...] = total


def _out_projection(x, attn_o, gdn_o, gt1, sc2, sh2, mla_out_g, w_out, norm2_g, rw, rb, tb, b0, nb):
    _, s, d = x.shape
    const = lambda shape: pl.BlockSpec(shape, lambda i, j: (0,) * len(shape))
    tok_in = lambda w: pl.BlockSpec((1, tb, w), lambda i, j: (i + b0, j, 0))
    tok_out = lambda w: pl.BlockSpec((1, tb, w), lambda i, j: (i, j, 0))
    per_b = pl.BlockSpec((1, 1, d), lambda i, j: (i + b0, 0, 0))
    half = attn_o.shape[-1]
    return pl.pallas_call(
        functools.partial(_outproj_kernel, tb=tb),
        out_shape=(jax.ShapeDtypeStruct((nb, s, d), F32),
                   jax.ShapeDtypeStruct((nb, s, d // 2), F32),
                   jax.ShapeDtypeStruct((nb, s, LANES), F32),
                   jax.ShapeDtypeStruct((nb, ROUTE_ROWS, s), F32),
                   jax.ShapeDtypeStruct((1, LANES), F32)),
        grid=(nb, s // tb),
        in_specs=[tok_in(d), tok_in(half), tok_in(half), per_b, per_b, per_b, const((1, half)),
                  const(w_out.shape), const((1, d)), const(rw.shape), const((1, LANES))],
        out_specs=(tok_out(d), tok_out(d // 2), tok_out(LANES),
                   pl.BlockSpec((1, ROUTE_ROWS, tb), lambda i, j: (i, 0, j)), const((1, LANES))),
        scratch_shapes=[pltpu.VMEM((1, LANES), F32)],
        compiler_params=_cparams(("arbitrary", "arbitrary")),
        name="out_projection_router",
    )(x, attn_o, gdn_o, gt1, sc2, sh2, mla_out_g.reshape(1, half), w_out,
      norm2_g.reshape(1, d), rw, rb)


def _sc_mesh():
    return plsc.VectorSubcoreMesh(core_axis_name="c", subcore_axis_name="s",
                                  num_cores=SC_CORES, num_subcores=SC_SUBCORES)


def _sc_worker():
    return lax.axis_index("s") * SC_CORES + lax.axis_index("c")


def _dispatch_rows(h2, dest_km, rows):
    t, d = h2.shape
    assert t % (SC_WORKERS * SC_CHUNK) == 0, "tokens must split evenly over subcores and chunks"
    per_worker = t // SC_WORKERS
    n_chunks = per_worker // SC_CHUNK

    @functools.partial(
        pl.kernel, out_type=jax.ShapeDtypeStruct((rows, d), h2.dtype), mesh=_sc_mesh(),
        scratch_types=[pltpu.VMEM((SC_CHUNK,), jnp.int32), pltpu.VMEM((SC_CHUNK, d), h2.dtype)],
        name="moe_dispatch")
    def run(h2_hbm, dest_hbm, xs_hbm, idx_v, rows_v):
        base_w = _sc_worker() * per_worker

        @pl.loop(0, n_chunks)
        def _(ci):
            base = pl.multiple_of(base_w + ci * SC_CHUNK, SC_CHUNK)
            pltpu.sync_copy(h2_hbm.at[pl.ds(base, SC_CHUNK)], rows_v)
            for kk in range(TOP_K):
                pltpu.sync_copy(dest_hbm.at[kk, pl.ds(base, SC_CHUNK)], idx_v)
                pltpu.sync_copy(rows_v, xs_hbm.at[idx_v])

    return run(h2, dest_km)


def _collect_rows(y_rows, dest_km):
    _, d = y_rows.shape
    t = dest_km.shape[1]
    assert t % (SC_WORKERS * SC_CHUNK) == 0, "tokens must split evenly over subcores and chunks"
    per_worker = t // SC_WORKERS
    n_chunks = per_worker // SC_CHUNK

    @functools.partial(
        pl.kernel, out_type=jax.ShapeDtypeStruct((TOP_K, t, d), y_rows.dtype), mesh=_sc_mesh(),
        scratch_types=[pltpu.VMEM((SC_CHUNK,), jnp.int32), pltpu.VMEM((SC_CHUNK, d), y_rows.dtype)],
        name="moe_collect")
    def run(y_hbm, dest_hbm, out_hbm, idx_v, rows_v):
        base_w = _sc_worker() * per_worker

        @pl.loop(0, n_chunks)
        def _(ci):
            base = pl.multiple_of(base_w + ci * SC_CHUNK, SC_CHUNK)
            for kk in range(TOP_K):
                pltpu.sync_copy(dest_hbm.at[kk, pl.ds(base, SC_CHUNK)], idx_v)
                pltpu.sync_copy(y_hbm.at[idx_v], rows_v)
                pltpu.sync_copy(rows_v, out_hbm.at[kk, pl.ds(base, SC_CHUNK)])

    return run(y_rows, dest_km)


def _experts_kernel(be_ref, nu_ref, nxt_ref, slot_ref, live_ref, xs_ref, wgu_hbm, bgu_ref, wd_hbm,
                    bd_ref, y_ref, wgu_f32, wd_f32, wgu_bf, wd_bf, sem):
    i = pl.program_id(0)
    e = be_ref[i]
    new_expert = (i == 0) | (e != be_ref[jnp.maximum(i - 1, 0)])

    def fetch(expert, slot):
        return (pltpu.make_async_copy(wgu_hbm.at[expert], wgu_f32.at[slot], sem.at[0, slot]),
                pltpu.make_async_copy(wd_hbm.at[expert], wd_f32.at[slot], sem.at[1, slot]))

    @pl.when(i == 0)
    def _():
        for cp in fetch(e, slot_ref[e]):
            cp.start()

    @pl.when(new_expert)
    def _():
        slot = slot_ref[e]
        for cp in fetch(e, slot):
            cp.wait()
        wgu_bf[...] = wgu_f32[slot].astype(BF16)
        wd_bf[...] = wd_f32[slot].astype(BF16)
        nxt = nxt_ref[e]

        @pl.when(nxt >= 0)
        def _():
            for cp in fetch(nxt, 1 - slot):
                cp.start()

    def mlp(m):
        x_lo, x_hi = _unpack_halves(xs_ref[0:m, :])
        half = x_lo.shape[-1]
        gu = (jnp.dot(x_lo.astype(BF16), wgu_bf[:half, :], preferred_element_type=F32)
              + jnp.dot(x_hi.astype(BF16), wgu_bf[half:, :], preferred_element_type=F32)
              + bgu_ref[0])
        gate = jnp.minimum(gu[:, :D_EXPERT], SWIGLU_LIMIT)
        up = jnp.clip(gu[:, D_EXPERT:], -SWIGLU_LIMIT, SWIGLU_LIMIT)
        act = (up + 1.0) * (gate * _sigmoid(SWIGLU_ALPHA * gate))
        y = jnp.dot(act.astype(BF16), wd_bf[...], preferred_element_type=F32) + bd_ref[0]
        y_ref[0:m, :] = _pack_halves(y)

    quarter = xs_ref.shape[0] // ROW_QUARTERS
    for nq in range(1, ROW_QUARTERS + 1):
        @pl.when((i < nu_ref[0]) & (live_ref[i] == nq))
        def _(nq=nq):
            mlp(nq * quarter)


def _experts(blk_e, n_used, next_e, slot_e, live, xs, wgu, bgu, wd, bd, bm):
    rows, half = xs.shape
    d = 2 * half
    n_blocks = rows // bm
    row_map = lambda i, be, nu, nx, sl, lv: (jnp.maximum(jnp.minimum(i, nu[0] - 1), 0), 0)
    exp_map = lambda i, be, nu, nx, sl, lv: (be[i], 0, 0)
    grid_spec = pltpu.PrefetchScalarGridSpec(
        num_scalar_prefetch=5,
        grid=(n_blocks,),
        in_specs=[pl.BlockSpec((bm, half), row_map),
                  pl.BlockSpec(memory_space=pl.ANY),
                  pl.BlockSpec((1, 1, 2 * D_EXPERT), exp_map),
                  pl.BlockSpec(memory_space=pl.ANY),
                  pl.BlockSpec((1, 1, d), exp_map)],
        out_specs=pl.BlockSpec((bm, half), row_map),
        scratch_shapes=[pltpu.VMEM((2, d, 2 * D_EXPERT), F32),
                        pltpu.VMEM((2, D_EXPERT, d), F32),
                        pltpu.VMEM((d, 2 * D_EXPERT), BF16),
                        pltpu.VMEM((D_EXPERT, d), BF16),
                        pltpu.SemaphoreType.DMA((2, 2))])
    return pl.pallas_call(
        _experts_kernel,
        out_shape=jax.ShapeDtypeStruct((rows, half), F32),
        grid_spec=grid_spec,
        compiler_params=_cparams(("arbitrary",)),
        name="expert_mlp",
    )(blk_e, n_used, next_e, slot_e, live, xs, wgu, bgu, wd, bd)


def _final_kernel(x1_ref, yk_ref, route_ref, gt_ref, fg_ref, *rest):
    o_ref = rest[-1]
    route = route_ref[0]
    ffn = jnp.zeros(x1_ref.shape[1:], F32)
    for kk in range(TOP_K):
        wk = route[:, 2 * TOP_K + kk:2 * TOP_K + kk + 1]
        y_lo, y_hi = _unpack_halves(yk_ref[kk, 0])
        ffn = ffn + wk * jnp.concatenate([y_lo, y_hi], axis=1)
    x2 = x1_ref[0] + gt_ref[0] * ffn
    o_ref[0] = _rms(x2, fg_ref[...])


def _final(x1, yk, route, gt2, final_g, tb, b0, b_total, prev):
    nb, s, d = x1.shape
    in_specs = [pl.BlockSpec((1, tb, d), lambda i, j: (i, j, 0)),
                pl.BlockSpec((TOP_K, 1, tb, d // 2), lambda i, j: (0, i, j, 0)),
                pl.BlockSpec((1, tb, LANES), lambda i, j: (i, j, 0)),
                pl.BlockSpec((1, 1, d), lambda i, j: (i + b0, 0, 0)),
                pl.BlockSpec((1, d), lambda i, j: (0, 0))]
    args = [x1, yk, route, gt2, final_g.reshape(1, d)]
    aliases = {}
    if prev is not None:
        in_specs.append(pl.BlockSpec(memory_space=pl.ANY))
        args.append(prev)
        aliases = {len(args) - 1: 0}
    return pl.pallas_call(
        _final_kernel,
        out_shape=jax.ShapeDtypeStruct((b_total, s, d), F32),
        grid=(nb, s // tb),
        in_specs=in_specs,
        out_specs=pl.BlockSpec((1, tb, d), lambda i, j: (i + b0, j, 0)),
        input_output_aliases=aliases,
        compiler_params=_cparams(("parallel", "parallel")),
        name="combine_final_norm",
    )(*args)


def _rot_cols(w):
    half = MLA_ROPE // 2
    return jnp.concatenate([-w[..., half:], w[..., :half]], axis=-1)


def _prep_weights(w_in, w_q_b, w_kv_b, conv_w, A_log, dt_bias):
    d = w_in.shape[0]
    cuts = [MLA_Q_LORA, MLA_KV_LORA, MLA_ROPE, GDN_HEADS * GDN_DK, GDN_HEADS * GDN_DK,
            GDN_HEADS * GDN_DV, GDN_HEADS * GDN_DV, GDN_HEADS, GDN_HEADS]
    offs = [0]
    for cw in cuts:
        offs.append(offs[-1] + cw)
    part = lambda i: w_in[:, offs[i]:offs[i + 1]]
    k_pe = part(2)
    misc = jnp.concatenate(
        [part(7), part(8), jnp.zeros((d, MLA_NOPE - 2 * GDN_HEADS), w_in.dtype), k_pe,
         _rot_cols(k_pe)], axis=1)
    heads = lambda w, n: w.reshape(w.shape[0], GDN_HEADS // GDN_GW, GDN_GW * n)
    slab = jnp.concatenate([heads(part(3), GDN_DK), heads(part(4), GDN_DK),
                            heads(part(5), GDN_DV), heads(part(6), GDN_DV)], axis=-1)
    w1 = jnp.concatenate([part(0), part(1), misc, slab.reshape(d, GDN_HEADS * GDN_SLAB)],
                         axis=1).astype(BF16)

    wq3 = w_q_b.reshape(MLA_Q_LORA, MLA_HEADS, MLA_QK)
    pe = wq3[..., MLA_NOPE:]
    wq = jnp.concatenate([wq3[..., :MLA_NOPE], pe, _rot_cols(pe)], axis=-1)
    wq = wq.reshape(MLA_Q_LORA, MLA_HEADS * HEAD_SLOT).astype(BF16)

    wkv3 = w_kv_b.reshape(MLA_KV_LORA, MLA_HEADS, MLA_NOPE + MLA_V)
    wk = jnp.concatenate([wkv3[..., :MLA_NOPE],
                          jnp.zeros((MLA_KV_LORA, MLA_HEADS, HEAD_SLOT - MLA_NOPE), w_kv_b.dtype)],
                         axis=-1).reshape(MLA_KV_LORA, MLA_HEADS * HEAD_SLOT)
    wv = wkv3[..., MLA_NOPE:].reshape(MLA_KV_LORA, MLA_HEADS * MLA_V)
    wkv = jnp.concatenate([wk, wv], axis=1).astype(BF16)

    nk = GDN_HEADS * GDN_DK
    cheads = lambda w, n: w.reshape(CONV_WIDTH, GDN_HEADS // GDN_GW, GDN_GW * n)
    conv_slab = jnp.concatenate(
        [cheads(conv_w[:, :nk], GDN_DK), cheads(conv_w[:, nk:2 * nk], GDN_DK),
         cheads(conv_w[:, 2 * nk:], GDN_DV),
         jnp.zeros((CONV_WIDTH, GDN_HEADS // GDN_GW, GDN_GW * GDN_DV), conv_w.dtype)],
        axis=-1).reshape(CONV_WIDTH, GDN_HEADS * GDN_SLAB)

    half = MLA_ROPE // 2
    inv_freq = ROPE_THETA ** (-jnp.arange(half, dtype=F32) / half)
    lane = jnp.arange(LANES)
    freq = jnp.where(lane[None, :] // MLA_ROPE == jnp.arange(ROPE_GROUPS)[:, None],
                     jnp.tile(inv_freq, LANES // half)[None, :], 0.0)
    padl = lambda a: jnp.concatenate([a.astype(F32), jnp.zeros((LANES - a.shape[0],), F32)]).reshape(1, LANES)
    return w1, wq, wkv, conv_slab, freq, padl(A_log), padl(dt_bias)


def _layer(x, mod, positions, w_in, q_norm_g, w_q_b, kv_norm_g, w_kv_b, mla_out_g, conv_w,
           A_log, dt_bias, gdn_norm_g, w_out, norm1_g, norm2_g, router_w, router_b, w_gate_up,
           b_gate_up, w_down, b_down, final_g):
    b, s, d = x.shape
    t = b * s
    sh1, sc1, gt1, sh2, sc2, gt2 = [m.reshape(b, 1, d) for m in jnp.split(mod, 6, axis=-1)]
    w1, wq, wkv, conv_slab, freq, alog, dtb = _prep_weights(w_in, w_q_b, w_kv_b, conv_w, A_log,
                                                            dt_bias)
    tb = min(TOKEN_BLOCK, s)
    q, k, v, slab, gb = _in_projection(x, positions, sc1, sh1, norm1_g, w1, q_norm_g, wq,
                                       kv_norm_g, wkv, freq, alog, dtb, tb)
    attn_o = _attention(q, k, v, tb, MLA_HEADS)
    gdn_o = _gdn(slab, gb, conv_slab, gdn_norm_g, tb)

    rw = jnp.concatenate([router_w, jnp.zeros((d, LANES - N_EXPERTS), router_w.dtype)], axis=1)
    rw_hi = rw.astype(BF16)
    rw = jnp.concatenate([rw_hi, (rw - rw_hi.astype(F32)).astype(BF16)], axis=1)
    rb = jnp.concatenate([router_b, jnp.zeros((LANES - N_EXPERTS,), router_b.dtype)]).reshape(1, LANES)
    w_out_bf = w_out.astype(BF16)
    bgu = b_gate_up.reshape(N_EXPERTS, 1, -1)
    bd = b_down.reshape(N_EXPERTS, 1, -1)
    n_parts = MOE_PARTS if b % MOE_PARTS == 0 else 1
    nb = b // n_parts
    out = None
    for part in range(n_parts):
        b0 = part * nb
        x1, h2, route, route_t, counts = _out_projection(x, attn_o, gdn_o, gt1, sc2, sh2,
                                                         mla_out_g, w_out_bf, norm2_g, rw, rb,
                                                         tb, b0, nb)
        yk = _moe(h2, route_t, counts, w_gate_up, bgu, w_down, bd)
        out = _final(x1, yk, route, gt2, final_g, tb, b0, b, out)
    return out


def _moe(h2, route_t, counts, w_gate_up, bgu, w_down, bd):
    nb, s, half = h2.shape
    t = nb * s
    bm = EXPERT_ROWS
    slot_major = lambda r0: jnp.swapaxes(route_t[:, r0:r0 + TOP_K, :], 0, 1).reshape(TOP_K, t)
    idx = slot_major(0).astype(jnp.int32)
    rank = slot_major(TOP_K).astype(jnp.int32)
    cnt = counts[0, :N_EXPERTS].astype(jnp.int32)
    padded = ((cnt + bm - 1) // bm) * bm
    pend = jnp.cumsum(padded)
    pstart = pend - padded
    base = jnp.zeros_like(idx)
    for e in range(N_EXPERTS):
        base = jnp.where(idx == e, pstart[e], base)
    dest_km = base + rank
    n_blocks = (t * TOP_K + N_EXPERTS * (bm - 1) + bm - 1) // bm
    rows = n_blocks * bm
    n_used = (pend[-1] // bm).astype(jnp.int32).reshape(1)
    blk_start = jnp.arange(n_blocks, dtype=jnp.int32) * bm
    blk_e = jnp.minimum(jnp.sum(blk_start[:, None] >= pend[None, :], axis=1), N_EXPERTS - 1)
    last_e = blk_e[jnp.maximum(n_used[0] - 1, 0)]
    blk_e = jnp.where(jnp.arange(n_blocks) < n_used[0], blk_e, last_e).astype(jnp.int32)

    present = padded > 0
    eidx = jnp.arange(N_EXPERTS, dtype=jnp.int32)
    later = present[None, :] & (eidx[None, :] > eidx[:, None])
    next_e = jnp.where(jnp.any(later, axis=1), jnp.argmax(later, axis=1), -1).astype(jnp.int32)
    slot_e = ((jnp.cumsum(present.astype(jnp.int32)) - 1) % 2).astype(jnp.int32)

    own = blk_e[:, None] == jnp.arange(N_EXPERTS, dtype=jnp.int32)[None, :]
    pick = lambda tab: jnp.sum(jnp.where(own, tab[None, :], 0), axis=1)
    in_blk = jnp.clip(pick(cnt) - (blk_start - pick(pstart)), 0, bm)
    quarter = bm // ROW_QUARTERS
    live = jnp.clip((in_blk + quarter - 1) // quarter, 1, ROW_QUARTERS).astype(jnp.int32)

    xs = _dispatch_rows(h2.reshape(t, half), dest_km, rows)
    y_rows = _experts(blk_e, n_used, next_e, slot_e, live, xs, w_gate_up, bgu, w_down, bd, bm)
    return _collect_rows(y_rows, dest_km).reshape(TOP_K, nb, s, half)


def kernel(x, c, positions, ada_w, ada_b, norm1_g, w_in, q_norm_g, w_q_b, kv_norm_g, w_kv_b, mla_out_g, conv_w, A_log, dt_bias, gdn_norm_g, w_out, norm2_g, router_w, router_b, w_gate_up, b_gate_up, w_down, b_down, final_g):
    depth = ada_w.shape[0]
    assert depth == 1
    l = 0
    mod = _modulation(c, ada_w[l], ada_b[l])
    return _layer(x, mod, positions, w_in[l], q_norm_g[l], w_q_b[l], kv_norm_g[l],
                  w_kv_b[l], mla_out_g[l], conv_w[l], A_log[l], dt_bias[l], gdn_norm_g[l],
                  w_out[l], norm1_g[l], norm2_g[l], router_w[l], router_b[l],
                  w_gate_up[l], b_gate_up[l], w_down[l], b_down[l], final_g)
```

```python
import functools

import jax
import jax.numpy as jnp
from jax import lax
from jax.experimental import pallas as pl
from jax.experimental.pallas import tpu as pltpu
from jax.experimental.pallas import tpu_sc as plsc

F32 = jnp.float32
BF16 = jnp.bfloat16

D_MODEL = 1024
EPS = 1e-6
MLA_HEADS = 8
MLA_NOPE = 64
MLA_ROPE = 32
MLA_V = 64
MLA_QK = MLA_NOPE + MLA_ROPE
MLA_Q_LORA = 384
MLA_KV_LORA = 256
ROPE_THETA = 10000.0
GDN_HEADS = 8
GDN_DK = 64
GDN_DV = 64
CONV_WIDTH = 4
CHUNK = 64
N_EXPERTS = 32
TOP_K = 4
D_EXPERT = D_MODEL
SWIGLU_ALPHA = 1.702
SWIGLU_LIMIT = 7.0

LANES = 128
HEAD_SLOT = 128
GDN_SLAB = 256
GDN_GW = 4
CONV_BLOCK = 256
MISC_W = 128
PROJ_W = MLA_Q_LORA + MLA_KV_LORA + MISC_W + GDN_HEADS * GDN_SLAB
VMEM_LIMIT = 56 * 1024 * 1024
TOKEN_BLOCK = 512
MOE_PARTS = 2
EXPERT_ROWS = 1024
MLP_ROWS = 512
ROW_PIECES = 8
SC_CORES = 2
SC_SUBCORES = 16
SC_WORKERS = SC_CORES * SC_SUBCORES
SC_CHUNK = 128
ROPE_GROUPS = LANES // MLA_ROPE
ROUTE_ROWS = 16
ATT_STRIP = 32
LOG2E = 1.4426950408889634


def _cparams(sem):
    return pltpu.CompilerParams(dimension_semantics=sem, vmem_limit_bytes=VMEM_LIMIT)


def _rms(x, g):
    return x * lax.rsqrt(jnp.mean(x * x, axis=-1, keepdims=True) + EPS) * g


def _sigmoid(x):
    return 1.0 / (1.0 + jnp.exp(-x))


def _silu(x):
    return x * _sigmoid(x)


def _pack_halves(x):
    w = x.shape[-1] // 2
    lo = lax.bitcast_convert_type(x[:, :w].astype(BF16).astype(F32), jnp.uint32) >> 16
    hi = lax.bitcast_convert_type(x[:, w:].astype(BF16).astype(F32), jnp.uint32) & jnp.uint32(0xFFFF0000)
    return lax.bitcast_convert_type(lo | hi, F32)


def _unpack_halves(p):
    u = lax.bitcast_convert_type(p, jnp.uint32)
    return (lax.bitcast_convert_type(u << 16, F32),
            lax.bitcast_convert_type(u & jnp.uint32(0xFFFF0000), F32))


def _mod_kernel(c_ref, w_ref, b_ref, o_ref):
    c = c_ref[...]
    o_ref[...] = jnp.dot(_silu(c), w_ref[...], preferred_element_type=F32,
                         precision=lax.Precision.HIGHEST) + b_ref[...]


def _modulation(c, ada_w, ada_b):
    b, d = c.shape
    n = ada_w.shape[1]
    return pl.pallas_call(
        _mod_kernel,
        out_shape=jax.ShapeDtypeStruct((b, n), F32),
        grid=(n // d,),
        in_specs=[pl.BlockSpec((b, d), lambda j: (0, 0)),
                  pl.BlockSpec((d, d), lambda j: (0, j)),
                  pl.BlockSpec((1, d), lambda j: (0, j))],
        out_specs=pl.BlockSpec((b, d), lambda j: (0, j)),
        compiler_params=_cparams(("arbitrary",)),
        name="adaln_mod",
    )(c, ada_w, ada_b.reshape(1, n))


def _inproj_kernel(x_ref, pos_ref, sc_ref, sh_ref, g1_ref, w1_ref, qg_ref, wq_ref, kvg_ref,
                   wkv_ref, freq_ref, alog_ref, dtb_ref,
                   q_ref, k_ref, v_ref, gdn_ref, gb_ref):
    x = x_ref[0]
    h = _rms(x, g1_ref[...]) * (1.0 + sc_ref[0]) + sh_ref[0]
    proj = jnp.dot(h.astype(BF16), w1_ref[...], preferred_element_type=F32)

    tb = x.shape[0]
    lane = lax.broadcasted_iota(jnp.int32, (tb, LANES), 1)
    in_rope = (lane >= MLA_NOPE) & (lane < MLA_QK)
    pos = pos_ref[0].astype(F32)
    quarter = tb // ROPE_GROUPS
    ang = pos[0:quarter] * freq_ref[0:1, :]
    for g in range(1, ROPE_GROUPS):
        ang = ang + pos[g * quarter:(g + 1) * quarter] * freq_ref[g:g + 1, :]
    cos_p = jnp.cos(ang)
    sin_p = jnp.sin(ang)
    unpack = lambda t: jnp.concatenate(
        [pltpu.roll(t, (MLA_NOPE - MLA_ROPE * g) % LANES, axis=1) for g in range(ROPE_GROUPS)],
        axis=0)
    cosv = unpack(cos_p)
    sinv = jnp.where(in_rope, unpack(sin_p), 0.0)

    scale = (MLA_QK ** -0.5) * LOG2E
    qn = _rms(proj[:, :MLA_Q_LORA], qg_ref[...])
    qa = jnp.dot(qn.astype(BF16), wq_ref[...], preferred_element_type=F32)
    cq = jnp.where(lane < MLA_NOPE, scale, jnp.where(in_rope, cosv * scale, 0.0))
    sq = sinv * scale
    cq_t = jnp.concatenate([cq] * MLA_HEADS, axis=1)
    sq_t = jnp.concatenate([sq] * MLA_HEADS, axis=1)
    width = MLA_HEADS * HEAD_SLOT
    q = qa * cq_t + pltpu.roll(qa, width - MLA_ROPE, axis=1) * sq_t
    q_ref[0] = q.astype(BF16)

    kvn = _rms(proj[:, MLA_Q_LORA:MLA_Q_LORA + MLA_KV_LORA], kvg_ref[...])
    kva = jnp.dot(kvn.astype(BF16), wkv_ref[...], preferred_element_type=F32)
    misc = proj[:, MLA_Q_LORA + MLA_KV_LORA:MLA_Q_LORA + MLA_KV_LORA + MISC_W]
    kp = misc * jnp.where(in_rope, cosv, 0.0) + pltpu.roll(misc, MISC_W - MLA_ROPE, axis=1) * sinv
    k = kva[:, :width] + jnp.concatenate([kp] * MLA_HEADS, axis=1)
    k_ref[0] = k.astype(BF16)
    v_ref[0] = kva[:, width:].astype(BF16)

    z = misc + dtb_ref[...]
    softplus = jnp.maximum(z, 0.0) + jnp.log(1.0 + jnp.exp(-jnp.abs(z)))
    g = -jnp.exp(alog_ref[...]) * softplus
    gb_ref[0] = jnp.where(lane < GDN_HEADS, g, _sigmoid(misc))

    gdn_ref[0] = proj[:, MLA_Q_LORA + MLA_KV_LORA + MISC_W:].astype(BF16)


def _in_projection(x, positions, sc1, sh1, norm1_g, w1, q_norm_g, wq, kv_norm_g, wkv,
                   freq, alog, dtb, tb):
    b, s, d = x.shape
    hw = MLA_HEADS * HEAD_SLOT
    const = lambda shape: pl.BlockSpec(shape, lambda i, j: (0,) * len(shape))
    tok = lambda w: pl.BlockSpec((1, tb, w), lambda i, j: (i, j, 0))
    per_b = pl.BlockSpec((1, 1, d), lambda i, j: (i, 0, 0))
    return pl.pallas_call(
        _inproj_kernel,
        out_shape=(jax.ShapeDtypeStruct((b, s, hw), BF16),
                   jax.ShapeDtypeStruct((b, s, hw), BF16),
                   jax.ShapeDtypeStruct((b, s, MLA_HEADS * MLA_V), BF16),
                   jax.ShapeDtypeStruct((b, s, GDN_HEADS * GDN_SLAB), BF16),
                   jax.ShapeDtypeStruct((b, s, MISC_W), F32)),
        grid=(b, s // tb),
        in_specs=[tok(d), tok(1), per_b, per_b, const((1, d)), const(w1.shape),
                  const((1, MLA_Q_LORA)), const(wq.shape), const((1, MLA_KV_LORA)),
                  const(wkv.shape), const((ROPE_GROUPS, LANES)), const((1, LANES)),
                  const((1, LANES))],
        out_specs=(tok(hw), tok(hw), tok(MLA_HEADS * MLA_V), tok(GDN_HEADS * GDN_SLAB),
                   tok(MISC_W)),
        compiler_params=_cparams(("parallel", "parallel")),
        name="in_projection",
    )(x, positions.reshape(b, s, 1), sc1, sh1, norm1_g.reshape(1, d), w1,
      q_norm_g.reshape(1, -1), wq, kv_norm_g.reshape(1, -1), wkv, freq, alog, dtb)


def _attn_kernel(q_ref, k_ref, v_ref, o_ref, s_ref, p_ref, m_ref, l_ref, acc_ref, *, tq, nh):
    qi = pl.program_id(2)
    m_ref[...] = jnp.full(m_ref.shape, -jnp.inf, F32)
    l_ref[...] = jnp.zeros(l_ref.shape, F32)
    acc_ref[...] = jnp.zeros(acc_ref.shape, F32)

    def step(off, masked):
        segs = [(0, tq // 2, tq // 2), (tq // 2, tq, tq)] if masked else [(0, tq, tq)]
        width = lambda r: next(nc for r0, r1, nc in segs if r0 <= r * ATT_STRIP < r1)
        for h in range(nh):
            for r0, r1, nc in segs:
                kj = k_ref[0, pl.ds(off, nc), h * HEAD_SLOT:(h + 1) * HEAD_SLOT]
                s_ref[h, r0:r1, 0:nc] = lax.dot_general(
                    q_ref[0, r0:r1, h * HEAD_SLOT:(h + 1) * HEAD_SLOT], kj,
                    (((1,), (1,)), ((), ())), preferred_element_type=F32)
        n_strips = tq // ATT_STRIP

        def strip(h, r):
            rows = slice(r * ATT_STRIP, (r + 1) * ATT_STRIP)
            nc = width(r)
            sc = s_ref[h, rows, 0:nc]
            if masked:
                rid = lax.broadcasted_iota(jnp.int32, (ATT_STRIP, nc), 0) + r * ATT_STRIP
                cid = lax.broadcasted_iota(jnp.int32, (ATT_STRIP, nc), 1)
                sc = jnp.where(cid <= rid, sc, -jnp.inf)
            return rows, sc

        rep = lambda col: jnp.broadcast_to(col, (col.shape[0], LANES))
        m_new = []
        for h in range(nh):
            mx = jnp.concatenate([rep(jnp.max(strip(h, r)[1], axis=-1, keepdims=True))
                                  for r in range(n_strips)], axis=0)
            m_new.append(jnp.maximum(m_ref[h], mx))
        for h in range(nh):
            v_lanes = slice((h // 2) * 2 * MLA_V, (h // 2 + 1) * 2 * MLA_V)
            sums = []
            for r in range(n_strips):
                rows, sc = strip(h, r)
                nc = width(r)
                p = jnp.exp2(sc - jnp.concatenate([m_new[h][rows]] * (nc // LANES), axis=1))
                p_ref[h, rows, 0:nc] = p.astype(BF16)
                sums.append(rep(jnp.sum(p, axis=-1, keepdims=True)))
            alpha = jnp.exp2(m_ref[h] - m_new[h])
            l_ref[h] = alpha * l_ref[h] + jnp.concatenate(sums, axis=0)
            m_ref[h] = m_new[h]
            for r0, r1, nc in segs:
                acc_ref[h, r0:r1, :] = alpha[r0:r1] * acc_ref[h, r0:r1, :] + jnp.dot(
                    p_ref[h, r0:r1, 0:nc], v_ref[0, pl.ds(off, nc), v_lanes],
                    preferred_element_type=F32)

    def body(j, carry):
        step(pl.multiple_of(j * tq, tq), False)
        return carry

    lax.fori_loop(0, qi, body, 0)
    step(pl.multiple_of(qi * tq, tq), True)
    lane = lax.broadcasted_iota(jnp.int32, (tq, 2 * MLA_V), 1)
    outs = []
    for p in range(nh // 2):
        o0 = acc_ref[2 * p] / l_ref[2 * p]
        o1 = acc_ref[2 * p + 1] / l_ref[2 * p + 1]
        outs.append(jnp.where(lane < MLA_V, o0, o1))
    o_ref[0] = jnp.concatenate(outs, axis=1).astype(o_ref.dtype)


def _attention(q, k, v, tq, nh):
    b, s, _ = q.shape
    groups = MLA_HEADS // nh
    return pl.pallas_call(
        functools.partial(_attn_kernel, tq=tq, nh=nh),
        out_shape=jax.ShapeDtypeStruct((b, s, MLA_HEADS * MLA_V), BF16),
        grid=(b, groups, s // tq),
        in_specs=[pl.BlockSpec((1, tq, nh * HEAD_SLOT), lambda i, p, j: (i, j, p)),
                  pl.BlockSpec((1, s, nh * HEAD_SLOT), lambda i, p, j: (i, 0, p)),
                  pl.BlockSpec((1, s, nh * MLA_V), lambda i, p, j: (i, 0, p))],
        out_specs=pl.BlockSpec((1, tq, nh * MLA_V), lambda i, p, j: (i, j, p)),
        scratch_shapes=[pltpu.VMEM((nh, tq, tq), F32), pltpu.VMEM((nh, tq, tq), BF16),
                        pltpu.VMEM((nh, tq, LANES), F32), pltpu.VMEM((nh, tq, LANES), F32),
                        pltpu.VMEM((nh, tq, 2 * MLA_V), F32)],
        compiler_params=_cparams(("parallel", "parallel", "arbitrary")),
        name="mla_attention",
    )(q, k, v)


def _gdn_kernel(slab_ref, gb_ref, cw_ref, ng_ref, shf_ref, ind_ref, sel_ref, o_ref, tail_ref,
                state_ref, *, ts, groups):
    si = pl.program_id(1)
    gw = GDN_GW * GDN_DK
    hist_rows = 8

    @pl.when(si == 0)
    def _():
        tail_ref[...] = jnp.zeros(tail_ref.shape, F32)
        state_ref[...] = jnp.zeros(state_ref.shape, F32)

    xs_bf = slab_ref[0]
    xs = xs_bf.astype(F32)
    width = xs.shape[1]
    sub = shf_ref.shape[1]
    conv_parts = []
    for r0 in range(0, ts, sub):
        xs_sub = xs[r0:r0 + sub]
        conv = cw_ref[CONV_WIDTH - 1:CONV_WIDTH, :] * xs_sub
        hist = jnp.zeros((hist_rows, width), F32)
        for j in range(CONV_WIDTH - 1):
            wj = cw_ref[j:j + 1, :]
            conv = conv + wj * jnp.dot(shf_ref[j], xs_bf[r0:r0 + sub],
                                       preferred_element_type=F32)
            start = hist_rows - (CONV_WIDTH - 1) + j
            hist = hist + wj * tail_ref[start:start + hist_rows, :]
        conv_parts += [conv[:hist_rows] + hist, conv[hist_rows:]]
        tail_ref[0:hist_rows, :] = xs_sub[sub - hist_rows:sub, :]
    act = _silu(jnp.concatenate(conv_parts, axis=0))

    gb = gb_ref[0]
    g1 = gb.astype(BF16)
    r1 = gb - g1.astype(F32)
    g2 = r1.astype(BF16)
    g3 = (r1 - g2.astype(F32)).astype(BF16)
    sel = sel_ref[...]
    gate_w = (jnp.dot(g1, sel, preferred_element_type=F32)
              + jnp.dot(g2, sel, preferred_element_type=F32)
              + jnp.dot(g3, sel, preferred_element_type=F32))

    n_chunks = ts // CHUNK
    row = lax.broadcasted_iota(jnp.int32, (CHUNK, gw), 0)
    col = lax.broadcasted_iota(jnp.int32, (CHUNK, gw), 1) % GDN_DK
    tri_incl = col <= row
    tri_strict = col < row
    eye = col == row
    brow = lax.broadcasted_iota(jnp.int32, (gw, gw), 0) // GDN_DK
    bcol = lax.broadcasted_iota(jnp.int32, (gw, gw), 1) // GDN_DK
    same_head = brow == bcol
    crow = lax.broadcasted_iota(jnp.int32, (ts, gw), 0) % CHUNK

    def split2(x):
        hi = x.astype(BF16)
        return hi, (x - hi.astype(F32)).astype(BF16)

    def head_sums(x):
        hi, lo = split2(x)
        return (jnp.dot(hi, ind_ref[...], preferred_element_type=F32)
                + jnp.dot(lo, ind_ref[...], preferred_element_type=F32))

    def blockdiag(x):
        return jnp.where(same_head, jnp.concatenate([x.astype(BF16)] * GDN_GW, axis=0), 0)

    def wdot(a, b):
        return jnp.dot(a.astype(BF16), blockdiag(b), preferred_element_type=F32)

    prep = []
    for gi in range(groups):
        base = gi * GDN_GW * GDN_SLAB
        q_raw = act[:, base:base + gw]
        k_raw = act[:, base + gw:base + 2 * gw]
        v_all = act[:, base + 2 * gw:base + 3 * gw]
        z_all = xs[:, base + 3 * gw:base + 4 * gw]
        q_all = q_raw * (lax.rsqrt(head_sums(q_raw * q_raw) + EPS) * (GDN_DK ** -0.5))
        k_all = k_raw * lax.rsqrt(head_sums(k_raw * k_raw) + EPS)
        g_w = gate_w[:, gi * gw:(gi + 1) * gw]
        b_w = gate_w[:, (groups + gi) * gw:(groups + gi + 1) * gw]
        gc_w = g_w
        shift = 1
        while shift < CHUNK:
            rolled = pltpu.roll(gc_w, shift, axis=0)
            gc_w = gc_w + jnp.where(crow >= shift, rolled, 0.0)
            shift *= 2
        prep.append((q_all, k_all, v_all, z_all, b_w, gc_w, jnp.exp(gc_w)))

    units = [(gi, c) for c in range(n_chunks) for gi in range(groups)]
    a_low, qk, yu0, yw0, qd, kt, cd = ({} for _ in range(7))
    for u in units:
        gi, c = u
        q_all, k_all, v_all, _, b_w, gc_w, eg_w = prep[gi]
        sl = slice(c * CHUNK, (c + 1) * CHUNK)
        qc, kc, vc, bc, gcum, eg = q_all[sl], k_all[sl], v_all[sl], b_w[sl], gc_w[sl], eg_w[sl]
        grow = jnp.sum(jnp.where(eye, gcum, 0.0), axis=0, keepdims=True)
        diff = gcum - grow
        decay = jnp.where(tri_incl, jnp.exp(jnp.where(tri_incl, diff, 0.0)), 0.0)
        k_beta = kc * bc
        kq = jnp.concatenate([k_beta, qc], axis=0).astype(BF16)
        kk = lax.dot_general(kq, blockdiag(kc), (((1,), (1,)), ((), ())),
                             preferred_element_type=F32)
        a_low[u] = jnp.where(tri_strict, kk[:CHUNK] * decay, 0.0)
        qk[u] = jnp.where(tri_incl, kk[CHUNK:] * decay, 0.0)
        yu0[u] = vc * bc
        yw0[u] = k_beta * eg
        qd[u] = qc * eg
        glast = gcum[CHUNK - 1:CHUNK, :]
        kt[u] = kc * jnp.exp(glast - gcum)
        cd[u] = eg[CHUNK - 1:CHUNK, :]

    blk = 2
    tinv = {u: jnp.where(eye, 1.0, 0.0) - jnp.where((row // blk) == (col // blk), a_low[u], 0.0)
            for u in units}
    while blk < CHUNK:
        in_big = (row // (2 * blk)) == (col // (2 * blk))
        off_mask = in_big & ((row // blk) != (col // blk))
        left = {u: wdot(tinv[u], jnp.where(off_mask, a_low[u], 0.0)) for u in units}
        tinv = {u: tinv[u] - wdot(left[u], tinv[u]) for u in units}
        blk *= 2

    yu = {u: wdot(tinv[u], yu0[u]) for u in units}
    yw = {u: wdot(tinv[u], yw0[u]) for u in units}
    tdot = lambda a, b: jnp.where(same_head, lax.dot_general(
        a.astype(BF16), b.astype(BF16), (((0,), (0,)), ((), ())),
        preferred_element_type=F32), 0.0)
    q_bd = {u: tdot(kt[u], yu[u]) for u in units}
    p_bd = {u: tdot(kt[u], yw[u]) for u in units}
    qu = {u: wdot(qk[u], yu[u]) for u in units}
    qw = {u: wdot(qk[u], yw[u]) for u in units}

    outs = [[] for _ in range(groups)]
    states = [state_ref[gi] for gi in range(groups)]
    for c in range(n_chunks):
        for gi in range(groups):
            u = (gi, c)
            sb = states[gi].astype(BF16)
            r_mat = qd[u] - qw[u]
            outs[gi].append(jnp.dot(r_mat.astype(BF16), sb, preferred_element_type=F32) + qu[u])
            states[gi] = (states[gi] * cd[u]
                          - jnp.dot(p_bd[u].astype(BF16), sb, preferred_element_type=F32)
                          + q_bd[u])
    finals = []
    for gi in range(groups):
        state_ref[gi] = states[gi]
        o_all = jnp.concatenate(outs[gi], axis=0)
        inv = lax.rsqrt(head_sums(o_all * o_all) * (1.0 / GDN_DV) + EPS)
        finals.append(o_all * inv * ng_ref[...] * _silu(prep[gi][3]))
    o_ref[0] = jnp.concatenate(finals, axis=1).astype(o_ref.dtype)


def _gdn(slab, gb, conv_slab, gdn_norm_g, ts):
    b, s, width = slab.shape
    groups = GDN_HEADS // GDN_GW
    gw = GDN_GW * GDN_DK
    sub = min(ts, CONV_BLOCK)
    r = jnp.arange(sub)
    shifts = jnp.stack([(r[:, None] - r[None, :]) == (CONV_WIDTH - 1 - j)
                        for j in range(CONV_WIDTH - 1)]).astype(BF16)
    li = jnp.arange(gw)
    ind = (li[:, None] // GDN_DK == li[None, :] // GDN_DK).astype(BF16)
    src = jnp.arange(LANES)[:, None]
    dst = jnp.arange(2 * GDN_HEADS * GDN_DK)[None, :]
    sel = (src == dst // GDN_DK).astype(BF16)
    return pl.pallas_call(
        functools.partial(_gdn_kernel, ts=ts, groups=groups),
        out_shape=jax.ShapeDtypeStruct((b, s, GDN_HEADS * GDN_DV), BF16),
        grid=(b, s // ts),
        in_specs=[pl.BlockSpec((1, ts, width), lambda i, j: (i, j, 0)),
                  pl.BlockSpec((1, ts, MISC_W), lambda i, j: (i, j, 0)),
                  pl.BlockSpec((CONV_WIDTH, width), lambda i, j: (0, 0)),
                  pl.BlockSpec((1, gw), lambda i, j: (0, 0)),
                  pl.BlockSpec((CONV_WIDTH - 1, sub, sub), lambda i, j: (0, 0, 0)),
                  pl.BlockSpec((gw, gw), lambda i, j: (0, 0)),
                  pl.BlockSpec((LANES, 2 * GDN_HEADS * GDN_DK), lambda i, j: (0, 0))],
        out_specs=pl.BlockSpec((1, ts, GDN_HEADS * GDN_DV), lambda i, j: (i, j, 0)),
        scratch_shapes=[pltpu.VMEM((16, width), F32),
                        pltpu.VMEM((groups, gw, gw), F32)],
        compiler_params=_cparams(("parallel", "arbitrary")),
        name="gated_deltanet",
    )(slab, gb, conv_slab, jnp.tile(gdn_norm_g, GDN_GW).reshape(1, gw), shifts, ind, sel)


def _outproj_kernel(x_ref, ao_ref, go_ref, gt_ref, sc_ref, sh_ref, mg_ref, wo_ref, g2_ref,
                    rw_ref, rb_ref, x1_ref, h2_ref, route_ref, rt_ref, cnt_ref, carry_ref, *, tb):
    first = (pl.program_id(0) == 0) & (pl.program_id(1) == 0)

    @pl.when(first)
    def _():
        carry_ref[...] = jnp.zeros(carry_ref.shape, F32)

    mla = _rms(ao_ref[0].astype(F32), mg_ref[...])
    cat = jnp.concatenate([mla.astype(BF16), go_ref[0]], axis=1)
    mix = jnp.dot(cat, wo_ref[...], preferred_element_type=F32)
    x1 = x_ref[0] + gt_ref[0] * mix
    x1_ref[0] = x1
    h2 = _rms(x1, g2_ref[...]) * (1.0 + sc_ref[0]) + sh_ref[0]
    h2_ref[0] = _pack_halves(h2)

    h_hi = h2.astype(BF16)
    h_lo = (h2 - h_hi.astype(F32)).astype(BF16)
    main = jnp.dot(h_hi, rw_ref[...], preferred_element_type=F32)
    logits = (main[:, :LANES] + main[:, LANES:]
              + jnp.dot(h_lo, rw_ref[:, :LANES], preferred_element_type=F32) + rb_ref[...])
    lane = lax.broadcasted_iota(jnp.int32, (tb, LANES), 1).astype(F32)
    work = jnp.where(lane < N_EXPERTS, logits, -jnp.inf)
    vals, idxs = [], []
    onehot = jnp.zeros((tb, LANES), F32)
    for _ in range(TOP_K):
        mx = jnp.max(work, axis=-1, keepdims=True)
        ix = jnp.min(jnp.where(work == mx, lane, float(LANES)), axis=-1, keepdims=True)
        sel = lane == ix
        onehot = jnp.where(sel, 1.0, onehot)
        work = jnp.where(sel, -jnp.inf, work)
        vals.append(mx)
        idxs.append(ix)
    exps = [jnp.exp(v - vals[0]) for v in vals]
    den = exps[0] + exps[1] + exps[2] + exps[3]

    r = lax.broadcasted_iota(jnp.int32, (tb, tb), 0)
    c = lax.broadcasted_iota(jnp.int32, (tb, tb), 1)
    tri = jnp.where(c < r, 1.0, 0.0).astype(BF16)
    before = jnp.dot(tri, onehot.astype(BF16), preferred_element_type=F32) + carry_ref[...]
    route = jnp.zeros((tb, LANES), F32)
    for kk in range(TOP_K):
        rank = jnp.sum(jnp.where(lane == idxs[kk], before, 0.0), axis=-1, keepdims=True)
        route = jnp.where(lane == kk, idxs[kk], route)
        route = jnp.where(lane == TOP_K + kk, rank, route)
        route = jnp.where(lane == 2 * TOP_K + kk, exps[kk] / den, route)
    route_ref[0] = route
    rt_ref[0] = route.T[:ROUTE_ROWS, :]
    total = carry_ref[...] + jnp.sum(onehot, axis=0, keepdims=True)
    carry_ref[...] = total
    cnt_ref[...] = total


def _out_projection(x, attn_o, gdn_o, gt1, sc2, sh2, mla_out_g, w_out, norm2_g, rw, rb, tb, b0, nb):
    _, s, d = x.shape
    const = lambda shape: pl.BlockSpec(shape, lambda i, j: (0,) * len(shape))
    tok_in = lambda w: pl.BlockSpec((1, tb, w), lambda i, j: (i + b0, j, 0))
    tok_out = lambda w: pl.BlockSpec((1, tb, w), lambda i, j: (i, j, 0))
    per_b = pl.BlockSpec((1, 1, d), lambda i, j: (i + b0, 0, 0))
    half = attn_o.shape[-1]
    return pl.pallas_call(
        functools.partial(_outproj_kernel, tb=tb),
        out_shape=(jax.ShapeDtypeStruct((nb, s, d), F32),
                   jax.ShapeDtypeStruct((nb, s, d // 2), F32),
                   jax.ShapeDtypeStruct((nb, s, LANES), F32),
                   jax.ShapeDtypeStruct((nb, ROUTE_ROWS, s), F32),
                   jax.ShapeDtypeStruct((1, LANES), F32)),
        grid=(nb, s // tb),
        in_specs=[tok_in(d), tok_in(half), tok_in(half), per_b, per_b, per_b, const((1, half)),
                  const(w_out.shape), const((1, d)), const(rw.shape), const((1, LANES))],
        out_specs=(tok_out(d), tok_out(d // 2), tok_out(LANES),
                   pl.BlockSpec((1, ROUTE_ROWS, tb), lambda i, j: (i, 0, j)), const((1, LANES))),
        scratch_shapes=[pltpu.VMEM((1, LANES), F32)],
        compiler_params=_cparams(("arbitrary", "arbitrary")),
        name="out_projection_router",
    )(x, attn_o, gdn_o, gt1, sc2, sh2, mla_out_g.reshape(1, half), w_out,
      norm2_g.reshape(1, d), rw, rb)


def _sc_mesh():
    return plsc.VectorSubcoreMesh(core_axis_name="c", subcore_axis_name="s",
                                  num_cores=SC_CORES, num_subcores=SC_SUBCORES)


def _sc_worker():
    return lax.axis_index("s") * SC_CORES + lax.axis_index("c")


def _dispatch_rows(h2, dest_km, rows):
    t, d = h2.shape
    assert t % (SC_WORKERS * SC_CHUNK) == 0, "tokens must split evenly over subcores and chunks"
    per_worker = t // SC_WORKERS
    n_chunks = per_worker // SC_CHUNK

    @functools.partial(
        pl.kernel, out_type=jax.ShapeDtypeStruct((rows, d), h2.dtype), mesh=_sc_mesh(),
        scratch_types=[pltpu.VMEM((SC_CHUNK,), jnp.int32), pltpu.VMEM((SC_CHUNK, d), h2.dtype)],
        name="moe_dispatch")
    def run(h2_hbm, dest_hbm, xs_hbm, idx_v, rows_v):
        base_w = _sc_worker() * per_worker

        @pl.loop(0, n_chunks)
        def _(ci):
            base = pl.multiple_of(base_w + ci * SC_CHUNK, SC_CHUNK)
            pltpu.sync_copy(h2_hbm.at[pl.ds(base, SC_CHUNK)], rows_v)
            for kk in range(TOP_K):
                pltpu.sync_copy(dest_hbm.at[kk, pl.ds(base, SC_CHUNK)], idx_v)
                pltpu.sync_copy(rows_v, xs_hbm.at[idx_v])

    return run(h2, dest_km)


def _collect_rows(y_rows, dest_km):
    _, d = y_rows.shape
    t = dest_km.shape[1]
    assert t % (SC_WORKERS * SC_CHUNK) == 0, "tokens must split evenly over subcores and chunks"
    per_worker = t // SC_WORKERS
    n_chunks = per_worker // SC_CHUNK

    @functools.partial(
        pl.kernel, out_type=jax.ShapeDtypeStruct((TOP_K, t, d), y_rows.dtype), mesh=_sc_mesh(),
        scratch_types=[pltpu.VMEM((SC_CHUNK,), jnp.int32), pltpu.VMEM((SC_CHUNK, d), y_rows.dtype)],
        name="moe_collect")
    def run(y_hbm, dest_hbm, out_hbm, idx_v, rows_v):
        base_w = _sc_worker() * per_worker

        @pl.loop(0, n_chunks)
        def _(ci):
            base = pl.multiple_of(base_w + ci * SC_CHUNK, SC_CHUNK)
            for kk in range(TOP_K):
                pltpu.sync_copy(dest_hbm.at[kk, pl.ds(base, SC_CHUNK)], idx_v)
                pltpu.sync_copy(y_hbm.at[idx_v], rows_v)
                pltpu.sync_copy(rows_v, out_hbm.at[kk, pl.ds(base, SC_CHUNK)])

    return run(y_rows, dest_km)


def _experts_kernel(be_ref, nu_ref, nxt_ref, slot_ref, live_ref, xs_ref, wgu_hbm, bgu_ref, wd_hbm,
                    bd_ref, y_ref, wgu_f32, wd_f32, wgu_bf, wd_bf, sem):
    i = pl.program_id(0)
    e = be_ref[i]
    new_expert = (i == 0) | (e != be_ref[jnp.maximum(i - 1, 0)])

    def fetch(expert, slot):
        return (pltpu.make_async_copy(wgu_hbm.at[expert], wgu_f32.at[slot], sem.at[0, slot]),
                pltpu.make_async_copy(wd_hbm.at[expert], wd_f32.at[slot], sem.at[1, slot]))

    @pl.when(i == 0)
    def _():
        for cp in fetch(e, slot_ref[e]):
            cp.start()

    @pl.when(new_expert)
    def _():
        slot = slot_ref[e]
        for cp in fetch(e, slot):
            cp.wait()
        wgu_bf[...] = wgu_f32[slot].astype(BF16)
        wd_bf[...] = wd_f32[slot].astype(BF16)
        nxt = nxt_ref[e]

        @pl.when(nxt >= 0)
        def _():
            for cp in fetch(nxt, 1 - slot):
                cp.start()

    def mlp(r0, m):
        x_lo, x_hi = _unpack_halves(xs_ref[r0:r0 + m, :])
        half = x_lo.shape[-1]
        gu = (jnp.dot(x_lo.astype(BF16), wgu_bf[:half, :], preferred_element_type=F32)
              + jnp.dot(x_hi.astype(BF16), wgu_bf[half:, :], preferred_element_type=F32)
              + bgu_ref[0])
        gate = jnp.minimum(gu[:, :D_EXPERT], SWIGLU_LIMIT)
        up = jnp.clip(gu[:, D_EXPERT:], -SWIGLU_LIMIT, SWIGLU_LIMIT)
        act = (up + 1.0) * (gate * _sigmoid(SWIGLU_ALPHA * gate))
        y = jnp.dot(act.astype(BF16), wd_bf[...], preferred_element_type=F32) + bd_ref[0]
        y_ref[r0:r0 + m, :] = _pack_halves(y)

    piece = xs_ref.shape[0] // ROW_PIECES
    for npc in range(1, ROW_PIECES + 1):
        @pl.when((i < nu_ref[0]) & (live_ref[i] == npc))
        def _(npc=npc):
            for r0 in range(0, npc * piece, MLP_ROWS):
                mlp(r0, min(MLP_ROWS, npc * piece - r0))


def _experts(blk_e, n_used, next_e, slot_e, live, xs, wgu, bgu, wd, bd, bm):
    rows, half = xs.shape
    d = 2 * half
    n_blocks = rows // bm
    row_map = lambda i, be, nu, nx, sl, lv: (jnp.maximum(jnp.minimum(i, nu[0] - 1), 0), 0)
    exp_map = lambda i, be, nu, nx, sl, lv: (be[i], 0, 0)
    grid_spec = pltpu.PrefetchScalarGridSpec(
        num_scalar_prefetch=5,
        grid=(n_blocks,),
        in_specs=[pl.BlockSpec((bm, half), row_map),
                  pl.BlockSpec(memory_space=pl.ANY),
                  pl.BlockSpec((1, 1, 2 * D_EXPERT), exp_map),
                  pl.BlockSpec(memory_space=pl.ANY),
                  pl.BlockSpec((1, 1, d), exp_map)],
        out_specs=pl.BlockSpec((bm, half), row_map),
        scratch_shapes=[pltpu.VMEM((2, d, 2 * D_EXPERT), F32),
                        pltpu.VMEM((2, D_EXPERT, d), F32),
                        pltpu.VMEM((d, 2 * D_EXPERT), BF16),
                        pltpu.VMEM((D_EXPERT, d), BF16),
                        pltpu.SemaphoreType.DMA((2, 2))])
    return pl.pallas_call(
        _experts_kernel,
        out_shape=jax.ShapeDtypeStruct((rows, half), F32),
        grid_spec=grid_spec,
        compiler_params=_cparams(("arbitrary",)),
        name="expert_mlp",
    )(blk_e, n_used, next_e, slot_e, live, xs, wgu, bgu, wd, bd)


def _final_kernel(x1_ref, yk_ref, route_ref, gt_ref, fg_ref, *rest):
    o_ref = rest[-1]
    route = route_ref[0]
    ffn = jnp.zeros(x1_ref.shape[1:], F32)
    for kk in range(TOP_K):
        wk = route[:, 2 * TOP_K + kk:2 * TOP_K + kk + 1]
        y_lo, y_hi = _unpack_halves(yk_ref[kk, 0])
        ffn = ffn + wk * jnp.concatenate([y_lo, y_hi], axis=1)
    x2 = x1_ref[0] + gt_ref[0] * ffn
    o_ref[0] = _rms(x2, fg_ref[...])


def _final(x1, yk, route, gt2, final_g, tb, b0, b_total, prev):
    nb, s, d = x1.shape
    in_specs = [pl.BlockSpec((1, tb, d), lambda i, j: (i, j, 0)),
                pl.BlockSpec((TOP_K, 1, tb, d // 2), lambda i, j: (0, i, j, 0)),
                pl.BlockSpec((1, tb, LANES), lambda i, j: (i, j, 0)),
                pl.BlockSpec((1, 1, d), lambda i, j: (i + b0, 0, 0)),
                pl.BlockSpec((1, d), lambda i, j: (0, 0))]
    args = [x1, yk, route, gt2, final_g.reshape(1, d)]
    aliases = {}
    if prev is not None:
        in_specs.append(pl.BlockSpec(memory_space=pl.ANY))
        args.append(prev)
        aliases = {len(args) - 1: 0}
    return pl.pallas_call(
        _final_kernel,
        out_shape=jax.ShapeDtypeStruct((b_total, s, d), F32),
        grid=(nb, s // tb),
        in_specs=in_specs,
        out_specs=pl.BlockSpec((1, tb, d), lambda i, j: (i + b0, j, 0)),
        input_output_aliases=aliases,
        compiler_params=_cparams(("parallel", "parallel")),
        name="combine_final_norm",
    )(*args)


def _rot_cols(w):
    half = MLA_ROPE // 2
    return jnp.concatenate([-w[..., half:], w[..., :half]], axis=-1)


def _prep_weights(w_in, w_q_b, w_kv_b, conv_w, A_log, dt_bias):
    d = w_in.shape[0]
    cuts = [MLA_Q_LORA, MLA_KV_LORA, MLA_ROPE, GDN_HEADS * GDN_DK, GDN_HEADS * GDN_DK,
            GDN_HEADS * GDN_DV, GDN_HEADS * GDN_DV, GDN_HEADS, GDN_HEADS]
    offs = [0]
    for cw in cuts:
        offs.append(offs[-1] + cw)
    part = lambda i: w_in[:, offs[i]:offs[i + 1]]
    k_pe = part(2)
    misc = jnp.concatenate(
        [part(7), part(8), jnp.zeros((d, MLA_NOPE - 2 * GDN_HEADS), w_in.dtype), k_pe,
         _rot_cols(k_pe)], axis=1)
    heads = lambda w, n: w.reshape(w.shape[0], GDN_HEADS // GDN_GW, GDN_GW * n)
    slab = jnp.concatenate([heads(part(3), GDN_DK), heads(part(4), GDN_DK),
                            heads(part(5), GDN_DV), heads(part(6), GDN_DV)], axis=-1)
    w1 = jnp.concatenate([part(0), part(1), misc, slab.reshape(d, GDN_HEADS * GDN_SLAB)],
                         axis=1).astype(BF16)

    wq3 = w_q_b.reshape(MLA_Q_LORA, MLA_HEADS, MLA_QK)
    pe = wq3[..., MLA_NOPE:]
    wq = jnp.concatenate([wq3[..., :MLA_NOPE], pe, _rot_cols(pe)], axis=-1)
    wq = wq.reshape(MLA_Q_LORA, MLA_HEADS * HEAD_SLOT).astype(BF16)

    wkv3 = w_kv_b.reshape(MLA_KV_LORA, MLA_HEADS, MLA_NOPE + MLA_V)
    wk = jnp.concatenate([wkv3[..., :MLA_NOPE],
                          jnp.zeros((MLA_KV_LORA, MLA_HEADS, HEAD_SLOT - MLA_NOPE), w_kv_b.dtype)],
                         axis=-1).reshape(MLA_KV_LORA, MLA_HEADS * HEAD_SLOT)
    wv = wkv3[..., MLA_NOPE:].reshape(MLA_KV_LORA, MLA_HEADS * MLA_V)
    wkv = jnp.concatenate([wk, wv], axis=1).astype(BF16)

    nk = GDN_HEADS * GDN_DK
    cheads = lambda w, n: w.reshape(CONV_WIDTH, GDN_HEADS // GDN_GW, GDN_GW * n)
    conv_slab = jnp.concatenate(
        [cheads(conv_w[:, :nk], GDN_DK), cheads(conv_w[:, nk:2 * nk], GDN_DK),
         cheads(conv_w[:, 2 * nk:], GDN_DV),
         jnp.zeros((CONV_WIDTH, GDN_HEADS // GDN_GW, GDN_GW * GDN_DV), conv_w.dtype)],
        axis=-1).reshape(CONV_WIDTH, GDN_HEADS * GDN_SLAB)

    half = MLA_ROPE // 2
    inv_freq = ROPE_THETA ** (-jnp.arange(half, dtype=F32) / half)
    lane = jnp.arange(LANES)
    freq = jnp.where(lane[None, :] // MLA_ROPE == jnp.arange(ROPE_GROUPS)[:, None],
                     jnp.tile(inv_freq, LANES // half)[None, :], 0.0)
    padl = lambda a: jnp.concatenate([a.astype(F32), jnp.zeros((LANES - a.shape[0],), F32)]).reshape(1, LANES)
    return w1, wq, wkv, conv_slab, freq, padl(A_log), padl(dt_bias)


def _layer(x, mod, positions, w_in, q_norm_g, w_q_b, kv_norm_g, w_kv_b, mla_out_g, conv_w,
           A_log, dt_bias, gdn_norm_g, w_out, norm1_g, norm2_g, router_w, router_b, w_gate_up,
           b_gate_up, w_down, b_down, final_g):
    b, s, d = x.shape
    t = b * s
    sh1, sc1, gt1, sh2, sc2, gt2 = [m.reshape(b, 1, d) for m in jnp.split(mod, 6, axis=-1)]
    w1, wq, wkv, conv_slab, freq, alog, dtb = _prep_weights(w_in, w_q_b, w_kv_b, conv_w, A_log,
                                                            dt_bias)
    tb = min(TOKEN_BLOCK, s)
    q, k, v, slab, gb = _in_projection(x, positions, sc1, sh1, norm1_g, w1, q_norm_g, wq,
                                       kv_norm_g, wkv, freq, alog, dtb, tb)
    attn_o = _attention(q, k, v, tb, MLA_HEADS)
    gdn_o = _gdn(slab, gb, conv_slab, gdn_norm_g, tb)

    rw = jnp.concatenate([router_w, jnp.zeros((d, LANES - N_EXPERTS), router_w.dtype)], axis=1)
    rw_hi = rw.astype(BF16)
    rw = jnp.concatenate([rw_hi, (rw - rw_hi.astype(F32)).astype(BF16)], axis=1)
    rb = jnp.concatenate([router_b, jnp.zeros((LANES - N_EXPERTS,), router_b.dtype)]).reshape(1, LANES)
    w_out_bf = w_out.astype(BF16)
    bgu = b_gate_up.reshape(N_EXPERTS, 1, -1)
    bd = b_down.reshape(N_EXPERTS, 1, -1)
    n_parts = MOE_PARTS if b % MOE_PARTS == 0 else 1
    nb = b // n_parts
    out = None
    for part in range(n_parts):
        b0 = part * nb
        x1, h2, route, route_t, counts = _out_projection(x, attn_o, gdn_o, gt1, sc2, sh2,
                                                         mla_out_g, w_out_bf, norm2_g, rw, rb,
                                                         tb, b0, nb)
        yk = _moe(h2, route_t, counts, w_gate_up, bgu, w_down, bd)
        out = _final(x1, yk, route, gt2, final_g, tb, b0, b, out)
    return out


def _moe(h2, route_t, counts, w_gate_up, bgu, w_down, bd):
    nb, s, half = h2.shape
    t = nb * s
    bm = EXPERT_ROWS
    slot_major = lambda r0: jnp.swapaxes(route_t[:, r0:r0 + TOP_K, :], 0, 1).reshape(TOP_K, t)
    idx = slot_major(0).astype(jnp.int32)
    rank = slot_major(TOP_K).astype(jnp.int32)
    cnt = counts[0, :N_EXPERTS].astype(jnp.int32)
    padded = ((cnt + bm - 1) // bm) * bm
    pend = jnp.cumsum(padded)
    pstart = pend - padded
    base = jnp.zeros_like(idx)
    for e in range(N_EXPERTS):
        base = jnp.where(idx == e, pstart[e], base)
    dest_km = base + rank
    n_blocks = (t * TOP_K + N_EXPERTS * (bm - 1) + bm - 1) // bm
    rows = n_blocks * bm
    n_used = (pend[-1] // bm).astype(jnp.int32).reshape(1)
    blk_start = jnp.arange(n_blocks, dtype=jnp.int32) * bm
    blk_e = jnp.minimum(jnp.sum(blk_start[:, None] >= pend[None, :], axis=1), N_EXPERTS - 1)
    last_e = blk_e[jnp.maximum(n_used[0] - 1, 0)]
    blk_e = jnp.where(jnp.arange(n_blocks) < n_used[0], blk_e, last_e).astype(jnp.int32)

    present = padded > 0
    eidx = jnp.arange(N_EXPERTS, dtype=jnp.int32)
    later = present[None, :] & (eidx[None, :] > eidx[:, None])
    next_e = jnp.where(jnp.any(later, axis=1), jnp.argmax(later, axis=1), -1).astype(jnp.int32)
    slot_e = ((jnp.cumsum(present.astype(jnp.int32)) - 1) % 2).astype(jnp.int32)

    own = blk_e[:, None] == jnp.arange(N_EXPERTS, dtype=jnp.int32)[None, :]
    pick = lambda tab: jnp.sum(jnp.where(own, tab[None, :], 0), axis=1)
    in_blk = jnp.clip(pick(cnt) - (blk_start - pick(pstart)), 0, bm)
    piece = bm // ROW_PIECES
    live = jnp.clip((in_blk + piece - 1) // piece, 1, ROW_PIECES).astype(jnp.int32)

    xs = _dispatch_rows(h2.reshape(t, half), dest_km, rows)
    y_rows = _experts(blk_e, n_used, next_e, slot_e, live, xs, w_gate_up, bgu, w_down, bd, bm)
    return _collect_rows(y_rows, dest_km).reshape(TOP_K, nb, s, half)


def kernel(x, c, positions, ada_w, ada_b, norm1_g, w_in, q_norm_g, w_q_b, kv_norm_g, w_kv_b, mla_out_g, conv_w, A_log, dt_bias, gdn_norm_g, w_out, norm2_g, router_w, router_b, w_gate_up, b_gate_up, w_down, b_down, final_g):
    depth = ada_w.shape[0]
    assert depth == 1
    l = 0
    mod = _modulation(c, ada_w[l], ada_b[l])
    return _layer(x, mod, positions, w_in[l], q_norm_g[l], w_q_b[l], kv_norm_g[l],
                  w_kv_b[l], mla_out_g[l], conv_w[l], A_log[l], dt_bias[l], gdn_norm_g[l],
                  w_out[l], norm1_g[l], norm2_g[l], router_w[l], router_b[l],
                  w_gate_up[l], b_gate_up[l], w_down[l], b_down[l], final_g)
```

```python
import functools

import jax
import jax.numpy as jnp
from jax import lax
from jax.experimental import pallas as pl
from jax.experimental.pallas import tpu as pltpu
from jax.experimental.pallas import tpu_sc as plsc

F32 = jnp.float32
BF16 = jnp.bfloat16

D_MODEL = 1024
EPS = 1e-6
MLA_HEADS = 8
MLA_NOPE = 64
MLA_ROPE = 32
MLA_V = 64
MLA_QK = MLA_NOPE + MLA_ROPE
MLA_Q_LORA = 384
MLA_KV_LORA = 256
ROPE_THETA = 10000.0
GDN_HEADS = 8
GDN_DK = 64
GDN_DV = 64
CONV_WIDTH = 4
CHUNK = 64
N_EXPERTS = 32
TOP_K = 4
D_EXPERT = D_MODEL
SWIGLU_ALPHA = 1.702
SWIGLU_LIMIT = 7.0

LANES = 128
HEAD_SLOT = 128
GDN_SLAB = 256
GDN_GW = 4
CONV_BLOCK = 256
MISC_W = 128
PROJ_W = MLA_Q_LORA + MLA_KV_LORA + MISC_W + GDN_HEADS * GDN_SLAB
VMEM_LIMIT = 56 * 1024 * 1024
TOKEN_BLOCK = 512
MOE_PARTS = 2
EXPERT_ROWS = 1024
MLP_ROWS = 512
ROW_PIECES = 8
SC_CORES = 2
SC_SUBCORES = 16
SC_WORKERS = SC_CORES * SC_SUBCORES
SC_CHUNK = 128
ROPE_GROUPS = LANES // MLA_ROPE
ROUTE_ROWS = 16
ATT_STRIP = 32
LOG2E = 1.4426950408889634


def _cparams(sem):
    return pltpu.CompilerParams(dimension_semantics=sem, vmem_limit_bytes=VMEM_LIMIT)


def _rms(x, g):
    return x * lax.rsqrt(jnp.mean(x * x, axis=-1, keepdims=True) + EPS) * g


def _sigmoid(x):
    return 1.0 / (1.0 + jnp.exp(-x))


def _silu(x):
    return x * _sigmoid(x)


def _pack_halves(x):
    w = x.shape[-1] // 2
    lo = lax.bitcast_convert_type(x[:, :w].astype(BF16).astype(F32), jnp.uint32) >> 16
    hi = lax.bitcast_convert_type(x[:, w:].astype(BF16).astype(F32), jnp.uint32) & jnp.uint32(0xFFFF0000)
    return lax.bitcast_convert_type(lo | hi, F32)


def _unpack_halves(p):
    u = lax.bitcast_convert_type(p, jnp.uint32)
    return (lax.bitcast_convert_type(u << 16, F32),
            lax.bitcast_convert_type(u & jnp.uint32(0xFFFF0000), F32))


def _mod_kernel(c_ref, w_ref, b_ref, o_ref):
    c = c_ref[...]
    o_ref[...] = jnp.dot(_silu(c), w_ref[...], preferred_element_type=F32,
                         precision=lax.Precision.HIGHEST) + b_ref[...]


def _modulation(c, ada_w, ada_b):
    b, d = c.shape
    n = ada_w.shape[1]
    return pl.pallas_call(
        _mod_kernel,
        out_shape=jax.ShapeDtypeStruct((b, n), F32),
        grid=(n // d,),
        in_specs=[pl.BlockSpec((b, d), lambda j: (0, 0)),
                  pl.BlockSpec((d, d), lambda j: (0, j)),
                  pl.BlockSpec((1, d), lambda j: (0, j))],
        out_specs=pl.BlockSpec((b, d), lambda j: (0, j)),
        compiler_params=_cparams(("arbitrary",)),
        name="adaln_mod",
    )(c, ada_w, ada_b.reshape(1, n))


def _inproj_kernel(x_ref, pos_ref, sc_ref, sh_ref, g1_ref, w1_ref, qg_ref, wq_ref, kvg_ref,
                   wkv_ref, freq_ref, alog_ref, dtb_ref,
                   q_ref, k_ref, v_ref, gdn_ref, gb_ref):
    x = x_ref[0]
    h = _rms(x, g1_ref[...]) * (1.0 + sc_ref[0]) + sh_ref[0]
    proj = jnp.dot(h.astype(BF16), w1_ref[...], preferred_element_type=F32)

    tb = x.shape[0]
    lane = lax.broadcasted_iota(jnp.int32, (tb, LANES), 1)
    in_rope = (lane >= MLA_NOPE) & (lane < MLA_QK)
    pos = pos_ref[0].astype(F32)
    quarter = tb // ROPE_GROUPS
    ang = pos[0:quarter] * freq_ref[0:1, :]
    for g in range(1, ROPE_GROUPS):
        ang = ang + pos[g * quarter:(g + 1) * quarter] * freq_ref[g:g + 1, :]
    cos_p = jnp.cos(ang)
    sin_p = jnp.sin(ang)
    unpack = lambda t: jnp.concatenate(
        [pltpu.roll(t, (MLA_NOPE - MLA_ROPE * g) % LANES, axis=1) for g in range(ROPE_GROUPS)],
        axis=0)
    cosv = unpack(cos_p)
    sinv = jnp.where(in_rope, unpack(sin_p), 0.0)

    scale = (MLA_QK ** -0.5) * LOG2E
    qn = _rms(proj[:, :MLA_Q_LORA], qg_ref[...])
    qa = jnp.dot(qn.astype(BF16), wq_ref[...], preferred_element_type=F32)
    cq = jnp.where(lane < MLA_NOPE, scale, jnp.where(in_rope, cosv * scale, 0.0))
    sq = sinv * scale
    cq_t = jnp.concatenate([cq] * MLA_HEADS, axis=1)
    sq_t = jnp.concatenate([sq] * MLA_HEADS, axis=1)
    width = MLA_HEADS * HEAD_SLOT
    q = qa * cq_t + pltpu.roll(qa, width - MLA_ROPE, axis=1) * sq_t
    q_ref[0] = q.astype(BF16)

    kvn = _rms(proj[:, MLA_Q_LORA:MLA_Q_LORA + MLA_KV_LORA], kvg_ref[...])
    kva = jnp.dot(kvn.astype(BF16), wkv_ref[...], preferred_element_type=F32)
    misc = proj[:, MLA_Q_LORA + MLA_KV_LORA:MLA_Q_LORA + MLA_KV_LORA + MISC_W]
    kp = misc * jnp.where(in_rope, cosv, 0.0) + pltpu.roll(misc, MISC_W - MLA_ROPE, axis=1) * sinv
    k = kva[:, :width] + jnp.concatenate([kp] * MLA_HEADS, axis=1)
    k_ref[0] = k.astype(BF16)
    v_ref[0] = kva[:, width:].astype(BF16)

    z = misc + dtb_ref[...]
    softplus = jnp.maximum(z, 0.0) + jnp.log(1.0 + jnp.exp(-jnp.abs(z)))
    g = -jnp.exp(alog_ref[...]) * softplus
    gb_ref[0] = jnp.where(lane < GDN_HEADS, g, _sigmoid(misc))

    gdn_ref[0] = proj[:, MLA_Q_LORA + MLA_KV_LORA + MISC_W:].astype(BF16)


def _in_projection(x, positions, sc1, sh1, norm1_g, w1, q_norm_g, wq, kv_norm_g, wkv,
                   freq, alog, dtb, tb):
    b, s, d = x.shape
    hw = MLA_HEADS * HEAD_SLOT
    const = lambda shape: pl.BlockSpec(shape, lambda i, j: (0,) * len(shape))
    tok = lambda w: pl.BlockSpec((1, tb, w), lambda i, j: (i, j, 0))
    per_b = pl.BlockSpec((1, 1, d), lambda i, j: (i, 0, 0))
    return pl.pallas_call(
        _inproj_kernel,
        out_shape=(jax.ShapeDtypeStruct((b, s, hw), BF16),
                   jax.ShapeDtypeStruct((b, s, hw), BF16),
                   jax.ShapeDtypeStruct((b, s, MLA_HEADS * MLA_V), BF16),
                   jax.ShapeDtypeStruct((b, s, GDN_HEADS * GDN_SLAB), BF16),
                   jax.ShapeDtypeStruct((b, s, MISC_W), F32)),
        grid=(b, s // tb),
        in_specs=[tok(d), tok(1), per_b, per_b, const((1, d)), const(w1.shape),
                  const((1, MLA_Q_LORA)), const(wq.shape), const((1, MLA_KV_LORA)),
                  const(wkv.shape), const((ROPE_GROUPS, LANES)), const((1, LANES)),
                  const((1, LANES))],
        out_specs=(tok(hw), tok(hw), tok(MLA_HEADS * MLA_V), tok(GDN_HEADS * GDN_SLAB),
                   tok(MISC_W)),
        compiler_params=_cparams(("parallel", "parallel")),
        name="in_projection",
    )(x, positions.reshape(b, s, 1), sc1, sh1, norm1_g.reshape(1, d), w1,
      q_norm_g.reshape(1, -1), wq, kv_norm_g.reshape(1, -1), wkv, freq, alog, dtb)


def _attn_kernel(q_ref, k_ref, v_ref, o_ref, s_ref, p_ref, m_ref, acc_ref, *, tq, nh):
    qi = pl.program_id(2)
    m_ref[...] = jnp.full(m_ref.shape, -jnp.inf, F32)
    acc_ref[...] = jnp.zeros(acc_ref.shape, F32)

    def step(off, masked):
        segs = [(0, tq // 2, tq // 2), (tq // 2, tq, tq)] if masked else [(0, tq, tq)]
        width = lambda r: next(nc for r0, r1, nc in segs if r0 <= r * ATT_STRIP < r1)
        for h in range(nh):
            for r0, r1, nc in segs:
                kj = k_ref[0, pl.ds(off, nc), h * HEAD_SLOT:(h + 1) * HEAD_SLOT]
                s_ref[h, r0:r1, 0:nc] = lax.dot_general(
                    q_ref[0, r0:r1, h * HEAD_SLOT:(h + 1) * HEAD_SLOT], kj,
                    (((1,), (1,)), ((), ())), preferred_element_type=F32)
        n_strips = tq // ATT_STRIP

        def strip(h, r):
            rows = slice(r * ATT_STRIP, (r + 1) * ATT_STRIP)
            nc = width(r)
            sc = s_ref[h, rows, 0:nc]
            if masked:
                rid = lax.broadcasted_iota(jnp.int32, (ATT_STRIP, nc), 0) + r * ATT_STRIP
                cid = lax.broadcasted_iota(jnp.int32, (ATT_STRIP, nc), 1)
                sc = jnp.where(cid <= rid, sc, -jnp.inf)
            return rows, sc

        rep = lambda col: jnp.broadcast_to(col, (col.shape[0], LANES))
        m_new = []
        for h in range(nh):
            mx = jnp.concatenate([rep(jnp.max(strip(h, r)[1], axis=-1, keepdims=True))
                                  for r in range(n_strips)], axis=0)
            m_new.append(jnp.maximum(m_ref[h], mx))
        for h in range(nh):
            v_lanes = slice((h // 2) * 2 * MLA_V, (h // 2 + 1) * 2 * MLA_V)
            for r in range(n_strips):
                rows, sc = strip(h, r)
                nc = width(r)
                p = jnp.exp2(sc - jnp.concatenate([m_new[h][rows]] * (nc // LANES), axis=1))
                p_ref[h, rows, 0:nc] = p.astype(BF16)
            alpha = jnp.exp2(m_ref[h] - m_new[h])
            m_ref[h] = m_new[h]
            alpha2 = jnp.concatenate([alpha, alpha], axis=1)
            for r0, r1, nc in segs:
                v_ext = jnp.concatenate([v_ref[0, pl.ds(off, nc), v_lanes],
                                         jnp.ones((nc, LANES), BF16)], axis=1)
                acc_ref[h, r0:r1, :] = alpha2[r0:r1] * acc_ref[h, r0:r1, :] + jnp.dot(
                    p_ref[h, r0:r1, 0:nc], v_ext, preferred_element_type=F32)

    def body(j, carry):
        step(pl.multiple_of(j * tq, tq), False)
        return carry

    lax.fori_loop(0, qi, body, 0)
    step(pl.multiple_of(qi * tq, tq), True)
    lane = lax.broadcasted_iota(jnp.int32, (tq, 2 * MLA_V), 1)
    outs = []
    for p in range(nh // 2):
        o0 = acc_ref[2 * p, :, :LANES] / acc_ref[2 * p, :, LANES:]
        o1 = acc_ref[2 * p + 1, :, :LANES] / acc_ref[2 * p + 1, :, LANES:]
        outs.append(jnp.where(lane < MLA_V, o0, o1))
    o_ref[0] = jnp.concatenate(outs, axis=1).astype(o_ref.dtype)


def _attention(q, k, v, tq, nh):
    b, s, _ = q.shape
    groups = MLA_HEADS // nh
    return pl.pallas_call(
        functools.partial(_attn_kernel, tq=tq, nh=nh),
        out_shape=jax.ShapeDtypeStruct((b, s, MLA_HEADS * MLA_V), BF16),
        grid=(b, groups, s // tq),
        in_specs=[pl.BlockSpec((1, tq, nh * HEAD_SLOT), lambda i, p, j: (i, j, p)),
                  pl.BlockSpec((1, s, nh * HEAD_SLOT), lambda i, p, j: (i, 0, p)),
                  pl.BlockSpec((1, s, nh * MLA_V), lambda i, p, j: (i, 0, p))],
        out_specs=pl.BlockSpec((1, tq, nh * MLA_V), lambda i, p, j: (i, j, p)),
        scratch_shapes=[pltpu.VMEM((nh, tq, tq), F32), pltpu.VMEM((nh, tq, tq), BF16),
                        pltpu.VMEM((nh, tq, LANES), F32),
                        pltpu.VMEM((nh, tq, 2 * LANES), F32)],
        compiler_params=_cparams(("parallel", "parallel", "arbitrary")),
        name="mla_attention",
    )(q, k, v)


def _gdn_kernel(slab_ref, gb_ref, cw_ref, ng_ref, shf_ref, ind_ref, sel_ref, o_ref, tail_ref,
                state_ref, *, ts, groups):
    si = pl.program_id(1)
    gw = GDN_GW * GDN_DK
    hist_rows = 8

    @pl.when(si == 0)
    def _():
        tail_ref[...] = jnp.zeros(tail_ref.shape, F32)
        state_ref[...] = jnp.zeros(state_ref.shape, F32)

    xs_bf = slab_ref[0]
    xs = xs_bf.astype(F32)
    width = xs.shape[1]
    sub = shf_ref.shape[1]
    conv_parts = []
    for r0 in range(0, ts, sub):
        xs_sub = xs[r0:r0 + sub]
        conv = cw_ref[CONV_WIDTH - 1:CONV_WIDTH, :] * xs_sub
        hist = jnp.zeros((hist_rows, width), F32)
        for j in range(CONV_WIDTH - 1):
            wj = cw_ref[j:j + 1, :]
            conv = conv + wj * jnp.dot(shf_ref[j], xs_bf[r0:r0 + sub],
                                       preferred_element_type=F32)
            start = hist_rows - (CONV_WIDTH - 1) + j
            hist = hist + wj * tail_ref[start:start + hist_rows, :]
        conv_parts += [conv[:hist_rows] + hist, conv[hist_rows:]]
        tail_ref[0:hist_rows, :] = xs_sub[sub - hist_rows:sub, :]
    act = _silu(jnp.concatenate(conv_parts, axis=0))

    gb = gb_ref[0]
    g1 = gb.astype(BF16)
    r1 = gb - g1.astype(F32)
    g2 = r1.astype(BF16)
    g3 = (r1 - g2.astype(F32)).astype(BF16)
    sel = sel_ref[...]
    gate_w = (jnp.dot(g1, sel, preferred_element_type=F32)
              + jnp.dot(g2, sel, preferred_element_type=F32)
              + jnp.dot(g3, sel, preferred_element_type=F32))

    n_chunks = ts // CHUNK
    row = lax.broadcasted_iota(jnp.int32, (CHUNK, gw), 0)
    col = lax.broadcasted_iota(jnp.int32, (CHUNK, gw), 1) % GDN_DK
    tri_incl = col <= row
    tri_strict = col < row
    eye = col == row
    brow = lax.broadcasted_iota(jnp.int32, (gw, gw), 0) // GDN_DK
    bcol = lax.broadcasted_iota(jnp.int32, (gw, gw), 1) // GDN_DK
    same_head = brow == bcol
    crow = lax.broadcasted_iota(jnp.int32, (ts, gw), 0) % CHUNK

    def split2(x):
        hi = x.astype(BF16)
        return hi, (x - hi.astype(F32)).astype(BF16)

    def head_sums(x):
        hi, lo = split2(x)
        return (jnp.dot(hi, ind_ref[...], preferred_element_type=F32)
                + jnp.dot(lo, ind_ref[...], preferred_element_type=F32))

    def blockdiag(x):
        return jnp.where(same_head, jnp.concatenate([x.astype(BF16)] * GDN_GW, axis=0), 0)

    def wdot(a, b):
        return jnp.dot(a.astype(BF16), blockdiag(b), preferred_element_type=F32)

    prep = []
    for gi in range(groups):
        base = gi * GDN_GW * GDN_SLAB
        q_raw = act[:, base:base + gw]
        k_raw = act[:, base + gw:base + 2 * gw]
        v_all = act[:, base + 2 * gw:base + 3 * gw]
        z_all = xs[:, base + 3 * gw:base + 4 * gw]
        q_all = q_raw * (lax.rsqrt(head_sums(q_raw * q_raw) + EPS) * (GDN_DK ** -0.5))
        k_all = k_raw * lax.rsqrt(head_sums(k_raw * k_raw) + EPS)
        g_w = gate_w[:, gi * gw:(gi + 1) * gw]
        b_w = gate_w[:, (groups + gi) * gw:(groups + gi + 1) * gw]
        gc_w = g_w
        shift = 1
        while shift < CHUNK:
            rolled = pltpu.roll(gc_w, shift, axis=0)
            gc_w = gc_w + jnp.where(crow >= shift, rolled, 0.0)
            shift *= 2
        prep.append((q_all, k_all, v_all, z_all, b_w, gc_w, jnp.exp(gc_w)))

    units = [(gi, c) for c in range(n_chunks) for gi in range(groups)]
    a_low, qk, yu0, yw0, qd, kt, cd = ({} for _ in range(7))
    for u in units:
        gi, c = u
        q_all, k_all, v_all, _, b_w, gc_w, eg_w = prep[gi]
        sl = slice(c * CHUNK, (c + 1) * CHUNK)
        qc, kc, vc, bc, gcum, eg = q_all[sl], k_all[sl], v_all[sl], b_w[sl], gc_w[sl], eg_w[sl]
        grow = jnp.sum(jnp.where(eye, gcum, 0.0), axis=0, keepdims=True)
        diff = gcum - grow
        decay = jnp.where(tri_incl, jnp.exp(jnp.where(tri_incl, diff, 0.0)), 0.0)
        k_beta = kc * bc
        kq = jnp.concatenate([k_beta, qc], axis=0).astype(BF16)
        kk = lax.dot_general(kq, blockdiag(kc), (((1,), (1,)), ((), ())),
                             preferred_element_type=F32)
        a_low[u] = jnp.where(tri_strict, kk[:CHUNK] * decay, 0.0)
        qk[u] = jnp.where(tri_incl, kk[CHUNK:] * decay, 0.0)
        yu0[u] = vc * bc
        yw0[u] = k_beta * eg
        qd[u] = qc * eg
        glast = gcum[CHUNK - 1:CHUNK, :]
        kt[u] = kc * jnp.exp(glast - gcum)
        cd[u] = eg[CHUNK - 1:CHUNK, :]

    blk = 2
    tinv = {u: jnp.where(eye, 1.0, 0.0) - jnp.where((row // blk) == (col // blk), a_low[u], 0.0)
            for u in units}
    while blk < CHUNK:
        in_big = (row // (2 * blk)) == (col // (2 * blk))
        off_mask = in_big & ((row // blk) != (col // blk))
        left = {u: wdot(tinv[u], jnp.where(off_mask, a_low[u], 0.0)) for u in units}
        tinv = {u: tinv[u] - wdot(left[u], tinv[u]) for u in units}
        blk *= 2

    yu = {u: wdot(tinv[u], yu0[u]) for u in units}
    yw = {u: wdot(tinv[u], yw0[u]) for u in units}
    tdot = lambda a, b: jnp.where(same_head, lax.dot_general(
        a.astype(BF16), b.astype(BF16), (((0,), (0,)), ((), ())),
        preferred_element_type=F32), 0.0)
    q_bd = {u: tdot(kt[u], yu[u]) for u in units}
    p_bd = {u: tdot(kt[u], yw[u]) for u in units}
    qu = {u: wdot(qk[u], yu[u]) for u in units}
    qw = {u: wdot(qk[u], yw[u]) for u in units}

    outs = [[] for _ in range(groups)]
    states = [state_ref[gi] for gi in range(groups)]
    for c in range(n_chunks):
        for gi in range(groups):
            u = (gi, c)
            sb = states[gi].astype(BF16)
            r_mat = qd[u] - qw[u]
            outs[gi].append(jnp.dot(r_mat.astype(BF16), sb, preferred_element_type=F32) + qu[u])
            states[gi] = (states[gi] * cd[u]
                          - jnp.dot(p_bd[u].astype(BF16), sb, preferred_element_type=F32)
                          + q_bd[u])
    finals = []
    for gi in range(groups):
        state_ref[gi] = states[gi]
        o_all = jnp.concatenate(outs[gi], axis=0)
        inv = lax.rsqrt(head_sums(o_all * o_all) * (1.0 / GDN_DV) + EPS)
        finals.append(o_all * inv * ng_ref[...] * _silu(prep[gi][3]))
    o_ref[0] = jnp.concatenate(finals, axis=1).astype(o_ref.dtype)


def _gdn(slab, gb, conv_slab, gdn_norm_g, ts):
    b, s, width = slab.shape
    groups = GDN_HEADS // GDN_GW
    gw = GDN_GW * GDN_DK
    sub = min(ts, CONV_BLOCK)
    r = jnp.arange(sub)
    shifts = jnp.stack([(r[:, None] - r[None, :]) == (CONV_WIDTH - 1 - j)
                        for j in range(CONV_WIDTH - 1)]).astype(BF16)
    li = jnp.arange(gw)
    ind = (li[:, None] // GDN_DK == li[None, :] // GDN_DK).astype(BF16)
    src = jnp.arange(LANES)[:, None]
    dst = jnp.arange(2 * GDN_HEADS * GDN_DK)[None, :]
    sel = (src == dst // GDN_DK).astype(BF16)
    return pl.pallas_call(
        functools.partial(_gdn_kernel, ts=ts, groups=groups),
        out_shape=jax.ShapeDtypeStruct((b, s, GDN_HEADS * GDN_DV), BF16),
        grid=(b, s // ts),
        in_specs=[pl.BlockSpec((1, ts, width), lambda i, j: (i, j, 0)),
                  pl.BlockSpec((1, ts, MISC_W), lambda i, j: (i, j, 0)),
                  pl.BlockSpec((CONV_WIDTH, width), lambda i, j: (0, 0)),
                  pl.BlockSpec((1, gw), lambda i, j: (0, 0)),
                  pl.BlockSpec((CONV_WIDTH - 1, sub, sub), lambda i, j: (0, 0, 0)),
                  pl.BlockSpec((gw, gw), lambda i, j: (0, 0)),
                  pl.BlockSpec((LANES, 2 * GDN_HEADS * GDN_DK), lambda i, j: (0, 0))],
        out_specs=pl.BlockSpec((1, ts, GDN_HEADS * GDN_DV), lambda i, j: (i, j, 0)),
        scratch_shapes=[pltpu.VMEM((16, width), F32),
                        pltpu.VMEM((groups, gw, gw), F32)],
        compiler_params=_cparams(("parallel", "arbitrary")),
        name="gated_deltanet",
    )(slab, gb, conv_slab, jnp.tile(gdn_norm_g, GDN_GW).reshape(1, gw), shifts, ind, sel)


def _outproj_kernel(x_ref, ao_ref, go_ref, gt_ref, sc_ref, sh_ref, mg_ref, wo_ref, g2_ref,
                    rw_ref, rb_ref, x1_ref, h2_ref, route_ref, rt_ref, cnt_ref, carry_ref, *, tb):
    first = (pl.program_id(0) == 0) & (pl.program_id(1) == 0)

    @pl.when(first)
    def _():
        carry_ref[...] = jnp.zeros(carry_ref.shape, F32)

    mla = _rms(ao_ref[0].astype(F32), mg_ref[...])
    cat = jnp.concatenate([mla.astype(BF16), go_ref[0]], axis=1)
    mix = jnp.dot(cat, wo_ref[...], preferred_element_type=F32)
    x1 = x_ref[0] + gt_ref[0] * mix
    x1_ref[0] = x1
    h2 = _rms(x1, g2_ref[...]) * (1.0 + sc_ref[0]) + sh_ref[0]
    h2_ref[0] = _pack_halves(h2)

    h_hi = h2.astype(BF16)
    h_lo = (h2 - h_hi.astype(F32)).astype(BF16)
    main = jnp.dot(h_hi, rw_ref[...], preferred_element_type=F32)
    logits = (main[:, :LANES] + main[:, LANES:]
              + jnp.dot(h_lo, rw_ref[:, :LANES], preferred_element_type=F32) + rb_ref[...])
    lane = lax.broadcasted_iota(jnp.int32, (tb, LANES), 1).astype(F32)
    work = jnp.where(lane < N_EXPERTS, logits, -jnp.inf)
    vals, idxs = [], []
    onehot = jnp.zeros((tb, LANES), F32)
    for _ in range(TOP_K):
        mx = jnp.max(work, axis=-1, keepdims=True)
        ix = jnp.min(jnp.where(work == mx, lane, float(LANES)), axis=-1, keepdims=True)
        sel = lane == ix
        onehot = jnp.where(sel, 1.0, onehot)
        work = jnp.where(sel, -jnp.inf, work)
        vals.append(mx)
        idxs.append(ix)
    exps = [jnp.exp(v - vals[0]) for v in vals]
    den = exps[0] + exps[1] + exps[2] + exps[3]

    r = lax.broadcasted_iota(jnp.int32, (tb, tb), 0)
    c = lax.broadcasted_iota(jnp.int32, (tb, tb), 1)
    tri = jnp.where(c < r, 1.0, 0.0).astype(BF16)
    before = jnp.dot(tri, onehot.astype(BF16), preferred_element_type=F32) + carry_ref[...]
    route = jnp.zeros((tb, LANES), F32)
    for kk in range(TOP_K):
        rank = jnp.sum(jnp.where(lane == idxs[kk], before, 0.0), axis=-1, keepdims=True)
        route = jnp.where(lane == kk, idxs[kk], route)
        route = jnp.where(lane == TOP_K + kk, rank, route)
        route = jnp.where(lane == 2 * TOP_K + kk, exps[kk] / den, route)
    route_ref[0] = route
    rt_ref[0] = route.T[:ROUTE_ROWS, :]
    total = carry_ref[...] + jnp.sum(onehot, axis=0, keepdims=True)
    carry_ref[...] = total
    cnt_ref[...] = total


def _out_projection(x, attn_o, gdn_o, gt1, sc2, sh2, mla_out_g, w_out, norm2_g, rw, rb, tb, b0, nb):
    _, s, d = x.shape
    const = lambda shape: pl.BlockSpec(shape, lambda i, j: (0,) * len(shape))
    tok_in = lambda w: pl.BlockSpec((1, tb, w), lambda i, j: (i + b0, j, 0))
    tok_out = lambda w: pl.BlockSpec((1, tb, w), lambda i, j: (i, j, 0))
    per_b = pl.BlockSpec((1, 1, d), lambda i, j: (i + b0, 0, 0))
    half = attn_o.shape[-1]
    return pl.pallas_call(
        functools.partial(_outproj_kernel, tb=tb),
        out_shape=(jax.ShapeDtypeStruct((nb, s, d), F32),
                   jax.ShapeDtypeStruct((nb, s, d // 2), F32),
                   jax.ShapeDtypeStruct((nb, s, LANES), F32),
                   jax.ShapeDtypeStruct((nb, ROUTE_ROWS, s), F32),
                   jax.ShapeDtypeStruct((1, LANES), F32)),
        grid=(nb, s // tb),
        in_specs=[tok_in(d), tok_in(half), tok_in(half), per_b, per_b, per_b, const((1, half)),
                  const(w_out.shape), const((1, d)), const(rw.shape), const((1, LANES))],
        out_specs=(tok_out(d), tok_out(d // 2), tok_out(LANES),
                   pl.BlockSpec((1, ROUTE_ROWS, tb), lambda i, j: (i, 0, j)), const((1, LANES))),
        scratch_shapes=[pltpu.VMEM((1, LANES), F32)],
        compiler_params=_cparams(("arbitrary", "arbitrary")),
        name="out_projection_router",
    )(x, attn_o, gdn_o, gt1, sc2, sh2, mla_out_g.reshape(1, half), w_out,
      norm2_g.reshape(1, d), rw, rb)


def _sc_mesh():
    return plsc.VectorSubcoreMesh(core_axis_name="c", subcore_axis_name="s",
                                  num_cores=SC_CORES, num_subcores=SC_SUBCORES)


def _sc_worker():
    return lax.axis_index("s") * SC_CORES + lax.axis_index("c")


def _dispatch_rows(h2, dest_km, rows):
    t, d = h2.shape
    assert t % (SC_WORKERS * SC_CHUNK) == 0, "tokens must split evenly over subcores and chunks"
    per_worker = t // SC_WORKERS
    n_chunks = per_worker // SC_CHUNK

    @functools.partial(
        pl.kernel, out_type=jax.ShapeDtypeStruct((rows, d), h2.dtype), mesh=_sc_mesh(),
        scratch_types=[pltpu.VMEM((SC_CHUNK,), jnp.int32), pltpu.VMEM((SC_CHUNK, d), h2.dtype)],
        name="moe_dispatch")
    def run(h2_hbm, dest_hbm, xs_hbm, idx_v, rows_v):
        base_w = _sc_worker() * per_worker

        @pl.loop(0, n_chunks)
        def _(ci):
            base = pl.multiple_of(base_w + ci * SC_CHUNK, SC_CHUNK)
            pltpu.sync_copy(h2_hbm.at[pl.ds(base, SC_CHUNK)], rows_v)
            for kk in range(TOP_K):
                pltpu.sync_copy(dest_hbm.at[kk, pl.ds(base, SC_CHUNK)], idx_v)
                pltpu.sync_copy(rows_v, xs_hbm.at[idx_v])

    return run(h2, dest_km)


def _collect_rows(y_rows, dest_km):
    _, d = y_rows.shape
    t = dest_km.shape[1]
    assert t % (SC_WORKERS * SC_CHUNK) == 0, "tokens must split evenly over subcores and chunks"
    per_worker = t // SC_WORKERS
    n_chunks = per_worker // SC_CHUNK

    @functools.partial(
        pl.kernel, out_type=jax.ShapeDtypeStruct((TOP_K, t, d), y_rows.dtype), mesh=_sc_mesh(),
        scratch_types=[pltpu.VMEM((SC_CHUNK,), jnp.int32), pltpu.VMEM((SC_CHUNK, d), y_rows.dtype)],
        name="moe_collect")
    def run(y_hbm, dest_hbm, out_hbm, idx_v, rows_v):
        base_w = _sc_worker() * per_worker

        @pl.loop(0, n_chunks)
        def _(ci):
            base = pl.multiple_of(base_w + ci * SC_CHUNK, SC_CHUNK)
            for kk in range(TOP_K):
                pltpu.sync_copy(dest_hbm.at[kk, pl.ds(base, SC_CHUNK)], idx_v)
                pltpu.sync_copy(y_hbm.at[idx_v], rows_v)
                pltpu.sync_copy(rows_v, out_hbm.at[kk, pl.ds(base, SC_CHUNK)])

    return run(y_rows, dest_km)


def _experts_kernel(be_ref, nu_ref, nxt_ref, slot_ref, live_ref, xs_ref, wgu_hbm, bgu_ref, wd_hbm,
                    bd_ref, y_ref, wgu_f32, wd_f32, wgu_bf, wd_bf, sem):
    i = pl.program_id(0)
    e = be_ref[i]
    new_expert = (i == 0) | (e != be_ref[jnp.maximum(i - 1, 0)])

    def fetch(expert, slot):
        return (pltpu.make_async_copy(wgu_hbm.at[expert], wgu_f32.at[slot], sem.at[0, slot]),
                pltpu.make_async_copy(wd_hbm.at[expert], wd_f32.at[slot], sem.at[1, slot]))

    @pl.when(i == 0)
    def _():
        for cp in fetch(e, slot_ref[e]):
            cp.start()

    @pl.when(new_expert)
    def _():
        slot = slot_ref[e]
        for cp in fetch(e, slot):
            cp.wait()
        wgu_bf[...] = wgu_f32[slot].astype(BF16)
        wd_bf[...] = wd_f32[slot].astype(BF16)
        nxt = nxt_ref[e]

        @pl.when(nxt >= 0)
        def _():
            for cp in fetch(nxt, 1 - slot):
                cp.start()

    def mlp(r0, m):
        x_lo, x_hi = _unpack_halves(xs_ref[r0:r0 + m, :])
        half = x_lo.shape[-1]
        gu = (jnp.dot(x_lo.astype(BF16), wgu_bf[:half, :], preferred_element_type=F32)
              + jnp.dot(x_hi.astype(BF16), wgu_bf[half:, :], preferred_element_type=F32)
              + bgu_ref[0])
        gate = jnp.minimum(gu[:, :D_EXPERT], SWIGLU_LIMIT)
        up = jnp.clip(gu[:, D_EXPERT:], -SWIGLU_LIMIT, SWIGLU_LIMIT)
        act = (up + 1.0) * (gate * _sigmoid(SWIGLU_ALPHA * gate))
        y = jnp.dot(act.astype(BF16), wd_bf[...], preferred_element_type=F32) + bd_ref[0]
        y_ref[r0:r0 + m, :] = _pack_halves(y)

    piece = xs_ref.shape[0] // ROW_PIECES
    for npc in range(1, ROW_PIECES + 1):
        @pl.when((i < nu_ref[0]) & (live_ref[i] == npc))
        def _(npc=npc):
            for r0 in range(0, npc * piece, MLP_ROWS):
                mlp(r0, min(MLP_ROWS, npc * piece - r0))


def _experts(blk_e, n_used, next_e, slot_e, live, xs, wgu, bgu, wd, bd, bm):
    rows, half = xs.shape
    d = 2 * half
    n_blocks = rows // bm
    row_map = lambda i, be, nu, nx, sl, lv: (jnp.maximum(jnp.minimum(i, nu[0] - 1), 0), 0)
    exp_map = lambda i, be, nu, nx, sl, lv: (be[i], 0, 0)
    grid_spec = pltpu.PrefetchScalarGridSpec(
        num_scalar_prefetch=5,
        grid=(n_blocks,),
        in_specs=[pl.BlockSpec((bm, half), row_map),
                  pl.BlockSpec(memory_space=pl.ANY),
                  pl.BlockSpec((1, 1, 2 * D_EXPERT), exp_map),
                  pl.BlockSpec(memory_space=pl.ANY),
                  pl.BlockSpec((1, 1, d), exp_map)],
        out_specs=pl.BlockSpec((bm, half), row_map),
        scratch_shapes=[pltpu.VMEM((2, d, 2 * D_EXPERT), F32),
                        pltpu.VMEM((2, D_EXPERT, d), F32),
                        pltpu.VMEM((d, 2 * D_EXPERT), BF16),
                        pltpu.VMEM((D_EXPERT, d), BF16),
                        pltpu.SemaphoreType.DMA((2, 2))])
    return pl.pallas_call(
        _experts_kernel,
        out_shape=jax.ShapeDtypeStruct((rows, half), F32),
        grid_spec=grid_spec,
        compiler_params=_cparams(("arbitrary",)),
        name="expert_mlp",
    )(blk_e, n_used, next_e, slot_e, live, xs, wgu, bgu, wd, bd)


def _final_kernel(x1_ref, yk_ref, route_ref, gt_ref, fg_ref, *rest):
    o_ref = rest[-1]
    route = route_ref[0]
    ffn = jnp.zeros(x1_ref.shape[1:], F32)
    for kk in range(TOP_K):
        wk = route[:, 2 * TOP_K + kk:2 * TOP_K + kk + 1]
        y_lo, y_hi = _unpack_halves(yk_ref[kk, 0])
        ffn = ffn + wk * jnp.concatenate([y_lo, y_hi], axis=1)
    x2 = x1_ref[0] + gt_ref[0] * ffn
    o_ref[0] = _rms(x2, fg_ref[...])


def _final(x1, yk, route, gt2, final_g, tb, b0, b_total, prev):
    nb, s, d = x1.shape
    in_specs = [pl.BlockSpec((1, tb, d), lambda i, j: (i, j, 0)),
                pl.BlockSpec((TOP_K, 1, tb, d // 2), lambda i, j: (0, i, j, 0)),
                pl.BlockSpec((1, tb, LANES), lambda i, j: (i, j, 0)),
                pl.BlockSpec((1, 1, d), lambda i, j: (i + b0, 0, 0)),
                pl.BlockSpec((1, d), lambda i, j: (0, 0))]
    args = [x1, yk, route, gt2, final_g.reshape(1, d)]
    aliases = {}
    if prev is not None:
        in_specs.append(pl.BlockSpec(memory_space=pl.ANY))
        args.append(prev)
        aliases = {len(args) - 1: 0}
    return pl.pallas_call(
        _final_kernel,
        out_shape=jax.ShapeDtypeStruct((b_total, s, d), F32),
        grid=(nb, s // tb),
        in_specs=in_specs,
        out_specs=pl.BlockSpec((1, tb, d), lambda i, j: (i + b0, j, 0)),
        input_output_aliases=aliases,
        compiler_params=_cparams(("parallel", "parallel")),
        name="combine_final_norm",
    )(*args)


def _rot_cols(w):
    half = MLA_ROPE // 2
    return jnp.concatenate([-w[..., half:], w[..., :half]], axis=-1)


def _prep_weights(w_in, w_q_b, w_kv_b, conv_w, A_log, dt_bias):
    d = w_in.shape[0]
    cuts = [MLA_Q_LORA, MLA_KV_LORA, MLA_ROPE, GDN_HEADS * GDN_DK, GDN_HEADS * GDN_DK,
            GDN_HEADS * GDN_DV, GDN_HEADS * GDN_DV, GDN_HEADS, GDN_HEADS]
    offs = [0]
    for cw in cuts:
        offs.append(offs[-1] + cw)
    part = lambda i: w_in[:, offs[i]:offs[i + 1]]
    k_pe = part(2)
    misc = jnp.concatenate(
        [part(7), part(8), jnp.zeros((d, MLA_NOPE - 2 * GDN_HEADS), w_in.dtype), k_pe,
         _rot_cols(k_pe)], axis=1)
    heads = lambda w, n: w.reshape(w.shape[0], GDN_HEADS // GDN_GW, GDN_GW * n)
    slab = jnp.concatenate([heads(part(3), GDN_DK), heads(part(4), GDN_DK),
                            heads(part(5), GDN_DV), heads(part(6), GDN_DV)], axis=-1)
    w1 = jnp.concatenate([part(0), part(1), misc, slab.reshape(d, GDN_HEADS * GDN_SLAB)],
                         axis=1).astype(BF16)

    wq3 = w_q_b.reshape(MLA_Q_LORA, MLA_HEADS, MLA_QK)
    pe = wq3[..., MLA_NOPE:]
    wq = jnp.concatenate([wq3[..., :MLA_NOPE], pe, _rot_cols(pe)], axis=-1)
    wq = wq.reshape(MLA_Q_LORA, MLA_HEADS * HEAD_SLOT).astype(BF16)

    wkv3 = w_kv_b.reshape(MLA_KV_LORA, MLA_HEADS, MLA_NOPE + MLA_V)
    wk = jnp.concatenate([wkv3[..., :MLA_NOPE],
                          jnp.zeros((MLA_KV_LORA, MLA_HEADS, HEAD_SLOT - MLA_NOPE), w_kv_b.dtype)],
                         axis=-1).reshape(MLA_KV_LORA, MLA_HEADS * HEAD_SLOT)
    wv = wkv3[..., MLA_NOPE:].reshape(MLA_KV_LORA, MLA_HEADS * MLA_V)
    wkv = jnp.concatenate([wk, wv], axis=1).astype(BF16)

    nk = GDN_HEADS * GDN_DK
    cheads = lambda w, n: w.reshape(CONV_WIDTH, GDN_HEADS // GDN_GW, GDN_GW * n)
    conv_slab = jnp.concatenate(
        [cheads(conv_w[:, :nk], GDN_DK), cheads(conv_w[:, nk:2 * nk], GDN_DK),
         cheads(conv_w[:, 2 * nk:], GDN_DV),
         jnp.zeros((CONV_WIDTH, GDN_HEADS // GDN_GW, GDN_GW * GDN_DV), conv_w.dtype)],
        axis=-1).reshape(CONV_WIDTH, GDN_HEADS * GDN_SLAB)

    half = MLA_ROPE // 2
    inv_freq = ROPE_THETA ** (-jnp.arange(half, dtype=F32) / half)
    lane = jnp.arange(LANES)
    freq = jnp.where(lane[None, :] // MLA_ROPE == jnp.arange(ROPE_GROUPS)[:, None],
                     jnp.tile(inv_freq, LANES // half)[None, :], 0.0)
    padl = lambda a: jnp.concatenate([a.astype(F32), jnp.zeros((LANES - a.shape[0],), F32)]).reshape(1, LANES)
    return w1, wq, wkv, conv_slab, freq, padl(A_log), padl(dt_bias)


def _layer(x, mod, positions, w_in, q_norm_g, w_q_b, kv_norm_g, w_kv_b, mla_out_g, conv_w,
           A_log, dt_bias, gdn_norm_g, w_out, norm1_g, norm2_g, router_w, router_b, w_gate_up,
           b_gate_up, w_down, b_down, final_g):
    b, s, d = x.shape
    t = b * s
    sh1, sc1, gt1, sh2, sc2, gt2 = [m.reshape(b, 1, d) for m in jnp.split(mod, 6, axis=-1)]
    w1, wq, wkv, conv_slab, freq, alog, dtb = _prep_weights(w_in, w_q_b, w_kv_b, conv_w, A_log,
                                                            dt_bias)
    tb = min(TOKEN_BLOCK, s)
    q, k, v, slab, gb = _in_projection(x, positions, sc1, sh1, norm1_g, w1, q_norm_g, wq,
                                       kv_norm_g, wkv, freq, alog, dtb, tb)
    attn_o = _attention(q, k, v, tb, MLA_HEADS)
    gdn_o = _gdn(slab, gb, conv_slab, gdn_norm_g, tb)

    rw = jnp.concatenate([router_w, jnp.zeros((d, LANES - N_EXPERTS), router_w.dtype)], axis=1)
    rw_hi = rw.astype(BF16)
    rw = jnp.concatenate([rw_hi, (rw - rw_hi.astype(F32)).astype(BF16)], axis=1)
    rb = jnp.concatenate([router_b, jnp.zeros((LANES - N_EXPERTS,), router_b.dtype)]).reshape(1, LANES)
    w_out_bf = w_out.astype(BF16)
    bgu = b_gate_up.reshape(N_EXPERTS, 1, -1)
    bd = b_down.reshape(N_EXPERTS, 1, -1)
    n_parts = MOE_PARTS if b % MOE_PARTS == 0 else 1
    nb = b // n_parts
    out = None
    for part in range(n_parts):
        b0 = part * nb
        x1, h2, route, route_t, counts = _out_projection(x, attn_o, gdn_o, gt1, sc2, sh2,
                                                         mla_out_g, w_out_bf, norm2_g, rw, rb,
                                                         tb, b0, nb)
        yk = _moe(h2, route_t, counts, w_gate_up, bgu, w_down, bd)
        out = _final(x1, yk, route, gt2, final_g, tb, b0, b, out)
    return out


def _moe(h2, route_t, counts, w_gate_up, bgu, w_down, bd):
    nb, s, half = h2.shape
    t = nb * s
    bm = EXPERT_ROWS
    slot_major = lambda r0: jnp.swapaxes(route_t[:, r0:r0 + TOP_K, :], 0, 1).reshape(TOP_K, t)
    idx = slot_major(0).astype(jnp.int32)
    rank = slot_major(TOP_K).astype(jnp.int32)
    cnt = counts[0, :N_EXPERTS].astype(jnp.int32)
    padded = ((cnt + bm - 1) // bm) * bm
    pend = jnp.cumsum(padded)
    pstart = pend - padded
    base = jnp.zeros_like(idx)
    for e in range(N_EXPERTS):
        base = jnp.where(idx == e, pstart[e], base)
    dest_km = base + rank
    n_blocks = (t * TOP_K + N_EXPERTS * (bm - 1) + bm - 1) // bm
    rows = n_blocks * bm
    n_used = (pend[-1] // bm).astype(jnp.int32).reshape(1)
    blk_start = jnp.arange(n_blocks, dtype=jnp.int32) * bm
    blk_e = jnp.minimum(jnp.sum(blk_start[:, None] >= pend[None, :], axis=1), N_EXPERTS - 1)
    last_e = blk_e[jnp.maximum(n_used[0] - 1, 0)]
    blk_e = jnp.where(jnp.arange(n_blocks) < n_used[0], blk_e, last_e).astype(jnp.int32)

    present = padded > 0
    eidx = jnp.arange(N_EXPERTS, dtype=jnp.int32)
    later = present[None, :] & (eidx[None, :] > eidx[:, None])
    next_e = jnp.where(jnp.any(later, axis=1), jnp.argmax(later, axis=1), -1).astype(jnp.int32)
    slot_e = ((jnp.cumsum(present.astype(jnp.int32)) - 1) % 2).astype(jnp.int32)

    own = blk_e[:, None] == jnp.arange(N_EXPERTS, dtype=jnp.int32)[None, :]
    pick = lambda tab: jnp.sum(jnp.where(own, tab[None, :], 0), axis=1)
    in_blk = jnp.clip(pick(cnt) - (blk_start - pick(pstart)), 0, bm)
    piece = bm // ROW_PIECES
    live = jnp.clip((in_blk + piece - 1) // piece, 1, ROW_PIECES).astype(jnp.int32)

    xs = _dispatch_rows(h2.reshape(t, half), dest_km, rows)
    y_rows = _experts(blk_e, n_used, next_e, slot_e, live, xs, w_gate_up, bgu, w_down, bd, bm)
    return _collect_rows(y_rows, dest_km).reshape(TOP_K, nb, s, half)


def kernel(x, c, positions, ada_w, ada_b, norm1_g, w_in, q_norm_g, w_q_b, kv_norm_g, w_kv_b, mla_out_g, conv_w, A_log, dt_bias, gdn_norm_g, w_out, norm2_g, router_w, router_b, w_gate_up, b_gate_up, w_down, b_down, final_g):
    depth = ada_w.shape[0]
    assert depth == 1
    l = 0
    mod = _modulation(c, ada_w[l], ada_b[l])
    return _layer(x, mod, positions, w_in[l], q_norm_g[l], w_q_b[l], kv_norm_g[l],
                  w_kv_b[l], mla_out_g[l], conv_w[l], A_log[l], dt_bias[l], gdn_norm_g[l],
                  w_out[l], norm1_g[l], norm2_g[l], router_w[l], router_b[l],
                  w_gate_up[l], b_gate_up[l], w_down[l], b_down[l], final_g)
```
